```python
import math, functools
import jax, jax.numpy as jnp
from jax import lax
import numpy as np

D_MODEL = 1024
BATCH = 8
SEQ = 2048
DEPTH = 1
DEC_BATCH = 32
DEC_SEQ = 8
PAST_LEN = 16384
PAGE_SIZE = 128

N_META = 16
D_RNN = D_MODEL
RG_BLOCKS = 8
RG_BLOCK = D_RNN // RG_BLOCKS
CONV_W = 4
RG_C = 8.0
N_HEADS = 8
QK_NOPE = 128
QK_ROPE = 64
V_HEAD = 128
KV_LORA = 512
Q_LORA = 768
ROPE_THETA = 10000.0
Q_BLOCK = 128
D_FF = 4 * D_MODEL
DN_ALPHA = (2.0 * DEPTH) ** 0.25
DN_BETA = (8.0 * DEPTH) ** -0.25
EPS = 1e-5
SPLITS = [D_RNN, 2 * D_RNN, 2 * D_RNN + Q_LORA, 2 * D_RNN + Q_LORA + KV_LORA,
          2 * D_RNN + Q_LORA + KV_LORA + QK_ROPE]
N_IN = 2 * D_RNN + Q_LORA + KV_LORA + QK_ROPE + 2 * D_MODEL

kernel_name = "hawk_mla_gated_hybrid_step"


def layer_norm(x, g, b):
    xf = x.astype(jnp.float32)
    mu = jnp.mean(xf, -1, keepdims=True)
    var = jnp.mean(jnp.square(xf - mu), -1, keepdims=True)
    return ((xf - mu) * lax.rsqrt(var + EPS) * g.astype(jnp.float32) + b.astype(jnp.float32)).astype(x.dtype)


def rms_norm(x, g):
    xf = x.astype(jnp.float32)
    return (xf * lax.rsqrt(jnp.mean(jnp.square(xf), -1, keepdims=True) + EPS) * g.astype(jnp.float32)).astype(x.dtype)


def rope(x, pos):
    half = QK_ROPE // 2
    inv = 1.0 / (ROPE_THETA ** (jnp.arange(half, dtype=jnp.float32) / half))
    ang = pos.astype(jnp.float32)[:, None] * inv[None, :]
    shape = (1, pos.shape[0]) + (1,) * (x.ndim - 3) + (half,)
    cos = jnp.cos(ang).reshape(shape)
    sin = jnp.sin(ang).reshape(shape)
    xf = x.astype(jnp.float32)
    x1, x2 = xf[..., :half], xf[..., half:]
    return jnp.concatenate([x1 * cos - x2 * sin, x1 * sin + x2 * cos], -1).astype(x.dtype)


def causal_conv(x_ext, w, b):
    t = x_ext.shape[1] - (CONV_W - 1)
    y = b
    for k in range(CONV_W):
        y = y + x_ext[:, k:k + t] * w[k]
    return y


def rg_lru(x, h0, wa, ba, wx, bx, lam):
    bn, t, c = x.shape
    xf = x.astype(jnp.float32)
    xb = xf.reshape(bn, t, RG_BLOCKS, RG_BLOCK)
    r = jax.nn.sigmoid(jnp.einsum('btnc,ncd->btnd', xb, wa.astype(jnp.float32)).reshape(bn, t, c) + ba.astype(jnp.float32))
    i = jax.nn.sigmoid(jnp.einsum('btnc,ncd->btnd', xb, wx.astype(jnp.float32)).reshape(bn, t, c) + bx.astype(jnp.float32))
    log_a = -RG_C * r * jax.nn.softplus(-lam.astype(jnp.float32))
    a = jnp.exp(log_a)
    u = jnp.sqrt(-jnp.expm1(2.0 * log_a)) * (i * xf)

    def combine(lhs, rhs):
        a1, b1 = lhs
        a2, b2 = rhs
        return a1 * a2, a2 * b1 + b2

    a_cum, h_zero = lax.associative_scan(combine, (a, u), axis=1)
    h = a_cum * h0.astype(jnp.float32)[:, None, :] + h_zero
    return h.astype(x.dtype), h[:, -1].astype(x.dtype)


def mla_prompt(q_nope, q_pe, ckv, kpe, w_uk, w_uv):
    bn, t = ckv.shape[:2]
    k_nope = jnp.einsum('btr,rhd->bthd', ckv, w_uk)
    v = jnp.einsum('btr,rhd->bthd', ckv, w_uv)
    nb = -(-t // Q_BLOCK)
    tp = nb * Q_BLOCK
    pad = ((0, 0), (0, tp - t), (0, 0), (0, 0))
    qn = jnp.pad(q_nope, pad).reshape(bn, nb, Q_BLOCK, N_HEADS, QK_NOPE).swapaxes(0, 1)
    qp = jnp.pad(q_pe, pad).reshape(bn, nb, Q_BLOCK, N_HEADS, QK_ROPE).swapaxes(0, 1)
    qpos = jnp.arange(tp).reshape(nb, Q_BLOCK)
    kpos = jnp.arange(t)
    scale = (QK_NOPE + QK_ROPE) ** -0.5
    neg = jnp.finfo(jnp.float32).min

    def block(args):
        qn_b, qp_b, qpos_b = args
        s = jnp.einsum('bqhd,bkhd->bhqk', qn_b, k_nope) + jnp.einsum('bqhp,bkp->bhqk', qp_b, kpe)
        s = jnp.where(kpos[None, :] <= qpos_b[:, None], s.astype(jnp.float32) * scale, neg)
        p = jax.nn.softmax(s, axis=-1).astype(v.dtype)
        return jnp.einsum('bhqk,bkhd->bqhd', p, v)

    o = lax.map(block, (qn, qp, qpos))
    return o.swapaxes(0, 1).reshape(bn, tp, N_HEADS, V_HEAD)[:, :t]


def mla_sample(q_nope, q_pe, ckv_new, kpe_new, w_uk, w_uv, ckv_pool, kpe_pool, page_table):
    bd, tq = q_nope.shape[:2]
    ckv_past = ckv_pool[page_table].reshape(bd, -1, KV_LORA)
    kpe_past = kpe_pool[page_table].reshape(bd, -1, QK_ROPE)
    n_past = ckv_past.shape[1]
    q_lat = jnp.einsum('bqhd,rhd->bqhr', q_nope, w_uk)
    scale = (QK_NOPE + QK_ROPE) ** -0.5
    s_past = jnp.einsum('bqhr,bkr->bhqk', q_lat, ckv_past) + jnp.einsum('bqhp,bkp->bhqk', q_pe, kpe_past)
    s_new = jnp.einsum('bqhr,bkr->bhqk', q_lat, ckv_new) + jnp.einsum('bqhp,bkp->bhqk', q_pe, kpe_new)
    causal = jnp.tril(jnp.ones((tq, tq), dtype=bool))
    s_new = jnp.where(causal, s_new.astype(jnp.float32) * scale, jnp.finfo(jnp.float32).min)
    s = jnp.concatenate([s_past.astype(jnp.float32) * scale, s_new], axis=-1)
    p = jax.nn.softmax(s, axis=-1).astype(ckv_past.dtype)
    o_lat = (jnp.einsum('bhqk,bkr->bqhr', p[..., :n_past], ckv_past)
             + jnp.einsum('bhqk,bkr->bqhr', p[..., n_past:], ckv_new))
    return jnp.einsum('bqhr,rhd->bqhd', o_lat, w_uv)


def trunk_layer(h, pos, conv_prev, h_prev, attend, lw):
    (w_in, b_gate, conv_w, conv_b, rg_wa, rg_ba, rg_wx, rg_bx, rg_lambda, w_br_r,
     q_norm_g, w_uq, kv_norm_g, w_uk, w_uv, w_br_m, w_o, ln1_g, ln1_b,
     w_up, b_up, w_down, b_down, ln2_g, ln2_b) = lw
    bn, t, _ = h.shape
    u = h @ w_in
    rx, rg, cq, ckv_raw, kpe_raw, gates = jnp.split(u, SPLITS + [N_IN - 2 * D_MODEL], axis=-1)[:6] if False else jnp.split(u, SPLITS + [SPLITS[-1] + QK_ROPE * 0 + 0], axis=-1)[:0] or jnp.split(u, SPLITS, axis=-1)[:5] + [u[..., SPLITS[-1] + QK_ROPE:]] if False else _split_cols(u)
    g_r, g_m = jnp.split(jax.nn.sigmoid(gates + b_gate), 2, axis=-1)
    x_ext = jnp.concatenate([conv_prev.astype(rx.dtype), rx], axis=1)
    xc = causal_conv(x_ext, conv_w, conv_b)
    hr, h_last = rg_lru(xc, h_prev, rg_wa, rg_ba, rg_wx, rg_bx, rg_lambda)
    z_r = (hr * jax.nn.gelu(rg)) @ w_br_r
    q = jnp.einsum('btc,chd->bthd', rms_norm(cq, q_norm_g), w_uq)
    q_nope = q[..., :QK_NOPE]
    q_pe = rope(q[..., QK_NOPE:], pos)
    ckv = rms_norm(ckv_raw, kv_norm_g)
    kpe = rope(kpe_raw, pos)
    o = attend(q_nope, q_pe, ckv, kpe, w_uk, w_uv)
    z_m = o.reshape(bn, t, N_HEADS * V_HEAD) @ w_br_m
    mix = (g_r * z_r + g_m * z_m) @ w_o
    x1 = layer_norm(DN_ALPHA * h + mix, ln1_g, ln1_b)
    f = jnp.square(jax.nn.relu(x1 @ w_up + b_up)) @ w_down + b_down
    x2 = layer_norm(DN_ALPHA * x1 + f, ln2_g, ln2_b)
    new_conv = x_ext[:, -(CONV_W - 1):]
    return x2, ckv, kpe, new_conv, h_last


def _split_cols(u):
    rx = u[..., :SPLITS[0]]
    rg = u[..., SPLITS[0]:SPLITS[1]]
    cq = u[..., SPLITS[1]:SPLITS[2]]
    ckv_raw = u[..., SPLITS[2]:SPLITS[3]]
    kpe_raw = u[..., SPLITS[3]:SPLITS[4]]
    gates = u[..., SPLITS[4]:]
    return rx, rg, cq, ckv_raw, kpe_raw, gates


def setup_inputs(seed: int = 0) -> dict:
    key = jax.random.key(seed)
    ks = jax.random.split(key, 40)
    n_pages = PAST_LEN // PAGE_SIZE
    n_used = DEC_BATCH * n_pages
    n_phys = n_used + max(1, n_used // 4)
    f32 = jnp.float32

    def nrm(k, shape, scale):
        return jax.random.normal(k, shape, f32) * scale

    def gain(k, n):
        return 1.0 + 0.02 * jax.random.normal(k, (DEPTH, n), f32)

    a0 = jax.random.uniform(ks[12], (DEPTH, D_RNN), f32, minval=0.9, maxval=0.999)
    return {
        "x_prompt": nrm(ks[0], (BATCH, SEQ, D_MODEL), 1.0),
        "x_sample": nrm(ks[1], (DEC_BATCH, DEC_SEQ, D_MODEL), 1.0),
        "cache_ckv": nrm(ks[2], (DEPTH, n_phys, PAGE_SIZE, KV_LORA), 1.0),
        "cache_kpe": nrm(ks[3], (DEPTH, n_phys, PAGE_SIZE, QK_ROPE), 1.0),
        "page_table": jax.random.permutation(ks[4], n_phys)[:n_used].reshape(DEC_BATCH, n_pages).astype(jnp.int32),
        "state_conv": nrm(ks[5], (DEPTH, DEC_BATCH, CONV_W - 1, D_RNN), 1.0),
        "state_rglru": nrm(ks[6], (DEPTH, DEC_BATCH, D_RNN), 0.5),
        "meta_tokens": nrm(ks[7], (N_META, D_MODEL), 1.0),
        "ln_in_g": 1.0 + 0.02 * jax.random.normal(ks[8], (D_MODEL,), f32),
        "ln_in_b": nrm(ks[9], (D_MODEL,), 0.02),
        "w_in": nrm(ks[10], (DEPTH, D_MODEL, N_IN), D_MODEL ** -0.5),
        "b_gate": nrm(ks[11], (DEPTH, 2 * D_MODEL), 0.02),
        "conv_w": nrm(ks[13], (DEPTH, CONV_W, D_RNN), CONV_W ** -0.5),
        "conv_b": nrm(ks[14], (DEPTH, D_RNN), 0.02),
        "rg_wa": nrm(ks[15], (DEPTH, RG_BLOCKS, RG_BLOCK, RG_BLOCK), RG_BLOCK ** -0.5),
        "rg_ba": nrm(ks[16], (DEPTH, D_RNN), 0.02),
        "rg_wx": nrm(ks[17], (DEPTH, RG_BLOCKS, RG_BLOCK, RG_BLOCK), RG_BLOCK ** -0.5),
        "rg_bx": nrm(ks[18], (DEPTH, D_RNN), 0.02),
        "rg_lambda": jnp.log(a0) - jnp.log1p(-a0),
        "w_br_r": nrm(ks[19], (DEPTH, D_RNN, D_MODEL), D_RNN ** -0.5),
        "q_norm_g": gain(ks[20], Q_LORA),
        "w_uq": nrm(ks[21], (DEPTH, Q_LORA, N_HEADS, QK_NOPE + QK_ROPE), Q_LORA ** -0.5),
        "kv_norm_g": gain(ks[22], KV_LORA),
        "w_uk": nrm(ks[23], (DEPTH, KV_LORA, N_HEADS, QK_NOPE), KV_LORA ** -0.5),
        "w_uv": nrm(ks[24], (DEPTH, KV_LORA, N_HEADS, V_HEAD), KV_LORA ** -0.5),
        "w_br_m": nrm(ks[25], (DEPTH, N_HEADS * V_HEAD, D_MODEL), (N_HEADS * V_HEAD) ** -0.5),
        "w_o": nrm(ks[26], (DEPTH, D_MODEL, D_MODEL), DN_BETA * D_MODEL ** -0.5),
        "ln1_g": gain(ks[27], D_MODEL),
        "ln1_b": nrm(ks[28], (DEPTH, D_MODEL), 0.02),
        "w_up": nrm(ks[29], (DEPTH, D_MODEL, D_FF), D_MODEL ** -0.5),
        "b_up": nrm(ks[30], (DEPTH, D_FF), 0.02),
        "w_down": nrm(ks[31], (DEPTH, D_FF, D_MODEL), DN_BETA * D_FF ** -0.5),
        "b_down": nrm(ks[32], (DEPTH, D_MODEL), 0.02),
        "ln2_g": gain(ks[33], D_MODEL),
        "ln2_b": nrm(ks[34], (DEPTH, D_MODEL), 0.02),
    }


def reference(x_prompt, x_sample, cache_ckv, cache_kpe, page_table, state_conv, state_rglru,
              meta_tokens, ln_in_g, ln_in_b, w_in, b_gate, conv_w, conv_b, rg_wa, rg_ba, rg_wx, rg_bx,
              rg_lambda, w_br_r, q_norm_g, w_uq, kv_norm_g, w_uk, w_uv, w_br_m, w_o, ln1_g, ln1_b,
              w_up, b_up, w_down, b_down, ln2_g, ln2_b):
    bn = x_prompt.shape[0]
    bd, ts = x_sample.shape[:2]
    meta = jnp.broadcast_to(meta_tokens.astype(x_prompt.dtype)[None], (bn, N_META, D_MODEL))
    hp = layer_norm(jnp.concatenate([meta, x_prompt], axis=1), ln_in_g, ln_in_b)
    hs = layer_norm(x_sample, ln_in_g, ln_in_b)
    pos_p = jnp.arange(hp.shape[1])
    pos_s = PAST_LEN + jnp.arange(ts)
    p_ckv, p_kpe, p_conv, p_h, s_ckv, s_kpe, s_conv, s_h = [], [], [], [], [], [], [], []
    for l in range(DEPTH):
        lw = (w_in[l], b_gate[l], conv_w[l], conv_b[l], rg_wa[l], rg_ba[l], rg_wx[l], rg_bx[l],
              rg_lambda[l], w_br_r[l], q_norm_g[l], w_uq[l], kv_norm_g[l], w_uk[l], w_uv[l],
              w_br_m[l], w_o[l], ln1_g[l], ln1_b[l], w_up[l], b_up[l], w_down[l], b_down[l],
              ln2_g[l], ln2_b[l])
        conv0 = jnp.zeros((bn, CONV_W - 1, D_RNN), hp.dtype)
        h0 = jnp.zeros((bn, D_RNN), hp.dtype)
        hp, ck, kp, cv, hl = trunk_layer(hp, pos_p, conv0, h0, mla_prompt, lw)
        p_ckv.append(ck); p_kpe.append(kp); p_conv.append(cv); p_h.append(hl)
        attend_s = functools.partial(mla_sample, ckv_pool=cache_ckv[l], kpe_pool=cache_kpe[l],
                                     page_table=page_table)
        hs, ck, kp, cv, hl = trunk_layer(hs, pos_s, state_conv[l], state_rglru[l], attend_s, lw)
        s_ckv.append(ck); s_kpe.append(kp); s_conv.append(cv); s_h.append(hl)
    y_prompt = hp[:, N_META:]
    y_sample = hs
    prompt_ckv = jnp.stack(p_ckv)
    prompt_kpe = jnp.stack(p_kpe)
    prompt_conv = jnp.stack(p_conv)
    prompt_rglru = jnp.stack(p_h)
    sample_ckv = jnp.stack(s_ckv)
    sample_kpe = jnp.stack(s_kpe)
    sample_conv = jnp.stack(s_conv)
    sample_rglru = jnp.stack(s_h)
    return (y_prompt, y_sample, prompt_ckv, prompt_kpe, prompt_conv, prompt_rglru,
            sample_ckv, sample_kpe, sample_conv, sample_rglru)
```

```python
import functools
import math

import jax
import jax.numpy as jnp
from jax import lax
from jax.experimental import pallas as pl
from jax.experimental.pallas import tpu as pltpu

F32 = jnp.float32
BF16 = jnp.bfloat16

D_MODEL = 1024
N_META = 16
D_RNN = D_MODEL
RG_BLOCKS = 8
RG_BLOCK = D_RNN // RG_BLOCKS
CONV_W = 4
RG_C = 8.0
N_HEADS = 8
QK_NOPE = 128
QK_ROPE = 64
V_HEAD = 128
KV_LORA = 512
Q_LORA = 768
ROPE_THETA = 10000.0
D_FF = 4 * D_MODEL
DEPTH = 1
DN_ALPHA = (2.0 * DEPTH) ** 0.25
EPS = 1e-5
SM_SCALE = (QK_NOPE + QK_ROPE) ** -0.5

LANES = 128
SUBLANES = 8
HEAD_PAD = 2 * LANES
ROPE_PAD = LANES
_C_RX, _C_RG, _C_CQ, _C_CKV, _C_KPE, _C_GATE = 0, 1024, 2048, 2816, 3328, 3456
N_IN_PAD = _C_GATE + 2 * D_MODEL
VMEM_LIMIT = 56 * 1024 * 1024
NEG = float(jnp.finfo(jnp.float32).min)


def _const_spec(shape):
    return pl.BlockSpec(shape, lambda *_: (0,) * len(shape), pipeline_mode=pl.Buffered(1))


def _layer_norm(x, g, b):
    mu = jnp.mean(x, -1, keepdims=True)
    xc = x - mu
    var = jnp.mean(xc * xc, -1, keepdims=True)
    return xc * lax.rsqrt(var + EPS) * g + b


def _rms_norm(x, g):
    return x * lax.rsqrt(jnp.mean(x * x, -1, keepdims=True) + EPS) * g


def _gelu_tanh(x):
    return 0.5 * x * (1.0 + jnp.tanh(math.sqrt(2.0 / math.pi) * (x + 0.044715 * (x * x * x))))


def _rope_block(y, cos, sin_lo, sin_hi):
    left = pltpu.roll(y, ROPE_PAD - QK_ROPE // 2, 1)
    right = pltpu.roll(y, QK_ROPE // 2, 1)
    return y * cos + left * sin_lo + right * sin_hi


def _dot(a, b):
    return jnp.dot(a, b, preferred_element_type=F32)


def _dot_nt(a, b):
    return lax.dot_general(a, b, (((1,), (1,)), ((), ())), preferred_element_type=F32)


def _in_proj_kernel(x_ref, cos_ref, slo_ref, shi_ref, lng_ref, lnb_ref, w_in_ref, bg_ref, qg_ref, wuq_ref,
                    kvg_ref, wuk_ref, wuv_ref,
                    rx_ref, grg_ref, g_ref, q_ref, k_ref, v_ref, ckv_ref, kpe_ref):
    h = _layer_norm(x_ref[...], lng_ref[...], lnb_ref[...])
    hb = h.astype(BF16)

    def proj(lo, hi):
        return _dot(hb, w_in_ref[:, lo:hi])

    cos, slo, shi = cos_ref[...], slo_ref[...], shi_ref[...]
    rx_ref[...] = proj(_C_RX, _C_RG)
    grg_ref[...] = _gelu_tanh(proj(_C_RG, _C_CQ))
    g_ref[...] = jax.nn.sigmoid(proj(_C_GATE, N_IN_PAD) + bg_ref[...])

    cqn = _rms_norm(proj(_C_CQ, _C_CKV), qg_ref[...])
    q = _dot(cqn.astype(BF16), wuq_ref[...])
    for hh in range(N_HEADS):
        c0 = hh * HEAD_PAD
        q_ref[:, c0:c0 + QK_NOPE] = q[:, c0:c0 + QK_NOPE].astype(BF16)
        q_ref[:, c0 + QK_NOPE:c0 + HEAD_PAD] = _rope_block(q[:, c0 + QK_NOPE:c0 + HEAD_PAD], cos, slo, shi).astype(BF16)

    ckv = _rms_norm(proj(_C_CKV, _C_KPE), kvg_ref[...])
    ckv_ref[...] = ckv
    kpe = _rope_block(proj(_C_KPE, _C_GATE), cos, slo, shi)
    kpe_ref[...] = kpe[:, :QK_ROPE]
    ckvb = ckv.astype(BF16)
    kpeb = kpe.astype(BF16)
    kn = _dot(ckvb, wuk_ref[...])
    v_ref[...] = _dot(ckvb, wuv_ref[...]).astype(BF16)
    for hh in range(N_HEADS):
        c0 = hh * HEAD_PAD
        k_ref[:, c0:c0 + QK_NOPE] = kn[:, hh * QK_NOPE:(hh + 1) * QK_NOPE].astype(BF16)
        k_ref[:, c0 + QK_NOPE:c0 + HEAD_PAD] = kpeb


def _in_proj(x, tabs, tab_blocks, p, tm):
    rows = x.shape[0]
    assert rows % tm == 0
    row = lambda w: pl.BlockSpec((tm, w), lambda i: (i, 0))
    tab = pl.BlockSpec((tm, ROPE_PAD), lambda i: (i % tab_blocks, 0))
    outs = [
        (D_RNN, F32),
        (D_RNN, F32),
        (2 * D_MODEL, F32),
        (N_HEADS * HEAD_PAD, BF16),
        (N_HEADS * HEAD_PAD, BF16),
        (N_HEADS * V_HEAD, BF16),
        (KV_LORA, F32),
        (QK_ROPE, F32),
    ]
    return pl.pallas_call(
        _in_proj_kernel,
        grid=(rows // tm,),
        in_specs=[row(D_MODEL), tab, tab, tab,
                  _const_spec((1, D_MODEL)), _const_spec((1, D_MODEL)),
                  _const_spec((D_MODEL, N_IN_PAD)), _const_spec((1, 2 * D_MODEL)),
                  _const_spec((1, Q_LORA)), _const_spec((Q_LORA, N_HEADS * HEAD_PAD)),
                  _const_spec((1, KV_LORA)), _const_spec((KV_LORA, N_HEADS * QK_NOPE)),
                  _const_spec((KV_LORA, N_HEADS * V_HEAD))],
        out_specs=[row(w) for w, _ in outs],
        out_shape=[jax.ShapeDtypeStruct((rows, w), dt) for w, dt in outs],
        compiler_params=pltpu.CompilerParams(dimension_semantics=("arbitrary",), vmem_limit_bytes=VMEM_LIMIT),
        name="in_proj",
    )(x, *tabs, p["ln_in_g"], p["ln_in_b"], p["w_in"], p["b_gate"], p["q_norm_g"], p["w_uq"],
      p["kv_norm_g"], p["w_uk"], p["w_uv"])


def _rglru_kernel(rx_ref, grg_ref, cprev_ref, hprev_ref, cw_ref, cb_ref, wax_ref, ba_ref, bx_ref, lam_ref,
                  y_ref, cout_ref, hout_ref, xe_ref, a_ref, u_ref, hc_ref, *, nb, tt):
    pitch = tt + SUBLANES
    n_slab = D_RNN // LANES
    tail = SUBLANES - (CONV_W - 1)

    @pl.when(pl.program_id(0) == 0)
    def _():
        xe_ref[:, tail:SUBLANES, :] = cprev_ref[...]
        hc_ref[...] = hprev_ref[...]

    xe_ref[:, SUBLANES:, :] = rx_ref[...]
    xc = cb_ref[...] + xe_ref[:, SUBLANES:, :] * cw_ref[CONV_W - 1:CONV_W, :]
    for k in range(CONV_W - 1):
        xc = xc + xe_ref[:, tail + k:tail + k + tt, :] * cw_ref[k:k + 1, :]
    new_tail = xe_ref[:, tt + tail:tt + SUBLANES, :]
    cout_ref[...] = new_tail
    xe_ref[:, tail:SUBLANES, :] = new_tail

    neg_lam = -lam_ref[...]
    softplus = jnp.maximum(neg_lam, 0.0) + jnp.log1p(jnp.exp(-jnp.abs(neg_lam)))
    xc2 = xc.reshape(nb * tt, D_RNN)
    for n in range(n_slab):
        cs = slice(n * LANES, (n + 1) * LANES)
        xn = xc2[:, cs]
        gates = _dot(xn.astype(BF16), wax_ref[n])
        r = jax.nn.sigmoid(gates[:, :RG_BLOCK] + ba_ref[:, cs])
        ig = jax.nn.sigmoid(gates[:, RG_BLOCK:] + bx_ref[:, cs])
        log_a = (-RG_C) * r * softplus[:, cs]
        a = jnp.exp(log_a)
        u = jnp.sqrt(-jnp.tanh(log_a) * (a * a + 1.0)) * (ig * xn)
        for b in range(nb):
            a_ref[n, b * pitch:b * pitch + tt, :] = a[b * tt:(b + 1) * tt]
            u_ref[n, b * pitch:b * pitch + tt, :] = u[b * tt:(b + 1) * tt]

    for grp in range(nb // SUBLANES):
        base = grp * SUBLANES * pitch
        rows = slice(grp * SUBLANES, (grp + 1) * SUBLANES)
        h0 = tuple(hc_ref[rows, n * LANES:(n + 1) * LANES] for n in range(n_slab))

        def step(t, hs, base=base):
            out = []
            for n in range(n_slab):
                idx = pl.ds(base + t, SUBLANES, stride=pitch)
                hn = a_ref[n, idx, :] * hs[n] + u_ref[n, idx, :]
                u_ref[n, idx, :] = hn
                out.append(hn)
            return tuple(out)

        hs = lax.fori_loop(0, tt, step, h0, unroll=min(tt, 8))
        for n in range(n_slab):
            hc_ref[rows, n * LANES:(n + 1) * LANES] = hs[n]
    hout_ref[...] = hc_ref[...]

    for b in range(nb):
        for n in range(n_slab):
            cs = slice(n * LANES, (n + 1) * LANES)
            y_ref[b, :, cs] = (u_ref[n, b * pitch:b * pitch + tt, :] * grg_ref[b, :, cs]).astype(BF16)


def _rglru(rx, grg, conv_prev, h_prev, p, tt):
    nb, t_len, _ = rx.shape
    assert nb % SUBLANES == 0 and t_len % tt == 0 and tt % SUBLANES == 0
    pitch = tt + SUBLANES
    seq = pl.BlockSpec((nb, tt, D_RNN), lambda i: (0, i, 0))
    return pl.pallas_call(
        functools.partial(_rglru_kernel, nb=nb, tt=tt),
        grid=(t_len // tt,),
        in_specs=[seq, seq, _const_spec((nb, CONV_W - 1, D_RNN)), _const_spec((nb, D_RNN)),
                  _const_spec((CONV_W, D_RNN)), _const_spec((1, D_RNN)),
                  _const_spec((RG_BLOCKS, RG_BLOCK, 2 * RG_BLOCK)),
                  _const_spec((1, D_RNN)), _const_spec((1, D_RNN)), _const_spec((1, D_RNN))],
        out_specs=[seq,
                   pl.BlockSpec((nb, CONV_W - 1, D_RNN), lambda i: (0, 0, 0)),
                   pl.BlockSpec((nb, D_RNN), lambda i: (0, 0))],
        out_shape=[jax.ShapeDtypeStruct((nb, t_len, D_RNN), BF16),
                   jax.ShapeDtypeStruct((nb, CONV_W - 1, D_RNN), F32),
                   jax.ShapeDtypeStruct((nb, D_RNN), F32)],
        scratch_shapes=[pltpu.VMEM((nb, tt + SUBLANES, D_RNN), F32),
                        pltpu.VMEM((D_RNN // LANES, nb * pitch, LANES), F32),
                        pltpu.VMEM((D_RNN // LANES, nb * pitch, LANES), F32),
                        pltpu.VMEM((nb, D_RNN), F32)],
        compiler_params=pltpu.CompilerParams(dimension_semantics=("arbitrary",), vmem_limit_bytes=VMEM_LIMIT),
        name="rglru",
    )(rx, grg, conv_prev, h_prev, p["conv_w"], p["conv_b"], p["w_ax"], p["rg_ba"], p["rg_bx"], p["rg_lambda"])


def _softmax_update(s, m, l, acc, v):
    m_new = jnp.maximum(m, jnp.max(s, axis=1, keepdims=True))
    alpha = jnp.exp(m - m_new)
    pr = jnp.exp(s - m_new)
    l = alpha * l + jnp.sum(pr, axis=1, keepdims=True)
    acc = alpha * acc + _dot(pr.astype(BF16), v)
    return m_new, l, acc


def _flash_kernel(q_ref, k_ref, v_ref, km_ref, vm_ref, o_ref, *, tq):
    i = pl.program_id(2)
    q = q_ref[...]
    s = _dot_nt(q, km_ref[...]) * SM_SCALE
    m = jnp.max(s, axis=1, keepdims=True)
    pr = jnp.exp(s - m)
    l = jnp.sum(pr, axis=1, keepdims=True)
    acc = _dot(pr.astype(BF16), vm_ref[...])

    def body(j, carry):
        rows = pl.ds(pl.multiple_of(j * tq, tq), tq)
        s = _dot_nt(q, k_ref[rows, :]) * SM_SCALE
        return _softmax_update(s, *carry, v_ref[rows, :])

    m, l, acc = lax.fori_loop(0, i, body, (m, l, acc))

    rows = pl.ds(pl.multiple_of(i * tq, tq), tq)
    s = _dot_nt(q, k_ref[rows, :]) * SM_SCALE
    qpos = lax.broadcasted_iota(jnp.int32, (tq, tq), 0)
    kpos = lax.broadcasted_iota(jnp.int32, (tq, tq), 1)
    s = jnp.where(kpos <= qpos, s, NEG)
    m, l, acc = _softmax_update(s, m, l, acc, v_ref[rows, :])
    o_ref[...] = (acc / l).astype(BF16)


def _flash(q, k, v, k_pre, v_pre, tq):
    nb, t_len, _ = q.shape
    assert t_len % tq == 0
    return pl.pallas_call(
        functools.partial(_flash_kernel, tq=tq),
        grid=(nb, N_HEADS, t_len // tq),
        in_specs=[pl.BlockSpec((None, tq, HEAD_PAD), lambda b, h, i: (b, i, h)),
                  pl.BlockSpec((None, t_len, HEAD_PAD), lambda b, h, i: (b, 0, h)),
                  pl.BlockSpec((None, t_len, V_HEAD), lambda b, h, i: (b, 0, h)),
                  pl.BlockSpec((N_META, HEAD_PAD), lambda b, h, i: (0, h)),
                  pl.BlockSpec((N_META, V_HEAD), lambda b, h, i: (0, h))],
        out_specs=pl.BlockSpec((None, tq, V_HEAD), lambda b, h, i: (b, i, h)),
        out_shape=jax.ShapeDtypeStruct((nb, t_len, N_HEADS * V_HEAD), BF16),
        compiler_params=pltpu.CompilerParams(dimension_semantics=("arbitrary",) * 3, vmem_limit_bytes=VMEM_LIMIT),
        name="flash_prompt",
    )(q, k, v, k_pre, v_pre)


def _meta_attn_kernel(q_ref, k_ref, v_ref, o_ref):
    qpos = lax.broadcasted_iota(jnp.int32, (N_META, N_META), 0)
    kpos = lax.broadcasted_iota(jnp.int32, (N_META, N_META), 1)
    for hh in range(N_HEADS):
        qs = slice(hh * HEAD_PAD, (hh + 1) * HEAD_PAD)
        vs = slice(hh * V_HEAD, (hh + 1) * V_HEAD)
        s = jnp.where(kpos <= qpos, _dot_nt(q_ref[:, qs], k_ref[:, qs]) * SM_SCALE, NEG)
        pr = jnp.exp(s - jnp.max(s, axis=1, keepdims=True))
        o = _dot(pr.astype(BF16), v_ref[:, vs]) / jnp.sum(pr, axis=1, keepdims=True)
        o_ref[:, vs] = o.astype(BF16)


def _meta_attn(q, k, v):
    return pl.pallas_call(
        _meta_attn_kernel,
        out_shape=jax.ShapeDtypeStruct((N_META, N_HEADS * V_HEAD), BF16),
        name="meta_attn",
    )(q, k, v)


def _absorb_kernel(q_ref, wukt_ref, qlat_ref, qpe_ref):
    q = q_ref[...]
    nb = qlat_ref.shape[0]
    qlat = _dot(q[:, :QK_NOPE], wukt_ref[...])
    qlat_ref[...] = qlat.reshape(nb, -1, KV_LORA)
    qpe_ref[...] = q[:, QK_NOPE:].astype(F32).reshape(nb, -1, ROPE_PAD)


def _absorb(q, w_ukt, nb):
    rows = q.shape[0]
    ts = rows // nb
    return pl.pallas_call(
        _absorb_kernel,
        grid=(N_HEADS,),
        in_specs=[pl.BlockSpec((rows, HEAD_PAD), lambda h: (0, h)),
                  pl.BlockSpec((None, QK_NOPE, KV_LORA), lambda h: (h, 0, 0))],
        out_specs=[pl.BlockSpec((nb, None, ts, KV_LORA), lambda h: (0, h, 0, 0)),
                   pl.BlockSpec((nb, None, ts, ROPE_PAD), lambda h: (0, h, 0, 0))],
        out_shape=[jax.ShapeDtypeStruct((nb, N_HEADS, ts, KV_LORA), F32),
                   jax.ShapeDtypeStruct((nb, N_HEADS, ts, ROPE_PAD), F32)],
        compiler_params=pltpu.CompilerParams(dimension_semantics=("arbitrary",)),
        name="absorb_q",
    )(q, w_ukt)


def _paged_attn_kernel(pt_ref, qlat_ref, qpe_ref, cnew_ref, knew_ref, *rest, pps, ts):
    pages_c = rest[:pps]
    pages_k = rest[pps:2 * pps]
    o_ref, m_ref, l_ref, acc_ref, cbuf_ref, kbuf_ref = rest[2 * pps:]
    j = pl.program_id(1)
    qlat = qlat_ref[...].astype(BF16)
    qpe = qpe_ref[...][:, :QK_ROPE].astype(BF16)

    @pl.when(j == 0)
    def _():
        cn = cnew_ref[...].astype(BF16)
        kn = knew_ref[...].astype(BF16)
        s = (_dot_nt(qlat, cn) + _dot_nt(qpe, kn)) * SM_SCALE
        t_q = lax.broadcasted_iota(jnp.int32, s.shape, 0) % ts
        t_k = lax.broadcasted_iota(jnp.int32, s.shape, 1)
        s = jnp.where(t_k <= t_q, s, NEG)
        m = jnp.max(s, axis=1, keepdims=True)
        pr = jnp.exp(s - m)
        m_ref[...] = m
        l_ref[...] = jnp.sum(pr, axis=1, keepdims=True)
        acc_ref[...] = _dot(pr.astype(BF16), cn)

    page = pages_c[0].shape[0]
    for i in range(pps):
        cbuf_ref[i * page:(i + 1) * page, :] = pages_c[i][...].astype(BF16)
        kbuf_ref[i * page:(i + 1) * page, :] = pages_k[i][...].astype(BF16)
    s = (_dot_nt(qlat, cbuf_ref[...]) + _dot_nt(qpe, kbuf_ref[...])) * SM_SCALE
    m, l, acc = _softmax_update(s, m_ref[...], l_ref[...], acc_ref[...], cbuf_ref[...])
    m_ref[...] = m
    l_ref[...] = l
    acc_ref[...] = acc

    @pl.when(j == pl.num_programs(1) - 1)
    def _():
        o_ref[...] = acc / l


def _paged_attn(page_table, q_lat, q_pe, ckv_new, kpe_new, ckv_pool, kpe_pool, pps):
    nb, n_pages = page_table.shape
    n_q = q_lat.shape[1]
    ts = n_q // N_HEADS
    page = ckv_pool.shape[1]
    assert n_pages % pps == 0
    n_new = ckv_new.shape[1]

    def page_spec(width, i):
        return pl.BlockSpec((None, page, width), lambda b, j, pt, i=i: (pt[b, j * pps + i], 0, 0))

    grid_spec = pltpu.PrefetchScalarGridSpec(
        num_scalar_prefetch=1,
        grid=(nb, n_pages // pps),
        in_specs=[pl.BlockSpec((None, n_q, KV_LORA), lambda b, j, pt: (b, 0, 0)),
                  pl.BlockSpec((None, n_q, ROPE_PAD), lambda b, j, pt: (b, 0, 0)),
                  pl.BlockSpec((None, n_new, KV_LORA), lambda b, j, pt: (b, 0, 0)),
                  pl.BlockSpec((None, n_new, QK_ROPE), lambda b, j, pt: (b, 0, 0))]
                 + [page_spec(KV_LORA, i) for i in range(pps)]
                 + [page_spec(QK_ROPE, i) for i in range(pps)],
        out_specs=pl.BlockSpec((None, n_q, KV_LORA), lambda b, j, pt: (b, 0, 0)),
        scratch_shapes=[pltpu.VMEM((n_q, 1), F32), pltpu.VMEM((n_q, 1), F32), pltpu.VMEM((n_q, KV_LORA), F32),
                        pltpu.VMEM((pps * page, KV_LORA), BF16), pltpu.VMEM((pps * page, QK_ROPE), BF16)],
    )
    return pl.pallas_call(
        functools.partial(_paged_attn_kernel, pps=pps, ts=ts),
        grid_spec=grid_spec,
        out_shape=jax.ShapeDtypeStruct((nb, n_q, KV_LORA), F32),
        compiler_params=pltpu.CompilerParams(dimension_semantics=("arbitrary", "arbitrary"),
                                             vmem_limit_bytes=VMEM_LIMIT),
        name="paged_attn",
    )(page_table, q_lat, q_pe, ckv_new, kpe_new, *([ckv_pool] * pps), *([kpe_pool] * pps))


def _unabsorb_kernel(olat_ref, wuv_ref, o_ref):
    x = olat_ref[...]
    x = x.reshape(x.shape[0] * x.shape[1], KV_LORA).astype(BF16)
    o_ref[...] = _dot(x, wuv_ref[...]).astype(BF16)


def _unabsorb(o_lat, w_uvh):
    nb, _, ts, _ = o_lat.shape
    return pl.pallas_call(
        _unabsorb_kernel,
        grid=(N_HEADS,),
        in_specs=[pl.BlockSpec((nb, None, ts, KV_LORA), lambda h: (0, h, 0, 0)),
                  pl.BlockSpec((None, KV_LORA, V_HEAD), lambda h: (h, 0, 0))],
        out_specs=pl.BlockSpec((nb * ts, V_HEAD), lambda h: (0, h)),
        out_shape=jax.ShapeDtypeStruct((nb * ts, N_HEADS * V_HEAD), BF16),
        compiler_params=pltpu.CompilerParams(dimension_semantics=("arbitrary",)),
        name="unabsorb_o",
    )(o_lat, w_uvh)


def _post_kernel(x_ref, y_ref, o_ref, g_ref, lng_ref, lnb_ref, wr_ref, wm_ref, wo_ref, g1_ref, b1_ref,
                 wup_ref, bup_ref, wdn_ref, bdn_ref, g2_ref, b2_ref, out_ref, *, ff_chunk):
    h = _layer_norm(x_ref[...], lng_ref[...], lnb_ref[...])
    z_r = _dot(y_ref[...], wr_ref[...])
    z_m = _dot(o_ref[...], wm_ref[...])
    mix_in = g_ref[:, :D_MODEL] * z_r + g_ref[:, D_MODEL:] * z_m
    mix = _dot(mix_in.astype(BF16), wo_ref[...])
    x1 = _layer_norm(DN_ALPHA * h + mix, g1_ref[...], b1_ref[...])
    x1b = x1.astype(BF16)
    f = bdn_ref[...]
    for c in range(D_FF // ff_chunk):
        cs = slice(c * ff_chunk, (c + 1) * ff_chunk)
        up = jnp.maximum(_dot(x1b, wup_ref[:, cs]) + bup_ref[:, cs], 0.0)
        f = f + _dot((up * up).astype(BF16), wdn_ref[cs, :])
    out_ref[...] = _layer_norm(DN_ALPHA * x1 + f, g2_ref[...], b2_ref[...])


def _post(x, y, o, g, p, tm):
    rows = x.shape[0]
    assert rows % tm == 0
    row = lambda w: pl.BlockSpec((tm, w), lambda i: (i, 0))
    vec = lambda w: _const_spec((1, w))
    sq = _const_spec((D_MODEL, D_MODEL))
    return pl.pallas_call(
        functools.partial(_post_kernel, ff_chunk=1024),
        grid=(rows // tm,),
        in_specs=[row(D_MODEL), row(D_RNN), row(N_HEADS * V_HEAD), row(2 * D_MODEL),
                  vec(D_MODEL), vec(D_MODEL), sq, sq, sq, vec(D_MODEL), vec(D_MODEL),
                  _const_spec((D_MODEL, D_FF)), vec(D_FF), _const_spec((D_FF, D_MODEL)), vec(D_MODEL),
                  vec(D_MODEL), vec(D_MODEL)],
        out_specs=row(D_MODEL),
        out_shape=jax.ShapeDtypeStruct((rows, D_MODEL), F32),
        compiler_params=pltpu.CompilerParams(dimension_semantics=("arbitrary",), vmem_limit_bytes=VMEM_LIMIT),
        name="merge_mlp",
    )(x, y, o, g, p["ln_in_g"], p["ln_in_b"], p["w_br_r"], p["w_br_m"], p["w_o"], p["ln1_g"], p["ln1_b"],
      p["w_up"], p["b_up"], p["w_down"], p["b_down"], p["ln2_g"], p["ln2_b"])


def _rope_tables(pos):
    half = QK_ROPE // 2
    inv = 1.0 / (ROPE_THETA ** (jnp.arange(half, dtype=F32) / half))
    ang = pos.astype(F32)[:, None] * inv[None, :]
    cos, sin = jnp.cos(ang), jnp.sin(ang)
    z = jnp.zeros_like(cos)
    return (jnp.concatenate([cos, cos, z, z], 1), jnp.concatenate([-sin, z, z, z], 1),
            jnp.concatenate([z, sin, z, z], 1))


def _prep_params(w_in, b_gate, conv_w, conv_b, rg_wa, rg_ba, rg_wx, rg_bx, rg_lambda, w_br_r, q_norm_g, w_uq,
                 kv_norm_g, w_uk, w_uv, w_br_m, w_o, ln1_g, ln1_b, w_up, b_up, w_down, b_down, ln2_g, ln2_b,
                 ln_in_g, ln_in_b):
    l = 0
    vec = lambda a: a.reshape(1, -1).astype(F32)
    wi = w_in[l]
    w_in_p = jnp.concatenate(
        [wi[:, :_C_KPE], wi[:, _C_KPE:_C_KPE + QK_ROPE], jnp.zeros((D_MODEL, ROPE_PAD - QK_ROPE), wi.dtype),
         wi[:, _C_KPE + QK_ROPE:]], axis=1).astype(BF16)
    w_uq_p = jnp.pad(w_uq[l], ((0, 0), (0, 0), (0, HEAD_PAD - QK_NOPE - QK_ROPE))).reshape(Q_LORA, N_HEADS * HEAD_PAD)
    return {
        "ln_in_g": vec(ln_in_g), "ln_in_b": vec(ln_in_b),
        "w_in": w_in_p, "b_gate": vec(b_gate[l]),
        "q_norm_g": vec(q_norm_g[l]), "w_uq": w_uq_p.astype(BF16),
        "kv_norm_g": vec(kv_norm_g[l]),
        "w_uk": w_uk[l].reshape(KV_LORA, N_HEADS * QK_NOPE).astype(BF16),
        "w_uv": w_uv[l].reshape(KV_LORA, N_HEADS * V_HEAD).astype(BF16),
        "w_ukt": jnp.transpose(w_uk[l], (1, 2, 0)).astype(BF16),
        "w_uvh": jnp.transpose(w_uv[l], (1, 0, 2)).astype(BF16),
        "conv_w": conv_w[l].astype(F32), "conv_b": vec(conv_b[l]),
        "w_ax": jnp.concatenate([rg_wa[l], rg_wx[l]], axis=-1).astype(BF16),
        "rg_ba": vec(rg_ba[l]), "rg_bx": vec(rg_bx[l]), "rg_lambda": vec(rg_lambda[l]),
        "w_br_r": w_br_r[l].astype(BF16), "w_br_m": w_br_m[l].astype(BF16), "w_o": w_o[l].astype(BF16),
        "ln1_g": vec(ln1_g[l]), "ln1_b": vec(ln1_b[l]),
        "w_up": w_up[l].astype(BF16), "b_up": vec(b_up[l]),
        "w_down": w_down[l].astype(BF16), "b_down": vec(b_down[l]),
        "ln2_g": vec(ln2_g[l]), "ln2_b": vec(ln2_b[l]),
    }


def kernel(x_prompt, x_sample, cache_ckv, cache_kpe, page_table, state_conv, state_rglru, meta_tokens, ln_in_g, ln_in_b, w_in, b_gate, conv_w, conv_b, rg_wa, rg_ba, rg_wx, rg_bx, rg_lambda, w_br_r, q_norm_g, w_uq, kv_norm_g, w_uk, w_uv, w_br_m, w_o, ln1_g, ln1_b, w_up, b_up, w_down, b_down, ln2_g, ln2_b):
    assert w_in.shape[0] == DEPTH == 1
    bn, seq, _ = x_prompt.shape
    bd, ts, _ = x_sample.shape
    past_len = page_table.shape[1] * cache_ckv.shape[2]
    p = _prep_params(w_in, b_gate, conv_w, conv_b, rg_wa, rg_ba, rg_wx, rg_bx, rg_lambda, w_br_r, q_norm_g, w_uq,
                     kv_norm_g, w_uk, w_uv, w_br_m, w_o, ln1_g, ln1_b, w_up, b_up, w_down, b_down, ln2_g, ln2_b,
                     ln_in_g, ln_in_b)
    n_s = bd * ts

    x_small = jnp.concatenate([x_sample.reshape(n_s, D_MODEL), meta_tokens.astype(F32)], axis=0)
    pos_small = jnp.concatenate([jnp.tile(past_len + jnp.arange(ts), bd), jnp.arange(N_META)])
    n_small = n_s + N_META
    rx, grg, g_small, q, k, v, ckv, kpe = _in_proj(x_small, _rope_tables(pos_small), 1, p, n_small)

    rep = lambda a: jnp.broadcast_to(a[None], (SUBLANES,) + a.shape)
    y_m, conv_m, h_m = _rglru(rep(rx[n_s:]), rep(grg[n_s:]), jnp.zeros((SUBLANES, CONV_W - 1, D_RNN), F32),
                              jnp.zeros((SUBLANES, D_RNN), F32), p, N_META)
    k_meta, v_meta = k[n_s:], v[n_s:]
    o_m = _meta_attn(q[n_s:], k_meta, v_meta)

    y_s, conv_s, h_s = _rglru(rx[:n_s].reshape(bd, ts, D_RNN), grg[:n_s].reshape(bd, ts, D_RNN),
                              state_conv[0], state_rglru[0], p, ts)
    q_lat, q_pe = _absorb(q[:n_s], p["w_ukt"], bd)
    ckv_s = ckv[:n_s].reshape(bd, ts, KV_LORA)
    kpe_s = kpe[:n_s].reshape(bd, ts, QK_ROPE)
    pad_new = lambda a: jnp.pad(a, ((0, 0), (0, 2 * SUBLANES - ts), (0, 0)))
    o_lat = _paged_attn(page_table, q_lat.reshape(bd, N_HEADS * ts, KV_LORA), q_pe.reshape(bd, N_HEADS * ts, ROPE_PAD),
                        pad_new(ckv_s), pad_new(kpe_s), cache_ckv[0], cache_kpe[0], pps=8)
    o_s = _unabsorb(o_lat.reshape(bd, N_HEADS, ts, KV_LORA), p["w_uvh"])

    y_small = jnp.concatenate([y_s.reshape(n_s, D_RNN), y_m[0]], axis=0)
    o_small = jnp.concatenate([o_s, o_m], axis=0)
    out_small = _post(x_small, y_small, o_small, g_small, p, n_small)

    n_p = bn * seq
    pos_p = N_META + jnp.arange(seq)
    tm = 256
    rx_p, grg_p, g_p, q_p, k_p, v_p, ckv_p, kpe_p = _in_proj(x_prompt.reshape(n_p, D_MODEL), _rope_tables(pos_p),
                                                             seq // tm, p, tm)
    bcast = lambda a: jnp.broadcast_to(a[:1], (bn,) + a.shape[1:])
    y_p, conv_p, h_p = _rglru(rx_p.reshape(bn, seq, D_RNN), grg_p.reshape(bn, seq, D_RNN), bcast(conv_m), bcast(h_m),
                              p, 128)
    o_p = _flash(q_p.reshape(bn, seq, -1), k_p.reshape(bn, seq, -1), v_p.reshape(bn, seq, -1), k_meta, v_meta, 512)
    out_p = _post(x_prompt.reshape(n_p, D_MODEL), y_p.reshape(n_p, D_RNN), o_p.reshape(n_p, -1), g_p, p, tm)

    meta_rows = lambda a: jnp.broadcast_to(a[n_s:][None], (bn, N_META, a.shape[-1]))
    y_prompt = out_p.reshape(bn, seq, D_MODEL)
    y_sample = out_small[:n_s].reshape(bd, ts, D_MODEL)
    prompt_ckv = jnp.concatenate([meta_rows(ckv), ckv_p.reshape(bn, seq, KV_LORA)], axis=1)[None]
    prompt_kpe = jnp.concatenate([meta_rows(kpe), kpe_p.reshape(bn, seq, QK_ROPE)], axis=1)[None]
    return (y_prompt, y_sample, prompt_ckv, prompt_kpe, conv_p[None], h_p[None],
            ckv_s[None], kpe_s[None], conv_s[None], h_s[None])
```

```python
import functools
import math

import jax
import jax.numpy as jnp
from jax import lax
from jax.experimental import pallas as pl
from jax.experimental.pallas import tpu as pltpu

F32 = jnp.float32
BF16 = jnp.bfloat16

D_MODEL = 1024
N_META = 16
D_RNN = D_MODEL
RG_BLOCKS = 8
RG_BLOCK = D_RNN // RG_BLOCKS
CONV_W = 4
RG_C = 8.0
N_HEADS = 8
QK_NOPE = 128
QK_ROPE = 64
V_HEAD = 128
KV_LORA = 512
Q_LORA = 768
ROPE_THETA = 10000.0
D_FF = 4 * D_MODEL
DEPTH = 1
DN_ALPHA = (2.0 * DEPTH) ** 0.25
EPS = 1e-5
SM_SCALE = (QK_NOPE + QK_ROPE) ** -0.5

LANES = 128
SUBLANES = 8
HEAD_PAD = 2 * LANES
ROPE_PAD = LANES
_C_RX, _C_RG, _C_CQ, _C_CKV, _C_KPE, _C_GATE = 0, 1024, 2048, 2816, 3328, 3456
N_IN_PAD = _C_GATE + 2 * D_MODEL
VMEM_LIMIT = 56 * 1024 * 1024
NEG = float(jnp.finfo(jnp.float32).min)


def _const_spec(shape):
    return pl.BlockSpec(shape, lambda *_: (0,) * len(shape), pipeline_mode=pl.Buffered(1))


def _layer_norm(x, g, b):
    mu = jnp.mean(x, -1, keepdims=True)
    xc = x - mu
    var = jnp.mean(xc * xc, -1, keepdims=True)
    return xc * lax.rsqrt(var + EPS) * g + b


def _rms_norm(x, g):
    return x * lax.rsqrt(jnp.mean(x * x, -1, keepdims=True) + EPS) * g


def _gelu_tanh(x):
    return 0.5 * x * (1.0 + jnp.tanh(math.sqrt(2.0 / math.pi) * (x + 0.044715 * (x * x * x))))


def _rope_block(y, cos, sin_lo, sin_hi):
    left = pltpu.roll(y, ROPE_PAD - QK_ROPE // 2, 1)
    right = pltpu.roll(y, QK_ROPE // 2, 1)
    return y * cos + left * sin_lo + right * sin_hi


def _dot(a, b):
    return jnp.dot(a, b, preferred_element_type=F32)


def _dot_nt(a, b):
    return lax.dot_general(a, b, (((1,), (1,)), ((), ())), preferred_element_type=F32)


def _in_proj_kernel(x_ref, cos_ref, slo_ref, shi_ref, lng_ref, lnb_ref, w_in_ref, bg_ref, qg_ref, wuq_ref,
                    kvg_ref, wuk_ref, wuv_ref,
                    rx_ref, grg_ref, g_ref, q_ref, k_ref, v_ref, ckv_ref, kpe_ref):
    h = _layer_norm(x_ref[...], lng_ref[...], lnb_ref[...])
    hb = h.astype(BF16)

    def proj(lo, hi):
        return _dot(hb, w_in_ref[:, lo:hi])

    cos, slo, shi = cos_ref[...], slo_ref[...], shi_ref[...]
    rx_ref[...] = proj(_C_RX, _C_RG)
    grg_ref[...] = _gelu_tanh(proj(_C_RG, _C_CQ))
    g_ref[...] = jax.nn.sigmoid(proj(_C_GATE, N_IN_PAD) + bg_ref[...])

    cqn = _rms_norm(proj(_C_CQ, _C_CKV), qg_ref[...])
    q = _dot(cqn.astype(BF16), wuq_ref[...])
    for hh in range(N_HEADS):
        c0 = hh * HEAD_PAD
        q_ref[:, c0:c0 + QK_NOPE] = q[:, c0:c0 + QK_NOPE].astype(BF16)
        q_ref[:, c0 + QK_NOPE:c0 + HEAD_PAD] = _rope_block(q[:, c0 + QK_NOPE:c0 + HEAD_PAD], cos, slo, shi).astype(BF16)

    ckv = _rms_norm(proj(_C_CKV, _C_KPE), kvg_ref[...])
    ckv_ref[...] = ckv
    kpe = _rope_block(proj(_C_KPE, _C_GATE), cos, slo, shi)
    kpe_ref[...] = kpe[:, :QK_ROPE]
    ckvb = ckv.astype(BF16)
    kpeb = kpe.astype(BF16)
    kn = _dot(ckvb, wuk_ref[...])
    v_ref[...] = _dot(ckvb, wuv_ref[...]).astype(BF16)
    for hh in range(N_HEADS):
        c0 = hh * HEAD_PAD
        k_ref[:, c0:c0 + QK_NOPE] = kn[:, hh * QK_NOPE:(hh + 1) * QK_NOPE].astype(BF16)
        k_ref[:, c0 + QK_NOPE:c0 + HEAD_PAD] = kpeb


def _in_proj(x, tabs, tab_blocks, p, tm):
    rows = x.shape[0]
    assert rows % tm == 0
    row = lambda w: pl.BlockSpec((tm, w), lambda i: (i, 0))
    tab = pl.BlockSpec((tm, ROPE_PAD), lambda i: (i % tab_blocks, 0))
    outs = [
        (D_RNN, F32),
        (D_RNN, F32),
        (2 * D_MODEL, F32),
        (N_HEADS * HEAD_PAD, BF16),
        (N_HEADS * HEAD_PAD, BF16),
        (N_HEADS * V_HEAD, BF16),
        (KV_LORA, F32),
        (QK_ROPE, F32),
    ]
    return pl.pallas_call(
        _in_proj_kernel,
        grid=(rows // tm,),
        in_specs=[row(D_MODEL), tab, tab, tab,
                  _const_spec((1, D_MODEL)), _const_spec((1, D_MODEL)),
                  _const_spec((D_MODEL, N_IN_PAD)), _const_spec((1, 2 * D_MODEL)),
                  _const_spec((1, Q_LORA)), _const_spec((Q_LORA, N_HEADS * HEAD_PAD)),
                  _const_spec((1, KV_LORA)), _const_spec((KV_LORA, N_HEADS * QK_NOPE)),
                  _const_spec((KV_LORA, N_HEADS * V_HEAD))],
        out_specs=[row(w) for w, _ in outs],
        out_shape=[jax.ShapeDtypeStruct((rows, w), dt) for w, dt in outs],
        compiler_params=pltpu.CompilerParams(dimension_semantics=("arbitrary",), vmem_limit_bytes=VMEM_LIMIT),
        name="in_proj",
    )(x, *tabs, p["ln_in_g"], p["ln_in_b"], p["w_in"], p["b_gate"], p["q_norm_g"], p["w_uq"],
      p["kv_norm_g"], p["w_uk"], p["w_uv"])


def _rglru_kernel(rx_ref, grg_ref, cprev_ref, hprev_ref, cw_ref, cb_ref, wax_ref, ba_ref, bx_ref, lam_ref,
                  y_ref, cout_ref, hout_ref, xe_ref, a_ref, u_ref, hc_ref, *, nb, tt):
    pitch = tt + SUBLANES
    n_slab = D_RNN // LANES
    tail = SUBLANES - (CONV_W - 1)

    @pl.when(pl.program_id(0) == 0)
    def _():
        xe_ref[:, tail:SUBLANES, :] = cprev_ref[...]
        hc_ref[...] = hprev_ref[...]

    xe_ref[:, SUBLANES:, :] = rx_ref[...]
    xc = cb_ref[...] + xe_ref[:, SUBLANES:, :] * cw_ref[CONV_W - 1:CONV_W, :]
    for k in range(CONV_W - 1):
        xc = xc + xe_ref[:, tail + k:tail + k + tt, :] * cw_ref[k:k + 1, :]
    new_tail = xe_ref[:, tt + tail:tt + SUBLANES, :]
    cout_ref[...] = new_tail
    xe_ref[:, tail:SUBLANES, :] = new_tail

    neg_lam = -lam_ref[...]
    softplus = jnp.maximum(neg_lam, 0.0) + jnp.log1p(jnp.exp(-jnp.abs(neg_lam)))
    xc2 = xc.reshape(nb * tt, D_RNN)
    for n in range(n_slab):
        cs = slice(n * LANES, (n + 1) * LANES)
        xn = xc2[:, cs]
        gates = _dot(xn.astype(BF16), wax_ref[n])
        r = jax.nn.sigmoid(gates[:, :RG_BLOCK] + ba_ref[:, cs])
        ig = jax.nn.sigmoid(gates[:, RG_BLOCK:] + bx_ref[:, cs])
        log_a = (-RG_C) * r * softplus[:, cs]
        a = jnp.exp(log_a)
        u = jnp.sqrt(-jnp.tanh(log_a) * (a * a + 1.0)) * (ig * xn)
        for b in range(nb):
            a_ref[n, b * pitch:b * pitch + tt, :] = a[b * tt:(b + 1) * tt]
            u_ref[n, b * pitch:b * pitch + tt, :] = u[b * tt:(b + 1) * tt]

    for grp in range(nb // SUBLANES):
        base = grp * SUBLANES * pitch
        rows = slice(grp * SUBLANES, (grp + 1) * SUBLANES)
        h0 = tuple(hc_ref[rows, n * LANES:(n + 1) * LANES] for n in range(n_slab))

        def step(t, hs, base=base):
            out = []
            for n in range(n_slab):
                idx = pl.ds(base + t, SUBLANES, stride=pitch)
                hn = a_ref[n, idx, :] * hs[n] + u_ref[n, idx, :]
                u_ref[n, idx, :] = hn
                out.append(hn)
            return tuple(out)

        hs = lax.fori_loop(0, tt, step, h0, unroll=min(tt, 8))
        for n in range(n_slab):
            hc_ref[rows, n * LANES:(n + 1) * LANES] = hs[n]
    hout_ref[...] = hc_ref[...]

    for b in range(nb):
        for n in range(n_slab):
            cs = slice(n * LANES, (n + 1) * LANES)
            y_ref[b, :, cs] = (u_ref[n, b * pitch:b * pitch + tt, :] * grg_ref[b, :, cs]).astype(BF16)


def _rglru(rx, grg, conv_prev, h_prev, p, tt):
    nb, t_len, _ = rx.shape
    assert nb % SUBLANES == 0 and t_len % tt == 0 and tt % SUBLANES == 0
    pitch = tt + SUBLANES
    seq = pl.BlockSpec((nb, tt, D_RNN), lambda i: (0, i, 0))
    return pl.pallas_call(
        functools.partial(_rglru_kernel, nb=nb, tt=tt),
        grid=(t_len // tt,),
        in_specs=[seq, seq, _const_spec((nb, CONV_W - 1, D_RNN)), _const_spec((nb, D_RNN)),
                  _const_spec((CONV_W, D_RNN)), _const_spec((1, D_RNN)),
                  _const_spec((RG_BLOCKS, RG_BLOCK, 2 * RG_BLOCK)),
                  _const_spec((1, D_RNN)), _const_spec((1, D_RNN)), _const_spec((1, D_RNN))],
        out_specs=[seq,
                   pl.BlockSpec((nb, CONV_W - 1, D_RNN), lambda i: (0, 0, 0)),
                   pl.BlockSpec((nb, D_RNN), lambda i: (0, 0))],
        out_shape=[jax.ShapeDtypeStruct((nb, t_len, D_RNN), BF16),
                   jax.ShapeDtypeStruct((nb, CONV_W - 1, D_RNN), F32),
                   jax.ShapeDtypeStruct((nb, D_RNN), F32)],
        scratch_shapes=[pltpu.VMEM((nb, tt + SUBLANES, D_RNN), F32),
                        pltpu.VMEM((D_RNN // LANES, nb * pitch, LANES), F32),
                        pltpu.VMEM((D_RNN // LANES, nb * pitch, LANES), F32),
                        pltpu.VMEM((nb, D_RNN), F32)],
        compiler_params=pltpu.CompilerParams(dimension_semantics=("arbitrary",), vmem_limit_bytes=VMEM_LIMIT),
        name="rglru",
    )(rx, grg, conv_prev, h_prev, p["conv_w"], p["conv_b"], p["w_ax"], p["rg_ba"], p["rg_bx"], p["rg_lambda"])


def _softmax_update(s, m, l, acc, v):
    m_new = jnp.maximum(m, jnp.max(s, axis=1, keepdims=True))
    alpha = jnp.exp(m - m_new)
    pr = jnp.exp(s - m_new)
    l = alpha * l + jnp.sum(pr, axis=1, keepdims=True)
    acc = alpha * acc + _dot(pr.astype(BF16), v)
    return m_new, l, acc


def _flash_kernel(q_ref, k_ref, v_ref, km_ref, vm_ref, o_ref, *, tq):
    i = pl.program_id(2)
    q = q_ref[...]
    s = _dot_nt(q, km_ref[...]) * SM_SCALE
    m = jnp.max(s, axis=1, keepdims=True)
    pr = jnp.exp(s - m)
    l = jnp.sum(pr, axis=1, keepdims=True)
    acc = _dot(pr.astype(BF16), vm_ref[...])

    def body(j, carry):
        rows = pl.ds(pl.multiple_of(j * tq, tq), tq)
        s = _dot_nt(q, k_ref[rows, :]) * SM_SCALE
        return _softmax_update(s, *carry, v_ref[rows, :])

    m, l, acc = lax.fori_loop(0, i, body, (m, l, acc))

    rows = pl.ds(pl.multiple_of(i * tq, tq), tq)
    s = _dot_nt(q, k_ref[rows, :]) * SM_SCALE
    qpos = lax.broadcasted_iota(jnp.int32, (tq, tq), 0)
    kpos = lax.broadcasted_iota(jnp.int32, (tq, tq), 1)
    s = jnp.where(kpos <= qpos, s, NEG)
    m, l, acc = _softmax_update(s, m, l, acc, v_ref[rows, :])
    o_ref[...] = (acc / l).astype(BF16)


def _flash(q, k, v, k_pre, v_pre, tq):
    nb, t_len, _ = q.shape
    assert t_len % tq == 0
    return pl.pallas_call(
        functools.partial(_flash_kernel, tq=tq),
        grid=(nb, N_HEADS, t_len // tq),
        in_specs=[pl.BlockSpec((None, tq, HEAD_PAD), lambda b, h, i: (b, i, h)),
                  pl.BlockSpec((None, t_len, HEAD_PAD), lambda b, h, i: (b, 0, h)),
                  pl.BlockSpec((None, t_len, V_HEAD), lambda b, h, i: (b, 0, h)),
                  pl.BlockSpec((N_META, HEAD_PAD), lambda b, h, i: (0, h)),
                  pl.BlockSpec((N_META, V_HEAD), lambda b, h, i: (0, h))],
        out_specs=pl.BlockSpec((None, tq, V_HEAD), lambda b, h, i: (b, i, h)),
        out_shape=jax.ShapeDtypeStruct((nb, t_len, N_HEADS * V_HEAD), BF16),
        compiler_params=pltpu.CompilerParams(dimension_semantics=("arbitrary",) * 3, vmem_limit_bytes=VMEM_LIMIT),
        name="flash_prompt",
    )(q, k, v, k_pre, v_pre)


def _meta_attn_kernel(q_ref, k_ref, v_ref, o_ref):
    qpos = lax.broadcasted_iota(jnp.int32, (N_META, N_META), 0)
    kpos = lax.broadcasted_iota(jnp.int32, (N_META, N_META), 1)
    for hh in range(N_HEADS):
        qs = slice(hh * HEAD_PAD, (hh + 1) * HEAD_PAD)
        vs = slice(hh * V_HEAD, (hh + 1) * V_HEAD)
        s = jnp.where(kpos <= qpos, _dot_nt(q_ref[:, qs], k_ref[:, qs]) * SM_SCALE, NEG)
        pr = jnp.exp(s - jnp.max(s, axis=1, keepdims=True))
        o = _dot(pr.astype(BF16), v_ref[:, vs]) / jnp.sum(pr, axis=1, keepdims=True)
        o_ref[:, vs] = o.astype(BF16)


def _meta_attn(q, k, v):
    return pl.pallas_call(
        _meta_attn_kernel,
        out_shape=jax.ShapeDtypeStruct((N_META, N_HEADS * V_HEAD), BF16),
        name="meta_attn",
    )(q, k, v)


def _absorb_kernel(q_ref, wukt_ref, qlat_ref, qpe_ref):
    q = q_ref[...]
    nb = qlat_ref.shape[0]
    qlat = _dot(q[:, :QK_NOPE], wukt_ref[...])
    qlat_ref[...] = qlat.reshape(nb, -1, KV_LORA)
    qpe_ref[...] = q[:, QK_NOPE:].astype(F32).reshape(nb, -1, ROPE_PAD)


def _absorb(q, w_ukt, nb):
    rows = q.shape[0]
    ts = rows // nb
    return pl.pallas_call(
        _absorb_kernel,
        grid=(N_HEADS,),
        in_specs=[pl.BlockSpec((rows, HEAD_PAD), lambda h: (0, h)),
                  pl.BlockSpec((None, QK_NOPE, KV_LORA), lambda h: (h, 0, 0))],
        out_specs=[pl.BlockSpec((nb, None, ts, KV_LORA), lambda h: (0, h, 0, 0)),
                   pl.BlockSpec((nb, None, ts, ROPE_PAD), lambda h: (0, h, 0, 0))],
        out_shape=[jax.ShapeDtypeStruct((nb, N_HEADS, ts, KV_LORA), F32),
                   jax.ShapeDtypeStruct((nb, N_HEADS, ts, ROPE_PAD), F32)],
        compiler_params=pltpu.CompilerParams(dimension_semantics=("arbitrary",)),
        name="absorb_q",
    )(q, w_ukt)


def _paged_attn_kernel(pt_ref, qlat_ref, qpe_ref, cnew_ref, knew_ref, ckv_hbm, kpe_hbm, o_ref,
                       m_ref, l_ref, acc_ref, cin_ref, kin_ref, cbuf_ref, kbuf_ref, sem, *, pps, ts, n_buf, n_b, n_j):
    b, j = pl.program_id(0), pl.program_id(1)
    g = b * n_j + j
    page = kin_ref.shape[3]
    ahead = n_buf - 1

    def chunk_copies(cb, cj, slot):
        out = []
        for i in range(pps):
            pid = pt_ref[cb, cj * pps + i]
            out.append(pltpu.make_async_copy(ckv_hbm.at[pid], cin_ref.at[slot, pl.ds(i * page, page)], sem.at[slot]))
            out.append(pltpu.make_async_copy(kpe_hbm.at[pid], kin_ref.at[slot, i], sem.at[slot]))
        return out

    @pl.when(g == 0)
    def _():
        for k in range(ahead):
            for c in chunk_copies(0, k, k):
                c.start()

    @pl.when(g + ahead < n_b * n_j)
    def _():
        jj = j + ahead
        wrap = jj >= n_j
        for c in chunk_copies(b + wrap.astype(jnp.int32), jnp.where(wrap, jj - n_j, jj), lax.rem(g + ahead, n_buf)):
            c.start()

    slot = lax.rem(g, n_buf)
    for c in chunk_copies(b, j, slot):
        c.wait()

    qlat = qlat_ref[...].astype(BF16)
    qpe = qpe_ref[...][:, :QK_ROPE].astype(BF16)

    @pl.when(j == 0)
    def _():
        cn = cnew_ref[...].astype(BF16)
        kn = knew_ref[...].astype(BF16)
        s = (_dot_nt(qlat, cn) + _dot_nt(qpe, kn)) * SM_SCALE
        t_q = lax.broadcasted_iota(jnp.int32, s.shape, 0) % ts
        t_k = lax.broadcasted_iota(jnp.int32, s.shape, 1)
        s = jnp.where(t_k <= t_q, s, NEG)
        m = jnp.max(s, axis=1, keepdims=True)
        pr = jnp.exp(s - m)
        m_ref[...] = m
        l_ref[...] = jnp.sum(pr, axis=1, keepdims=True)
        acc_ref[...] = _dot(pr.astype(BF16), cn)

    for i in range(pps):
        rows = pl.ds(i * page, page)
        cbuf_ref[rows, :] = cin_ref[slot, rows, :].astype(BF16)
        kbuf_ref[:, i * page:(i + 1) * page] = kin_ref[slot, i].astype(BF16)
    s = (_dot_nt(qlat, cbuf_ref[...]) + _dot(qpe, kbuf_ref[...])) * SM_SCALE
    m, l, acc = _softmax_update(s, m_ref[...], l_ref[...], acc_ref[...], cbuf_ref[...])
    m_ref[...] = m
    l_ref[...] = l
    acc_ref[...] = acc

    @pl.when(j == n_j - 1)
    def _():
        o_ref[...] = acc / l


def _paged_attn(page_table, q_lat, q_pe, ckv_new, kpe_new, ckv_pool, kpe_pool_t, pps, n_buf):
    nb, n_pages = page_table.shape
    n_q = q_lat.shape[1]
    ts = n_q // N_HEADS
    page = ckv_pool.shape[1]
    n_j = n_pages // pps
    assert n_pages % pps == 0 and 2 <= n_buf <= n_j + 1
    n_new = ckv_new.shape[1]
    per_seq = lambda rows, width: pl.BlockSpec((None, rows, width), lambda b, j, pt: (b, 0, 0))

    grid_spec = pltpu.PrefetchScalarGridSpec(
        num_scalar_prefetch=1,
        grid=(nb, n_j),
        in_specs=[per_seq(n_q, KV_LORA), per_seq(n_q, ROPE_PAD), per_seq(n_new, KV_LORA), per_seq(n_new, QK_ROPE),
                  pl.BlockSpec(memory_space=pl.ANY), pl.BlockSpec(memory_space=pl.ANY)],
        out_specs=per_seq(n_q, KV_LORA),
        scratch_shapes=[pltpu.VMEM((n_q, 1), F32), pltpu.VMEM((n_q, 1), F32), pltpu.VMEM((n_q, KV_LORA), F32),
                        pltpu.VMEM((n_buf, pps * page, KV_LORA), F32), pltpu.VMEM((n_buf, pps, QK_ROPE, page), F32),
                        pltpu.VMEM((pps * page, KV_LORA), BF16), pltpu.VMEM((QK_ROPE, pps * page), BF16),
                        pltpu.SemaphoreType.DMA((n_buf,))],
    )
    return pl.pallas_call(
        functools.partial(_paged_attn_kernel, pps=pps, ts=ts, n_buf=n_buf, n_b=nb, n_j=n_j),
        grid_spec=grid_spec,
        out_shape=jax.ShapeDtypeStruct((nb, n_q, KV_LORA), F32),
        compiler_params=pltpu.CompilerParams(dimension_semantics=("arbitrary", "arbitrary"),
                                             vmem_limit_bytes=VMEM_LIMIT),
        name="paged_attn",
    )(page_table, q_lat, q_pe, ckv_new, kpe_new, ckv_pool, kpe_pool_t)


def _unabsorb_kernel(olat_ref, wuv_ref, o_ref):
    x = olat_ref[...]
    x = x.reshape(x.shape[0] * x.shape[1], KV_LORA).astype(BF16)
    o_ref[...] = _dot(x, wuv_ref[...]).astype(BF16)


def _unabsorb(o_lat, w_uvh):
    nb, _, ts, _ = o_lat.shape
    return pl.pallas_call(
        _unabsorb_kernel,
        grid=(N_HEADS,),
        in_specs=[pl.BlockSpec((nb, None, ts, KV_LORA), lambda h: (0, h, 0, 0)),
                  pl.BlockSpec((None, KV_LORA, V_HEAD), lambda h: (h, 0, 0))],
        out_specs=pl.BlockSpec((nb * ts, V_HEAD), lambda h: (0, h)),
        out_shape=jax.ShapeDtypeStruct((nb * ts, N_HEADS * V_HEAD), BF16),
        compiler_params=pltpu.CompilerParams(dimension_semantics=("arbitrary",)),
        name="unabsorb_o",
    )(o_lat, w_uvh)


def _post_kernel(x_ref, y_ref, o_ref, g_ref, lng_ref, lnb_ref, wr_ref, wm_ref, wo_ref, g1_ref, b1_ref,
                 wup_ref, bup_ref, wdn_ref, bdn_ref, g2_ref, b2_ref, out_ref, *, ff_chunk):
    h = _layer_norm(x_ref[...], lng_ref[...], lnb_ref[...])
    z_r = _dot(y_ref[...], wr_ref[...])
    z_m = _dot(o_ref[...], wm_ref[...])
    mix_in = g_ref[:, :D_MODEL] * z_r + g_ref[:, D_MODEL:] * z_m
    mix = _dot(mix_in.astype(BF16), wo_ref[...])
    x1 = _layer_norm(DN_ALPHA * h + mix, g1_ref[...], b1_ref[...])
    x1b = x1.astype(BF16)
    f = bdn_ref[...]
    for c in range(D_FF // ff_chunk):
        cs = slice(c * ff_chunk, (c + 1) * ff_chunk)
        up = jnp.maximum(_dot(x1b, wup_ref[:, cs]) + bup_ref[:, cs], 0.0)
        f = f + _dot((up * up).astype(BF16), wdn_ref[cs, :])
    out_ref[...] = _layer_norm(DN_ALPHA * x1 + f, g2_ref[...], b2_ref[...])


def _post(x, y, o, g, p, tm):
    rows = x.shape[0]
    assert rows % tm == 0
    row = lambda w: pl.BlockSpec((tm, w), lambda i: (i, 0))
    vec = lambda w: _const_spec((1, w))
    sq = _const_spec((D_MODEL, D_MODEL))
    return pl.pallas_call(
        functools.partial(_post_kernel, ff_chunk=1024),
        grid=(rows // tm,),
        in_specs=[row(D_MODEL), row(D_RNN), row(N_HEADS * V_HEAD), row(2 * D_MODEL),
                  vec(D_MODEL), vec(D_MODEL), sq, sq, sq, vec(D_MODEL), vec(D_MODEL),
                  _const_spec((D_MODEL, D_FF)), vec(D_FF), _const_spec((D_FF, D_MODEL)), vec(D_MODEL),
                  vec(D_MODEL), vec(D_MODEL)],
        out_specs=row(D_MODEL),
        out_shape=jax.ShapeDtypeStruct((rows, D_MODEL), F32),
        compiler_params=pltpu.CompilerParams(dimension_semantics=("arbitrary",), vmem_limit_bytes=VMEM_LIMIT),
        name="merge_mlp",
    )(x, y, o, g, p["ln_in_g"], p["ln_in_b"], p["w_br_r"], p["w_br_m"], p["w_o"], p["ln1_g"], p["ln1_b"],
      p["w_up"], p["b_up"], p["w_down"], p["b_down"], p["ln2_g"], p["ln2_b"])


def _rope_tables(pos):
    half = QK_ROPE // 2
    inv = 1.0 / (ROPE_THETA ** (jnp.arange(half, dtype=F32) / half))
    ang = pos.astype(F32)[:, None] * inv[None, :]
    cos, sin = jnp.cos(ang), jnp.sin(ang)
    z = jnp.zeros_like(cos)
    return (jnp.concatenate([cos, cos, z, z], 1), jnp.concatenate([-sin, z, z, z], 1),
            jnp.concatenate([z, sin, z, z], 1))


def _prep_params(w_in, b_gate, conv_w, conv_b, rg_wa, rg_ba, rg_wx, rg_bx, rg_lambda, w_br_r, q_norm_g, w_uq,
                 kv_norm_g, w_uk, w_uv, w_br_m, w_o, ln1_g, ln1_b, w_up, b_up, w_down, b_down, ln2_g, ln2_b,
                 ln_in_g, ln_in_b):
    l = 0
    vec = lambda a: a.reshape(1, -1).astype(F32)
    wi = w_in[l]
    w_in_p = jnp.concatenate(
        [wi[:, :_C_KPE], wi[:, _C_KPE:_C_KPE + QK_ROPE], jnp.zeros((D_MODEL, ROPE_PAD - QK_ROPE), wi.dtype),
         wi[:, _C_KPE + QK_ROPE:]], axis=1).astype(BF16)
    w_uq_p = jnp.pad(w_uq[l], ((0, 0), (0, 0), (0, HEAD_PAD - QK_NOPE - QK_ROPE))).reshape(Q_LORA, N_HEADS * HEAD_PAD)
    return {
        "ln_in_g": vec(ln_in_g), "ln_in_b": vec(ln_in_b),
        "w_in": w_in_p, "b_gate": vec(b_gate[l]),
        "q_norm_g": vec(q_norm_g[l]), "w_uq": w_uq_p.astype(BF16),
        "kv_norm_g": vec(kv_norm_g[l]),
        "w_uk": w_uk[l].reshape(KV_LORA, N_HEADS * QK_NOPE).astype(BF16),
        "w_uv": w_uv[l].reshape(KV_LORA, N_HEADS * V_HEAD).astype(BF16),
        "w_ukt": jnp.transpose(w_uk[l], (1, 2, 0)).astype(BF16),
        "w_uvh": jnp.transpose(w_uv[l], (1, 0, 2)).astype(BF16),
        "conv_w": conv_w[l].astype(F32), "conv_b": vec(conv_b[l]),
        "w_ax": jnp.concatenate([rg_wa[l], rg_wx[l]], axis=-1).astype(BF16),
        "rg_ba": vec(rg_ba[l]), "rg_bx": vec(rg_bx[l]), "rg_lambda": vec(rg_lambda[l]),
        "w_br_r": w_br_r[l].astype(BF16), "w_br_m": w_br_m[l].astype(BF16), "w_o": w_o[l].astype(BF16),
        "ln1_g": vec(ln1_g[l]), "ln1_b": vec(ln1_b[l]),
        "w_up": w_up[l].astype(BF16), "b_up": vec(b_up[l]),
        "w_down": w_down[l].astype(BF16), "b_down": vec(b_down[l]),
        "ln2_g": vec(ln2_g[l]), "ln2_b": vec(ln2_b[l]),
    }


def kernel(x_prompt, x_sample, cache_ckv, cache_kpe, page_table, state_conv, state_rglru, meta_tokens, ln_in_g, ln_in_b, w_in, b_gate, conv_w, conv_b, rg_wa, rg_ba, rg_wx, rg_bx, rg_lambda, w_br_r, q_norm_g, w_uq, kv_norm_g, w_uk, w_uv, w_br_m, w_o, ln1_g, ln1_b, w_up, b_up, w_down, b_down, ln2_g, ln2_b):
    assert w_in.shape[0] == DEPTH == 1
    bn, seq, _ = x_prompt.shape
    bd, ts, _ = x_sample.shape
    past_len = page_table.shape[1] * cache_ckv.shape[2]
    p = _prep_params(w_in, b_gate, conv_w, conv_b, rg_wa, rg_ba, rg_wx, rg_bx, rg_lambda, w_br_r, q_norm_g, w_uq,
                     kv_norm_g, w_uk, w_uv, w_br_m, w_o, ln1_g, ln1_b, w_up, b_up, w_down, b_down, ln2_g, ln2_b,
                     ln_in_g, ln_in_b)
    n_s = bd * ts

    x_small = jnp.concatenate([x_sample.reshape(n_s, D_MODEL), meta_tokens.astype(F32)], axis=0)
    pos_small = jnp.concatenate([jnp.tile(past_len + jnp.arange(ts), bd), jnp.arange(N_META)])
    n_small = n_s + N_META
    rx, grg, g_small, q, k, v, ckv, kpe = _in_proj(x_small, _rope_tables(pos_small), 1, p, n_small)

    rep = lambda a: jnp.broadcast_to(a[None], (SUBLANES,) + a.shape)
    y_m, conv_m, h_m = _rglru(rep(rx[n_s:]), rep(grg[n_s:]), jnp.zeros((SUBLANES, CONV_W - 1, D_RNN), F32),
                              jnp.zeros((SUBLANES, D_RNN), F32), p, N_META)
    k_meta, v_meta = k[n_s:], v[n_s:]
    o_m = _meta_attn(q[n_s:], k_meta, v_meta)

    y_s, conv_s, h_s = _rglru(rx[:n_s].reshape(bd, ts, D_RNN), grg[:n_s].reshape(bd, ts, D_RNN),
                              state_conv[0], state_rglru[0], p, ts)
    q_lat, q_pe = _absorb(q[:n_s], p["w_ukt"], bd)
    ckv_s = ckv[:n_s].reshape(bd, ts, KV_LORA)
    kpe_s = kpe[:n_s].reshape(bd, ts, QK_ROPE)
    pad_new = lambda a: jnp.pad(a, ((0, 0), (0, 2 * SUBLANES - ts), (0, 0)))
    o_lat = _paged_attn(page_table, q_lat.reshape(bd, N_HEADS * ts, KV_LORA), q_pe.reshape(bd, N_HEADS * ts, ROPE_PAD),
                        pad_new(ckv_s), pad_new(kpe_s), cache_ckv[0], jnp.swapaxes(cache_kpe[0], 1, 2), pps=32, n_buf=2)
    o_s = _unabsorb(o_lat.reshape(bd, N_HEADS, ts, KV_LORA), p["w_uvh"])

    y_small = jnp.concatenate([y_s.reshape(n_s, D_RNN), y_m[0]], axis=0)
    o_small = jnp.concatenate([o_s, o_m], axis=0)
    out_small = _post(x_small, y_small, o_small, g_small, p, n_small)

    n_p = bn * seq
    pos_p = N_META + jnp.arange(seq)
    tm = 256
    rx_p, grg_p, g_p, q_p, k_p, v_p, ckv_p, kpe_p = _in_proj(x_prompt.reshape(n_p, D_MODEL), _rope_tables(pos_p),
                                                             seq // tm, p, tm)
    bcast = lambda a: jnp.broadcast_to(a[:1], (bn,) + a.shape[1:])
    y_p, conv_p, h_p = _rglru(rx_p.reshape(bn, seq, D_RNN), grg_p.reshape(bn, seq, D_RNN), bcast(conv_m), bcast(h_m),
                              p, 128)
    o_p = _flash(q_p.reshape(bn, seq, -1), k_p.reshape(bn, seq, -1), v_p.reshape(bn, seq, -1), k_meta, v_meta, 512)
    out_p = _post(x_prompt.reshape(n_p, D_MODEL), y_p.reshape(n_p, D_RNN), o_p.reshape(n_p, -1), g_p, p, tm)

    meta_rows = lambda a: jnp.broadcast_to(a[n_s:][None], (bn, N_META, a.shape[-1]))
    y_prompt = out_p.reshape(bn, seq, D_MODEL)
    y_sample = out_small[:n_s].reshape(bd, ts, D_MODEL)
    prompt_ckv = jnp.concatenate([meta_rows(ckv), ckv_p.reshape(bn, seq, KV_LORA)], axis=1)[None]
    prompt_kpe = jnp.concatenate([meta_rows(kpe), kpe_p.reshape(bn, seq, QK_ROPE)], axis=1)[None]
    return (y_prompt, y_sample, prompt_ckv, prompt_kpe, conv_p[None], h_p[None],
            ckv_s[None], kpe_s[None], conv_s[None], h_s[None])
```

```python
import functools
import math

import jax
import jax.numpy as jnp
from jax import lax
from jax.experimental import pallas as pl
from jax.experimental.pallas import tpu as pltpu

F32 = jnp.float32
BF16 = jnp.bfloat16

D_MODEL = 1024
N_META = 16
D_RNN = D_MODEL
RG_BLOCKS = 8
RG_BLOCK = D_RNN // RG_BLOCKS
CONV_W = 4
RG_C = 8.0
N_HEADS = 8
QK_NOPE = 128
QK_ROPE = 64
V_HEAD = 128
KV_LORA = 512
Q_LORA = 768
ROPE_THETA = 10000.0
D_FF = 4 * D_MODEL
DEPTH = 1
DN_ALPHA = (2.0 * DEPTH) ** 0.25
EPS = 1e-5
SM_SCALE = (QK_NOPE + QK_ROPE) ** -0.5
Q_PRESCALE = SM_SCALE * math.log2(math.e)

LANES = 128
SUBLANES = 8
HEAD_PAD = 2 * LANES
ROPE_PAD = LANES
_C_RX, _C_RG, _C_CQ, _C_CKV, _C_KPE, _C_GATE = 0, 1024, 2048, 2816, 3328, 3456
N_IN_PAD = _C_GATE + 2 * D_MODEL
VMEM_LIMIT = 56 * 1024 * 1024
NEG = float(jnp.finfo(jnp.float32).min)


def _const_spec(shape):
    return pl.BlockSpec(shape, lambda *_: (0,) * len(shape), pipeline_mode=pl.Buffered(1))


def _layer_norm(x, g, b):
    mu = jnp.mean(x, -1, keepdims=True)
    xc = x - mu
    var = jnp.mean(xc * xc, -1, keepdims=True)
    return xc * lax.rsqrt(var + EPS) * g + b


def _rms_norm(x, g):
    return x * lax.rsqrt(jnp.mean(x * x, -1, keepdims=True) + EPS) * g


def _gelu_tanh(x):
    return 0.5 * x * (1.0 + jnp.tanh(math.sqrt(2.0 / math.pi) * (x + 0.044715 * (x * x * x))))


def _rope_block(y, cos, sin_lo, sin_hi):
    left = pltpu.roll(y, ROPE_PAD - QK_ROPE // 2, 1)
    right = pltpu.roll(y, QK_ROPE // 2, 1)
    return y * cos + left * sin_lo + right * sin_hi


def _dot(a, b):
    return jnp.dot(a, b, preferred_element_type=F32)


def _dot_nt(a, b):
    return lax.dot_general(a, b, (((1,), (1,)), ((), ())), preferred_element_type=F32)


def _in_proj_kernel(x_ref, cos_ref, slo_ref, shi_ref, lng_ref, lnb_ref, w_in_ref, bg_ref, qg_ref, wuq_ref,
                    kvg_ref, wuk_ref, wuv_ref,
                    rx_ref, grg_ref, g_ref, q_ref, k_ref, v_ref, ckv_ref, kpe_ref, *, v_transposed):
    h = _layer_norm(x_ref[...], lng_ref[...], lnb_ref[...])
    hb = h.astype(BF16)

    def proj(lo, hi):
        return _dot(hb, w_in_ref[:, lo:hi])

    cos, slo, shi = cos_ref[...], slo_ref[...], shi_ref[...]
    rx_ref[...] = proj(_C_RX, _C_RG)
    grg_ref[...] = _gelu_tanh(proj(_C_RG, _C_CQ))
    g_ref[...] = jax.nn.sigmoid(proj(_C_GATE, N_IN_PAD) + bg_ref[...])

    cqn = _rms_norm(proj(_C_CQ, _C_CKV), qg_ref[...])
    q = _dot(cqn.astype(BF16), wuq_ref[...]) * Q_PRESCALE
    for hh in range(N_HEADS):
        c0 = hh * HEAD_PAD
        q_ref[:, c0:c0 + QK_NOPE] = q[:, c0:c0 + QK_NOPE].astype(BF16)
        q_ref[:, c0 + QK_NOPE:c0 + HEAD_PAD] = _rope_block(q[:, c0 + QK_NOPE:c0 + HEAD_PAD], cos, slo, shi).astype(BF16)

    ckv = _rms_norm(proj(_C_CKV, _C_KPE), kvg_ref[...])
    ckv_ref[...] = ckv
    kpe = _rope_block(proj(_C_KPE, _C_GATE), cos, slo, shi)
    kpe_ref[...] = kpe[:, :QK_ROPE]
    ckvb = ckv.astype(BF16)
    kpeb = kpe.astype(BF16)
    kn = _dot(ckvb, wuk_ref[...])
    if v_transposed:
        v_ref[...] = _dot_nt(wuv_ref[...], ckvb).astype(BF16)
    else:
        v_ref[...] = _dot(ckvb, wuv_ref[...]).astype(BF16)
    for hh in range(N_HEADS):
        c0 = hh * HEAD_PAD
        k_ref[:, c0:c0 + QK_NOPE] = kn[:, hh * QK_NOPE:(hh + 1) * QK_NOPE].astype(BF16)
        k_ref[:, c0 + QK_NOPE:c0 + HEAD_PAD] = kpeb


def _in_proj(x, tabs, tab_blocks, p, tm, v_transposed):
    rows = x.shape[0]
    assert rows % tm == 0
    n_tiles = rows // tm
    row = lambda w: pl.BlockSpec((tm, w), lambda i: (i, 0))
    tab = pl.BlockSpec((tm, ROPE_PAD), lambda i: (i % tab_blocks, 0))
    outs = [
        (D_RNN, F32),
        (D_RNN, F32),
        (2 * D_MODEL, F32),
        (N_HEADS * HEAD_PAD, BF16),
        (N_HEADS * HEAD_PAD, BF16),
        (N_HEADS * V_HEAD, BF16),
        (KV_LORA, F32),
        (QK_ROPE, F32),
    ]
    out_specs = [row(w) for w, _ in outs]
    out_shape = [jax.ShapeDtypeStruct((rows, w), dt) for w, dt in outs]
    if v_transposed:
        out_specs[5] = pl.BlockSpec((None, N_HEADS * V_HEAD, tm), lambda i: (i, 0, 0))
        out_shape[5] = jax.ShapeDtypeStruct((n_tiles, N_HEADS * V_HEAD, tm), BF16)
    return pl.pallas_call(
        functools.partial(_in_proj_kernel, v_transposed=v_transposed),
        grid=(n_tiles,),
        in_specs=[row(D_MODEL), tab, tab, tab,
                  _const_spec((1, D_MODEL)), _const_spec((1, D_MODEL)),
                  _const_spec((D_MODEL, N_IN_PAD)), _const_spec((1, 2 * D_MODEL)),
                  _const_spec((1, Q_LORA)), _const_spec((Q_LORA, N_HEADS * HEAD_PAD)),
                  _const_spec((1, KV_LORA)), _const_spec((KV_LORA, N_HEADS * QK_NOPE)),
                  _const_spec((N_HEADS * V_HEAD, KV_LORA) if v_transposed else (KV_LORA, N_HEADS * V_HEAD))],
        out_specs=out_specs,
        out_shape=out_shape,
        compiler_params=pltpu.CompilerParams(dimension_semantics=("arbitrary",), vmem_limit_bytes=VMEM_LIMIT),
        name="in_proj",
    )(x, *tabs, p["ln_in_g"], p["ln_in_b"], p["w_in"], p["b_gate"], p["q_norm_g"], p["w_uq"],
      p["kv_norm_g"], p["w_uk"], p["w_uv_t"] if v_transposed else p["w_uv"])


def _rglru_kernel(rx_ref, grg_ref, cprev_ref, hprev_ref, cw_ref, cb_ref, wax_ref, ba_ref, bx_ref, lam_ref,
                  y_ref, cout_ref, hout_ref, xe_ref, a_ref, u_ref, hc_ref, *, nb, tt):
    pitch = tt + SUBLANES
    n_slab = D_RNN // LANES
    tail = SUBLANES - (CONV_W - 1)

    @pl.when(pl.program_id(0) == 0)
    def _():
        xe_ref[:, tail:SUBLANES, :] = cprev_ref[...]
        hc_ref[...] = hprev_ref[...]

    xe_ref[:, SUBLANES:, :] = rx_ref[...]
    xc = cb_ref[...] + xe_ref[:, SUBLANES:, :] * cw_ref[CONV_W - 1:CONV_W, :]
    for k in range(CONV_W - 1):
        xc = xc + xe_ref[:, tail + k:tail + k + tt, :] * cw_ref[k:k + 1, :]
    new_tail = xe_ref[:, tt + tail:tt + SUBLANES, :]
    cout_ref[...] = new_tail
    xe_ref[:, tail:SUBLANES, :] = new_tail

    neg_lam = -lam_ref[...]
    softplus = jnp.maximum(neg_lam, 0.0) + jnp.log1p(jnp.exp(-jnp.abs(neg_lam)))
    xc2 = xc.reshape(nb * tt, D_RNN)
    for n in range(n_slab):
        cs = slice(n * LANES, (n + 1) * LANES)
        xn = xc2[:, cs]
        gates = _dot(xn.astype(BF16), wax_ref[n])
        r = jax.nn.sigmoid(gates[:, :RG_BLOCK] + ba_ref[:, cs])
        ig = jax.nn.sigmoid(gates[:, RG_BLOCK:] + bx_ref[:, cs])
        log_a = (-RG_C) * r * softplus[:, cs]
        a = jnp.exp(log_a)
        u = jnp.sqrt(-jnp.tanh(log_a) * (a * a + 1.0)) * (ig * xn)
        for b in range(nb):
            a_ref[n, b * pitch:b * pitch + tt, :] = a[b * tt:(b + 1) * tt]
            u_ref[n, b * pitch:b * pitch + tt, :] = u[b * tt:(b + 1) * tt]

    for grp in range(nb // SUBLANES):
        base = grp * SUBLANES * pitch
        rows = slice(grp * SUBLANES, (grp + 1) * SUBLANES)
        h0 = tuple(hc_ref[rows, n * LANES:(n + 1) * LANES] for n in range(n_slab))

        def step(t, hs, base=base):
            out = []
            for n in range(n_slab):
                idx = pl.ds(base + t, SUBLANES, stride=pitch)
                hn = a_ref[n, idx, :] * hs[n] + u_ref[n, idx, :]
                u_ref[n, idx, :] = hn
                out.append(hn)
            return tuple(out)

        hs = lax.fori_loop(0, tt, step, h0, unroll=min(tt, 8))
        for n in range(n_slab):
            hc_ref[rows, n * LANES:(n + 1) * LANES] = hs[n]
    hout_ref[...] = hc_ref[...]

    for b in range(nb):
        for n in range(n_slab):
            cs = slice(n * LANES, (n + 1) * LANES)
            y_ref[b, :, cs] = (u_ref[n, b * pitch:b * pitch + tt, :] * grg_ref[b, :, cs]).astype(BF16)


def _rglru(rx, grg, conv_prev, h_prev, p, tt):
    nb, t_len, _ = rx.shape
    assert nb % SUBLANES == 0 and t_len % tt == 0 and tt % SUBLANES == 0
    pitch = tt + SUBLANES
    seq = pl.BlockSpec((nb, tt, D_RNN), lambda i: (0, i, 0))
    return pl.pallas_call(
        functools.partial(_rglru_kernel, nb=nb, tt=tt),
        grid=(t_len // tt,),
        in_specs=[seq, seq, _const_spec((nb, CONV_W - 1, D_RNN)), _const_spec((nb, D_RNN)),
                  _const_spec((CONV_W, D_RNN)), _const_spec((1, D_RNN)),
                  _const_spec((RG_BLOCKS, RG_BLOCK, 2 * RG_BLOCK)),
                  _const_spec((1, D_RNN)), _const_spec((1, D_RNN)), _const_spec((1, D_RNN))],
        out_specs=[seq,
                   pl.BlockSpec((nb, CONV_W - 1, D_RNN), lambda i: (0, 0, 0)),
                   pl.BlockSpec((nb, D_RNN), lambda i: (0, 0))],
        out_shape=[jax.ShapeDtypeStruct((nb, t_len, D_RNN), BF16),
                   jax.ShapeDtypeStruct((nb, CONV_W - 1, D_RNN), F32),
                   jax.ShapeDtypeStruct((nb, D_RNN), F32)],
        scratch_shapes=[pltpu.VMEM((nb, tt + SUBLANES, D_RNN), F32),
                        pltpu.VMEM((D_RNN // LANES, nb * pitch, LANES), F32),
                        pltpu.VMEM((D_RNN // LANES, nb * pitch, LANES), F32),
                        pltpu.VMEM((nb, D_RNN), F32)],
        compiler_params=pltpu.CompilerParams(dimension_semantics=("arbitrary",), vmem_limit_bytes=VMEM_LIMIT),
        name="rglru",
    )(rx, grg, conv_prev, h_prev, p["conv_w"], p["conv_b"], p["w_ax"], p["rg_ba"], p["rg_bx"], p["rg_lambda"])


def _softmax_update(s, m, l, acc, v):
    m_new = jnp.maximum(m, jnp.max(s, axis=1, keepdims=True))
    alpha = jnp.exp2(m - m_new)
    pr = jnp.exp2(s - m_new)
    l = alpha * l + jnp.sum(pr, axis=1, keepdims=True)
    acc = alpha * acc + _dot(pr.astype(BF16), v)
    return m_new, l, acc


def _softmax_update_t(s_t, m, l, acc_t, v_t):
    m_new = jnp.maximum(m, jnp.max(s_t, axis=0, keepdims=True))
    alpha = jnp.exp2(m - m_new)
    pr = jnp.exp2(s_t - m_new)
    l = alpha * l + jnp.sum(pr, axis=0, keepdims=True)
    acc_t = alpha * acc_t + _dot(v_t, pr.astype(BF16))
    return m_new, l, acc_t


def _flash_kernel(q_ref, k_ref, vt_ref, km_ref, vmt_ref, o_ref, *, tq, tv, hps):
    qcols = [slice(hh * HEAD_PAD, (hh + 1) * HEAD_PAD) for hh in range(hps)]
    vrows = [slice(hh * V_HEAD, (hh + 1) * V_HEAD) for hh in range(hps)]

    def scores(j, hh):
        return _dot_nt(k_ref[j * tq:(j + 1) * tq, qcols[hh]], q_ref[:, qcols[hh]])

    def values_t(j, hh):
        return jnp.concatenate([vt_ref[j * (tq // tv) + c, vrows[hh], :] for c in range(tq // tv)], axis=1)

    def tile(n_full):
        kpos = lax.broadcasted_iota(jnp.int32, (tq, tq), 0)
        qpos = lax.broadcasted_iota(jnp.int32, (tq, tq), 1)
        state, s_next = [], []
        for hh in range(hps):
            s_t = _dot_nt(km_ref[:, qcols[hh]], q_ref[:, qcols[hh]])
            m = jnp.max(s_t, axis=0, keepdims=True)
            pr = jnp.exp2(s_t - m)
            state.append((m, jnp.sum(pr, axis=0, keepdims=True), _dot(vmt_ref[vrows[hh], :], pr.astype(BF16))))
            s_next.append(scores(0, hh))
        for j in range(n_full + 1):
            for hh in range(hps):
                s_t = s_next[hh]
                if j < n_full:
                    s_next[hh] = scores(j + 1, hh)
                else:
                    s_t = jnp.where(kpos <= qpos, s_t, NEG)
                state[hh] = _softmax_update_t(s_t, *state[hh], values_t(j, hh))
        for hh in range(hps):
            _, l, acc_t = state[hh]
            o_ref[:, vrows[hh]] = jnp.transpose(acc_t / l).astype(BF16)

    for c in range(k_ref.shape[0] // tq):
        pl.when(pl.program_id(2) == c)(functools.partial(tile, c))


def _flash(q, k, v_t, k_pre, v_pre_t, tq, hps):
    nb, t_len, _ = q.shape
    tv = v_t.shape[3]
    assert t_len % tq == 0 and tq % tv == 0 and N_HEADS % hps == 0
    return pl.pallas_call(
        functools.partial(_flash_kernel, tq=tq, tv=tv, hps=hps),
        grid=(nb, N_HEADS // hps, t_len // tq),
        in_specs=[pl.BlockSpec((None, tq, hps * HEAD_PAD), lambda b, h, i: (b, i, h)),
                  pl.BlockSpec((None, t_len, hps * HEAD_PAD), lambda b, h, i: (b, 0, h)),
                  pl.BlockSpec((None, t_len // tv, hps * V_HEAD, tv), lambda b, h, i: (b, 0, h, 0)),
                  pl.BlockSpec((N_META, hps * HEAD_PAD), lambda b, h, i: (0, h)),
                  pl.BlockSpec((hps * V_HEAD, N_META), lambda b, h, i: (h, 0))],
        out_specs=pl.BlockSpec((None, tq, hps * V_HEAD), lambda b, h, i: (b, i, h)),
        out_shape=jax.ShapeDtypeStruct((nb, t_len, N_HEADS * V_HEAD), BF16),
        compiler_params=pltpu.CompilerParams(dimension_semantics=("arbitrary",) * 3, vmem_limit_bytes=VMEM_LIMIT),
        name="flash_prompt",
    )(q, k, v_t, k_pre, v_pre_t)


def _meta_attn_kernel(q_ref, k_ref, v_ref, o_ref):
    qpos = lax.broadcasted_iota(jnp.int32, (N_META, N_META), 0)
    kpos = lax.broadcasted_iota(jnp.int32, (N_META, N_META), 1)
    for hh in range(N_HEADS):
        qs = slice(hh * HEAD_PAD, (hh + 1) * HEAD_PAD)
        vs = slice(hh * V_HEAD, (hh + 1) * V_HEAD)
        s = jnp.where(kpos <= qpos, _dot_nt(q_ref[:, qs], k_ref[:, qs]), NEG)
        pr = jnp.exp2(s - jnp.max(s, axis=1, keepdims=True))
        o = _dot(pr.astype(BF16), v_ref[:, vs]) / jnp.sum(pr, axis=1, keepdims=True)
        o_ref[:, vs] = o.astype(BF16)


def _meta_attn(q, k, v):
    return pl.pallas_call(
        _meta_attn_kernel,
        out_shape=jax.ShapeDtypeStruct((N_META, N_HEADS * V_HEAD), BF16),
        name="meta_attn",
    )(q, k, v)


def _absorb_kernel(q_ref, wukt_ref, qlat_ref, qpe_ref):
    q = q_ref[...]
    nb = qlat_ref.shape[0]
    qlat = _dot(q[:, :QK_NOPE], wukt_ref[...])
    qlat_ref[...] = qlat.reshape(nb, -1, KV_LORA)
    qpe_ref[...] = q[:, QK_NOPE:].astype(F32).reshape(nb, -1, ROPE_PAD)


def _absorb(q, w_ukt, nb):
    rows = q.shape[0]
    ts = rows // nb
    return pl.pallas_call(
        _absorb_kernel,
        grid=(N_HEADS,),
        in_specs=[pl.BlockSpec((rows, HEAD_PAD), lambda h: (0, h)),
                  pl.BlockSpec((None, QK_NOPE, KV_LORA), lambda h: (h, 0, 0))],
        out_specs=[pl.BlockSpec((nb, None, ts, KV_LORA), lambda h: (0, h, 0, 0)),
                   pl.BlockSpec((nb, None, ts, ROPE_PAD), lambda h: (0, h, 0, 0))],
        out_shape=[jax.ShapeDtypeStruct((nb, N_HEADS, ts, KV_LORA), F32),
                   jax.ShapeDtypeStruct((nb, N_HEADS, ts, ROPE_PAD), F32)],
        compiler_params=pltpu.CompilerParams(dimension_semantics=("arbitrary",)),
        name="absorb_q",
    )(q, w_ukt)


def _paged_attn_kernel(pt_ref, qlat_ref, qpe_ref, cnew_ref, knew_ref, ckv_hbm, kpe_hbm, o_ref,
                       m_ref, l_ref, acc_ref, cin_ref, kin_ref, cbuf_ref, kbuf_ref, sem, *, pps, ts, n_buf, n_b, n_j):
    b, j = pl.program_id(0), pl.program_id(1)
    g = b * n_j + j
    page = kin_ref.shape[3]
    ahead = n_buf - 1

    def chunk_copies(cb, cj, slot):
        out = []
        for i in range(pps):
            pid = pt_ref[cb, cj * pps + i]
            out.append(pltpu.make_async_copy(ckv_hbm.at[pid], cin_ref.at[slot, pl.ds(i * page, page)], sem.at[slot]))
            out.append(pltpu.make_async_copy(kpe_hbm.at[pid], kin_ref.at[slot, i], sem.at[slot]))
        return out

    @pl.when(g == 0)
    def _():
        for k in range(ahead):
            for c in chunk_copies(0, k, k):
                c.start()

    @pl.when(g + ahead < n_b * n_j)
    def _():
        jj = j + ahead
        wrap = jj >= n_j
        for c in chunk_copies(b + wrap.astype(jnp.int32), jnp.where(wrap, jj - n_j, jj), lax.rem(g + ahead, n_buf)):
            c.start()

    slot = lax.rem(g, n_buf)
    for c in chunk_copies(b, j, slot):
        c.wait()

    qlat = qlat_ref[...].astype(BF16)
    qpe = qpe_ref[...][:, :QK_ROPE].astype(BF16)

    @pl.when(j == 0)
    def _():
        cn = cnew_ref[...].astype(BF16)
        kn = knew_ref[...].astype(BF16)
        s = _dot_nt(qlat, cn) + _dot_nt(qpe, kn)
        t_q = lax.broadcasted_iota(jnp.int32, s.shape, 0) % ts
        t_k = lax.broadcasted_iota(jnp.int32, s.shape, 1)
        s = jnp.where(t_k <= t_q, s, NEG)
        m = jnp.max(s, axis=1, keepdims=True)
        pr = jnp.exp2(s - m)
        m_ref[...] = m
        l_ref[...] = jnp.sum(pr, axis=1, keepdims=True)
        acc_ref[...] = _dot(pr.astype(BF16), cn)

    for i in range(pps):
        rows = pl.ds(i * page, page)
        cbuf_ref[rows, :] = cin_ref[slot, rows, :].astype(BF16)
        kbuf_ref[:, i * page:(i + 1) * page] = kin_ref[slot, i].astype(BF16)
    s = _dot_nt(qlat, cbuf_ref[...]) + _dot(qpe, kbuf_ref[...])
    m, l, acc = _softmax_update(s, m_ref[...], l_ref[...], acc_ref[...], cbuf_ref[...])
    m_ref[...] = m
    l_ref[...] = l
    acc_ref[...] = acc

    @pl.when(j == n_j - 1)
    def _():
        o_ref[...] = acc / l


def _paged_attn(page_table, q_lat, q_pe, ckv_new, kpe_new, ckv_pool, kpe_pool_t, pps, n_buf):
    nb, n_pages = page_table.shape
    n_q = q_lat.shape[1]
    ts = n_q // N_HEADS
    page = ckv_pool.shape[1]
    n_j = n_pages // pps
    assert n_pages % pps == 0 and 2 <= n_buf <= n_j + 1
    n_new = ckv_new.shape[1]
    per_seq = lambda rows, width: pl.BlockSpec((None, rows, width), lambda b, j, pt: (b, 0, 0))

    grid_spec = pltpu.PrefetchScalarGridSpec(
        num_scalar_prefetch=1,
        grid=(nb, n_j),
        in_specs=[per_seq(n_q, KV_LORA), per_seq(n_q, ROPE_PAD), per_seq(n_new, KV_LORA), per_seq(n_new, QK_ROPE),
                  pl.BlockSpec(memory_space=pl.ANY), pl.BlockSpec(memory_space=pl.ANY)],
        out_specs=per_seq(n_q, KV_LORA),
        scratch_shapes=[pltpu.VMEM((n_q, 1), F32), pltpu.VMEM((n_q, 1), F32), pltpu.VMEM((n_q, KV_LORA), F32),
                        pltpu.VMEM((n_buf, pps * page, KV_LORA), F32), pltpu.VMEM((n_buf, pps, QK_ROPE, page), F32),
                        pltpu.VMEM((pps * page, KV_LORA), BF16), pltpu.VMEM((QK_ROPE, pps * page), BF16),
                        pltpu.SemaphoreType.DMA((n_buf,))],
    )
    return pl.pallas_call(
        functools.partial(_paged_attn_kernel, pps=pps, ts=ts, n_buf=n_buf, n_b=nb, n_j=n_j),
        grid_spec=grid_spec,
        out_shape=jax.ShapeDtypeStruct((nb, n_q, KV_LORA), F32),
        compiler_params=pltpu.CompilerParams(dimension_semantics=("arbitrary", "arbitrary"),
                                             vmem_limit_bytes=VMEM_LIMIT),
        name="paged_attn",
    )(page_table, q_lat, q_pe, ckv_new, kpe_new, ckv_pool, kpe_pool_t)


def _unabsorb_kernel(olat_ref, wuv_ref, o_ref):
    x = olat_ref[...]
    x = x.reshape(x.shape[0] * x.shape[1], KV_LORA).astype(BF16)
    o_ref[...] = _dot(x, wuv_ref[...]).astype(BF16)


def _unabsorb(o_lat, w_uvh):
    nb, _, ts, _ = o_lat.shape
    return pl.pallas_call(
        _unabsorb_kernel,
        grid=(N_HEADS,),
        in_specs=[pl.BlockSpec((nb, None, ts, KV_LORA), lambda h: (0, h, 0, 0)),
                  pl.BlockSpec((None, KV_LORA, V_HEAD), lambda h: (h, 0, 0))],
        out_specs=pl.BlockSpec((nb * ts, V_HEAD), lambda h: (0, h)),
        out_shape=jax.ShapeDtypeStruct((nb * ts, N_HEADS * V_HEAD), BF16),
        compiler_params=pltpu.CompilerParams(dimension_semantics=("arbitrary",)),
        name="unabsorb_o",
    )(o_lat, w_uvh)


def _post_kernel(x_ref, y_ref, o_ref, g_ref, lng_ref, lnb_ref, wr_ref, wm_ref, wo_ref, g1_ref, b1_ref,
                 wup_ref, bup_ref, wdn_ref, bdn_ref, g2_ref, b2_ref, out_ref, *, ff_chunk):
    h = _layer_norm(x_ref[...], lng_ref[...], lnb_ref[...])
    z_r = _dot(y_ref[...], wr_ref[...])
    z_m = _dot(o_ref[...], wm_ref[...])
    mix_in = g_ref[:, :D_MODEL] * z_r + g_ref[:, D_MODEL:] * z_m
    mix = _dot(mix_in.astype(BF16), wo_ref[...])
    x1 = _layer_norm(DN_ALPHA * h + mix, g1_ref[...], b1_ref[...])
    x1b = x1.astype(BF16)
    f = bdn_ref[...]
    for c in range(D_FF // ff_chunk):
        cs = slice(c * ff_chunk, (c + 1) * ff_chunk)
        up = jnp.maximum(_dot(x1b, wup_ref[:, cs]) + bup_ref[:, cs], 0.0)
        f = f + _dot((up * up).astype(BF16), wdn_ref[cs, :])
    out_ref[...] = _layer_norm(DN_ALPHA * x1 + f, g2_ref[...], b2_ref[...])


def _post(x, y, o, g, p, tm):
    rows = x.shape[0]
    assert rows % tm == 0
    row = lambda w: pl.BlockSpec((tm, w), lambda i: (i, 0))
    vec = lambda w: _const_spec((1, w))
    sq = _const_spec((D_MODEL, D_MODEL))
    return pl.pallas_call(
        functools.partial(_post_kernel, ff_chunk=1024),
        grid=(rows // tm,),
        in_specs=[row(D_MODEL), row(D_RNN), row(N_HEADS * V_HEAD), row(2 * D_MODEL),
                  vec(D_MODEL), vec(D_MODEL), sq, sq, sq, vec(D_MODEL), vec(D_MODEL),
                  _const_spec((D_MODEL, D_FF)), vec(D_FF), _const_spec((D_FF, D_MODEL)), vec(D_MODEL),
                  vec(D_MODEL), vec(D_MODEL)],
        out_specs=row(D_MODEL),
        out_shape=jax.ShapeDtypeStruct((rows, D_MODEL), F32),
        compiler_params=pltpu.CompilerParams(dimension_semantics=("arbitrary",), vmem_limit_bytes=VMEM_LIMIT),
        name="merge_mlp",
    )(x, y, o, g, p["ln_in_g"], p["ln_in_b"], p["w_br_r"], p["w_br_m"], p["w_o"], p["ln1_g"], p["ln1_b"],
      p["w_up"], p["b_up"], p["w_down"], p["b_down"], p["ln2_g"], p["ln2_b"])


def _rope_tables(pos):
    half = QK_ROPE // 2
    inv = 1.0 / (ROPE_THETA ** (jnp.arange(half, dtype=F32) / half))
    ang = pos.astype(F32)[:, None] * inv[None, :]
    cos, sin = jnp.cos(ang), jnp.sin(ang)
    z = jnp.zeros_like(cos)
    return (jnp.concatenate([cos, cos, z, z], 1), jnp.concatenate([-sin, z, z, z], 1),
            jnp.concatenate([z, sin, z, z], 1))


def _prep_params(w_in, b_gate, conv_w, conv_b, rg_wa, rg_ba, rg_wx, rg_bx, rg_lambda, w_br_r, q_norm_g, w_uq,
                 kv_norm_g, w_uk, w_uv, w_br_m, w_o, ln1_g, ln1_b, w_up, b_up, w_down, b_down, ln2_g, ln2_b,
                 ln_in_g, ln_in_b):
    l = 0
    vec = lambda a: a.reshape(1, -1).astype(F32)
    wi = w_in[l]
    w_in_p = jnp.concatenate(
        [wi[:, :_C_KPE], wi[:, _C_KPE:_C_KPE + QK_ROPE], jnp.zeros((D_MODEL, ROPE_PAD - QK_ROPE), wi.dtype),
         wi[:, _C_KPE + QK_ROPE:]], axis=1).astype(BF16)
    w_uq_p = jnp.pad(w_uq[l], ((0, 0), (0, 0), (0, HEAD_PAD - QK_NOPE - QK_ROPE))).reshape(Q_LORA, N_HEADS * HEAD_PAD)
    return {
        "ln_in_g": vec(ln_in_g), "ln_in_b": vec(ln_in_b),
        "w_in": w_in_p, "b_gate": vec(b_gate[l]),
        "q_norm_g": vec(q_norm_g[l]), "w_uq": w_uq_p.astype(BF16),
        "kv_norm_g": vec(kv_norm_g[l]),
        "w_uk": w_uk[l].reshape(KV_LORA, N_HEADS * QK_NOPE).astype(BF16),
        "w_uv": w_uv[l].reshape(KV_LORA, N_HEADS * V_HEAD).astype(BF16),
        "w_uv_t": w_uv[l].reshape(KV_LORA, N_HEADS * V_HEAD).T.astype(BF16),
        "w_ukt": jnp.transpose(w_uk[l], (1, 2, 0)).astype(BF16),
        "w_uvh": jnp.transpose(w_uv[l], (1, 0, 2)).astype(BF16),
        "conv_w": conv_w[l].astype(F32), "conv_b": vec(conv_b[l]),
        "w_ax": jnp.concatenate([rg_wa[l], rg_wx[l]], axis=-1).astype(BF16),
        "rg_ba": vec(rg_ba[l]), "rg_bx": vec(rg_bx[l]), "rg_lambda": vec(rg_lambda[l]),
        "w_br_r": w_br_r[l].astype(BF16), "w_br_m": w_br_m[l].astype(BF16), "w_o": w_o[l].astype(BF16),
        "ln1_g": vec(ln1_g[l]), "ln1_b": vec(ln1_b[l]),
        "w_up": w_up[l].astype(BF16), "b_up": vec(b_up[l]),
        "w_down": w_down[l].astype(BF16), "b_down": vec(b_down[l]),
        "ln2_g": vec(ln2_g[l]), "ln2_b": vec(ln2_b[l]),
    }


def kernel(x_prompt, x_sample, cache_ckv, cache_kpe, page_table, state_conv, state_rglru, meta_tokens, ln_in_g, ln_in_b, w_in, b_gate, conv_w, conv_b, rg_wa, rg_ba, rg_wx, rg_bx, rg_lambda, w_br_r, q_norm_g, w_uq, kv_norm_g, w_uk, w_uv, w_br_m, w_o, ln1_g, ln1_b, w_up, b_up, w_down, b_down, ln2_g, ln2_b):
    assert w_in.shape[0] == DEPTH == 1
    bn, seq, _ = x_prompt.shape
    bd, ts, _ = x_sample.shape
    past_len = page_table.shape[1] * cache_ckv.shape[2]
    p = _prep_params(w_in, b_gate, conv_w, conv_b, rg_wa, rg_ba, rg_wx, rg_bx, rg_lambda, w_br_r, q_norm_g, w_uq,
                     kv_norm_g, w_uk, w_uv, w_br_m, w_o, ln1_g, ln1_b, w_up, b_up, w_down, b_down, ln2_g, ln2_b,
                     ln_in_g, ln_in_b)
    n_s = bd * ts

    x_small = jnp.concatenate([x_sample.reshape(n_s, D_MODEL), meta_tokens.astype(F32)], axis=0)
    pos_small = jnp.concatenate([jnp.tile(past_len + jnp.arange(ts), bd), jnp.arange(N_META)])
    n_small = n_s + N_META
    rx, grg, g_small, q, k, v, ckv, kpe = _in_proj(x_small, _rope_tables(pos_small), 1, p, n_small, False)

    rep = lambda a: jnp.broadcast_to(a[None], (SUBLANES,) + a.shape)
    y_m, conv_m, h_m = _rglru(rep(rx[n_s:]), rep(grg[n_s:]), jnp.zeros((SUBLANES, CONV_W - 1, D_RNN), F32),
                              jnp.zeros((SUBLANES, D_RNN), F32), p, N_META)
    k_meta, v_meta = k[n_s:], v[n_s:]
    o_m = _meta_attn(q[n_s:], k_meta, v_meta)

    y_s, conv_s, h_s = _rglru(rx[:n_s].reshape(bd, ts, D_RNN), grg[:n_s].reshape(bd, ts, D_RNN),
                              state_conv[0], state_rglru[0], p, ts)
    q_lat, q_pe = _absorb(q[:n_s], p["w_ukt"], bd)
    ckv_s = ckv[:n_s].reshape(bd, ts, KV_LORA)
    kpe_s = kpe[:n_s].reshape(bd, ts, QK_ROPE)
    pad_new = lambda a: jnp.pad(a, ((0, 0), (0, 2 * SUBLANES - ts), (0, 0)))
    o_lat = _paged_attn(page_table, q_lat.reshape(bd, N_HEADS * ts, KV_LORA), q_pe.reshape(bd, N_HEADS * ts, ROPE_PAD),
                        pad_new(ckv_s), pad_new(kpe_s), cache_ckv[0], jnp.swapaxes(cache_kpe[0], 1, 2), pps=32, n_buf=2)
    o_s = _unabsorb(o_lat.reshape(bd, N_HEADS, ts, KV_LORA), p["w_uvh"])

    y_small = jnp.concatenate([y_s.reshape(n_s, D_RNN), y_m[0]], axis=0)
    o_small = jnp.concatenate([o_s, o_m], axis=0)
    out_small = _post(x_small, y_small, o_small, g_small, p, n_small)

    n_p = bn * seq
    pos_p = N_META + jnp.arange(seq)
    tm = 256
    rx_p, grg_p, g_p, q_p, k_p, vt_p, ckv_p, kpe_p = _in_proj(x_prompt.reshape(n_p, D_MODEL), _rope_tables(pos_p),
                                                              seq // tm, p, tm, True)
    bcast = lambda a: jnp.broadcast_to(a[:1], (bn,) + a.shape[1:])
    y_p, conv_p, h_p = _rglru(rx_p.reshape(bn, seq, D_RNN), grg_p.reshape(bn, seq, D_RNN), bcast(conv_m), bcast(h_m),
                              p, 128)
    o_p = _flash(q_p.reshape(bn, seq, -1), k_p.reshape(bn, seq, -1), vt_p.reshape(bn, seq // tm, -1, tm),
                 k_meta, v_meta.T, 512, 2)
    out_p = _post(x_prompt.reshape(n_p, D_MODEL), y_p.reshape(n_p, D_RNN), o_p.reshape(n_p, -1), g_p, p, tm)

    meta_rows = lambda a: jnp.broadcast_to(a[n_s:][None], (bn, N_META, a.shape[-1]))
    y_prompt = out_p.reshape(bn, seq, D_MODEL)
    y_sample = out_small[:n_s].reshape(bd, ts, D_MODEL)
    prompt_ckv = jnp.concatenate([meta_rows(ckv), ckv_p.reshape(bn, seq, KV_LORA)], axis=1)[None]
    prompt_kpe = jnp.concatenate([meta_rows(kpe), kpe_p.reshape(bn, seq, QK_ROPE)], axis=1)[None]
    return (y_prompt, y_sample, prompt_ckv, prompt_kpe, conv_p[None], h_p[None],
            ckv_s[None], kpe_s[None], conv_s[None], h_s[None])
```

```python
import functools
import math

import jax
import jax.numpy as jnp
from jax import lax
from jax.experimental import pallas as pl
from jax.experimental.pallas import tpu as pltpu

F32 = jnp.float32
BF16 = jnp.bfloat16

D_MODEL = 1024
N_META = 16
D_RNN = D_MODEL
RG_BLOCKS = 8
RG_BLOCK = D_RNN // RG_BLOCKS
CONV_W = 4
RG_C = 8.0
N_HEADS = 8
QK_NOPE = 128
QK_ROPE = 64
V_HEAD = 128
KV_LORA = 512
Q_LORA = 768
ROPE_THETA = 10000.0
D_FF = 4 * D_MODEL
DEPTH = 1
DN_ALPHA = (2.0 * DEPTH) ** 0.25
EPS = 1e-5
SM_SCALE = (QK_NOPE + QK_ROPE) ** -0.5
Q_PRESCALE = SM_SCALE * math.log2(math.e)

LANES = 128
SUBLANES = 8
HEAD_PAD = 2 * LANES
ROPE_PAD = LANES
_C_RX, _C_RG, _C_CQ, _C_CKV, _C_KPE, _C_GATE = 0, 1024, 2048, 2816, 3328, 3456
N_IN_PAD = _C_GATE + 2 * D_MODEL
VMEM_LIMIT = 56 * 1024 * 1024
NEG = float(jnp.finfo(jnp.float32).min)


def _const_spec(shape):
    return pl.BlockSpec(shape, lambda *_: (0,) * len(shape), pipeline_mode=pl.Buffered(1))


def _layer_norm(x, g, b):
    mu = jnp.mean(x, -1, keepdims=True)
    xc = x - mu
    var = jnp.mean(xc * xc, -1, keepdims=True)
    return xc * lax.rsqrt(var + EPS) * g + b


def _rms_norm(x, g):
    return x * lax.rsqrt(jnp.mean(x * x, -1, keepdims=True) + EPS) * g


def _gelu_tanh(x):
    return 0.5 * x * (1.0 + jnp.tanh(math.sqrt(2.0 / math.pi) * (x + 0.044715 * (x * x * x))))


def _rope_block(y, cos, sin_lo, sin_hi):
    left = pltpu.roll(y, ROPE_PAD - QK_ROPE // 2, 1)
    right = pltpu.roll(y, QK_ROPE // 2, 1)
    return y * cos + left * sin_lo + right * sin_hi


def _dot(a, b):
    return jnp.dot(a, b, preferred_element_type=F32)


def _dot_nt(a, b):
    return lax.dot_general(a, b, (((1,), (1,)), ((), ())), preferred_element_type=F32)


def _in_proj_kernel(x_ref, cos_ref, slo_ref, shi_ref, lng_ref, lnb_ref, w_in_ref, bg_ref, qg_ref, wuq_ref,
                    kvg_ref, wuk_ref, wuv_ref,
                    rx_ref, grg_ref, g_ref, q_ref, k_ref, v_ref, ckv_ref, kpe_ref, *, v_transposed):
    h = _layer_norm(x_ref[...], lng_ref[...], lnb_ref[...])
    hb = h.astype(BF16)

    def proj(lo, hi):
        return _dot(hb, w_in_ref[:, lo:hi])

    cos, slo, shi = cos_ref[...], slo_ref[...], shi_ref[...]
    rx_ref[...] = proj(_C_RX, _C_RG)
    grg_ref[...] = _gelu_tanh(proj(_C_RG, _C_CQ))
    g_ref[...] = jax.nn.sigmoid(proj(_C_GATE, N_IN_PAD) + bg_ref[...])

    cqn = _rms_norm(proj(_C_CQ, _C_CKV), qg_ref[...])
    q = _dot(cqn.astype(BF16), wuq_ref[...]) * Q_PRESCALE
    for hh in range(N_HEADS):
        c0 = hh * HEAD_PAD
        q_ref[:, c0:c0 + QK_NOPE] = q[:, c0:c0 + QK_NOPE].astype(BF16)
        q_ref[:, c0 + QK_NOPE:c0 + HEAD_PAD] = _rope_block(q[:, c0 + QK_NOPE:c0 + HEAD_PAD], cos, slo, shi).astype(BF16)

    ckv = _rms_norm(proj(_C_CKV, _C_KPE), kvg_ref[...])
    ckv_ref[...] = ckv
    kpe = _rope_block(proj(_C_KPE, _C_GATE), cos, slo, shi)
    kpe_ref[...] = kpe[:, :QK_ROPE]
    ckvb = ckv.astype(BF16)
    kpeb = kpe.astype(BF16)
    kn = _dot(ckvb, wuk_ref[...])
    if v_transposed:
        v_ref[...] = _dot_nt(wuv_ref[...], ckvb).astype(BF16)
    else:
        v_ref[...] = _dot(ckvb, wuv_ref[...]).astype(BF16)
    for hh in range(N_HEADS):
        c0 = hh * HEAD_PAD
        k_ref[:, c0:c0 + QK_NOPE] = kn[:, hh * QK_NOPE:(hh + 1) * QK_NOPE].astype(BF16)
        k_ref[:, c0 + QK_NOPE:c0 + HEAD_PAD] = kpeb


def _in_proj(x, tabs, tab_blocks, p, tm, v_transposed):
    rows = x.shape[0]
    assert rows % tm == 0
    n_tiles = rows // tm
    row = lambda w: pl.BlockSpec((tm, w), lambda i: (i, 0))
    tab = pl.BlockSpec((tm, ROPE_PAD), lambda i: (i % tab_blocks, 0))
    outs = [
        (D_RNN, F32),
        (D_RNN, F32),
        (2 * D_MODEL, F32),
        (N_HEADS * HEAD_PAD, BF16),
        (N_HEADS * HEAD_PAD, BF16),
        (N_HEADS * V_HEAD, BF16),
        (KV_LORA, F32),
        (QK_ROPE, F32),
    ]
    out_specs = [row(w) for w, _ in outs]
    out_shape = [jax.ShapeDtypeStruct((rows, w), dt) for w, dt in outs]
    if v_transposed:
        out_specs[5] = pl.BlockSpec((None, N_HEADS * V_HEAD, tm), lambda i: (i, 0, 0))
        out_shape[5] = jax.ShapeDtypeStruct((n_tiles, N_HEADS * V_HEAD, tm), BF16)
    return pl.pallas_call(
        functools.partial(_in_proj_kernel, v_transposed=v_transposed),
        grid=(n_tiles,),
        in_specs=[row(D_MODEL), tab, tab, tab,
                  _const_spec((1, D_MODEL)), _const_spec((1, D_MODEL)),
                  _const_spec((D_MODEL, N_IN_PAD)), _const_spec((1, 2 * D_MODEL)),
                  _const_spec((1, Q_LORA)), _const_spec((Q_LORA, N_HEADS * HEAD_PAD)),
                  _const_spec((1, KV_LORA)), _const_spec((KV_LORA, N_HEADS * QK_NOPE)),
                  _const_spec((N_HEADS * V_HEAD, KV_LORA) if v_transposed else (KV_LORA, N_HEADS * V_HEAD))],
        out_specs=out_specs,
        out_shape=out_shape,
        compiler_params=pltpu.CompilerParams(dimension_semantics=("arbitrary",), vmem_limit_bytes=VMEM_LIMIT),
        name="in_proj",
    )(x, *tabs, p["ln_in_g"], p["ln_in_b"], p["w_in"], p["b_gate"], p["q_norm_g"], p["w_uq"],
      p["kv_norm_g"], p["w_uk"], p["w_uv_t"] if v_transposed else p["w_uv"])


def _rglru_kernel(rx_ref, grg_ref, cprev_ref, hprev_ref, cw_ref, cb_ref, wax_ref, ba_ref, bx_ref, lam_ref,
                  y_ref, cout_ref, hout_ref, xe_ref, a_ref, u_ref, hc_ref, *, nb, tt):
    pitch = tt + SUBLANES
    n_slab = D_RNN // LANES
    tail = SUBLANES - (CONV_W - 1)

    @pl.when(pl.program_id(0) == 0)
    def _():
        xe_ref[:, tail:SUBLANES, :] = cprev_ref[...]
        hc_ref[...] = hprev_ref[...]

    xe_ref[:, SUBLANES:, :] = rx_ref[...]
    xc = cb_ref[...] + xe_ref[:, SUBLANES:, :] * cw_ref[CONV_W - 1:CONV_W, :]
    for k in range(CONV_W - 1):
        xc = xc + xe_ref[:, tail + k:tail + k + tt, :] * cw_ref[k:k + 1, :]
    new_tail = xe_ref[:, tt + tail:tt + SUBLANES, :]
    cout_ref[...] = new_tail
    xe_ref[:, tail:SUBLANES, :] = new_tail

    neg_lam = -lam_ref[...]
    softplus = jnp.maximum(neg_lam, 0.0) + jnp.log1p(jnp.exp(-jnp.abs(neg_lam)))
    xc2 = xc.reshape(nb * tt, D_RNN)
    for n in range(n_slab):
        cs = slice(n * LANES, (n + 1) * LANES)
        xn = xc2[:, cs]
        gates = _dot(xn.astype(BF16), wax_ref[n])
        r = jax.nn.sigmoid(gates[:, :RG_BLOCK] + ba_ref[:, cs])
        ig = jax.nn.sigmoid(gates[:, RG_BLOCK:] + bx_ref[:, cs])
        log_a = (-RG_C) * r * softplus[:, cs]
        a = jnp.exp(log_a)
        u = jnp.sqrt(-jnp.tanh(log_a) * (a * a + 1.0)) * (ig * xn)
        for b in range(nb):
            a_ref[n, b * pitch:b * pitch + tt, :] = a[b * tt:(b + 1) * tt]
            u_ref[n, b * pitch:b * pitch + tt, :] = u[b * tt:(b + 1) * tt]

    for grp in range(nb // SUBLANES):
        base = grp * SUBLANES * pitch
        rows = slice(grp * SUBLANES, (grp + 1) * SUBLANES)
        h0 = tuple(hc_ref[rows, n * LANES:(n + 1) * LANES] for n in range(n_slab))

        def step(t, hs, base=base):
            out = []
            for n in range(n_slab):
                idx = pl.ds(base + t, SUBLANES, stride=pitch)
                hn = a_ref[n, idx, :] * hs[n] + u_ref[n, idx, :]
                u_ref[n, idx, :] = hn
                out.append(hn)
            return tuple(out)

        hs = lax.fori_loop(0, tt, step, h0, unroll=min(tt, 8))
        for n in range(n_slab):
            hc_ref[rows, n * LANES:(n + 1) * LANES] = hs[n]
    hout_ref[...] = hc_ref[...]

    for b in range(nb):
        for n in range(n_slab):
            cs = slice(n * LANES, (n + 1) * LANES)
            y_ref[b, :, cs] = (u_ref[n, b * pitch:b * pitch + tt, :] * grg_ref[b, :, cs]).astype(BF16)


def _rglru(rx, grg, conv_prev, h_prev, p, tt):
    nb, t_len, _ = rx.shape
    assert nb % SUBLANES == 0 and t_len % tt == 0 and tt % SUBLANES == 0
    pitch = tt + SUBLANES
    seq = pl.BlockSpec((nb, tt, D_RNN), lambda i: (0, i, 0))
    return pl.pallas_call(
        functools.partial(_rglru_kernel, nb=nb, tt=tt),
        grid=(t_len // tt,),
        in_specs=[seq, seq, _const_spec((nb, CONV_W - 1, D_RNN)), _const_spec((nb, D_RNN)),
                  _const_spec((CONV_W, D_RNN)), _const_spec((1, D_RNN)),
                  _const_spec((RG_BLOCKS, RG_BLOCK, 2 * RG_BLOCK)),
                  _const_spec((1, D_RNN)), _const_spec((1, D_RNN)), _const_spec((1, D_RNN))],
        out_specs=[seq,
                   pl.BlockSpec((nb, CONV_W - 1, D_RNN), lambda i: (0, 0, 0)),
                   pl.BlockSpec((nb, D_RNN), lambda i: (0, 0))],
        out_shape=[jax.ShapeDtypeStruct((nb, t_len, D_RNN), BF16),
                   jax.ShapeDtypeStruct((nb, CONV_W - 1, D_RNN), F32),
                   jax.ShapeDtypeStruct((nb, D_RNN), F32)],
        scratch_shapes=[pltpu.VMEM((nb, tt + SUBLANES, D_RNN), F32),
                        pltpu.VMEM((D_RNN // LANES, nb * pitch, LANES), F32),
                        pltpu.VMEM((D_RNN // LANES, nb * pitch, LANES), F32),
                        pltpu.VMEM((nb, D_RNN), F32)],
        compiler_params=pltpu.CompilerParams(dimension_semantics=("arbitrary",), vmem_limit_bytes=VMEM_LIMIT),
        name="rglru",
    )(rx, grg, conv_prev, h_prev, p["conv_w"], p["conv_b"], p["w_ax"], p["rg_ba"], p["rg_bx"], p["rg_lambda"])


def _softmax_update(s, m, l, acc, v):
    m_new = jnp.maximum(m, jnp.max(s, axis=1, keepdims=True))
    alpha = jnp.exp2(m - m_new)
    pr = jnp.exp2(s - m_new)
    l = alpha * l + jnp.sum(pr, axis=1, keepdims=True)
    acc = alpha * acc + _dot(pr.astype(BF16), v)
    return m_new, l, acc


def _softmax_update_t(s_t, m, l, acc_t, v_t):
    m_new = jnp.maximum(m, jnp.max(s_t, axis=0, keepdims=True))
    alpha = jnp.exp2(m - m_new)
    pr = jnp.exp2(s_t - m_new)
    l = alpha * l + jnp.sum(pr, axis=0, keepdims=True)
    acc_t = alpha * acc_t + _dot(v_t, pr.astype(BF16))
    return m_new, l, acc_t


def _flash_kernel(q_ref, k_ref, vt_ref, km_ref, vmt_ref, o_ref, *, tq, tv, hps):
    qcols = [slice(hh * HEAD_PAD, (hh + 1) * HEAD_PAD) for hh in range(hps)]
    vrows = [slice(hh * V_HEAD, (hh + 1) * V_HEAD) for hh in range(hps)]

    def scores(j, hh):
        return _dot_nt(k_ref[j * tq:(j + 1) * tq, qcols[hh]], q_ref[:, qcols[hh]])

    def values_t(j, hh):
        return jnp.concatenate([vt_ref[j * (tq // tv) + c, vrows[hh], :] for c in range(tq // tv)], axis=1)

    def tile(n_full):
        kpos = lax.broadcasted_iota(jnp.int32, (tq, tq), 0)
        qpos = lax.broadcasted_iota(jnp.int32, (tq, tq), 1)
        state, s_next = [], []
        for hh in range(hps):
            s_t = _dot_nt(km_ref[:, qcols[hh]], q_ref[:, qcols[hh]])
            m = jnp.max(s_t, axis=0, keepdims=True)
            pr = jnp.exp2(s_t - m)
            state.append((m, jnp.sum(pr, axis=0, keepdims=True), _dot(vmt_ref[vrows[hh], :], pr.astype(BF16))))
            s_next.append(scores(0, hh))
        for j in range(n_full + 1):
            for hh in range(hps):
                s_t = s_next[hh]
                if j < n_full:
                    s_next[hh] = scores(j + 1, hh)
                else:
                    s_t = jnp.where(kpos <= qpos, s_t, NEG)
                state[hh] = _softmax_update_t(s_t, *state[hh], values_t(j, hh))
        for hh in range(hps):
            _, l, acc_t = state[hh]
            o_ref[:, vrows[hh]] = jnp.transpose(acc_t / l).astype(BF16)

    for c in range(k_ref.shape[0] // tq):
        pl.when(pl.program_id(2) == c)(functools.partial(tile, c))


def _flash(q, k, v_t, k_pre, v_pre_t, tq, hps):
    nb, t_len, _ = q.shape
    tv = v_t.shape[3]
    assert t_len % tq == 0 and tq % tv == 0 and N_HEADS % hps == 0
    return pl.pallas_call(
        functools.partial(_flash_kernel, tq=tq, tv=tv, hps=hps),
        grid=(nb, N_HEADS // hps, t_len // tq),
        in_specs=[pl.BlockSpec((None, tq, hps * HEAD_PAD), lambda b, h, i: (b, i, h)),
                  pl.BlockSpec((None, t_len, hps * HEAD_PAD), lambda b, h, i: (b, 0, h)),
                  pl.BlockSpec((None, t_len // tv, hps * V_HEAD, tv), lambda b, h, i: (b, 0, h, 0)),
                  pl.BlockSpec((N_META, hps * HEAD_PAD), lambda b, h, i: (0, h)),
                  pl.BlockSpec((hps * V_HEAD, N_META), lambda b, h, i: (h, 0))],
        out_specs=pl.BlockSpec((None, tq, hps * V_HEAD), lambda b, h, i: (b, i, h)),
        out_shape=jax.ShapeDtypeStruct((nb, t_len, N_HEADS * V_HEAD), BF16),
        compiler_params=pltpu.CompilerParams(dimension_semantics=("arbitrary",) * 3, vmem_limit_bytes=VMEM_LIMIT),
        name="flash_prompt",
    )(q, k, v_t, k_pre, v_pre_t)


def _meta_attn_kernel(q_ref, k_ref, v_ref, o_ref):
    qpos = lax.broadcasted_iota(jnp.int32, (N_META, N_META), 0)
    kpos = lax.broadcasted_iota(jnp.int32, (N_META, N_META), 1)
    for hh in range(N_HEADS):
        qs = slice(hh * HEAD_PAD, (hh + 1) * HEAD_PAD)
        vs = slice(hh * V_HEAD, (hh + 1) * V_HEAD)
        s = jnp.where(kpos <= qpos, _dot_nt(q_ref[:, qs], k_ref[:, qs]), NEG)
        pr = jnp.exp2(s - jnp.max(s, axis=1, keepdims=True))
        o = _dot(pr.astype(BF16), v_ref[:, vs]) / jnp.sum(pr, axis=1, keepdims=True)
        o_ref[:, vs] = o.astype(BF16)


def _meta_attn(q, k, v):
    return pl.pallas_call(
        _meta_attn_kernel,
        out_shape=jax.ShapeDtypeStruct((N_META, N_HEADS * V_HEAD), BF16),
        name="meta_attn",
    )(q, k, v)


def _absorb_kernel(q_ref, wukt_ref, qlat_ref, qpe_ref):
    q = q_ref[...]
    nb = qlat_ref.shape[0]
    qlat = _dot(q[:, :QK_NOPE], wukt_ref[...])
    qlat_ref[...] = qlat.reshape(nb, -1, KV_LORA)
    qpe_ref[...] = q[:, QK_NOPE:].astype(F32).reshape(nb, -1, ROPE_PAD)


def _absorb(q, w_ukt, nb):
    rows = q.shape[0]
    ts = rows // nb
    return pl.pallas_call(
        _absorb_kernel,
        grid=(N_HEADS,),
        in_specs=[pl.BlockSpec((rows, HEAD_PAD), lambda h: (0, h)),
                  pl.BlockSpec((None, QK_NOPE, KV_LORA), lambda h: (h, 0, 0))],
        out_specs=[pl.BlockSpec((nb, None, ts, KV_LORA), lambda h: (0, h, 0, 0)),
                   pl.BlockSpec((nb, None, ts, ROPE_PAD), lambda h: (0, h, 0, 0))],
        out_shape=[jax.ShapeDtypeStruct((nb, N_HEADS, ts, KV_LORA), F32),
                   jax.ShapeDtypeStruct((nb, N_HEADS, ts, ROPE_PAD), F32)],
        compiler_params=pltpu.CompilerParams(dimension_semantics=("arbitrary",)),
        name="absorb_q",
    )(q, w_ukt)


PAGED_BUFS = 2


def _paged_scratch(n_q, pps, page):
    return [pltpu.VMEM((n_q, 1), F32), pltpu.VMEM((n_q, 1), F32), pltpu.VMEM((n_q, KV_LORA), F32),
            pltpu.VMEM((PAGED_BUFS, pps * page, KV_LORA), F32), pltpu.VMEM((PAGED_BUFS, pps, QK_ROPE, page), F32),
            pltpu.VMEM((pps * page, KV_LORA), BF16), pltpu.VMEM((QK_ROPE, pps * page), BF16),
            pltpu.SemaphoreType.DMA((PAGED_BUFS,))]


def _paged_copies(pt_ref, ckv_hbm, kpe_hbm, cin_ref, kin_ref, sem, seq, chunk, slot, pps):
    page = kin_ref.shape[3]
    out = []
    for i in range(pps):
        pid = pt_ref[seq, chunk * pps + i]
        out.append(pltpu.make_async_copy(ckv_hbm.at[pid], cin_ref.at[slot, pl.ds(i * page, page)], sem.at[slot]))
        out.append(pltpu.make_async_copy(kpe_hbm.at[pid], kin_ref.at[slot, i], sem.at[slot]))
    return out


def _paged_new_tokens(qlat, qpe, cnew_ref, knew_ref, m_ref, l_ref, acc_ref, ts):
    cn = cnew_ref[...].astype(BF16)
    kn = knew_ref[...].astype(BF16)
    s = _dot_nt(qlat, cn) + _dot_nt(qpe, kn)
    t_q = lax.broadcasted_iota(jnp.int32, s.shape, 0) % ts
    t_k = lax.broadcasted_iota(jnp.int32, s.shape, 1)
    s = jnp.where(t_k <= t_q, s, NEG)
    m = jnp.max(s, axis=1, keepdims=True)
    pr = jnp.exp2(s - m)
    m_ref[...] = m
    l_ref[...] = jnp.sum(pr, axis=1, keepdims=True)
    acc_ref[...] = _dot(pr.astype(BF16), cn)


def _paged_scores(qlat, qpe, cin_ref, kin_ref, cbuf_ref, kbuf_ref, slot, pps):
    page = kin_ref.shape[3]
    for i in range(pps):
        rows = pl.ds(i * page, page)
        cbuf_ref[rows, :] = cin_ref[slot, rows, :].astype(BF16)
        kbuf_ref[:, i * page:(i + 1) * page] = kin_ref[slot, i].astype(BF16)
    return _dot_nt(qlat, cbuf_ref[...]) + _dot(qpe, kbuf_ref[...])


def _paged_probs(s, m_ref, l_ref):
    m = m_ref[...]
    m_new = jnp.maximum(m, jnp.max(s, axis=1, keepdims=True))
    alpha = jnp.exp2(m - m_new)
    pr = jnp.exp2(s - m_new)
    m_ref[...] = m_new
    l_ref[...] = alpha * l_ref[...] + jnp.sum(pr, axis=1, keepdims=True)
    return alpha, pr.astype(BF16)


def _paged_accumulate(alpha, pr, cbuf_ref, acc_ref):
    acc_ref[...] = alpha * acc_ref[...] + _dot(pr, cbuf_ref[...])


def _unabsorb_kernel(olat_ref, wuv_ref, o_ref):
    x = olat_ref[...]
    x = x.reshape(x.shape[0] * x.shape[1], KV_LORA).astype(BF16)
    o_ref[...] = _dot(x, wuv_ref[...]).astype(BF16)


def _unabsorb(o_lat, w_uvh):
    nb, _, ts, _ = o_lat.shape
    return pl.pallas_call(
        _unabsorb_kernel,
        grid=(N_HEADS,),
        in_specs=[pl.BlockSpec((nb, None, ts, KV_LORA), lambda h: (0, h, 0, 0)),
                  pl.BlockSpec((None, KV_LORA, V_HEAD), lambda h: (h, 0, 0))],
        out_specs=pl.BlockSpec((nb * ts, V_HEAD), lambda h: (0, h)),
        out_shape=jax.ShapeDtypeStruct((nb * ts, N_HEADS * V_HEAD), BF16),
        compiler_params=pltpu.CompilerParams(dimension_semantics=("arbitrary",)),
        name="unabsorb_o",
    )(o_lat, w_uvh)


FF_CHUNKS = 4
N_POST_WEIGHTS = 13


def _post_compute(x_ref, y_ref, o_ref, g_ref, weights, out_ref, side_work=None):
    before_up, before_down, after_down = side_work if side_work is not None else (lambda c: None,) * 3
    lng_ref, lnb_ref, wr_ref, wm_ref, wo_ref, g1_ref, b1_ref, wup_ref, bup_ref, wdn_ref, bdn_ref, g2_ref, b2_ref = weights
    h = _layer_norm(x_ref[...], lng_ref[...], lnb_ref[...])
    z_r = _dot(y_ref[...], wr_ref[...])
    z_m = _dot(o_ref[...], wm_ref[...])
    mix_in = g_ref[:, :D_MODEL] * z_r + g_ref[:, D_MODEL:] * z_m
    mix = _dot(mix_in.astype(BF16), wo_ref[...])
    x1 = _layer_norm(DN_ALPHA * h + mix, g1_ref[...], b1_ref[...])
    x1b = x1.astype(BF16)
    f = bdn_ref[...]
    ff_chunk = D_FF // FF_CHUNKS
    for c in range(FF_CHUNKS):
        cs = slice(c * ff_chunk, (c + 1) * ff_chunk)
        before_up(c)
        up = jnp.maximum(_dot(x1b, wup_ref[:, cs]) + bup_ref[:, cs], 0.0)
        before_down(c)
        f = f + _dot((up * up).astype(BF16), wdn_ref[cs, :])
        after_down(c)
    out_ref[...] = _layer_norm(DN_ALPHA * x1 + f, g2_ref[...], b2_ref[...])


def _post_kernel(x_ref, y_ref, o_ref, g_ref, *rest):
    _post_compute(x_ref, y_ref, o_ref, g_ref, rest[:N_POST_WEIGHTS], rest[N_POST_WEIGHTS])


def _post_paged_kernel(pt_ref, x_ref, y_ref, o_ref, g_ref, *rest, pps, ts, n_j, n_steps):
    weights = rest[:N_POST_WEIGHTS]
    (qlat_ref, qpe_ref, cnew_ref, knew_ref, ckv_hbm, kpe_hbm, out_ref, olat_ref,
     m_ref, l_ref, acc_ref, cin_ref, kin_ref, cbuf_ref, kbuf_ref, sem) = rest[N_POST_WEIGHTS:]
    s = pl.program_id(0)
    spb = n_j // FF_CHUNKS
    seq, j0 = lax.div(s, spb), lax.rem(s, spb) * FF_CHUNKS
    copies = functools.partial(_paged_copies, pt_ref, ckv_hbm, kpe_hbm, cin_ref, kin_ref, sem, pps=pps)

    @pl.when(s == 0)
    def _():
        for cp in copies(0, 0, 0):
            cp.start()

    live = {}

    def gather_and_score(c):
        slot, nxt = c % PAGED_BUFS, (c + 1) % PAGED_BUFS
        if c + 1 < FF_CHUNKS:
            for cp in copies(seq, j0 + c + 1, nxt):
                cp.start()
        else:
            @pl.when(s + 1 < n_steps)
            def _():
                for cp in copies(lax.div(s + 1, spb), lax.rem(s + 1, spb) * FF_CHUNKS, nxt):
                    cp.start()
        for cp in copies(seq, j0 + c, slot):
            cp.wait()
        qlat = qlat_ref[...].astype(BF16)
        qpe = qpe_ref[...][:, :QK_ROPE].astype(BF16)
        if c == 0:
            pl.when(j0 == 0)(functools.partial(_paged_new_tokens, qlat, qpe, cnew_ref, knew_ref,
                                               m_ref, l_ref, acc_ref, ts))
        live["s"] = _paged_scores(qlat, qpe, cin_ref, kin_ref, cbuf_ref, kbuf_ref, slot, pps)

    def probs(c):
        live["alpha"], live["pr"] = _paged_probs(live.pop("s"), m_ref, l_ref)

    def accumulate(c):
        _paged_accumulate(live.pop("alpha"), live.pop("pr"), cbuf_ref, acc_ref)
        if c == FF_CHUNKS - 1:
            @pl.when(j0 == n_j - FF_CHUNKS)
            def _():
                olat_ref[...] = acc_ref[...] / l_ref[...]

    _post_compute(x_ref, y_ref, o_ref, g_ref, weights, out_ref, side_work=(gather_and_score, probs, accumulate))


def _post_weight_specs():
    vec = lambda w: _const_spec((1, w))
    sq = _const_spec((D_MODEL, D_MODEL))
    return [vec(D_MODEL), vec(D_MODEL), sq, sq, sq, vec(D_MODEL), vec(D_MODEL),
            _const_spec((D_MODEL, D_FF)), vec(D_FF), _const_spec((D_FF, D_MODEL)), vec(D_MODEL),
            vec(D_MODEL), vec(D_MODEL)]


def _post_weights(p):
    return (p["ln_in_g"], p["ln_in_b"], p["w_br_r"], p["w_br_m"], p["w_o"], p["ln1_g"], p["ln1_b"],
            p["w_up"], p["b_up"], p["w_down"], p["b_down"], p["ln2_g"], p["ln2_b"])


def _post(x, y, o, g, p, tm):
    rows = x.shape[0]
    assert rows % tm == 0
    row = lambda w: pl.BlockSpec((tm, w), lambda i: (i, 0))
    return pl.pallas_call(
        _post_kernel,
        grid=(rows // tm,),
        in_specs=[row(D_MODEL), row(D_RNN), row(N_HEADS * V_HEAD), row(2 * D_MODEL)] + _post_weight_specs(),
        out_specs=row(D_MODEL),
        out_shape=jax.ShapeDtypeStruct((rows, D_MODEL), F32),
        compiler_params=pltpu.CompilerParams(dimension_semantics=("arbitrary",), vmem_limit_bytes=VMEM_LIMIT),
        name="merge_mlp",
    )(x, y, o, g, *_post_weights(p))


def _post_with_paged_attn(x, y, o, g, p, tm, page_table, q_lat, q_pe, ckv_new, kpe_new, ckv_pool, kpe_pool_t):
    rows = x.shape[0]
    n_steps = rows // tm
    nb, n_pages = page_table.shape
    n_q = q_lat.shape[1]
    page = ckv_pool.shape[1]
    n_new = ckv_new.shape[1]
    chunks = n_steps * FF_CHUNKS
    assert rows % tm == 0 and (nb * n_pages) % chunks == 0 and FF_CHUNKS % PAGED_BUFS == 0
    pps = nb * n_pages // chunks
    n_j = n_pages // pps
    assert n_pages % pps == 0 and n_j % FF_CHUNKS == 0
    spb = n_j // FF_CHUNKS
    row = lambda w: pl.BlockSpec((tm, w), lambda i, pt: (i, 0))
    per_seq = lambda r, w: pl.BlockSpec((None, r, w), lambda i, pt: (i // spb, 0, 0))
    grid_spec = pltpu.PrefetchScalarGridSpec(
        num_scalar_prefetch=1,
        grid=(n_steps,),
        in_specs=[row(D_MODEL), row(D_RNN), row(N_HEADS * V_HEAD), row(2 * D_MODEL)] + _post_weight_specs()
                 + [per_seq(n_q, KV_LORA), per_seq(n_q, ROPE_PAD), per_seq(n_new, KV_LORA), per_seq(n_new, QK_ROPE),
                    pl.BlockSpec(memory_space=pl.ANY), pl.BlockSpec(memory_space=pl.ANY)],
        out_specs=[row(D_MODEL), per_seq(n_q, KV_LORA)],
        scratch_shapes=_paged_scratch(n_q, pps, page),
    )
    return pl.pallas_call(
        functools.partial(_post_paged_kernel, pps=pps, ts=n_q // N_HEADS, n_j=n_j, n_steps=n_steps),
        grid_spec=grid_spec,
        out_shape=[jax.ShapeDtypeStruct((rows, D_MODEL), F32), jax.ShapeDtypeStruct((nb, n_q, KV_LORA), F32)],
        compiler_params=pltpu.CompilerParams(dimension_semantics=("arbitrary",), vmem_limit_bytes=VMEM_LIMIT),
        name="merge_mlp_paged_attn",
    )(page_table, x, y, o, g, *_post_weights(p), q_lat, q_pe, ckv_new, kpe_new, ckv_pool, kpe_pool_t)


def _rope_tables(pos):
    half = QK_ROPE // 2
    inv = 1.0 / (ROPE_THETA ** (jnp.arange(half, dtype=F32) / half))
    ang = pos.astype(F32)[:, None] * inv[None, :]
    cos, sin = jnp.cos(ang), jnp.sin(ang)
    z = jnp.zeros_like(cos)
    return (jnp.concatenate([cos, cos, z, z], 1), jnp.concatenate([-sin, z, z, z], 1),
            jnp.concatenate([z, sin, z, z], 1))


def _prep_params(w_in, b_gate, conv_w, conv_b, rg_wa, rg_ba, rg_wx, rg_bx, rg_lambda, w_br_r, q_norm_g, w_uq,
                 kv_norm_g, w_uk, w_uv, w_br_m, w_o, ln1_g, ln1_b, w_up, b_up, w_down, b_down, ln2_g, ln2_b,
                 ln_in_g, ln_in_b):
    l = 0
    vec = lambda a: a.reshape(1, -1).astype(F32)
    wi = w_in[l]
    w_in_p = jnp.concatenate(
        [wi[:, :_C_KPE], wi[:, _C_KPE:_C_KPE + QK_ROPE], jnp.zeros((D_MODEL, ROPE_PAD - QK_ROPE), wi.dtype),
         wi[:, _C_KPE + QK_ROPE:]], axis=1).astype(BF16)
    w_uq_p = jnp.pad(w_uq[l], ((0, 0), (0, 0), (0, HEAD_PAD - QK_NOPE - QK_ROPE))).reshape(Q_LORA, N_HEADS * HEAD_PAD)
    return {
        "ln_in_g": vec(ln_in_g), "ln_in_b": vec(ln_in_b),
        "w_in": w_in_p, "b_gate": vec(b_gate[l]),
        "q_norm_g": vec(q_norm_g[l]), "w_uq": w_uq_p.astype(BF16),
        "kv_norm_g": vec(kv_norm_g[l]),
        "w_uk": w_uk[l].reshape(KV_LORA, N_HEADS * QK_NOPE).astype(BF16),
        "w_uv": w_uv[l].reshape(KV_LORA, N_HEADS * V_HEAD).astype(BF16),
        "w_uv_t": w_uv[l].reshape(KV_LORA, N_HEADS * V_HEAD).T.astype(BF16),
        "w_ukt": jnp.transpose(w_uk[l], (1, 2, 0)).astype(BF16),
        "w_uvh": jnp.transpose(w_uv[l], (1, 0, 2)).astype(BF16),
        "conv_w": conv_w[l].astype(F32), "conv_b": vec(conv_b[l]),
        "w_ax": jnp.concatenate([rg_wa[l], rg_wx[l]], axis=-1).astype(BF16),
        "rg_ba": vec(rg_ba[l]), "rg_bx": vec(rg_bx[l]), "rg_lambda": vec(rg_lambda[l]),
        "w_br_r": w_br_r[l].astype(BF16), "w_br_m": w_br_m[l].astype(BF16), "w_o": w_o[l].astype(BF16),
        "ln1_g": vec(ln1_g[l]), "ln1_b": vec(ln1_b[l]),
        "w_up": w_up[l].astype(BF16), "b_up": vec(b_up[l]),
        "w_down": w_down[l].astype(BF16), "b_down": vec(b_down[l]),
        "ln2_g": vec(ln2_g[l]), "ln2_b": vec(ln2_b[l]),
    }


def kernel(x_prompt, x_sample, cache_ckv, cache_kpe, page_table, state_conv, state_rglru, meta_tokens, ln_in_g, ln_in_b, w_in, b_gate, conv_w, conv_b, rg_wa, rg_ba, rg_wx, rg_bx, rg_lambda, w_br_r, q_norm_g, w_uq, kv_norm_g, w_uk, w_uv, w_br_m, w_o, ln1_g, ln1_b, w_up, b_up, w_down, b_down, ln2_g, ln2_b):
    assert w_in.shape[0] == DEPTH == 1
    bn, seq, _ = x_prompt.shape
    bd, ts, _ = x_sample.shape
    past_len = page_table.shape[1] * cache_ckv.shape[2]
    p = _prep_params(w_in, b_gate, conv_w, conv_b, rg_wa, rg_ba, rg_wx, rg_bx, rg_lambda, w_br_r, q_norm_g, w_uq,
                     kv_norm_g, w_uk, w_uv, w_br_m, w_o, ln1_g, ln1_b, w_up, b_up, w_down, b_down, ln2_g, ln2_b,
                     ln_in_g, ln_in_b)
    n_s = bd * ts

    x_small = jnp.concatenate([x_sample.reshape(n_s, D_MODEL), meta_tokens.astype(F32)], axis=0)
    pos_small = jnp.concatenate([jnp.tile(past_len + jnp.arange(ts), bd), jnp.arange(N_META)])
    n_small = n_s + N_META
    rx, grg, g_small, q, k, v, ckv, kpe = _in_proj(x_small, _rope_tables(pos_small), 1, p, n_small, False)

    rep = lambda a: jnp.broadcast_to(a[None], (SUBLANES,) + a.shape)
    y_m, conv_m, h_m = _rglru(rep(rx[n_s:]), rep(grg[n_s:]), jnp.zeros((SUBLANES, CONV_W - 1, D_RNN), F32),
                              jnp.zeros((SUBLANES, D_RNN), F32), p, N_META)
    k_meta, v_meta = k[n_s:], v[n_s:]
    o_m = _meta_attn(q[n_s:], k_meta, v_meta)

    y_s, conv_s, h_s = _rglru(rx[:n_s].reshape(bd, ts, D_RNN), grg[:n_s].reshape(bd, ts, D_RNN),
                              state_conv[0], state_rglru[0], p, ts)
    q_lat, q_pe = _absorb(q[:n_s], p["w_ukt"], bd)
    ckv_s = ckv[:n_s].reshape(bd, ts, KV_LORA)
    kpe_s = kpe[:n_s].reshape(bd, ts, QK_ROPE)
    pad_new = lambda a: jnp.pad(a, ((0, 0), (0, 2 * SUBLANES - ts), (0, 0)))

    n_p = bn * seq
    pos_p = N_META + jnp.arange(seq)
    tm = 256
    rx_p, grg_p, g_p, q_p, k_p, vt_p, ckv_p, kpe_p = _in_proj(x_prompt.reshape(n_p, D_MODEL), _rope_tables(pos_p),
                                                              seq // tm, p, tm, True)
    bcast = lambda a: jnp.broadcast_to(a[:1], (bn,) + a.shape[1:])
    y_p, conv_p, h_p = _rglru(rx_p.reshape(bn, seq, D_RNN), grg_p.reshape(bn, seq, D_RNN), bcast(conv_m), bcast(h_m),
                              p, 128)
    o_p = _flash(q_p.reshape(bn, seq, -1), k_p.reshape(bn, seq, -1), vt_p.reshape(bn, seq // tm, -1, tm),
                 k_meta, v_meta.T, 512, 2)
    out_p, o_lat = _post_with_paged_attn(
        x_prompt.reshape(n_p, D_MODEL), y_p.reshape(n_p, D_RNN), o_p.reshape(n_p, -1), g_p, p, tm,
        page_table, q_lat.reshape(bd, N_HEADS * ts, KV_LORA), q_pe.reshape(bd, N_HEADS * ts, ROPE_PAD),
        pad_new(ckv_s), pad_new(kpe_s), cache_ckv[0], jnp.swapaxes(cache_kpe[0], 1, 2))

    o_s = _unabsorb(o_lat.reshape(bd, N_HEADS, ts, KV_LORA), p["w_uvh"])
    y_small = jnp.concatenate([y_s.reshape(n_s, D_RNN), y_m[0]], axis=0)
    o_small = jnp.concatenate([o_s, o_m], axis=0)
    out_small = _post(x_small, y_small, o_small, g_small, p, n_small)

    meta_rows = lambda a: jnp.broadcast_to(a[n_s:][None], (bn, N_META, a.shape[-1]))
    y_prompt = out_p.reshape(bn, seq, D_MODEL)
    y_sample = out_small[:n_s].reshape(bd, ts, D_MODEL)
    prompt_ckv = jnp.concatenate([meta_rows(ckv), ckv_p.reshape(bn, seq, KV_LORA)], axis=1)[None]
    prompt_kpe = jnp.concatenate([meta_rows(kpe), kpe_p.reshape(bn, seq, QK_ROPE)], axis=1)[None]
    return (y_prompt, y_sample, prompt_ckv, prompt_kpe, conv_p[None], h_p[None],
            ckv_s[None], kpe_s[None], conv_s[None], h_s[None])
```

```python
import functools
import math

import jax
import jax.numpy as jnp
from jax import lax
from jax.experimental import pallas as pl
from jax.experimental.pallas import tpu as pltpu

F32 = jnp.float32
BF16 = jnp.bfloat16

D_MODEL = 1024
N_META = 16
D_RNN = D_MODEL
RG_BLOCKS = 8
RG_BLOCK = D_RNN // RG_BLOCKS
CONV_W = 4
RG_C = 8.0
N_HEADS = 8
QK_NOPE = 128
QK_ROPE = 64
V_HEAD = 128
KV_LORA = 512
Q_LORA = 768
ROPE_THETA = 10000.0
D_FF = 4 * D_MODEL
DEPTH = 1
DN_ALPHA = (2.0 * DEPTH) ** 0.25
EPS = 1e-5
SM_SCALE = (QK_NOPE + QK_ROPE) ** -0.5
Q_PRESCALE = SM_SCALE * math.log2(math.e)

LANES = 128
SUBLANES = 8
HEAD_PAD = 2 * LANES
ROPE_PAD = LANES
_C_RX, _C_RG, _C_CQ, _C_CKV, _C_KPE = 0, D_RNN, 2 * D_RNN, 2 * D_RNN + Q_LORA, 2 * D_RNN + Q_LORA + KV_LORA
_C_GATE = _C_KPE + QK_ROPE
VMEM_LIMIT = 56 * 1024 * 1024
NEG = float(jnp.finfo(jnp.float32).min)


def _const_spec(shape):
    return pl.BlockSpec(shape, lambda *_: (0,) * len(shape), pipeline_mode=pl.Buffered(1))


def _layer_norm(x, g, b):
    mu = jnp.mean(x, -1, keepdims=True)
    xc = x - mu
    var = jnp.mean(xc * xc, -1, keepdims=True)
    return xc * lax.rsqrt(var + EPS) * g + b


def _rms_norm(x, g):
    return x * lax.rsqrt(jnp.mean(x * x, -1, keepdims=True) + EPS) * g


def _gelu_tanh(x):
    return 0.5 * x * (1.0 + jnp.tanh(math.sqrt(2.0 / math.pi) * (x + 0.044715 * (x * x * x))))


def _rope_block(y, cos, sin_lo, sin_hi):
    left = pltpu.roll(y, ROPE_PAD - QK_ROPE // 2, 1)
    right = pltpu.roll(y, QK_ROPE // 2, 1)
    return y * cos + left * sin_lo + right * sin_hi


def _dot(a, b):
    return jnp.dot(a, b, preferred_element_type=F32)


def _dot_nt(a, b):
    return lax.dot_general(a, b, (((1,), (1,)), ((), ())), preferred_element_type=F32)


def _in_proj_kernel(x_ref, cos_ref, slo_ref, shi_ref, lng_ref, lnb_ref, w_in_ref, w_kpe_ref, w_gate_ref, bg_ref,
                    qg_ref, wuq_ref, kvg_ref, wuk_ref, wuv_ref,
                    rx_ref, grg_ref, g_ref, q_ref, k_ref, v_ref, ckv_ref, kpe_ref, *, v_transposed):
    h = _layer_norm(x_ref[...], lng_ref[...], lnb_ref[...])
    hb = h.astype(BF16)

    def proj(lo, hi):
        return _dot(hb, w_in_ref[:, lo:hi])

    cos, slo, shi = cos_ref[...], slo_ref[...], shi_ref[...]
    cqn = _rms_norm(proj(_C_CQ, _C_CKV), qg_ref[...])
    ckv = _rms_norm(proj(_C_CKV, _C_KPE), kvg_ref[...])
    kpe = _rope_block(_dot(hb, w_kpe_ref[...]), cos, slo, shi)
    ckv_ref[...] = ckv
    kpe_ref[...] = kpe[:, :QK_ROPE]
    ckvb = ckv.astype(BF16)
    kpeb = kpe.astype(BF16)

    rx_ref[...] = proj(_C_RX, _C_RG)
    grg_ref[...] = _gelu_tanh(proj(_C_RG, _C_CQ))
    g_ref[...] = jax.nn.sigmoid(_dot(hb, w_gate_ref[...]) + bg_ref[...])

    q = _dot(cqn.astype(BF16), wuq_ref[...]) * Q_PRESCALE
    for hh in range(N_HEADS):
        c0 = hh * HEAD_PAD
        q_ref[:, c0:c0 + QK_NOPE] = q[:, c0:c0 + QK_NOPE].astype(BF16)
        q_ref[:, c0 + QK_NOPE:c0 + HEAD_PAD] = _rope_block(q[:, c0 + QK_NOPE:c0 + HEAD_PAD], cos, slo, shi).astype(BF16)

    kn = _dot(ckvb, wuk_ref[...])
    if v_transposed:
        v_ref[...] = _dot_nt(wuv_ref[...], ckvb).astype(BF16)
    else:
        v_ref[...] = _dot(ckvb, wuv_ref[...]).astype(BF16)
    for hh in range(N_HEADS):
        c0 = hh * HEAD_PAD
        k_ref[:, c0:c0 + QK_NOPE] = kn[:, hh * QK_NOPE:(hh + 1) * QK_NOPE].astype(BF16)
        k_ref[:, c0 + QK_NOPE:c0 + HEAD_PAD] = kpeb


def _in_proj(x, tabs, tab_blocks, p, tm, v_transposed):
    rows = x.shape[0]
    assert rows % tm == 0
    n_tiles = rows // tm
    row = lambda w: pl.BlockSpec((tm, w), lambda i: (i, 0))
    tab = pl.BlockSpec((tm, ROPE_PAD), lambda i: (i % tab_blocks, 0))
    outs = [
        (D_RNN, F32),
        (D_RNN, F32),
        (2 * D_MODEL, F32),
        (N_HEADS * HEAD_PAD, BF16),
        (N_HEADS * HEAD_PAD, BF16),
        (N_HEADS * V_HEAD, BF16),
        (KV_LORA, F32),
        (QK_ROPE, F32),
    ]
    out_specs = [row(w) for w, _ in outs]
    out_shape = [jax.ShapeDtypeStruct((rows, w), dt) for w, dt in outs]
    if v_transposed:
        out_specs[5] = pl.BlockSpec((None, N_HEADS * V_HEAD, tm), lambda i: (i, 0, 0))
        out_shape[5] = jax.ShapeDtypeStruct((n_tiles, N_HEADS * V_HEAD, tm), BF16)
    return pl.pallas_call(
        functools.partial(_in_proj_kernel, v_transposed=v_transposed),
        grid=(n_tiles,),
        in_specs=[row(D_MODEL), tab, tab, tab,
                  _const_spec((1, D_MODEL)), _const_spec((1, D_MODEL)),
                  _const_spec((D_MODEL, _C_KPE)), _const_spec((D_MODEL, ROPE_PAD)),
                  _const_spec((D_MODEL, 2 * D_MODEL)), _const_spec((1, 2 * D_MODEL)),
                  _const_spec((1, Q_LORA)), _const_spec((Q_LORA, N_HEADS * HEAD_PAD)),
                  _const_spec((1, KV_LORA)), _const_spec((KV_LORA, N_HEADS * QK_NOPE)),
                  _const_spec((N_HEADS * V_HEAD, KV_LORA) if v_transposed else (KV_LORA, N_HEADS * V_HEAD))],
        out_specs=out_specs,
        out_shape=out_shape,
        compiler_params=pltpu.CompilerParams(dimension_semantics=("arbitrary",), vmem_limit_bytes=VMEM_LIMIT),
        name="in_proj",
    )(x, *tabs, p["ln_in_g"], p["ln_in_b"], p["w_in"], p["w_kpe"], p["w_gate"], p["b_gate"], p["q_norm_g"], p["w_uq"],
      p["kv_norm_g"], p["w_uk"], p["w_uv_t"] if v_transposed else p["w_uv"])


def _rglru_kernel(rx_ref, grg_ref, cprev_ref, hprev_ref, cw_ref, cb_ref, wax_ref, ba_ref, bx_ref, lam_ref,
                  y_ref, cout_ref, hout_ref, xe_ref, a_ref, u_ref, hc_ref, *, nb, tt):
    pitch = tt + SUBLANES
    n_slab = D_RNN // LANES
    tail = SUBLANES - (CONV_W - 1)

    @pl.when(pl.program_id(0) == 0)
    def _():
        xe_ref[:, tail:SUBLANES, :] = cprev_ref[...]
        hc_ref[...] = hprev_ref[...]

    xe_ref[:, SUBLANES:, :] = rx_ref[...]
    xc = cb_ref[...] + xe_ref[:, SUBLANES:, :] * cw_ref[CONV_W - 1:CONV_W, :]
    for k in range(CONV_W - 1):
        xc = xc + xe_ref[:, tail + k:tail + k + tt, :] * cw_ref[k:k + 1, :]
    new_tail = xe_ref[:, tt + tail:tt + SUBLANES, :]
    cout_ref[...] = new_tail
    xe_ref[:, tail:SUBLANES, :] = new_tail

    neg_lam = -lam_ref[...]
    softplus = jnp.maximum(neg_lam, 0.0) + jnp.log1p(jnp.exp(-jnp.abs(neg_lam)))
    xc2 = xc.reshape(nb * tt, D_RNN)
    for n in range(n_slab):
        cs = slice(n * LANES, (n + 1) * LANES)
        xn = xc2[:, cs]
        gates = _dot(xn.astype(BF16), wax_ref[n])
        r = jax.nn.sigmoid(gates[:, :RG_BLOCK] + ba_ref[:, cs])
        ig = jax.nn.sigmoid(gates[:, RG_BLOCK:] + bx_ref[:, cs])
        log_a = (-RG_C) * r * softplus[:, cs]
        a = jnp.exp(log_a)
        u = jnp.sqrt(-jnp.tanh(log_a) * (a * a + 1.0)) * (ig * xn)
        for b in range(nb):
            a_ref[n, b * pitch:b * pitch + tt, :] = a[b * tt:(b + 1) * tt]
            u_ref[n, b * pitch:b * pitch + tt, :] = u[b * tt:(b + 1) * tt]

    for grp in range(nb // SUBLANES):
        base = grp * SUBLANES * pitch
        rows = slice(grp * SUBLANES, (grp + 1) * SUBLANES)
        h0 = tuple(hc_ref[rows, n * LANES:(n + 1) * LANES] for n in range(n_slab))

        def step(t, hs, base=base):
            out = []
            for n in range(n_slab):
                idx = pl.ds(base + t, SUBLANES, stride=pitch)
                hn = a_ref[n, idx, :] * hs[n] + u_ref[n, idx, :]
                u_ref[n, idx, :] = hn
                out.append(hn)
            return tuple(out)

        hs = lax.fori_loop(0, tt, step, h0, unroll=min(tt, 8))
        for n in range(n_slab):
            hc_ref[rows, n * LANES:(n + 1) * LANES] = hs[n]
    hout_ref[...] = hc_ref[...]

    for b in range(nb):
        for n in range(n_slab):
            cs = slice(n * LANES, (n + 1) * LANES)
            y_ref[b, :, cs] = (u_ref[n, b * pitch:b * pitch + tt, :] * grg_ref[b, :, cs]).astype(BF16)


def _rglru(rx, grg, conv_prev, h_prev, p, tt):
    nb, t_len, _ = rx.shape
    assert nb % SUBLANES == 0 and t_len % tt == 0 and tt % SUBLANES == 0
    pitch = tt + SUBLANES
    seq = pl.BlockSpec((nb, tt, D_RNN), lambda i: (0, i, 0))
    return pl.pallas_call(
        functools.partial(_rglru_kernel, nb=nb, tt=tt),
        grid=(t_len // tt,),
        in_specs=[seq, seq, _const_spec((nb, CONV_W - 1, D_RNN)), _const_spec((nb, D_RNN)),
                  _const_spec((CONV_W, D_RNN)), _const_spec((1, D_RNN)),
                  _const_spec((RG_BLOCKS, RG_BLOCK, 2 * RG_BLOCK)),
                  _const_spec((1, D_RNN)), _const_spec((1, D_RNN)), _const_spec((1, D_RNN))],
        out_specs=[seq,
                   pl.BlockSpec((nb, CONV_W - 1, D_RNN), lambda i: (0, 0, 0)),
                   pl.BlockSpec((nb, D_RNN), lambda i: (0, 0))],
        out_shape=[jax.ShapeDtypeStruct((nb, t_len, D_RNN), BF16),
                   jax.ShapeDtypeStruct((nb, CONV_W - 1, D_RNN), F32),
                   jax.ShapeDtypeStruct((nb, D_RNN), F32)],
        scratch_shapes=[pltpu.VMEM((nb, tt + SUBLANES, D_RNN), F32),
                        pltpu.VMEM((D_RNN // LANES, nb * pitch, LANES), F32),
                        pltpu.VMEM((D_RNN // LANES, nb * pitch, LANES), F32),
                        pltpu.VMEM((nb, D_RNN), F32)],
        compiler_params=pltpu.CompilerParams(dimension_semantics=("arbitrary",), vmem_limit_bytes=VMEM_LIMIT),
        name="rglru",
    )(rx, grg, conv_prev, h_prev, p["conv_w"], p["conv_b"], p["w_ax"], p["rg_ba"], p["rg_bx"], p["rg_lambda"])


def _softmax_update(s, m, l, acc, v):
    m_new = jnp.maximum(m, jnp.max(s, axis=1, keepdims=True))
    alpha = jnp.exp2(m - m_new)
    pr = jnp.exp2(s - m_new)
    l = alpha * l + jnp.sum(pr, axis=1, keepdims=True)
    acc = alpha * acc + _dot(pr.astype(BF16), v)
    return m_new, l, acc


def _softmax_update_t(s_t, m, l, acc_t, v_t):
    m_new = jnp.maximum(m, jnp.max(s_t, axis=0, keepdims=True))
    alpha = jnp.exp2(m - m_new)
    pr = jnp.exp2(s_t - m_new)
    l = alpha * l + jnp.sum(pr, axis=0, keepdims=True)
    acc_t = alpha * acc_t + _dot(v_t, pr.astype(BF16))
    return m_new, l, acc_t


def _flash_kernel(q_ref, k_ref, vt_ref, km_ref, vmt_ref, o_ref, *, tq, tv, hps):
    qcols = [slice(hh * HEAD_PAD, (hh + 1) * HEAD_PAD) for hh in range(hps)]
    vrows = [slice(hh * V_HEAD, (hh + 1) * V_HEAD) for hh in range(hps)]

    def scores(j, hh):
        return _dot_nt(k_ref[j * tq:(j + 1) * tq, qcols[hh]], q_ref[:, qcols[hh]])

    def values_t(j, hh):
        return jnp.concatenate([vt_ref[j * (tq // tv) + c, vrows[hh], :] for c in range(tq // tv)], axis=1)

    def tile(n_full):
        kpos = lax.broadcasted_iota(jnp.int32, (tq, tq), 0)
        qpos = lax.broadcasted_iota(jnp.int32, (tq, tq), 1)
        state, s_next = [], []
        for hh in range(hps):
            s_t = _dot_nt(km_ref[:, qcols[hh]], q_ref[:, qcols[hh]])
            m = jnp.max(s_t, axis=0, keepdims=True)
            pr = jnp.exp2(s_t - m)
            state.append((m, jnp.sum(pr, axis=0, keepdims=True), _dot(vmt_ref[vrows[hh], :], pr.astype(BF16))))
            s_next.append(scores(0, hh))
        for j in range(n_full + 1):
            for hh in range(hps):
                s_t = s_next[hh]
                if j < n_full:
                    s_next[hh] = scores(j + 1, hh)
                else:
                    s_t = jnp.where(kpos <= qpos, s_t, NEG)
                state[hh] = _softmax_update_t(s_t, *state[hh], values_t(j, hh))
        for hh in range(hps):
            _, l, acc_t = state[hh]
            o_ref[:, vrows[hh]] = jnp.transpose(acc_t / l).astype(BF16)

    for c in range(k_ref.shape[0] // tq):
        pl.when(pl.program_id(2) == c)(functools.partial(tile, c))


def _flash(q, k, v_t, k_pre, v_pre_t, tq, hps):
    nb, t_len, _ = q.shape
    tv = v_t.shape[3]
    assert t_len % tq == 0 and tq % tv == 0 and N_HEADS % hps == 0
    return pl.pallas_call(
        functools.partial(_flash_kernel, tq=tq, tv=tv, hps=hps),
        grid=(nb, N_HEADS // hps, t_len // tq),
        in_specs=[pl.BlockSpec((None, tq, hps * HEAD_PAD), lambda b, h, i: (b, i, h)),
                  pl.BlockSpec((None, t_len, hps * HEAD_PAD), lambda b, h, i: (b, 0, h)),
                  pl.BlockSpec((None, t_len // tv, hps * V_HEAD, tv), lambda b, h, i: (b, 0, h, 0)),
                  pl.BlockSpec((N_META, hps * HEAD_PAD), lambda b, h, i: (0, h)),
                  pl.BlockSpec((hps * V_HEAD, N_META), lambda b, h, i: (h, 0))],
        out_specs=pl.BlockSpec((None, tq, hps * V_HEAD), lambda b, h, i: (b, i, h)),
        out_shape=jax.ShapeDtypeStruct((nb, t_len, N_HEADS * V_HEAD), BF16),
        compiler_params=pltpu.CompilerParams(dimension_semantics=("arbitrary",) * 3, vmem_limit_bytes=VMEM_LIMIT),
        name="flash_prompt",
    )(q, k, v_t, k_pre, v_pre_t)


def _meta_attn_kernel(q_ref, k_ref, v_ref, o_ref):
    qpos = lax.broadcasted_iota(jnp.int32, (N_META, N_META), 0)
    kpos = lax.broadcasted_iota(jnp.int32, (N_META, N_META), 1)
    for hh in range(N_HEADS):
        qs = slice(hh * HEAD_PAD, (hh + 1) * HEAD_PAD)
        vs = slice(hh * V_HEAD, (hh + 1) * V_HEAD)
        s = jnp.where(kpos <= qpos, _dot_nt(q_ref[:, qs], k_ref[:, qs]), NEG)
        pr = jnp.exp2(s - jnp.max(s, axis=1, keepdims=True))
        o = _dot(pr.astype(BF16), v_ref[:, vs]) / jnp.sum(pr, axis=1, keepdims=True)
        o_ref[:, vs] = o.astype(BF16)


def _meta_attn(q, k, v):
    return pl.pallas_call(
        _meta_attn_kernel,
        out_shape=jax.ShapeDtypeStruct((N_META, N_HEADS * V_HEAD), BF16),
        name="meta_attn",
    )(q, k, v)


def _absorb_kernel(q_ref, wukt_ref, qlat_ref, qpe_ref):
    q = q_ref[...]
    nb = qlat_ref.shape[0]
    qlat = _dot(q[:, :QK_NOPE], wukt_ref[...])
    qlat_ref[...] = qlat.reshape(nb, -1, KV_LORA)
    qpe_ref[...] = q[:, QK_NOPE:].astype(F32).reshape(nb, -1, ROPE_PAD)


def _absorb(q, w_ukt, nb):
    rows = q.shape[0]
    ts = rows // nb
    return pl.pallas_call(
        _absorb_kernel,
        grid=(N_HEADS,),
        in_specs=[pl.BlockSpec((rows, HEAD_PAD), lambda h: (0, h)),
                  pl.BlockSpec((None, QK_NOPE, KV_LORA), lambda h: (h, 0, 0))],
        out_specs=[pl.BlockSpec((nb, None, ts, KV_LORA), lambda h: (0, h, 0, 0)),
                   pl.BlockSpec((nb, None, ts, ROPE_PAD), lambda h: (0, h, 0, 0))],
        out_shape=[jax.ShapeDtypeStruct((nb, N_HEADS, ts, KV_LORA), F32),
                   jax.ShapeDtypeStruct((nb, N_HEADS, ts, ROPE_PAD), F32)],
        compiler_params=pltpu.CompilerParams(dimension_semantics=("arbitrary",)),
        name="absorb_q",
    )(q, w_ukt)


PAGED_BUFS = 2


def _paged_scratch(n_q, pps, page):
    return [pltpu.VMEM((n_q, 1), F32), pltpu.VMEM((n_q, 1), F32), pltpu.VMEM((n_q, KV_LORA), F32),
            pltpu.VMEM((PAGED_BUFS, pps * page, KV_LORA), F32), pltpu.VMEM((PAGED_BUFS, pps, QK_ROPE, page), F32),
            pltpu.VMEM((pps * page, KV_LORA), BF16), pltpu.VMEM((QK_ROPE, pps * page), BF16),
            pltpu.SemaphoreType.DMA((PAGED_BUFS,))]


def _paged_copies(pt_ref, ckv_hbm, kpe_hbm, cin_ref, kin_ref, sem, seq, chunk, slot, pps):
    page = kin_ref.shape[3]
    out = []
    for i in range(pps):
        pid = pt_ref[seq, chunk * pps + i]
        out.append(pltpu.make_async_copy(ckv_hbm.at[pid], cin_ref.at[slot, pl.ds(i * page, page)], sem.at[slot]))
        out.append(pltpu.make_async_copy(kpe_hbm.at[pid], kin_ref.at[slot, i], sem.at[slot]))
    return out


def _paged_new_tokens(qlat, qpe, cnew_ref, knew_ref, m_ref, l_ref, acc_ref, ts):
    cn = cnew_ref[...].astype(BF16)
    kn = knew_ref[...].astype(BF16)
    s = _dot_nt(qlat, cn) + _dot_nt(qpe, kn)
    t_q = lax.broadcasted_iota(jnp.int32, s.shape, 0) % ts
    t_k = lax.broadcasted_iota(jnp.int32, s.shape, 1)
    s = jnp.where(t_k <= t_q, s, NEG)
    m = jnp.max(s, axis=1, keepdims=True)
    pr = jnp.exp2(s - m)
    m_ref[...] = m
    l_ref[...] = jnp.sum(pr, axis=1, keepdims=True)
    acc_ref[...] = _dot(pr.astype(BF16), cn)


def _paged_scores(qlat, qpe, cin_ref, kin_ref, cbuf_ref, kbuf_ref, slot, pps):
    page = kin_ref.shape[3]
    for i in range(pps):
        rows = pl.ds(i * page, page)
        cbuf_ref[rows, :] = cin_ref[slot, rows, :].astype(BF16)
        kbuf_ref[:, i * page:(i + 1) * page] = kin_ref[slot, i].astype(BF16)
    return _dot_nt(qlat, cbuf_ref[...]) + _dot(qpe, kbuf_ref[...])


def _paged_probs(s, m_ref, l_ref):
    m = m_ref[...]
    m_new = jnp.maximum(m, jnp.max(s, axis=1, keepdims=True))
    alpha = jnp.exp2(m - m_new)
    pr = jnp.exp2(s - m_new)
    m_ref[...] = m_new
    l_ref[...] = alpha * l_ref[...] + jnp.sum(pr, axis=1, keepdims=True)
    return alpha, pr.astype(BF16)


def _paged_accumulate(alpha, pr, cbuf_ref, acc_ref):
    acc_ref[...] = alpha * acc_ref[...] + _dot(pr, cbuf_ref[...])


def _unabsorb_kernel(olat_ref, wuv_ref, o_ref):
    x = olat_ref[...]
    x = x.reshape(x.shape[0] * x.shape[1], KV_LORA).astype(BF16)
    o_ref[...] = _dot(x, wuv_ref[...]).astype(BF16)


def _unabsorb(o_lat, w_uvh):
    nb, _, ts, _ = o_lat.shape
    return pl.pallas_call(
        _unabsorb_kernel,
        grid=(N_HEADS,),
        in_specs=[pl.BlockSpec((nb, None, ts, KV_LORA), lambda h: (0, h, 0, 0)),
                  pl.BlockSpec((None, KV_LORA, V_HEAD), lambda h: (h, 0, 0))],
        out_specs=pl.BlockSpec((nb * ts, V_HEAD), lambda h: (0, h)),
        out_shape=jax.ShapeDtypeStruct((nb * ts, N_HEADS * V_HEAD), BF16),
        compiler_params=pltpu.CompilerParams(dimension_semantics=("arbitrary",)),
        name="unabsorb_o",
    )(o_lat, w_uvh)


FF_CHUNKS = 4
N_POST_WEIGHTS = 13


def _post_compute(x_ref, y_ref, o_ref, g_ref, weights, out_ref, side_work=None):
    before_up, before_down, after_down = side_work if side_work is not None else (lambda c: None,) * 3
    lng_ref, lnb_ref, wr_ref, wm_ref, wo_ref, g1_ref, b1_ref, wup_ref, bup_ref, wdn_ref, bdn_ref, g2_ref, b2_ref = weights
    h = _layer_norm(x_ref[...], lng_ref[...], lnb_ref[...])
    z_r = _dot(y_ref[...], wr_ref[...])
    z_m = _dot(o_ref[...], wm_ref[...])
    mix_in = g_ref[:, :D_MODEL] * z_r + g_ref[:, D_MODEL:] * z_m
    mix = _dot(mix_in.astype(BF16), wo_ref[...])
    before_up(0)
    x1 = _layer_norm(DN_ALPHA * h + mix, g1_ref[...], b1_ref[...])
    x1b = x1.astype(BF16)
    f = bdn_ref[...]
    ff_chunk = D_FF // FF_CHUNKS
    for c in range(FF_CHUNKS):
        cs = slice(c * ff_chunk, (c + 1) * ff_chunk)
        if c > 0:
            before_up(c)
        up = jnp.maximum(_dot(x1b, wup_ref[:, cs]) + bup_ref[:, cs], 0.0)
        before_down(c)
        f = f + _dot((up * up).astype(BF16), wdn_ref[cs, :])
        after_down(c)
    out_ref[...] = _layer_norm(DN_ALPHA * x1 + f, g2_ref[...], b2_ref[...])


def _post_kernel(x_ref, y_ref, o_ref, g_ref, *rest):
    _post_compute(x_ref, y_ref, o_ref, g_ref, rest[:N_POST_WEIGHTS], rest[N_POST_WEIGHTS])


def _post_paged_kernel(pt_ref, x_ref, y_ref, o_ref, g_ref, *rest, pps, ts, n_j, n_steps):
    weights = rest[:N_POST_WEIGHTS]
    (qlat_ref, qpe_ref, cnew_ref, knew_ref, ckv_hbm, kpe_hbm, out_ref, olat_ref,
     m_ref, l_ref, acc_ref, cin_ref, kin_ref, cbuf_ref, kbuf_ref, sem) = rest[N_POST_WEIGHTS:]
    s = pl.program_id(0)
    spb = n_j // FF_CHUNKS
    seq, j0 = lax.div(s, spb), lax.rem(s, spb) * FF_CHUNKS
    copies = functools.partial(_paged_copies, pt_ref, ckv_hbm, kpe_hbm, cin_ref, kin_ref, sem, pps=pps)

    @pl.when(s == 0)
    def _():
        for cp in copies(0, 0, 0):
            cp.start()

    live = {}

    def gather_and_score(c):
        slot, nxt = c % PAGED_BUFS, (c + 1) % PAGED_BUFS
        if c + 1 < FF_CHUNKS:
            for cp in copies(seq, j0 + c + 1, nxt):
                cp.start()
        else:
            @pl.when(s + 1 < n_steps)
            def _():
                for cp in copies(lax.div(s + 1, spb), lax.rem(s + 1, spb) * FF_CHUNKS, nxt):
                    cp.start()
        for cp in copies(seq, j0 + c, slot):
            cp.wait()
        qlat = qlat_ref[...].astype(BF16)
        qpe = qpe_ref[...][:, :QK_ROPE].astype(BF16)
        if c == 0:
            pl.when(j0 == 0)(functools.partial(_paged_new_tokens, qlat, qpe, cnew_ref, knew_ref,
                                               m_ref, l_ref, acc_ref, ts))
        live["s"] = _paged_scores(qlat, qpe, cin_ref, kin_ref, cbuf_ref, kbuf_ref, slot, pps)

    def probs(c):
        live["alpha"], live["pr"] = _paged_probs(live.pop("s"), m_ref, l_ref)

    def accumulate(c):
        _paged_accumulate(live.pop("alpha"), live.pop("pr"), cbuf_ref, acc_ref)
        if c == FF_CHUNKS - 1:
            @pl.when(j0 == n_j - FF_CHUNKS)
            def _():
                olat_ref[...] = acc_ref[...] / l_ref[...]

    _post_compute(x_ref, y_ref, o_ref, g_ref, weights, out_ref, side_work=(gather_and_score, probs, accumulate))


def _post_weight_specs():
    vec = lambda w: _const_spec((1, w))
    sq = _const_spec((D_MODEL, D_MODEL))
    return [vec(D_MODEL), vec(D_MODEL), sq, sq, sq, vec(D_MODEL), vec(D_MODEL),
            _const_spec((D_MODEL, D_FF)), vec(D_FF), _const_spec((D_FF, D_MODEL)), vec(D_MODEL),
            vec(D_MODEL), vec(D_MODEL)]


def _post_weights(p):
    return (p["ln_in_g"], p["ln_in_b"], p["w_br_r"], p["w_br_m"], p["w_o"], p["ln1_g"], p["ln1_b"],
            p["w_up"], p["b_up"], p["w_down"], p["b_down"], p["ln2_g"], p["ln2_b"])


def _post(x, y, o, g, p, tm):
    rows = x.shape[0]
    assert rows % tm == 0
    row = lambda w: pl.BlockSpec((tm, w), lambda i: (i, 0))
    return pl.pallas_call(
        _post_kernel,
        grid=(rows // tm,),
        in_specs=[row(D_MODEL), row(D_RNN), row(N_HEADS * V_HEAD), row(2 * D_MODEL)] + _post_weight_specs(),
        out_specs=row(D_MODEL),
        out_shape=jax.ShapeDtypeStruct((rows, D_MODEL), F32),
        compiler_params=pltpu.CompilerParams(dimension_semantics=("arbitrary",), vmem_limit_bytes=VMEM_LIMIT),
        name="merge_mlp",
    )(x, y, o, g, *_post_weights(p))


def _post_with_paged_attn(x, y, o, g, p, tm, page_table, q_lat, q_pe, ckv_new, kpe_new, ckv_pool, kpe_pool_t):
    rows = x.shape[0]
    n_steps = rows // tm
    nb, n_pages = page_table.shape
    n_q = q_lat.shape[1]
    page = ckv_pool.shape[1]
    n_new = ckv_new.shape[1]
    chunks = n_steps * FF_CHUNKS
    assert rows % tm == 0 and (nb * n_pages) % chunks == 0 and FF_CHUNKS % PAGED_BUFS == 0
    pps = nb * n_pages // chunks
    n_j = n_pages // pps
    assert n_pages % pps == 0 and n_j % FF_CHUNKS == 0
    spb = n_j // FF_CHUNKS
    row = lambda w: pl.BlockSpec((tm, w), lambda i, pt: (i, 0))
    per_seq = lambda r, w: pl.BlockSpec((None, r, w), lambda i, pt: (i // spb, 0, 0))
    grid_spec = pltpu.PrefetchScalarGridSpec(
        num_scalar_prefetch=1,
        grid=(n_steps,),
        in_specs=[row(D_MODEL), row(D_RNN), row(N_HEADS * V_HEAD), row(2 * D_MODEL)] + _post_weight_specs()
                 + [per_seq(n_q, KV_LORA), per_seq(n_q, ROPE_PAD), per_seq(n_new, KV_LORA), per_seq(n_new, QK_ROPE),
                    pl.BlockSpec(memory_space=pl.ANY), pl.BlockSpec(memory_space=pl.ANY)],
        out_specs=[row(D_MODEL), per_seq(n_q, KV_LORA)],
        scratch_shapes=_paged_scratch(n_q, pps, page),
    )
    return pl.pallas_call(
        functools.partial(_post_paged_kernel, pps=pps, ts=n_q // N_HEADS, n_j=n_j, n_steps=n_steps),
        grid_spec=grid_spec,
        out_shape=[jax.ShapeDtypeStruct((rows, D_MODEL), F32), jax.ShapeDtypeStruct((nb, n_q, KV_LORA), F32)],
        compiler_params=pltpu.CompilerParams(dimension_semantics=("arbitrary",), vmem_limit_bytes=VMEM_LIMIT),
        name="merge_mlp_paged_attn",
    )(page_table, x, y, o, g, *_post_weights(p), q_lat, q_pe, ckv_new, kpe_new, ckv_pool, kpe_pool_t)


def _rope_tables(pos):
    half = QK_ROPE // 2
    inv = 1.0 / (ROPE_THETA ** (jnp.arange(half, dtype=F32) / half))
    ang = pos.astype(F32)[:, None] * inv[None, :]
    cos, sin = jnp.cos(ang), jnp.sin(ang)
    z = jnp.zeros_like(cos)
    return (jnp.concatenate([cos, cos, z, z], 1), jnp.concatenate([-sin, z, z, z], 1),
            jnp.concatenate([z, sin, z, z], 1))


def _prep_params(w_in, b_gate, conv_w, conv_b, rg_wa, rg_ba, rg_wx, rg_bx, rg_lambda, w_br_r, q_norm_g, w_uq,
                 kv_norm_g, w_uk, w_uv, w_br_m, w_o, ln1_g, ln1_b, w_up, b_up, w_down, b_down, ln2_g, ln2_b,
                 ln_in_g, ln_in_b):
    l = 0
    vec = lambda a: a.reshape(1, -1).astype(F32)
    wi = w_in[l]
    w_uq_p = jnp.pad(w_uq[l], ((0, 0), (0, 0), (0, HEAD_PAD - QK_NOPE - QK_ROPE))).reshape(Q_LORA, N_HEADS * HEAD_PAD)
    return {
        "ln_in_g": vec(ln_in_g), "ln_in_b": vec(ln_in_b),
        "w_in": wi[:, :_C_KPE].astype(BF16),
        "w_kpe": jnp.pad(wi[:, _C_KPE:_C_GATE], ((0, 0), (0, ROPE_PAD - QK_ROPE))).astype(BF16),
        "w_gate": wi[:, _C_GATE:].astype(BF16), "b_gate": vec(b_gate[l]),
        "q_norm_g": vec(q_norm_g[l]), "w_uq": w_uq_p.astype(BF16),
        "kv_norm_g": vec(kv_norm_g[l]),
        "w_uk": w_uk[l].reshape(KV_LORA, N_HEADS * QK_NOPE).astype(BF16),
        "w_uv": w_uv[l].reshape(KV_LORA, N_HEADS * V_HEAD).astype(BF16),
        "w_uv_t": w_uv[l].reshape(KV_LORA, N_HEADS * V_HEAD).T.astype(BF16),
        "w_ukt": jnp.transpose(w_uk[l], (1, 2, 0)).astype(BF16),
        "w_uvh": jnp.transpose(w_uv[l], (1, 0, 2)).astype(BF16),
        "conv_w": conv_w[l].astype(F32), "conv_b": vec(conv_b[l]),
        "w_ax": jnp.concatenate([rg_wa[l], rg_wx[l]], axis=-1).astype(BF16),
        "rg_ba": vec(rg_ba[l]), "rg_bx": vec(rg_bx[l]), "rg_lambda": vec(rg_lambda[l]),
        "w_br_r": w_br_r[l].astype(BF16), "w_br_m": w_br_m[l].astype(BF16), "w_o": w_o[l].astype(BF16),
        "ln1_g": vec(ln1_g[l]), "ln1_b": vec(ln1_b[l]),
        "w_up": w_up[l].astype(BF16), "b_up": vec(b_up[l]),
        "w_down": w_down[l].astype(BF16), "b_down": vec(b_down[l]),
        "ln2_g": vec(ln2_g[l]), "ln2_b": vec(ln2_b[l]),
    }


def kernel(x_prompt, x_sample, cache_ckv, cache_kpe, page_table, state_conv, state_rglru, meta_tokens, ln_in_g, ln_in_b, w_in, b_gate, conv_w, conv_b, rg_wa, rg_ba, rg_wx, rg_bx, rg_lambda, w_br_r, q_norm_g, w_uq, kv_norm_g, w_uk, w_uv, w_br_m, w_o, ln1_g, ln1_b, w_up, b_up, w_down, b_down, ln2_g, ln2_b):
    assert w_in.shape[0] == DEPTH == 1
    bn, seq, _ = x_prompt.shape
    bd, ts, _ = x_sample.shape
    past_len = page_table.shape[1] * cache_ckv.shape[2]
    p = _prep_params(w_in, b_gate, conv_w, conv_b, rg_wa, rg_ba, rg_wx, rg_bx, rg_lambda, w_br_r, q_norm_g, w_uq,
                     kv_norm_g, w_uk, w_uv, w_br_m, w_o, ln1_g, ln1_b, w_up, b_up, w_down, b_down, ln2_g, ln2_b,
                     ln_in_g, ln_in_b)
    n_s = bd * ts

    x_small = jnp.concatenate([x_sample.reshape(n_s, D_MODEL), meta_tokens.astype(F32)], axis=0)
    pos_small = jnp.concatenate([jnp.tile(past_len + jnp.arange(ts), bd), jnp.arange(N_META)])
    n_small = n_s + N_META
    rx, grg, g_small, q, k, v, ckv, kpe = _in_proj(x_small, _rope_tables(pos_small), 1, p, n_small, False)

    rep = lambda a: jnp.broadcast_to(a[None], (SUBLANES,) + a.shape)
    y_m, conv_m, h_m = _rglru(rep(rx[n_s:]), rep(grg[n_s:]), jnp.zeros((SUBLANES, CONV_W - 1, D_RNN), F32),
                              jnp.zeros((SUBLANES, D_RNN), F32), p, N_META)
    k_meta, v_meta = k[n_s:], v[n_s:]
    o_m = _meta_attn(q[n_s:], k_meta, v_meta)

    y_s, conv_s, h_s = _rglru(rx[:n_s].reshape(bd, ts, D_RNN), grg[:n_s].reshape(bd, ts, D_RNN),
                              state_conv[0], state_rglru[0], p, ts)
    q_lat, q_pe = _absorb(q[:n_s], p["w_ukt"], bd)
    ckv_s = ckv[:n_s].reshape(bd, ts, KV_LORA)
    kpe_s = kpe[:n_s].reshape(bd, ts, QK_ROPE)
    pad_new = lambda a: jnp.pad(a, ((0, 0), (0, 2 * SUBLANES - ts), (0, 0)))

    n_p = bn * seq
    pos_p = N_META + jnp.arange(seq)
    tm = 256
    rx_p, grg_p, g_p, q_p, k_p, vt_p, ckv_p, kpe_p = _in_proj(x_prompt.reshape(n_p, D_MODEL), _rope_tables(pos_p),
                                                              seq // tm, p, tm, True)
    bcast = lambda a: jnp.broadcast_to(a[:1], (bn,) + a.shape[1:])
    y_p, conv_p, h_p = _rglru(rx_p.reshape(bn, seq, D_RNN), grg_p.reshape(bn, seq, D_RNN), bcast(conv_m), bcast(h_m),
                              p, 128)
    o_p = _flash(q_p.reshape(bn, seq, -1), k_p.reshape(bn, seq, -1), vt_p.reshape(bn, seq // tm, -1, tm),
                 k_meta, v_meta.T, 512, 4)
    out_p, o_lat = _post_with_paged_attn(
        x_prompt.reshape(n_p, D_MODEL), y_p.reshape(n_p, D_RNN), o_p.reshape(n_p, -1), g_p, p, tm,
        page_table, q_lat.reshape(bd, N_HEADS * ts, KV_LORA), q_pe.reshape(bd, N_HEADS * ts, ROPE_PAD),
        pad_new(ckv_s), pad_new(kpe_s), cache_ckv[0], jnp.swapaxes(cache_kpe[0], 1, 2))

    o_s = _unabsorb(o_lat.reshape(bd, N_HEADS, ts, KV_LORA), p["w_uvh"])
    y_small = jnp.concatenate([y_s.reshape(n_s, D_RNN), y_m[0]], axis=0)
    o_small = jnp.concatenate([o_s, o_m], axis=0)
    out_small = _post(x_small, y_small, o_small, g_small, p, n_small)

    meta_rows = lambda a: jnp.broadcast_to(a[n_s:][None], (bn, N_META, a.shape[-1]))
    y_prompt = out_p.reshape(bn, seq, D_MODEL)
    y_sample = out_small[:n_s].reshape(bd, ts, D_MODEL)
    prompt_ckv = jnp.concatenate([meta_rows(ckv), ckv_p.reshape(bn, seq, KV_LORA)], axis=1)[None]
    prompt_kpe = jnp.concatenate([meta_rows(kpe), kpe_p.reshape(bn, seq, QK_ROPE)], axis=1)[None]
    return (y_prompt, y_sample, prompt_ckv, prompt_kpe, conv_p[None], h_p[None],
            ckv_s[None], kpe_s[None], conv_s[None], h_s[None])
```

```python
import functools
import math

import jax
import jax.numpy as jnp
from jax import lax
from jax.experimental import pallas as pl
from jax.experimental.pallas import tpu as pltpu

F32 = jnp.float32
BF16 = jnp.bfloat16

D_MODEL = 1024
N_META = 16
D_RNN = D_MODEL
RG_BLOCKS = 8
RG_BLOCK = D_RNN // RG_BLOCKS
CONV_W = 4
RG_C = 8.0
N_HEADS = 8
QK_NOPE = 128
QK_ROPE = 64
V_HEAD = 128
KV_LORA = 512
Q_LORA = 768
ROPE_THETA = 10000.0
D_FF = 4 * D_MODEL
DEPTH = 1
DN_ALPHA = (2.0 * DEPTH) ** 0.25
EPS = 1e-5
SM_SCALE = (QK_NOPE + QK_ROPE) ** -0.5
Q_PRESCALE = SM_SCALE * math.log2(math.e)

LANES = 128
SUBLANES = 8
HEAD_PAD = 2 * LANES
ROPE_PAD = LANES
_C_RX, _C_RG, _C_CQ, _C_CKV, _C_KPE = 0, D_RNN, 2 * D_RNN, 2 * D_RNN + Q_LORA, 2 * D_RNN + Q_LORA + KV_LORA
_C_GATE = _C_KPE + QK_ROPE
VMEM_LIMIT = 56 * 1024 * 1024
TM_IN_PROJ = 256
TM_POST = 256
TT_SCAN = 128
TQ_FLASH = 512
FLASH_HEADS = 4
NEG = float(jnp.finfo(jnp.float32).min)


def _const_spec(shape):
    return pl.BlockSpec(shape, lambda *_: (0,) * len(shape), pipeline_mode=pl.Buffered(1))


def _layer_norm(x, g, b):
    mu = jnp.mean(x, -1, keepdims=True)
    xc = x - mu
    var = jnp.mean(xc * xc, -1, keepdims=True)
    return xc * lax.rsqrt(var + EPS) * g + b


def _rms_norm(x, g):
    return x * lax.rsqrt(jnp.mean(x * x, -1, keepdims=True) + EPS) * g


def _gelu_tanh(x):
    return 0.5 * x * (1.0 + jnp.tanh(math.sqrt(2.0 / math.pi) * (x + 0.044715 * (x * x * x))))


def _rope_block(y, cos, sin_lo, sin_hi):
    left = pltpu.roll(y, ROPE_PAD - QK_ROPE // 2, 1)
    right = pltpu.roll(y, QK_ROPE // 2, 1)
    return y * cos + left * sin_lo + right * sin_hi


def _dot(a, b):
    return jnp.dot(a, b, preferred_element_type=F32)


def _dot_nt(a, b):
    return lax.dot_general(a, b, (((1,), (1,)), ((), ())), preferred_element_type=F32)


CONV_TAIL = SUBLANES - (CONV_W - 1)


def _conv_tile(xe_ref, x, cw_ref, cb_ref):
    tt = x.shape[1]
    xe_ref[:, SUBLANES:, :] = x
    xc = cb_ref[...] + x * cw_ref[CONV_W - 1:CONV_W, :]
    for k in range(CONV_W - 1):
        xc = xc + xe_ref[:, CONV_TAIL + k:CONV_TAIL + k + tt, :] * cw_ref[k:k + 1, :]
    new_tail = xe_ref[:, tt + CONV_TAIL:tt + SUBLANES, :]
    xe_ref[:, CONV_TAIL:SUBLANES, :] = new_tail
    return xc, new_tail


def _decay_rate(lam_ref):
    z = -lam_ref[...]
    return RG_C * (jnp.maximum(z, 0.0) + jnp.log1p(jnp.exp(-jnp.abs(z))))


def _recurrence_inputs(xn, wax_n, ba_n, bx_n, rate_n):
    gates = _dot(xn.astype(BF16), wax_n)
    r = jax.nn.sigmoid(gates[:, :RG_BLOCK] + ba_n)
    ig = jax.nn.sigmoid(gates[:, RG_BLOCK:] + bx_n)
    neg_log_a = r * rate_n
    a = jnp.exp(-neg_log_a)
    u = jnp.sqrt(jnp.tanh(neg_log_a) * (a * a + 1.0)) * (ig * xn)
    return a, u


def _scan_slabs(a_ref, u_ref, hc_ref, nb, tt):
    pitch = tt + SUBLANES
    n_slab = D_RNN // LANES
    for grp in range(nb // SUBLANES):
        base = grp * SUBLANES * pitch
        rows = slice(grp * SUBLANES, (grp + 1) * SUBLANES)
        h0 = tuple(hc_ref[rows, n * LANES:(n + 1) * LANES] for n in range(n_slab))

        def step(t, hs, base=base):
            out = []
            for n in range(n_slab):
                idx = pl.ds(base + t, SUBLANES, stride=pitch)
                hn = a_ref[n, idx, :] * hs[n] + u_ref[n, idx, :]
                u_ref[n, idx, :] = hn
                out.append(hn)
            return tuple(out)

        hs = lax.fori_loop(0, tt, step, h0, unroll=min(tt, 8))
        for n in range(n_slab):
            hc_ref[rows, n * LANES:(n + 1) * LANES] = hs[n]


def _gated_branch_output(y_ref, h_ref, grg_ref, nb, tt):
    pitch = tt + SUBLANES
    for b in range(nb):
        for n in range(D_RNN // LANES):
            cs = slice(n * LANES, (n + 1) * LANES)
            y_ref[b, :, cs] = (h_ref[n, b * pitch:b * pitch + tt, :] * grg_ref[b, :, cs]).astype(BF16)


def _in_proj_kernel(x_ref, cos_ref, slo_ref, shi_ref, lng_ref, lnb_ref, w_in_ref, w_kpe_ref, w_gate_ref, bg_ref,
                    qg_ref, wuq_ref, kvg_ref, wuk_ref, wuv_ref,
                    rx_ref, grg_ref, g_ref, q_ref, k_ref, v_ref, ckv_ref, kpe_ref, *, v_transposed):
    h = _layer_norm(x_ref[...], lng_ref[...], lnb_ref[...])
    hb = h.astype(BF16)

    def proj(lo, hi):
        return _dot(hb, w_in_ref[:, lo:hi])

    cos, slo, shi = cos_ref[...], slo_ref[...], shi_ref[...]
    cqn = _rms_norm(proj(_C_CQ, _C_CKV), qg_ref[...])
    ckv = _rms_norm(proj(_C_CKV, _C_KPE), kvg_ref[...])
    kpe = _rope_block(_dot(hb, w_kpe_ref[...]), cos, slo, shi)
    ckv_ref[...] = ckv
    kpe_ref[...] = kpe[:, :QK_ROPE]
    ckvb = ckv.astype(BF16)
    kpeb = kpe.astype(BF16)

    rx_ref[...] = proj(_C_RX, _C_RG)
    grg_ref[...] = _gelu_tanh(proj(_C_RG, _C_CQ))
    g_ref[...] = jax.nn.sigmoid(_dot(hb, w_gate_ref[...]) + bg_ref[...])

    q = _dot(cqn.astype(BF16), wuq_ref[...]) * Q_PRESCALE
    for hh in range(N_HEADS):
        c0 = hh * HEAD_PAD
        q_ref[:, c0:c0 + QK_NOPE] = q[:, c0:c0 + QK_NOPE].astype(BF16)
        q_ref[:, c0 + QK_NOPE:c0 + HEAD_PAD] = _rope_block(q[:, c0 + QK_NOPE:c0 + HEAD_PAD], cos, slo, shi).astype(BF16)

    kn = _dot(ckvb, wuk_ref[...])
    if v_transposed:
        v_ref[...] = _dot_nt(wuv_ref[...], ckvb).astype(BF16)
    else:
        v_ref[...] = _dot(ckvb, wuv_ref[...]).astype(BF16)
    for hh in range(N_HEADS):
        c0 = hh * HEAD_PAD
        k_ref[:, c0:c0 + QK_NOPE] = kn[:, hh * QK_NOPE:(hh + 1) * QK_NOPE].astype(BF16)
        k_ref[:, c0 + QK_NOPE:c0 + HEAD_PAD] = kpeb


def _in_proj(x, tabs, tab_blocks, p, tm, v_transposed):
    rows = x.shape[0]
    assert rows % tm == 0
    n_tiles = rows // tm
    row = lambda w: pl.BlockSpec((tm, w), lambda i: (i, 0))
    tab = pl.BlockSpec((tm, ROPE_PAD), lambda i: (i % tab_blocks, 0))
    outs = [
        (D_RNN, F32),
        (D_RNN, F32),
        (2 * D_MODEL, F32),
        (N_HEADS * HEAD_PAD, BF16),
        (N_HEADS * HEAD_PAD, BF16),
        (N_HEADS * V_HEAD, BF16),
        (KV_LORA, F32),
        (QK_ROPE, F32),
    ]
    out_specs = [row(w) for w, _ in outs]
    out_shape = [jax.ShapeDtypeStruct((rows, w), dt) for w, dt in outs]
    if v_transposed:
        out_specs[5] = pl.BlockSpec((None, N_HEADS * V_HEAD, tm), lambda i: (i, 0, 0))
        out_shape[5] = jax.ShapeDtypeStruct((n_tiles, N_HEADS * V_HEAD, tm), BF16)
    in_specs = [row(D_MODEL), tab, tab, tab,
                _const_spec((1, D_MODEL)), _const_spec((1, D_MODEL)),
                _const_spec((D_MODEL, _C_KPE)), _const_spec((D_MODEL, ROPE_PAD)),
                _const_spec((D_MODEL, 2 * D_MODEL)), _const_spec((1, 2 * D_MODEL)),
                _const_spec((1, Q_LORA)), _const_spec((Q_LORA, N_HEADS * HEAD_PAD)),
                _const_spec((1, KV_LORA)), _const_spec((KV_LORA, N_HEADS * QK_NOPE)),
                _const_spec((N_HEADS * V_HEAD, KV_LORA) if v_transposed else (KV_LORA, N_HEADS * V_HEAD))]
    args = [x, *tabs, p["ln_in_g"], p["ln_in_b"], p["w_in"], p["w_kpe"], p["w_gate"], p["b_gate"], p["q_norm_g"],
            p["w_uq"], p["kv_norm_g"], p["w_uk"], p["w_uv_t"] if v_transposed else p["w_uv"]]
    return pl.pallas_call(
        functools.partial(_in_proj_kernel, v_transposed=v_transposed),
        grid=(n_tiles,),
        in_specs=in_specs,
        out_specs=out_specs,
        out_shape=out_shape,
        compiler_params=pltpu.CompilerParams(dimension_semantics=("arbitrary",), vmem_limit_bytes=VMEM_LIMIT),
        name="in_proj",
    )(*args)


def _rglru_kernel(rx_ref, grg_ref, cprev_ref, hprev_ref, cw_ref, cb_ref, wax_ref, ba_ref, bx_ref, lam_ref,
                  y_ref, cout_ref, hout_ref, xe_ref, a_ref, u_ref, hc_ref, *, nb, tt):
    pitch = tt + SUBLANES

    @pl.when(pl.program_id(0) == 0)
    def _():
        xe_ref[:, CONV_TAIL:SUBLANES, :] = cprev_ref[...]
        hc_ref[...] = hprev_ref[...]

    xc, new_tail = _conv_tile(xe_ref, rx_ref[...], cw_ref, cb_ref)
    cout_ref[...] = new_tail
    rate = _decay_rate(lam_ref)
    xc2 = xc.reshape(nb * tt, D_RNN)
    for n in range(D_RNN // LANES):
        cs = slice(n * LANES, (n + 1) * LANES)
        a, u = _recurrence_inputs(xc2[:, cs], wax_ref[n], ba_ref[:, cs], bx_ref[:, cs], rate[:, cs])
        for b in range(nb):
            a_ref[n, b * pitch:b * pitch + tt, :] = a[b * tt:(b + 1) * tt]
            u_ref[n, b * pitch:b * pitch + tt, :] = u[b * tt:(b + 1) * tt]
    _scan_slabs(a_ref, u_ref, hc_ref, nb, tt)
    hout_ref[...] = hc_ref[...]
    _gated_branch_output(y_ref, u_ref, grg_ref, nb, tt)


def _rglru(rx, grg, conv_prev, h_prev, p, tt):
    nb, t_len, _ = rx.shape
    assert nb % SUBLANES == 0 and t_len % tt == 0 and tt % SUBLANES == 0
    pitch = tt + SUBLANES
    seq = pl.BlockSpec((nb, tt, D_RNN), lambda i: (0, i, 0))
    return pl.pallas_call(
        functools.partial(_rglru_kernel, nb=nb, tt=tt),
        grid=(t_len // tt,),
        in_specs=[seq, seq, _const_spec((nb, CONV_W - 1, D_RNN)), _const_spec((nb, D_RNN)),
                  _const_spec((CONV_W, D_RNN)), _const_spec((1, D_RNN)),
                  _const_spec((RG_BLOCKS, RG_BLOCK, 2 * RG_BLOCK)),
                  _const_spec((1, D_RNN)), _const_spec((1, D_RNN)), _const_spec((1, D_RNN))],
        out_specs=[seq,
                   pl.BlockSpec((nb, CONV_W - 1, D_RNN), lambda i: (0, 0, 0)),
                   pl.BlockSpec((nb, D_RNN), lambda i: (0, 0))],
        out_shape=[jax.ShapeDtypeStruct((nb, t_len, D_RNN), BF16),
                   jax.ShapeDtypeStruct((nb, CONV_W - 1, D_RNN), F32),
                   jax.ShapeDtypeStruct((nb, D_RNN), F32)],
        scratch_shapes=[pltpu.VMEM((nb, tt + SUBLANES, D_RNN), F32),
                        pltpu.VMEM((D_RNN // LANES, nb * pitch, LANES), F32),
                        pltpu.VMEM((D_RNN // LANES, nb * pitch, LANES), F32),
                        pltpu.VMEM((nb, D_RNN), F32)],
        compiler_params=pltpu.CompilerParams(dimension_semantics=("arbitrary",), vmem_limit_bytes=VMEM_LIMIT),
        name="rglru",
    )(rx, grg, conv_prev, h_prev, p["conv_w"], p["conv_b"], p["w_ax"], p["rg_ba"], p["rg_bx"], p["rg_lambda"])


def _softmax_update(s, m, l, acc, v):
    m_new = jnp.maximum(m, jnp.max(s, axis=1, keepdims=True))
    alpha = jnp.exp2(m - m_new)
    pr = jnp.exp2(s - m_new)
    l = alpha * l + jnp.sum(pr, axis=1, keepdims=True)
    acc = alpha * acc + _dot(pr.astype(BF16), v)
    return m_new, l, acc


def _softmax_update_t(s_t, m, l, acc_t, v_t):
    m_new = jnp.maximum(m, jnp.max(s_t, axis=0, keepdims=True))
    alpha = jnp.exp2(m - m_new)
    pr = jnp.exp2(s_t - m_new)
    l = alpha * l + jnp.sum(pr, axis=0, keepdims=True)
    acc_t = alpha * acc_t + _dot(v_t, pr.astype(BF16))
    return m_new, l, acc_t


def _flash_kernel(q_ref, k_ref, vt_ref, km_ref, vmt_ref, o_ref, *, tq, tv, hps):
    qcols = [slice(hh * HEAD_PAD, (hh + 1) * HEAD_PAD) for hh in range(hps)]
    vrows = [slice(hh * V_HEAD, (hh + 1) * V_HEAD) for hh in range(hps)]

    def scores(j, hh):
        return _dot_nt(k_ref[j * tq:(j + 1) * tq, qcols[hh]], q_ref[:, qcols[hh]])

    def values_t(j, hh):
        return jnp.concatenate([vt_ref[j * (tq // tv) + c, vrows[hh], :] for c in range(tq // tv)], axis=1)

    def tile(n_full):
        kpos = lax.broadcasted_iota(jnp.int32, (tq, tq), 0)
        qpos = lax.broadcasted_iota(jnp.int32, (tq, tq), 1)
        state, s_next = [], []
        for hh in range(hps):
            s_t = _dot_nt(km_ref[:, qcols[hh]], q_ref[:, qcols[hh]])
            m = jnp.max(s_t, axis=0, keepdims=True)
            pr = jnp.exp2(s_t - m)
            state.append((m, jnp.sum(pr, axis=0, keepdims=True), _dot(vmt_ref[vrows[hh], :], pr.astype(BF16))))
            s_next.append(scores(0, hh))
        for j in range(n_full + 1):
            for hh in range(hps):
                s_t = s_next[hh]
                if j < n_full:
                    s_next[hh] = scores(j + 1, hh)
                else:
                    s_t = jnp.where(kpos <= qpos, s_t, NEG)
                state[hh] = _softmax_update_t(s_t, *state[hh], values_t(j, hh))
        for hh in range(hps):
            _, l, acc_t = state[hh]
            o_ref[:, vrows[hh]] = jnp.transpose(acc_t / l).astype(BF16)

    for c in range(k_ref.shape[0] // tq):
        pl.when(pl.program_id(2) == c)(functools.partial(tile, c))


def _flash(q, k, v_t, k_pre, v_pre_t, tq, hps):
    nb, t_len, _ = q.shape
    tv = v_t.shape[3]
    assert t_len % tq == 0 and tq % tv == 0 and N_HEADS % hps == 0
    return pl.pallas_call(
        functools.partial(_flash_kernel, tq=tq, tv=tv, hps=hps),
        grid=(nb, N_HEADS // hps, t_len // tq),
        in_specs=[pl.BlockSpec((None, tq, hps * HEAD_PAD), lambda b, h, i: (b, i, h)),
                  pl.BlockSpec((None, t_len, hps * HEAD_PAD), lambda b, h, i: (b, 0, h)),
                  pl.BlockSpec((None, t_len // tv, hps * V_HEAD, tv), lambda b, h, i: (b, 0, h, 0)),
                  pl.BlockSpec((N_META, hps * HEAD_PAD), lambda b, h, i: (0, h)),
                  pl.BlockSpec((hps * V_HEAD, N_META), lambda b, h, i: (h, 0))],
        out_specs=pl.BlockSpec((None, tq, hps * V_HEAD), lambda b, h, i: (b, i, h)),
        out_shape=jax.ShapeDtypeStruct((nb, t_len, N_HEADS * V_HEAD), BF16),
        compiler_params=pltpu.CompilerParams(dimension_semantics=("arbitrary",) * 3, vmem_limit_bytes=VMEM_LIMIT),
        name="flash_prompt",
    )(q, k, v_t, k_pre, v_pre_t)


def _meta_attn_kernel(q_ref, k_ref, v_ref, o_ref):
    qpos = lax.broadcasted_iota(jnp.int32, (N_META, N_META), 0)
    kpos = lax.broadcasted_iota(jnp.int32, (N_META, N_META), 1)
    for hh in range(N_HEADS):
        qs = slice(hh * HEAD_PAD, (hh + 1) * HEAD_PAD)
        vs = slice(hh * V_HEAD, (hh + 1) * V_HEAD)
        s = jnp.where(kpos <= qpos, _dot_nt(q_ref[:, qs], k_ref[:, qs]), NEG)
        pr = jnp.exp2(s - jnp.max(s, axis=1, keepdims=True))
        o = _dot(pr.astype(BF16), v_ref[:, vs]) / jnp.sum(pr, axis=1, keepdims=True)
        o_ref[:, vs] = o.astype(BF16)


def _meta_attn(q, k, v):
    return pl.pallas_call(
        _meta_attn_kernel,
        out_shape=jax.ShapeDtypeStruct((N_META, N_HEADS * V_HEAD), BF16),
        name="meta_attn",
    )(q, k, v)


def _absorb_kernel(q_ref, wukt_ref, qlat_ref, qpe_ref):
    q = q_ref[...]
    nb = qlat_ref.shape[0]
    qlat = _dot(q[:, :QK_NOPE], wukt_ref[...])
    qlat_ref[...] = qlat.reshape(nb, -1, KV_LORA)
    qpe_ref[...] = q[:, QK_NOPE:].astype(F32).reshape(nb, -1, ROPE_PAD)


def _absorb(q, w_ukt, nb):
    rows = q.shape[0]
    ts = rows // nb
    return pl.pallas_call(
        _absorb_kernel,
        grid=(N_HEADS,),
        in_specs=[pl.BlockSpec((rows, HEAD_PAD), lambda h: (0, h)),
                  pl.BlockSpec((None, QK_NOPE, KV_LORA), lambda h: (h, 0, 0))],
        out_specs=[pl.BlockSpec((nb, None, ts, KV_LORA), lambda h: (0, h, 0, 0)),
                   pl.BlockSpec((nb, None, ts, ROPE_PAD), lambda h: (0, h, 0, 0))],
        out_shape=[jax.ShapeDtypeStruct((nb, N_HEADS, ts, KV_LORA), F32),
                   jax.ShapeDtypeStruct((nb, N_HEADS, ts, ROPE_PAD), F32)],
        compiler_params=pltpu.CompilerParams(dimension_semantics=("arbitrary",)),
        name="absorb_q",
    )(q, w_ukt)


PAGED_BUFS = 2


def _paged_scratch(n_q, pps, page):
    return [pltpu.VMEM((n_q, 1), F32), pltpu.VMEM((n_q, 1), F32), pltpu.VMEM((n_q, KV_LORA), F32),
            pltpu.VMEM((PAGED_BUFS, pps * page, KV_LORA), F32), pltpu.VMEM((PAGED_BUFS, pps, QK_ROPE, page), F32),
            pltpu.VMEM((pps * page, KV_LORA), BF16), pltpu.VMEM((QK_ROPE, pps * page), BF16),
            pltpu.SemaphoreType.DMA((PAGED_BUFS,))]


def _paged_copies(pt_ref, ckv_hbm, kpe_hbm, cin_ref, kin_ref, sem, seq, chunk, slot, pps):
    page = kin_ref.shape[3]
    out = []
    for i in range(pps):
        pid = pt_ref[seq, chunk * pps + i]
        out.append(pltpu.make_async_copy(ckv_hbm.at[pid], cin_ref.at[slot, pl.ds(i * page, page)], sem.at[slot]))
        out.append(pltpu.make_async_copy(kpe_hbm.at[pid], kin_ref.at[slot, i], sem.at[slot]))
    return out


def _paged_new_tokens(qlat, qpe, cnew_ref, knew_ref, m_ref, l_ref, acc_ref, ts):
    cn = cnew_ref[...].astype(BF16)
    kn = knew_ref[...].astype(BF16)
    s = _dot_nt(qlat, cn) + _dot_nt(qpe, kn)
    t_q = lax.broadcasted_iota(jnp.int32, s.shape, 0) % ts
    t_k = lax.broadcasted_iota(jnp.int32, s.shape, 1)
    s = jnp.where(t_k <= t_q, s, NEG)
    m = jnp.max(s, axis=1, keepdims=True)
    pr = jnp.exp2(s - m)
    m_ref[...] = m
    l_ref[...] = jnp.sum(pr, axis=1, keepdims=True)
    acc_ref[...] = _dot(pr.astype(BF16), cn)


def _paged_scores(qlat, qpe, cin_ref, kin_ref, cbuf_ref, kbuf_ref, slot, pps):
    page = kin_ref.shape[3]
    for i in range(pps):
        rows = pl.ds(i * page, page)
        cbuf_ref[rows, :] = cin_ref[slot, rows, :].astype(BF16)
        kbuf_ref[:, i * page:(i + 1) * page] = kin_ref[slot, i].astype(BF16)
    return _dot_nt(qlat, cbuf_ref[...]) + _dot(qpe, kbuf_ref[...])


def _paged_probs(s, m_ref, l_ref):
    m = m_ref[...]
    m_new = jnp.maximum(m, jnp.max(s, axis=1, keepdims=True))
    alpha = jnp.exp2(m - m_new)
    pr = jnp.exp2(s - m_new)
    m_ref[...] = m_new
    l_ref[...] = alpha * l_ref[...] + jnp.sum(pr, axis=1, keepdims=True)
    return alpha, pr.astype(BF16)


def _paged_accumulate(alpha, pr, cbuf_ref, acc_ref):
    acc_ref[...] = alpha * acc_ref[...] + _dot(pr, cbuf_ref[...])


def _unabsorb_kernel(olat_ref, wuv_ref, o_ref):
    x = olat_ref[...]
    x = x.reshape(x.shape[0] * x.shape[1], KV_LORA).astype(BF16)
    o_ref[...] = _dot(x, wuv_ref[...]).astype(BF16)


def _unabsorb(o_lat, w_uvh):
    nb, _, ts, _ = o_lat.shape
    return pl.pallas_call(
        _unabsorb_kernel,
        grid=(N_HEADS,),
        in_specs=[pl.BlockSpec((nb, None, ts, KV_LORA), lambda h: (0, h, 0, 0)),
                  pl.BlockSpec((None, KV_LORA, V_HEAD), lambda h: (h, 0, 0))],
        out_specs=pl.BlockSpec((nb * ts, V_HEAD), lambda h: (0, h)),
        out_shape=jax.ShapeDtypeStruct((nb * ts, N_HEADS * V_HEAD), BF16),
        compiler_params=pltpu.CompilerParams(dimension_semantics=("arbitrary",)),
        name="unabsorb_o",
    )(o_lat, w_uvh)


FF_CHUNKS = 4
N_POST_WEIGHTS = 13


def _post_compute(x_ref, y_ref, o_ref, g_ref, weights, out_ref, side_work=None):
    before_up, before_down, after_down = side_work if side_work is not None else (lambda c: None,) * 3
    lng_ref, lnb_ref, wr_ref, wm_ref, wo_ref, g1_ref, b1_ref, wup_ref, bup_ref, wdn_ref, bdn_ref, g2_ref, b2_ref = weights
    h = _layer_norm(x_ref[...], lng_ref[...], lnb_ref[...])
    z_r = _dot(y_ref[...], wr_ref[...])
    z_m = _dot(o_ref[...], wm_ref[...])
    mix_in = g_ref[:, :D_MODEL] * z_r + g_ref[:, D_MODEL:] * z_m
    mix = _dot(mix_in.astype(BF16), wo_ref[...])
    before_up(0)
    x1 = _layer_norm(DN_ALPHA * h + mix, g1_ref[...], b1_ref[...])
    x1b = x1.astype(BF16)
    f = bdn_ref[...]
    ff_chunk = D_FF // FF_CHUNKS
    for c in range(FF_CHUNKS):
        cs = slice(c * ff_chunk, (c + 1) * ff_chunk)
        if c > 0:
            before_up(c)
        up = jnp.maximum(_dot(x1b, wup_ref[:, cs]) + bup_ref[:, cs], 0.0)
        before_down(c)
        f = f + _dot((up * up).astype(BF16), wdn_ref[cs, :])
        after_down(c)
    out_ref[...] = _layer_norm(DN_ALPHA * x1 + f, g2_ref[...], b2_ref[...])


def _post_kernel(x_ref, y_ref, o_ref, g_ref, *rest):
    _post_compute(x_ref, y_ref, o_ref, g_ref, rest[:N_POST_WEIGHTS], rest[N_POST_WEIGHTS])


def _post_paged_kernel(pt_ref, x_ref, y_ref, o_ref, g_ref, *rest, pps, ts, n_j, n_steps, tiles_per_seq):
    weights = rest[:N_POST_WEIGHTS]
    (qlat_ref, qpe_ref, cnew_ref, knew_ref, ckv_hbm, kpe_hbm, lat_hbm, latm_hbm, out_ref, olat_ref, platent_hbm,
     m_ref, l_ref, acc_ref, cin_ref, kin_ref, cbuf_ref, kbuf_ref, sem, lsem) = rest[N_POST_WEIGHTS:]
    s = pl.program_id(0)
    spb = n_j // FF_CHUNKS
    seq, j0 = lax.div(s, spb), lax.rem(s, spb) * FF_CHUNKS
    copies = functools.partial(_paged_copies, pt_ref, ckv_hbm, kpe_hbm, cin_ref, kin_ref, sem, pps=pps)

    tm = out_ref.shape[0]
    lat_rows = pltpu.make_async_copy(
        lat_hbm.at[pl.ds(s * tm, tm)],
        platent_hbm.at[lax.div(s, tiles_per_seq), pl.ds(N_META + lax.rem(s, tiles_per_seq) * tm, tm)], lsem.at[0])
    lat_meta = [pltpu.make_async_copy(latm_hbm, platent_hbm.at[b, pl.ds(0, N_META)], lsem.at[1])
                for b in range(platent_hbm.shape[0])]
    lat_rows.start()

    @pl.when(s == 0)
    def _():
        for cp in lat_meta:
            cp.start()
        for cp in copies(0, 0, 0):
            cp.start()

    live = {}

    def gather_and_score(c):
        slot, nxt = c % PAGED_BUFS, (c + 1) % PAGED_BUFS
        if c + 1 < FF_CHUNKS:
            for cp in copies(seq, j0 + c + 1, nxt):
                cp.start()
        else:
            @pl.when(s + 1 < n_steps)
            def _():
                for cp in copies(lax.div(s + 1, spb), lax.rem(s + 1, spb) * FF_CHUNKS, nxt):
                    cp.start()
        for cp in copies(seq, j0 + c, slot):
            cp.wait()
        qlat = qlat_ref[...].astype(BF16)
        qpe = qpe_ref[...][:, :QK_ROPE].astype(BF16)
        if c == 0:
            pl.when(j0 == 0)(functools.partial(_paged_new_tokens, qlat, qpe, cnew_ref, knew_ref,
                                               m_ref, l_ref, acc_ref, ts))
        live["s"] = _paged_scores(qlat, qpe, cin_ref, kin_ref, cbuf_ref, kbuf_ref, slot, pps)

    def probs(c):
        live["alpha"], live["pr"] = _paged_probs(live.pop("s"), m_ref, l_ref)

    def accumulate(c):
        _paged_accumulate(live.pop("alpha"), live.pop("pr"), cbuf_ref, acc_ref)
        if c == FF_CHUNKS - 1:
            @pl.when(j0 == n_j - FF_CHUNKS)
            def _():
                olat_ref[...] = acc_ref[...] / l_ref[...]

    _post_compute(x_ref, y_ref, o_ref, g_ref, weights, out_ref, side_work=(gather_and_score, probs, accumulate))

    lat_rows.wait()

    @pl.when(s == 0)
    def _():
        for cp in lat_meta:
            cp.wait()


def _post_weight_specs():
    vec = lambda w: _const_spec((1, w))
    sq = _const_spec((D_MODEL, D_MODEL))
    return [vec(D_MODEL), vec(D_MODEL), sq, sq, sq, vec(D_MODEL), vec(D_MODEL),
            _const_spec((D_MODEL, D_FF)), vec(D_FF), _const_spec((D_FF, D_MODEL)), vec(D_MODEL),
            vec(D_MODEL), vec(D_MODEL)]


def _post_weights(p):
    return (p["ln_in_g"], p["ln_in_b"], p["w_br_r"], p["w_br_m"], p["w_o"], p["ln1_g"], p["ln1_b"],
            p["w_up"], p["b_up"], p["w_down"], p["b_down"], p["ln2_g"], p["ln2_b"])


def _post(x, y, o, g, p, tm):
    rows = x.shape[0]
    assert rows % tm == 0
    row = lambda w: pl.BlockSpec((tm, w), lambda i: (i, 0))
    return pl.pallas_call(
        _post_kernel,
        grid=(rows // tm,),
        in_specs=[row(D_MODEL), row(D_RNN), row(N_HEADS * V_HEAD), row(2 * D_MODEL)] + _post_weight_specs(),
        out_specs=row(D_MODEL),
        out_shape=jax.ShapeDtypeStruct((rows, D_MODEL), F32),
        compiler_params=pltpu.CompilerParams(dimension_semantics=("arbitrary",), vmem_limit_bytes=VMEM_LIMIT),
        name="merge_mlp",
    )(x, y, o, g, *_post_weights(p))


def _post_with_paged_attn(x, y, o, g, p, tm, page_table, q_lat, q_pe, ckv_new, kpe_new, ckv_pool, kpe_pool_t,
                          latent, latent_meta, n_seq):
    rows = x.shape[0]
    n_steps = rows // tm
    assert n_steps % n_seq == 0
    nb, n_pages = page_table.shape
    n_q = q_lat.shape[1]
    page = ckv_pool.shape[1]
    n_new = ckv_new.shape[1]
    chunks = n_steps * FF_CHUNKS
    assert rows % tm == 0 and (nb * n_pages) % chunks == 0 and FF_CHUNKS % PAGED_BUFS == 0
    pps = nb * n_pages // chunks
    n_j = n_pages // pps
    assert n_pages % pps == 0 and n_j % FF_CHUNKS == 0
    spb = n_j // FF_CHUNKS
    row = lambda w: pl.BlockSpec((tm, w), lambda i, pt: (i, 0))
    per_seq = lambda r, w: pl.BlockSpec((None, r, w), lambda i, pt: (i // spb, 0, 0))
    grid_spec = pltpu.PrefetchScalarGridSpec(
        num_scalar_prefetch=1,
        grid=(n_steps,),
        in_specs=[row(D_MODEL), row(D_RNN), row(N_HEADS * V_HEAD), row(2 * D_MODEL)] + _post_weight_specs()
                 + [per_seq(n_q, KV_LORA), per_seq(n_q, ROPE_PAD), per_seq(n_new, KV_LORA), per_seq(n_new, QK_ROPE)]
                 + [pl.BlockSpec(memory_space=pl.ANY)] * 4,
        out_specs=[row(D_MODEL), per_seq(n_q, KV_LORA), pl.BlockSpec(memory_space=pl.ANY)],
        scratch_shapes=_paged_scratch(n_q, pps, page) + [pltpu.SemaphoreType.DMA((2,))],
    )
    return pl.pallas_call(
        functools.partial(_post_paged_kernel, pps=pps, ts=n_q // N_HEADS, n_j=n_j, n_steps=n_steps,
                          tiles_per_seq=n_steps // n_seq),
        grid_spec=grid_spec,
        out_shape=[jax.ShapeDtypeStruct((rows, D_MODEL), F32), jax.ShapeDtypeStruct((nb, n_q, KV_LORA), F32),
                   jax.ShapeDtypeStruct((n_seq, N_META + rows // n_seq, KV_LORA), F32)],
        compiler_params=pltpu.CompilerParams(dimension_semantics=("arbitrary",), vmem_limit_bytes=VMEM_LIMIT),
        name="merge_mlp_paged_attn",
    )(page_table, x, y, o, g, *_post_weights(p), q_lat, q_pe, ckv_new, kpe_new, ckv_pool, kpe_pool_t,
      latent, latent_meta)


def _rope_tables(pos):
    half = QK_ROPE // 2
    inv = 1.0 / (ROPE_THETA ** (jnp.arange(half, dtype=F32) / half))
    ang = pos.astype(F32)[:, None] * inv[None, :]
    cos, sin = jnp.cos(ang), jnp.sin(ang)
    z = jnp.zeros_like(cos)
    return (jnp.concatenate([cos, cos, z, z], 1), jnp.concatenate([-sin, z, z, z], 1),
            jnp.concatenate([z, sin, z, z], 1))


def _prep_params(w_in, b_gate, conv_w, conv_b, rg_wa, rg_ba, rg_wx, rg_bx, rg_lambda, w_br_r, q_norm_g, w_uq,
                 kv_norm_g, w_uk, w_uv, w_br_m, w_o, ln1_g, ln1_b, w_up, b_up, w_down, b_down, ln2_g, ln2_b,
                 ln_in_g, ln_in_b):
    l = 0
    vec = lambda a: a.reshape(1, -1).astype(F32)
    wi = w_in[l]
    w_uq_p = jnp.pad(w_uq[l], ((0, 0), (0, 0), (0, HEAD_PAD - QK_NOPE - QK_ROPE))).reshape(Q_LORA, N_HEADS * HEAD_PAD)
    return {
        "ln_in_g": vec(ln_in_g), "ln_in_b": vec(ln_in_b),
        "w_in": wi[:, :_C_KPE].astype(BF16),
        "w_kpe": jnp.pad(wi[:, _C_KPE:_C_GATE], ((0, 0), (0, ROPE_PAD - QK_ROPE))).astype(BF16),
        "w_gate": wi[:, _C_GATE:].astype(BF16), "b_gate": vec(b_gate[l]),
        "q_norm_g": vec(q_norm_g[l]), "w_uq": w_uq_p.astype(BF16),
        "kv_norm_g": vec(kv_norm_g[l]),
        "w_uk": w_uk[l].reshape(KV_LORA, N_HEADS * QK_NOPE).astype(BF16),
        "w_uv": w_uv[l].reshape(KV_LORA, N_HEADS * V_HEAD).astype(BF16),
        "w_uv_t": w_uv[l].reshape(KV_LORA, N_HEADS * V_HEAD).T.astype(BF16),
        "w_ukt": jnp.transpose(w_uk[l], (1, 2, 0)).astype(BF16),
        "w_uvh": jnp.transpose(w_uv[l], (1, 0, 2)).astype(BF16),
        "conv_w": conv_w[l].astype(F32), "conv_b": vec(conv_b[l]),
        "w_ax": jnp.concatenate([rg_wa[l], rg_wx[l]], axis=-1).astype(BF16),
        "rg_ba": vec(rg_ba[l]), "rg_bx": vec(rg_bx[l]), "rg_lambda": vec(rg_lambda[l]),
        "w_br_r": w_br_r[l].astype(BF16), "w_br_m": w_br_m[l].astype(BF16), "w_o": w_o[l].astype(BF16),
        "ln1_g": vec(ln1_g[l]), "ln1_b": vec(ln1_b[l]),
        "w_up": w_up[l].astype(BF16), "b_up": vec(b_up[l]),
        "w_down": w_down[l].astype(BF16), "b_down": vec(b_down[l]),
        "ln2_g": vec(ln2_g[l]), "ln2_b": vec(ln2_b[l]),
    }


def kernel(x_prompt, x_sample, cache_ckv, cache_kpe, page_table, state_conv, state_rglru, meta_tokens, ln_in_g, ln_in_b, w_in, b_gate, conv_w, conv_b, rg_wa, rg_ba, rg_wx, rg_bx, rg_lambda, w_br_r, q_norm_g, w_uq, kv_norm_g, w_uk, w_uv, w_br_m, w_o, ln1_g, ln1_b, w_up, b_up, w_down, b_down, ln2_g, ln2_b):
    assert w_in.shape[0] == DEPTH == 1
    bn, seq, _ = x_prompt.shape
    bd, ts, _ = x_sample.shape
    past_len = page_table.shape[1] * cache_ckv.shape[2]
    p = _prep_params(w_in, b_gate, conv_w, conv_b, rg_wa, rg_ba, rg_wx, rg_bx, rg_lambda, w_br_r, q_norm_g, w_uq,
                     kv_norm_g, w_uk, w_uv, w_br_m, w_o, ln1_g, ln1_b, w_up, b_up, w_down, b_down, ln2_g, ln2_b,
                     ln_in_g, ln_in_b)
    n_s = bd * ts

    x_small = jnp.concatenate([x_sample.reshape(n_s, D_MODEL), meta_tokens.astype(F32)], axis=0)
    pos_small = jnp.concatenate([jnp.tile(past_len + jnp.arange(ts), bd), jnp.arange(N_META)])
    n_small = n_s + N_META
    rx, grg, g_small, q, k, v, ckv, kpe = _in_proj(x_small, _rope_tables(pos_small), 1, p, n_small, False)

    rep = lambda a: jnp.broadcast_to(a[None], (SUBLANES,) + a.shape)
    y_m, conv_m, h_m = _rglru(rep(rx[n_s:]), rep(grg[n_s:]), jnp.zeros((SUBLANES, CONV_W - 1, D_RNN), F32),
                              jnp.zeros((SUBLANES, D_RNN), F32), p, N_META)
    k_meta, v_meta = k[n_s:], v[n_s:]
    o_m = _meta_attn(q[n_s:], k_meta, v_meta)

    y_s, conv_s, h_s = _rglru(rx[:n_s].reshape(bd, ts, D_RNN), grg[:n_s].reshape(bd, ts, D_RNN),
                              state_conv[0], state_rglru[0], p, ts)
    q_lat, q_pe = _absorb(q[:n_s], p["w_ukt"], bd)
    ckv_s = ckv[:n_s].reshape(bd, ts, KV_LORA)
    kpe_s = kpe[:n_s].reshape(bd, ts, QK_ROPE)
    pad_new = lambda a: jnp.pad(a, ((0, 0), (0, 2 * SUBLANES - ts), (0, 0)))

    n_p = bn * seq
    pos_p = N_META + jnp.arange(seq)
    tm = TM_IN_PROJ
    bcast = lambda a: jnp.broadcast_to(a[:1], (bn,) + a.shape[1:])
    rx_p, grg_p, g_p, q_p, k_p, vt_p, ckv_p, kpe_p = _in_proj(x_prompt.reshape(n_p, D_MODEL), _rope_tables(pos_p),
                                                              seq // tm, p, tm, True)
    y_p, conv_p, h_p = _rglru(rx_p.reshape(bn, seq, D_RNN), grg_p.reshape(bn, seq, D_RNN), bcast(conv_m), bcast(h_m),
                              p, TT_SCAN)
    o_p = _flash(q_p.reshape(bn, seq, -1), k_p.reshape(bn, seq, -1), vt_p.reshape(bn, seq // tm, -1, tm),
                 k_meta, v_meta.T, TQ_FLASH, FLASH_HEADS)
    out_p, o_lat, prompt_ckv = _post_with_paged_attn(
        x_prompt.reshape(n_p, D_MODEL), y_p.reshape(n_p, D_RNN), o_p.reshape(n_p, -1), g_p, p, TM_POST,
        page_table, q_lat.reshape(bd, N_HEADS * ts, KV_LORA), q_pe.reshape(bd, N_HEADS * ts, ROPE_PAD),
        pad_new(ckv_s), pad_new(kpe_s), cache_ckv[0], jnp.swapaxes(cache_kpe[0], 1, 2),
        ckv_p, ckv[n_s:], bn)

    o_s = _unabsorb(o_lat.reshape(bd, N_HEADS, ts, KV_LORA), p["w_uvh"])
    y_small = jnp.concatenate([y_s.reshape(n_s, D_RNN), y_m[0]], axis=0)
    o_small = jnp.concatenate([o_s, o_m], axis=0)
    out_small = _post(x_small, y_small, o_small, g_small, p, n_small)

    meta_rows = lambda a: jnp.broadcast_to(a[n_s:][None], (bn, N_META, a.shape[-1]))
    y_prompt = out_p.reshape(bn, seq, D_MODEL)
    y_sample = out_small[:n_s].reshape(bd, ts, D_MODEL)
    prompt_kpe = jnp.concatenate([meta_rows(kpe), kpe_p.reshape(bn, seq, QK_ROPE)], axis=1)[None]
    return (y_prompt, y_sample, prompt_ckv[None], prompt_kpe, conv_p[None], h_p[None],
            ckv_s[None], kpe_s[None], conv_s[None], h_s[None])
```

```python
import functools
import math

import jax
import jax.numpy as jnp
from jax import lax
from jax.experimental import pallas as pl
from jax.experimental.pallas import tpu as pltpu

F32 = jnp.float32
BF16 = jnp.bfloat16

D_MODEL = 1024
N_META = 16
D_RNN = D_MODEL
RG_BLOCKS = 8
RG_BLOCK = D_RNN // RG_BLOCKS
CONV_W = 4
RG_C = 8.0
N_HEADS = 8
QK_NOPE = 128
QK_ROPE = 64
V_HEAD = 128
KV_LORA = 512
Q_LORA = 768
ROPE_THETA = 10000.0
D_FF = 4 * D_MODEL
DEPTH = 1
DN_ALPHA = (2.0 * DEPTH) ** 0.25
EPS = 1e-5
SM_SCALE = (QK_NOPE + QK_ROPE) ** -0.5
Q_PRESCALE = SM_SCALE * math.log2(math.e)

LANES = 128
SUBLANES = 8
HEAD_PAD = 2 * LANES
ROPE_PAD = LANES
_C_RX, _C_RG, _C_CQ, _C_CKV, _C_KPE = 0, D_RNN, 2 * D_RNN, 2 * D_RNN + Q_LORA, 2 * D_RNN + Q_LORA + KV_LORA
_C_GATE = _C_KPE + QK_ROPE
VMEM_LIMIT = 56 * 1024 * 1024
TM_IN_PROJ = 256
TM_POST = 256
TT_SCAN = 128
TQ_FLASH = 512
FLASH_HEADS = 4
NEG = float(jnp.finfo(jnp.float32).min)


def _const_spec(shape):
    return pl.BlockSpec(shape, lambda *_: (0,) * len(shape), pipeline_mode=pl.Buffered(1))


def _layer_norm(x, g, b):
    mu = jnp.mean(x, -1, keepdims=True)
    xc = x - mu
    var = jnp.mean(xc * xc, -1, keepdims=True)
    return xc * lax.rsqrt(var + EPS) * g + b


def _rms_norm(x, g):
    return x * lax.rsqrt(jnp.mean(x * x, -1, keepdims=True) + EPS) * g


def _gelu_tanh(x):
    return 0.5 * x * (1.0 + jnp.tanh(math.sqrt(2.0 / math.pi) * (x + 0.044715 * (x * x * x))))


def _rope_block(y, cos, sin_lo, sin_hi):
    left = pltpu.roll(y, ROPE_PAD - QK_ROPE // 2, 1)
    right = pltpu.roll(y, QK_ROPE // 2, 1)
    return y * cos + left * sin_lo + right * sin_hi


def _dot(a, b):
    return jnp.dot(a, b, preferred_element_type=F32)


def _dot_nt(a, b):
    return lax.dot_general(a, b, (((1,), (1,)), ((), ())), preferred_element_type=F32)


CONV_TAIL = SUBLANES - (CONV_W - 1)


def _conv_tile(xe_ref, x, cw_ref, cb_ref):
    tt = x.shape[1]
    xe_ref[:, SUBLANES:, :] = x
    xc = cb_ref[...] + x * cw_ref[CONV_W - 1:CONV_W, :]
    for k in range(CONV_W - 1):
        xc = xc + xe_ref[:, CONV_TAIL + k:CONV_TAIL + k + tt, :] * cw_ref[k:k + 1, :]
    new_tail = xe_ref[:, tt + CONV_TAIL:tt + SUBLANES, :]
    xe_ref[:, CONV_TAIL:SUBLANES, :] = new_tail
    return xc, new_tail


def _decay_rate(lam_ref):
    z = -lam_ref[...]
    return RG_C * (jnp.maximum(z, 0.0) + jnp.log1p(jnp.exp(-jnp.abs(z))))


def _recurrence_inputs(xn, wax_n, ba_n, bx_n, rate_n):
    gates = _dot(xn.astype(BF16), wax_n)
    r = jax.nn.sigmoid(gates[:, :RG_BLOCK] + ba_n)
    ig = jax.nn.sigmoid(gates[:, RG_BLOCK:] + bx_n)
    neg_log_a = r * rate_n
    a = jnp.exp(-neg_log_a)
    u = jnp.sqrt(jnp.tanh(neg_log_a) * (a * a + 1.0)) * (ig * xn)
    return a, u


def _scan_slabs(a_ref, u_ref, hc_ref, nb, tt):
    pitch = tt + SUBLANES
    n_slab = D_RNN // LANES
    for grp in range(nb // SUBLANES):
        base = grp * SUBLANES * pitch
        rows = slice(grp * SUBLANES, (grp + 1) * SUBLANES)
        h0 = tuple(hc_ref[rows, n * LANES:(n + 1) * LANES] for n in range(n_slab))

        def step(t, hs, base=base):
            out = []
            for n in range(n_slab):
                idx = pl.ds(base + t, SUBLANES, stride=pitch)
                hn = a_ref[n, idx, :] * hs[n] + u_ref[n, idx, :]
                u_ref[n, idx, :] = hn
                out.append(hn)
            return tuple(out)

        hs = lax.fori_loop(0, tt, step, h0, unroll=min(tt, 8))
        for n in range(n_slab):
            hc_ref[rows, n * LANES:(n + 1) * LANES] = hs[n]


def _gated_branch_output(y_ref, h_ref, grg_ref, nb, tt):
    pitch = tt + SUBLANES
    for b in range(nb):
        for n in range(D_RNN // LANES):
            cs = slice(n * LANES, (n + 1) * LANES)
            y_ref[b, :, cs] = (h_ref[n, b * pitch:b * pitch + tt, :] * grg_ref[b, :, cs]).astype(BF16)


def _in_proj_kernel(x_ref, cos_ref, slo_ref, shi_ref, lng_ref, lnb_ref, w_in_ref, w_kpe_ref, w_gate_ref, bg_ref,
                    qg_ref, wuq_ref, kvg_ref, wuk_ref, wuv_ref,
                    rx_ref, grg_ref, g_ref, q_ref, k_ref, v_ref, ckv_ref, kpe_ref, *, v_transposed):
    h = _layer_norm(x_ref[...], lng_ref[...], lnb_ref[...])
    hb = h.astype(BF16)

    def proj(lo, hi):
        return _dot(hb, w_in_ref[:, lo:hi])

    cos, slo, shi = cos_ref[...], slo_ref[...], shi_ref[...]
    cqn = _rms_norm(proj(_C_CQ, _C_CKV), qg_ref[...])
    ckv = _rms_norm(proj(_C_CKV, _C_KPE), kvg_ref[...])
    kpe = _rope_block(_dot(hb, w_kpe_ref[...]), cos, slo, shi)
    ckv_ref[...] = ckv
    kpe_ref[...] = kpe[:, :QK_ROPE]
    ckvb = ckv.astype(BF16)
    kpeb = kpe.astype(BF16)

    rx_ref[...] = proj(_C_RX, _C_RG)
    grg_ref[...] = _gelu_tanh(proj(_C_RG, _C_CQ))
    g_ref[...] = jax.nn.sigmoid(_dot(hb, w_gate_ref[...]) + bg_ref[...])

    q = _dot(cqn.astype(BF16), wuq_ref[...]) * Q_PRESCALE
    for hh in range(N_HEADS):
        c0 = hh * HEAD_PAD
        q_ref[:, c0:c0 + QK_NOPE] = q[:, c0:c0 + QK_NOPE].astype(BF16)
        q_ref[:, c0 + QK_NOPE:c0 + HEAD_PAD] = _rope_block(q[:, c0 + QK_NOPE:c0 + HEAD_PAD], cos, slo, shi).astype(BF16)

    kn = _dot(ckvb, wuk_ref[...])
    if v_transposed:
        v_ref[...] = _dot_nt(wuv_ref[...], ckvb).astype(BF16)
    else:
        v_ref[...] = _dot(ckvb, wuv_ref[...]).astype(BF16)
    for hh in range(N_HEADS):
        c0 = hh * HEAD_PAD
        k_ref[:, c0:c0 + QK_NOPE] = kn[:, hh * QK_NOPE:(hh + 1) * QK_NOPE].astype(BF16)
        k_ref[:, c0 + QK_NOPE:c0 + HEAD_PAD] = kpeb


def _in_proj(x, tabs, tab_blocks, p, tm, v_transposed):
    rows = x.shape[0]
    assert rows % tm == 0
    n_tiles = rows // tm
    row = lambda w: pl.BlockSpec((tm, w), lambda i: (i, 0))
    tab = pl.BlockSpec((tm, ROPE_PAD), lambda i: (i % tab_blocks, 0))
    outs = [
        (D_RNN, F32),
        (D_RNN, F32),
        (2 * D_MODEL, F32),
        (N_HEADS * HEAD_PAD, BF16),
        (N_HEADS * HEAD_PAD, BF16),
        (N_HEADS * V_HEAD, BF16),
        (KV_LORA, F32),
        (QK_ROPE, F32),
    ]
    out_specs = [row(w) for w, _ in outs]
    out_shape = [jax.ShapeDtypeStruct((rows, w), dt) for w, dt in outs]
    if v_transposed:
        out_specs[5] = pl.BlockSpec((None, N_HEADS * V_HEAD, tm), lambda i: (i, 0, 0))
        out_shape[5] = jax.ShapeDtypeStruct((n_tiles, N_HEADS * V_HEAD, tm), BF16)
    in_specs = [row(D_MODEL), tab, tab, tab,
                _const_spec((1, D_MODEL)), _const_spec((1, D_MODEL)),
                _const_spec((D_MODEL, _C_KPE)), _const_spec((D_MODEL, ROPE_PAD)),
                _const_spec((D_MODEL, 2 * D_MODEL)), _const_spec((1, 2 * D_MODEL)),
                _const_spec((1, Q_LORA)), _const_spec((Q_LORA, N_HEADS * HEAD_PAD)),
                _const_spec((1, KV_LORA)), _const_spec((KV_LORA, N_HEADS * QK_NOPE)),
                _const_spec((N_HEADS * V_HEAD, KV_LORA) if v_transposed else (KV_LORA, N_HEADS * V_HEAD))]
    args = [x, *tabs, p["ln_in_g"], p["ln_in_b"], p["w_in"], p["w_kpe"], p["w_gate"], p["b_gate"], p["q_norm_g"],
            p["w_uq"], p["kv_norm_g"], p["w_uk"], p["w_uv_t"] if v_transposed else p["w_uv"]]
    return pl.pallas_call(
        functools.partial(_in_proj_kernel, v_transposed=v_transposed),
        grid=(n_tiles,),
        in_specs=in_specs,
        out_specs=out_specs,
        out_shape=out_shape,
        compiler_params=pltpu.CompilerParams(dimension_semantics=("arbitrary",), vmem_limit_bytes=VMEM_LIMIT),
        name="in_proj",
    )(*args)


def _rglru_kernel(rx_ref, grg_ref, cprev_ref, hprev_ref, cw_ref, cb_ref, wax_ref, ba_ref, bx_ref, lam_ref,
                  y_ref, cout_ref, hout_ref, xe_ref, a_ref, u_ref, hc_ref, *, nb, tt):
    pitch = tt + SUBLANES

    @pl.when(pl.program_id(0) == 0)
    def _():
        xe_ref[:, CONV_TAIL:SUBLANES, :] = cprev_ref[...]
        hc_ref[...] = hprev_ref[...]

    xc, new_tail = _conv_tile(xe_ref, rx_ref[...], cw_ref, cb_ref)
    cout_ref[...] = new_tail
    rate = _decay_rate(lam_ref)
    xc2 = xc.reshape(nb * tt, D_RNN)
    for n in range(D_RNN // LANES):
        cs = slice(n * LANES, (n + 1) * LANES)
        a, u = _recurrence_inputs(xc2[:, cs], wax_ref[n], ba_ref[:, cs], bx_ref[:, cs], rate[:, cs])
        for b in range(nb):
            a_ref[n, b * pitch:b * pitch + tt, :] = a[b * tt:(b + 1) * tt]
            u_ref[n, b * pitch:b * pitch + tt, :] = u[b * tt:(b + 1) * tt]
    _scan_slabs(a_ref, u_ref, hc_ref, nb, tt)
    hout_ref[...] = hc_ref[...]
    _gated_branch_output(y_ref, u_ref, grg_ref, nb, tt)


def _rglru(rx, grg, conv_prev, h_prev, p, tt):
    nb, t_len, _ = rx.shape
    assert nb % SUBLANES == 0 and t_len % tt == 0 and tt % SUBLANES == 0
    pitch = tt + SUBLANES
    seq = pl.BlockSpec((nb, tt, D_RNN), lambda i: (0, i, 0))
    return pl.pallas_call(
        functools.partial(_rglru_kernel, nb=nb, tt=tt),
        grid=(t_len // tt,),
        in_specs=[seq, seq, _const_spec((nb, CONV_W - 1, D_RNN)), _const_spec((nb, D_RNN)),
                  _const_spec((CONV_W, D_RNN)), _const_spec((1, D_RNN)),
                  _const_spec((RG_BLOCKS, RG_BLOCK, 2 * RG_BLOCK)),
                  _const_spec((1, D_RNN)), _const_spec((1, D_RNN)), _const_spec((1, D_RNN))],
        out_specs=[seq,
                   pl.BlockSpec((nb, CONV_W - 1, D_RNN), lambda i: (0, 0, 0)),
                   pl.BlockSpec((nb, D_RNN), lambda i: (0, 0))],
        out_shape=[jax.ShapeDtypeStruct((nb, t_len, D_RNN), BF16),
                   jax.ShapeDtypeStruct((nb, CONV_W - 1, D_RNN), F32),
                   jax.ShapeDtypeStruct((nb, D_RNN), F32)],
        scratch_shapes=[pltpu.VMEM((nb, tt + SUBLANES, D_RNN), F32),
                        pltpu.VMEM((D_RNN // LANES, nb * pitch, LANES), F32),
                        pltpu.VMEM((D_RNN // LANES, nb * pitch, LANES), F32),
                        pltpu.VMEM((nb, D_RNN), F32)],
        compiler_params=pltpu.CompilerParams(dimension_semantics=("arbitrary",), vmem_limit_bytes=VMEM_LIMIT),
        name="rglru",
    )(rx, grg, conv_prev, h_prev, p["conv_w"], p["conv_b"], p["w_ax"], p["rg_ba"], p["rg_bx"], p["rg_lambda"])


def _softmax_update(s, m, l, acc, v):
    m_new = jnp.maximum(m, jnp.max(s, axis=1, keepdims=True))
    alpha = jnp.exp2(m - m_new)
    pr = jnp.exp2(s - m_new)
    l = alpha * l + jnp.sum(pr, axis=1, keepdims=True)
    acc = alpha * acc + _dot(pr.astype(BF16), v)
    return m_new, l, acc


def _softmax_update_t(s_t, m, l, acc_t, v_t):
    m_new = jnp.maximum(m, jnp.max(s_t, axis=0, keepdims=True))
    alpha = jnp.exp2(m - m_new)
    pr = jnp.exp2(s_t - m_new)
    l = alpha * l + jnp.sum(pr, axis=0, keepdims=True)
    acc_t = alpha * acc_t + _dot(v_t, pr.astype(BF16))
    return m_new, l, acc_t


def _flash_kernel(q_ref, k_ref, vt_ref, km_ref, vmt_ref, o_ref, *, tq, tv, hps):
    qcols = [slice(hh * HEAD_PAD, (hh + 1) * HEAD_PAD) for hh in range(hps)]
    vrows = [slice(hh * V_HEAD, (hh + 1) * V_HEAD) for hh in range(hps)]

    def scores(j, hh):
        return _dot_nt(k_ref[j * tq:(j + 1) * tq, qcols[hh]], q_ref[:, qcols[hh]])

    def values_t(j, hh):
        return jnp.concatenate([vt_ref[j * (tq // tv) + c, vrows[hh], :] for c in range(tq // tv)], axis=1)

    def tile(n_full):
        kpos = lax.broadcasted_iota(jnp.int32, (tq, tq), 0)
        qpos = lax.broadcasted_iota(jnp.int32, (tq, tq), 1)
        state, s_next = [], []
        for hh in range(hps):
            s_t = _dot_nt(km_ref[:, qcols[hh]], q_ref[:, qcols[hh]])
            m = jnp.max(s_t, axis=0, keepdims=True)
            pr = jnp.exp2(s_t - m)
            state.append((m, jnp.sum(pr, axis=0, keepdims=True), _dot(vmt_ref[vrows[hh], :], pr.astype(BF16))))
            s_next.append(scores(0, hh))
        for j in range(n_full + 1):
            for hh in range(hps):
                s_t = s_next[hh]
                if j < n_full:
                    s_next[hh] = scores(j + 1, hh)
                else:
                    s_t = jnp.where(kpos <= qpos, s_t, NEG)
                state[hh] = _softmax_update_t(s_t, *state[hh], values_t(j, hh))
        for hh in range(hps):
            _, l, acc_t = state[hh]
            o_ref[:, vrows[hh]] = jnp.transpose(acc_t / l).astype(BF16)

    for c in range(k_ref.shape[0] // tq):
        pl.when(pl.program_id(2) == c)(functools.partial(tile, c))


def _flash(q, k, v_t, k_pre, v_pre_t, tq, hps):
    nb, t_len, _ = q.shape
    tv = v_t.shape[3]
    assert t_len % tq == 0 and tq % tv == 0 and N_HEADS % hps == 0
    return pl.pallas_call(
        functools.partial(_flash_kernel, tq=tq, tv=tv, hps=hps),
        grid=(nb, N_HEADS // hps, t_len // tq),
        in_specs=[pl.BlockSpec((None, tq, hps * HEAD_PAD), lambda b, h, i: (b, i, h)),
                  pl.BlockSpec((None, t_len, hps * HEAD_PAD), lambda b, h, i: (b, 0, h)),
                  pl.BlockSpec((None, t_len // tv, hps * V_HEAD, tv), lambda b, h, i: (b, 0, h, 0)),
                  pl.BlockSpec((N_META, hps * HEAD_PAD), lambda b, h, i: (0, h)),
                  pl.BlockSpec((hps * V_HEAD, N_META), lambda b, h, i: (h, 0))],
        out_specs=pl.BlockSpec((None, tq, hps * V_HEAD), lambda b, h, i: (b, i, h)),
        out_shape=jax.ShapeDtypeStruct((nb, t_len, N_HEADS * V_HEAD), BF16),
        compiler_params=pltpu.CompilerParams(dimension_semantics=("arbitrary",) * 3, vmem_limit_bytes=VMEM_LIMIT),
        name="flash_prompt",
    )(q, k, v_t, k_pre, v_pre_t)


def _meta_attn_kernel(q_ref, k_ref, v_ref, o_ref):
    qpos = lax.broadcasted_iota(jnp.int32, (N_META, N_META), 0)
    kpos = lax.broadcasted_iota(jnp.int32, (N_META, N_META), 1)
    for hh in range(N_HEADS):
        qs = slice(hh * HEAD_PAD, (hh + 1) * HEAD_PAD)
        vs = slice(hh * V_HEAD, (hh + 1) * V_HEAD)
        s = jnp.where(kpos <= qpos, _dot_nt(q_ref[:, qs], k_ref[:, qs]), NEG)
        pr = jnp.exp2(s - jnp.max(s, axis=1, keepdims=True))
        o = _dot(pr.astype(BF16), v_ref[:, vs]) / jnp.sum(pr, axis=1, keepdims=True)
        o_ref[:, vs] = o.astype(BF16)


def _meta_attn(q, k, v):
    return pl.pallas_call(
        _meta_attn_kernel,
        out_shape=jax.ShapeDtypeStruct((N_META, N_HEADS * V_HEAD), BF16),
        name="meta_attn",
    )(q, k, v)


def _absorb_kernel(q_ref, wukt_ref, qlat_ref, qpe_ref):
    q = q_ref[...]
    nb = qlat_ref.shape[0]
    qlat = _dot(q[:, :QK_NOPE], wukt_ref[...])
    qlat_ref[...] = qlat.reshape(nb, -1, KV_LORA)
    qpe_ref[...] = q[:, QK_NOPE:].astype(F32).reshape(nb, -1, ROPE_PAD)


def _absorb(q, w_ukt, nb):
    rows = q.shape[0]
    ts = rows // nb
    return pl.pallas_call(
        _absorb_kernel,
        grid=(N_HEADS,),
        in_specs=[pl.BlockSpec((rows, HEAD_PAD), lambda h: (0, h)),
                  pl.BlockSpec((None, QK_NOPE, KV_LORA), lambda h: (h, 0, 0))],
        out_specs=[pl.BlockSpec((nb, None, ts, KV_LORA), lambda h: (0, h, 0, 0)),
                   pl.BlockSpec((nb, None, ts, ROPE_PAD), lambda h: (0, h, 0, 0))],
        out_shape=[jax.ShapeDtypeStruct((nb, N_HEADS, ts, KV_LORA), F32),
                   jax.ShapeDtypeStruct((nb, N_HEADS, ts, ROPE_PAD), F32)],
        compiler_params=pltpu.CompilerParams(dimension_semantics=("arbitrary",)),
        name="absorb_q",
    )(q, w_ukt)


PAGED_BUFS = 2


def _paged_scratch(n_q, pps, page):
    return [pltpu.VMEM((n_q, 1), F32), pltpu.VMEM((n_q, 1), F32), pltpu.VMEM((n_q, KV_LORA), F32),
            pltpu.VMEM((PAGED_BUFS, pps * page, KV_LORA), F32), pltpu.VMEM((PAGED_BUFS, pps, QK_ROPE, page), F32),
            pltpu.VMEM((pps * page, KV_LORA), BF16), pltpu.VMEM((QK_ROPE, pps * page), BF16),
            pltpu.SemaphoreType.DMA((PAGED_BUFS,))]


def _paged_copies(pt_ref, ckv_hbm, kpe_hbm, cin_ref, kin_ref, sem, seq, chunk, slot, pps):
    page = kin_ref.shape[3]
    out = []
    for i in range(pps):
        pid = pt_ref[seq, chunk * pps + i]
        out.append(pltpu.make_async_copy(ckv_hbm.at[pid], cin_ref.at[slot, pl.ds(i * page, page)], sem.at[slot]))
        out.append(pltpu.make_async_copy(kpe_hbm.at[pid], kin_ref.at[slot, i], sem.at[slot]))
    return out


def _paged_new_tokens(qlat, qpe, cnew_ref, knew_ref, m_ref, l_ref, acc_ref, ts):
    cn = cnew_ref[...].astype(BF16)
    kn = knew_ref[...].astype(BF16)
    s = _dot_nt(qlat, cn) + _dot_nt(qpe, kn)
    t_q = lax.broadcasted_iota(jnp.int32, s.shape, 0) % ts
    t_k = lax.broadcasted_iota(jnp.int32, s.shape, 1)
    s = jnp.where(t_k <= t_q, s, NEG)
    m = jnp.max(s, axis=1, keepdims=True)
    pr = jnp.exp2(s - m)
    m_ref[...] = m
    l_ref[...] = jnp.sum(pr, axis=1, keepdims=True)
    acc_ref[...] = _dot(pr.astype(BF16), cn)


def _paged_scores(qlat, qpe, cin_ref, kin_ref, cbuf_ref, kbuf_ref, slot, pps):
    page = kin_ref.shape[3]
    for i in range(pps):
        rows = pl.ds(i * page, page)
        cbuf_ref[rows, :] = cin_ref[slot, rows, :].astype(BF16)
        kbuf_ref[:, i * page:(i + 1) * page] = kin_ref[slot, i].astype(BF16)
    return _dot_nt(qlat, cbuf_ref[...]) + _dot(qpe, kbuf_ref[...])


def _paged_probs(s, m_ref, l_ref):
    m = m_ref[...]
    m_new = jnp.maximum(m, jnp.max(s, axis=1, keepdims=True))
    alpha = jnp.exp2(m - m_new)
    pr = jnp.exp2(s - m_new)
    m_ref[...] = m_new
    l_ref[...] = alpha * l_ref[...] + jnp.sum(pr, axis=1, keepdims=True)
    return alpha, pr.astype(BF16)


def _paged_accumulate(alpha, pr, cbuf_ref, acc_ref):
    acc_ref[...] = alpha * acc_ref[...] + _dot(pr, cbuf_ref[...])


def _unabsorb_kernel(olat_ref, wuv_ref, o_ref):
    x = olat_ref[...]
    x = x.reshape(x.shape[0] * x.shape[1], KV_LORA).astype(BF16)
    o_ref[...] = _dot(x, wuv_ref[...]).astype(BF16)


def _unabsorb(o_lat, w_uvh):
    nb, _, ts, _ = o_lat.shape
    return pl.pallas_call(
        _unabsorb_kernel,
        grid=(N_HEADS,),
        in_specs=[pl.BlockSpec((nb, None, ts, KV_LORA), lambda h: (0, h, 0, 0)),
                  pl.BlockSpec((None, KV_LORA, V_HEAD), lambda h: (h, 0, 0))],
        out_specs=pl.BlockSpec((nb * ts, V_HEAD), lambda h: (0, h)),
        out_shape=jax.ShapeDtypeStruct((nb * ts, N_HEADS * V_HEAD), BF16),
        compiler_params=pltpu.CompilerParams(dimension_semantics=("arbitrary",)),
        name="unabsorb_o",
    )(o_lat, w_uvh)


FF_CHUNKS = 4
N_POST_WEIGHTS = 13


def _post_compute(x_ref, y_ref, o_ref, g_ref, weights, out_ref, side_work=None):
    before_up, before_down, after_down = side_work if side_work is not None else (lambda c: None,) * 3
    lng_ref, lnb_ref, wr_ref, wm_ref, wo_ref, g1_ref, b1_ref, wup_ref, bup_ref, wdn_ref, bdn_ref, g2_ref, b2_ref = weights
    h = _layer_norm(x_ref[...], lng_ref[...], lnb_ref[...])
    z_r = _dot(y_ref[...], wr_ref[...])
    z_m = _dot(o_ref[...], wm_ref[...])
    mix_in = g_ref[:, :D_MODEL] * z_r + g_ref[:, D_MODEL:] * z_m
    mix = _dot(mix_in.astype(BF16), wo_ref[...])
    before_up(0)
    x1 = _layer_norm(DN_ALPHA * h + mix, g1_ref[...], b1_ref[...])
    x1b = x1.astype(BF16)
    f = bdn_ref[...]
    ff_chunk = D_FF // FF_CHUNKS
    for c in range(FF_CHUNKS):
        cs = slice(c * ff_chunk, (c + 1) * ff_chunk)
        if c > 0:
            before_up(c)
        up = jnp.maximum(_dot(x1b, wup_ref[:, cs]) + bup_ref[:, cs], 0.0)
        before_down(c)
        f = f + _dot((up * up).astype(BF16), wdn_ref[cs, :])
        after_down(c)
    out_ref[...] = _layer_norm(DN_ALPHA * x1 + f, g2_ref[...], b2_ref[...])


def _post_kernel(x_ref, y_ref, o_ref, g_ref, *rest):
    _post_compute(x_ref, y_ref, o_ref, g_ref, rest[:N_POST_WEIGHTS], rest[N_POST_WEIGHTS])


def _post_paged_kernel(pt_ref, x_ref, y_ref, o_ref, g_ref, *rest, pps, ts, n_j, n_steps, tiles_per_seq):
    weights = rest[:N_POST_WEIGHTS]
    (qlat_ref, qpe_ref, cnew_ref, knew_ref, lat_ref, latm_ref, ckv_hbm, kpe_hbm, out_ref, olat_ref, platent_hbm,
     m_ref, l_ref, acc_ref, cin_ref, kin_ref, cbuf_ref, kbuf_ref, sem, lsem) = rest[N_POST_WEIGHTS:]
    s = pl.program_id(0)
    spb = n_j // FF_CHUNKS
    seq, j0 = lax.div(s, spb), lax.rem(s, spb) * FF_CHUNKS
    copies = functools.partial(_paged_copies, pt_ref, ckv_hbm, kpe_hbm, cin_ref, kin_ref, sem, pps=pps)

    tm = out_ref.shape[0]
    lat_rows = pltpu.make_async_copy(
        lat_ref,
        platent_hbm.at[lax.div(s, tiles_per_seq), pl.ds(N_META + lax.rem(s, tiles_per_seq) * tm, tm)], lsem.at[0])
    lat_meta = [pltpu.make_async_copy(latm_ref, platent_hbm.at[b, pl.ds(0, N_META)], lsem.at[1])
                for b in range(platent_hbm.shape[0])]
    lat_rows.start()

    @pl.when(s == 0)
    def _():
        for cp in lat_meta:
            cp.start()
        for cp in copies(0, 0, 0):
            cp.start()

    live = {}

    def gather_and_score(c):
        slot, nxt = c % PAGED_BUFS, (c + 1) % PAGED_BUFS
        if c + 1 < FF_CHUNKS:
            for cp in copies(seq, j0 + c + 1, nxt):
                cp.start()
        else:
            @pl.when(s + 1 < n_steps)
            def _():
                for cp in copies(lax.div(s + 1, spb), lax.rem(s + 1, spb) * FF_CHUNKS, nxt):
                    cp.start()
        for cp in copies(seq, j0 + c, slot):
            cp.wait()
        qlat = qlat_ref[...].astype(BF16)
        qpe = qpe_ref[...][:, :QK_ROPE].astype(BF16)
        if c == 0:
            pl.when(j0 == 0)(functools.partial(_paged_new_tokens, qlat, qpe, cnew_ref, knew_ref,
                                               m_ref, l_ref, acc_ref, ts))
        live["s"] = _paged_scores(qlat, qpe, cin_ref, kin_ref, cbuf_ref, kbuf_ref, slot, pps)

    def probs(c):
        live["alpha"], live["pr"] = _paged_probs(live.pop("s"), m_ref, l_ref)

    def accumulate(c):
        _paged_accumulate(live.pop("alpha"), live.pop("pr"), cbuf_ref, acc_ref)
        if c == FF_CHUNKS - 1:
            @pl.when(j0 == n_j - FF_CHUNKS)
            def _():
                olat_ref[...] = acc_ref[...] / l_ref[...]

    _post_compute(x_ref, y_ref, o_ref, g_ref, weights, out_ref, side_work=(gather_and_score, probs, accumulate))

    lat_rows.wait()

    @pl.when(s == 0)
    def _():
        for cp in lat_meta:
            cp.wait()


def _post_weight_specs():
    vec = lambda w: _const_spec((1, w))
    sq = _const_spec((D_MODEL, D_MODEL))
    return [vec(D_MODEL), vec(D_MODEL), sq, sq, sq, vec(D_MODEL), vec(D_MODEL),
            _const_spec((D_MODEL, D_FF)), vec(D_FF), _const_spec((D_FF, D_MODEL)), vec(D_MODEL),
            vec(D_MODEL), vec(D_MODEL)]


def _post_weights(p):
    return (p["ln_in_g"], p["ln_in_b"], p["w_br_r"], p["w_br_m"], p["w_o"], p["ln1_g"], p["ln1_b"],
            p["w_up"], p["b_up"], p["w_down"], p["b_down"], p["ln2_g"], p["ln2_b"])


def _post(x, y, o, g, p, tm):
    rows = x.shape[0]
    assert rows % tm == 0
    row = lambda w: pl.BlockSpec((tm, w), lambda i: (i, 0))
    return pl.pallas_call(
        _post_kernel,
        grid=(rows // tm,),
        in_specs=[row(D_MODEL), row(D_RNN), row(N_HEADS * V_HEAD), row(2 * D_MODEL)] + _post_weight_specs(),
        out_specs=row(D_MODEL),
        out_shape=jax.ShapeDtypeStruct((rows, D_MODEL), F32),
        compiler_params=pltpu.CompilerParams(dimension_semantics=("arbitrary",), vmem_limit_bytes=VMEM_LIMIT),
        name="merge_mlp",
    )(x, y, o, g, *_post_weights(p))


def _post_with_paged_attn(x, y, o, g, p, tm, page_table, q_lat, q_pe, ckv_new, kpe_new, ckv_pool, kpe_pool_t,
                          latent, latent_meta, n_seq):
    rows = x.shape[0]
    n_steps = rows // tm
    assert n_steps % n_seq == 0
    nb, n_pages = page_table.shape
    n_q = q_lat.shape[1]
    page = ckv_pool.shape[1]
    n_new = ckv_new.shape[1]
    chunks = n_steps * FF_CHUNKS
    assert rows % tm == 0 and (nb * n_pages) % chunks == 0 and FF_CHUNKS % PAGED_BUFS == 0
    pps = nb * n_pages // chunks
    n_j = n_pages // pps
    assert n_pages % pps == 0 and n_j % FF_CHUNKS == 0
    spb = n_j // FF_CHUNKS
    row = lambda w: pl.BlockSpec((tm, w), lambda i, pt: (i, 0))
    per_seq = lambda r, w: pl.BlockSpec((None, r, w), lambda i, pt: (i // spb, 0, 0))
    grid_spec = pltpu.PrefetchScalarGridSpec(
        num_scalar_prefetch=1,
        grid=(n_steps,),
        in_specs=[row(D_MODEL), row(D_RNN), row(N_HEADS * V_HEAD), row(2 * D_MODEL)] + _post_weight_specs()
                 + [per_seq(n_q, KV_LORA), per_seq(n_q, ROPE_PAD), per_seq(n_new, KV_LORA), per_seq(n_new, QK_ROPE),
                    row(KV_LORA), _const_spec((N_META, KV_LORA))]
                 + [pl.BlockSpec(memory_space=pl.ANY)] * 2,
        out_specs=[row(D_MODEL), per_seq(n_q, KV_LORA), pl.BlockSpec(memory_space=pl.ANY)],
        scratch_shapes=_paged_scratch(n_q, pps, page) + [pltpu.SemaphoreType.DMA((2,))],
    )
    return pl.pallas_call(
        functools.partial(_post_paged_kernel, pps=pps, ts=n_q // N_HEADS, n_j=n_j, n_steps=n_steps,
                          tiles_per_seq=n_steps // n_seq),
        grid_spec=grid_spec,
        out_shape=[jax.ShapeDtypeStruct((rows, D_MODEL), F32), jax.ShapeDtypeStruct((nb, n_q, KV_LORA), F32),
                   jax.ShapeDtypeStruct((n_seq, N_META + rows // n_seq, KV_LORA), F32)],
        compiler_params=pltpu.CompilerParams(dimension_semantics=("arbitrary",), vmem_limit_bytes=VMEM_LIMIT),
        name="merge_mlp_paged_attn",
    )(page_table, x, y, o, g, *_post_weights(p), q_lat, q_pe, ckv_new, kpe_new, latent, latent_meta,
      ckv_pool, kpe_pool_t)


def _rope_tables(pos):
    half = QK_ROPE // 2
    inv = 1.0 / (ROPE_THETA ** (jnp.arange(half, dtype=F32) / half))
    ang = pos.astype(F32)[:, None] * inv[None, :]
    cos, sin = jnp.cos(ang), jnp.sin(ang)
    z = jnp.zeros_like(cos)
    return (jnp.concatenate([cos, cos, z, z], 1), jnp.concatenate([-sin, z, z, z], 1),
            jnp.concatenate([z, sin, z, z], 1))


def _prep_params(w_in, b_gate, conv_w, conv_b, rg_wa, rg_ba, rg_wx, rg_bx, rg_lambda, w_br_r, q_norm_g, w_uq,
                 kv_norm_g, w_uk, w_uv, w_br_m, w_o, ln1_g, ln1_b, w_up, b_up, w_down, b_down, ln2_g, ln2_b,
                 ln_in_g, ln_in_b):
    l = 0
    vec = lambda a: a.reshape(1, -1).astype(F32)
    wi = w_in[l]
    w_uq_p = jnp.pad(w_uq[l], ((0, 0), (0, 0), (0, HEAD_PAD - QK_NOPE - QK_ROPE))).reshape(Q_LORA, N_HEADS * HEAD_PAD)
    return {
        "ln_in_g": vec(ln_in_g), "ln_in_b": vec(ln_in_b),
        "w_in": wi[:, :_C_KPE].astype(BF16),
        "w_kpe": jnp.pad(wi[:, _C_KPE:_C_GATE], ((0, 0), (0, ROPE_PAD - QK_ROPE))).astype(BF16),
        "w_gate": wi[:, _C_GATE:].astype(BF16), "b_gate": vec(b_gate[l]),
        "q_norm_g": vec(q_norm_g[l]), "w_uq": w_uq_p.astype(BF16),
        "kv_norm_g": vec(kv_norm_g[l]),
        "w_uk": w_uk[l].reshape(KV_LORA, N_HEADS * QK_NOPE).astype(BF16),
        "w_uv": w_uv[l].reshape(KV_LORA, N_HEADS * V_HEAD).astype(BF16),
        "w_uv_t": w_uv[l].reshape(KV_LORA, N_HEADS * V_HEAD).T.astype(BF16),
        "w_ukt": jnp.transpose(w_uk[l], (1, 2, 0)).astype(BF16),
        "w_uvh": jnp.transpose(w_uv[l], (1, 0, 2)).astype(BF16),
        "conv_w": conv_w[l].astype(F32), "conv_b": vec(conv_b[l]),
        "w_ax": jnp.concatenate([rg_wa[l], rg_wx[l]], axis=-1).astype(BF16),
        "rg_ba": vec(rg_ba[l]), "rg_bx": vec(rg_bx[l]), "rg_lambda": vec(rg_lambda[l]),
        "w_br_r": w_br_r[l].astype(BF16), "w_br_m": w_br_m[l].astype(BF16), "w_o": w_o[l].astype(BF16),
        "ln1_g": vec(ln1_g[l]), "ln1_b": vec(ln1_b[l]),
        "w_up": w_up[l].astype(BF16), "b_up": vec(b_up[l]),
        "w_down": w_down[l].astype(BF16), "b_down": vec(b_down[l]),
        "ln2_g": vec(ln2_g[l]), "ln2_b": vec(ln2_b[l]),
    }


def kernel(x_prompt, x_sample, cache_ckv, cache_kpe, page_table, state_conv, state_rglru, meta_tokens, ln_in_g, ln_in_b, w_in, b_gate, conv_w, conv_b, rg_wa, rg_ba, rg_wx, rg_bx, rg_lambda, w_br_r, q_norm_g, w_uq, kv_norm_g, w_uk, w_uv, w_br_m, w_o, ln1_g, ln1_b, w_up, b_up, w_down, b_down, ln2_g, ln2_b):
    assert w_in.shape[0] == DEPTH == 1
    bn, seq, _ = x_prompt.shape
    bd, ts, _ = x_sample.shape
    past_len = page_table.shape[1] * cache_ckv.shape[2]
    p = _prep_params(w_in, b_gate, conv_w, conv_b, rg_wa, rg_ba, rg_wx, rg_bx, rg_lambda, w_br_r, q_norm_g, w_uq,
                     kv_norm_g, w_uk, w_uv, w_br_m, w_o, ln1_g, ln1_b, w_up, b_up, w_down, b_down, ln2_g, ln2_b,
                     ln_in_g, ln_in_b)
    n_s = bd * ts

    x_small = jnp.concatenate([x_sample.reshape(n_s, D_MODEL), meta_tokens.astype(F32)], axis=0)
    pos_small = jnp.concatenate([jnp.tile(past_len + jnp.arange(ts), bd), jnp.arange(N_META)])
    n_small = n_s + N_META
    rx, grg, g_small, q, k, v, ckv, kpe = _in_proj(x_small, _rope_tables(pos_small), 1, p, n_small, False)

    rep = lambda a: jnp.broadcast_to(a[None], (SUBLANES,) + a.shape)
    y_m, conv_m, h_m = _rglru(rep(rx[n_s:]), rep(grg[n_s:]), jnp.zeros((SUBLANES, CONV_W - 1, D_RNN), F32),
                              jnp.zeros((SUBLANES, D_RNN), F32), p, N_META)
    k_meta, v_meta = k[n_s:], v[n_s:]
    o_m = _meta_attn(q[n_s:], k_meta, v_meta)

    y_s, conv_s, h_s = _rglru(rx[:n_s].reshape(bd, ts, D_RNN), grg[:n_s].reshape(bd, ts, D_RNN),
                              state_conv[0], state_rglru[0], p, ts)
    q_lat, q_pe = _absorb(q[:n_s], p["w_ukt"], bd)
    ckv_s = ckv[:n_s].reshape(bd, ts, KV_LORA)
    kpe_s = kpe[:n_s].reshape(bd, ts, QK_ROPE)
    pad_new = lambda a: jnp.pad(a, ((0, 0), (0, 2 * SUBLANES - ts), (0, 0)))

    n_p = bn * seq
    pos_p = N_META + jnp.arange(seq)
    tm = TM_IN_PROJ
    bcast = lambda a: jnp.broadcast_to(a[:1], (bn,) + a.shape[1:])
    rx_p, grg_p, g_p, q_p, k_p, vt_p, ckv_p, kpe_p = _in_proj(x_prompt.reshape(n_p, D_MODEL), _rope_tables(pos_p),
                                                              seq // tm, p, tm, True)
    y_p, conv_p, h_p = _rglru(rx_p.reshape(bn, seq, D_RNN), grg_p.reshape(bn, seq, D_RNN), bcast(conv_m), bcast(h_m),
                              p, TT_SCAN)
    o_p = _flash(q_p.reshape(bn, seq, -1), k_p.reshape(bn, seq, -1), vt_p.reshape(bn, seq // tm, -1, tm),
                 k_meta, v_meta.T, TQ_FLASH, FLASH_HEADS)
    out_p, o_lat, prompt_ckv = _post_with_paged_attn(
        x_prompt.reshape(n_p, D_MODEL), y_p.reshape(n_p, D_RNN), o_p.reshape(n_p, -1), g_p, p, TM_POST,
        page_table, q_lat.reshape(bd, N_HEADS * ts, KV_LORA), q_pe.reshape(bd, N_HEADS * ts, ROPE_PAD),
        pad_new(ckv_s), pad_new(kpe_s), cache_ckv[0], jnp.swapaxes(cache_kpe[0], 1, 2),
        ckv_p, ckv[n_s:], bn)

    o_s = _unabsorb(o_lat.reshape(bd, N_HEADS, ts, KV_LORA), p["w_uvh"])
    y_small = jnp.concatenate([y_s.reshape(n_s, D_RNN), y_m[0]], axis=0)
    o_small = jnp.concatenate([o_s, o_m], axis=0)
    out_small = _post(x_small, y_small, o_small, g_small, p, n_small)

    meta_rows = lambda a: jnp.broadcast_to(a[n_s:][None], (bn, N_META, a.shape[-1]))
    y_prompt = out_p.reshape(bn, seq, D_MODEL)
    y_sample = out_small[:n_s].reshape(bd, ts, D_MODEL)
    prompt_kpe = jnp.concatenate([meta_rows(kpe), kpe_p.reshape(bn, seq, QK_ROPE)], axis=1)[None]
    return (y_prompt, y_sample, prompt_ckv[None], prompt_kpe, conv_p[None], h_p[None],
            ckv_s[None], kpe_s[None], conv_s[None], h_s[None])
```

```python
import functools
import math

import jax
import jax.numpy as jnp
from jax import lax
from jax.experimental import pallas as pl
from jax.experimental.pallas import tpu as pltpu

F32 = jnp.float32
BF16 = jnp.bfloat16

D_MODEL = 1024
N_META = 16
D_RNN = D_MODEL
RG_BLOCKS = 8
RG_BLOCK = D_RNN // RG_BLOCKS
CONV_W = 4
RG_C = 8.0
N_HEADS = 8
QK_NOPE = 128
QK_ROPE = 64
V_HEAD = 128
KV_LORA = 512
Q_LORA = 768
ROPE_THETA = 10000.0
D_FF = 4 * D_MODEL
DEPTH = 1
DN_ALPHA = (2.0 * DEPTH) ** 0.25
EPS = 1e-5
SM_SCALE = (QK_NOPE + QK_ROPE) ** -0.5
Q_PRESCALE = SM_SCALE * math.log2(math.e)

LANES = 128
SUBLANES = 8
HEAD_PAD = 2 * LANES
ROPE_PAD = LANES
_C_RX, _C_RG, _C_CQ, _C_CKV, _C_KPE = 0, D_RNN, 2 * D_RNN, 2 * D_RNN + Q_LORA, 2 * D_RNN + Q_LORA + KV_LORA
_C_GATE = _C_KPE + QK_ROPE
VMEM_LIMIT = 56 * 1024 * 1024
TM_IN_PROJ = 256
TM_POST = 256
TT_SCAN = 128
TQ_FLASH = 512
FLASH_HEADS = 4
NEG = float(jnp.finfo(jnp.float32).min)


def _const_spec(shape):
    return pl.BlockSpec(shape, lambda *_: (0,) * len(shape), pipeline_mode=pl.Buffered(1))


def _layer_norm(x, g, b):
    mu = jnp.mean(x, -1, keepdims=True)
    xc = x - mu
    var = jnp.mean(xc * xc, -1, keepdims=True)
    return xc * lax.rsqrt(var + EPS) * g + b


def _rms_norm(x, g):
    return x * lax.rsqrt(jnp.mean(x * x, -1, keepdims=True) + EPS) * g


def _gelu_tanh(x):
    return 0.5 * x * (1.0 + jnp.tanh(math.sqrt(2.0 / math.pi) * (x + 0.044715 * (x * x * x))))


def _rope_block(y, cos, sin_lo, sin_hi):
    left = pltpu.roll(y, ROPE_PAD - QK_ROPE // 2, 1)
    right = pltpu.roll(y, QK_ROPE // 2, 1)
    return y * cos + left * sin_lo + right * sin_hi


def _dot(a, b):
    return jnp.dot(a, b, preferred_element_type=F32)


def _dot_nt(a, b):
    return lax.dot_general(a, b, (((1,), (1,)), ((), ())), preferred_element_type=F32)


CONV_TAIL = SUBLANES - (CONV_W - 1)


def _conv_tile(xe_ref, x, cw_ref, cb_ref):
    tt = x.shape[1]
    xe_ref[:, SUBLANES:, :] = x
    xc = cb_ref[...] + x * cw_ref[CONV_W - 1:CONV_W, :]
    for k in range(CONV_W - 1):
        xc = xc + xe_ref[:, CONV_TAIL + k:CONV_TAIL + k + tt, :] * cw_ref[k:k + 1, :]
    new_tail = xe_ref[:, tt + CONV_TAIL:tt + SUBLANES, :]
    xe_ref[:, CONV_TAIL:SUBLANES, :] = new_tail
    return xc, new_tail


def _decay_rate(lam_ref):
    z = -lam_ref[...]
    return RG_C * (jnp.maximum(z, 0.0) + jnp.log1p(jnp.exp(-jnp.abs(z))))


def _recurrence_inputs(xn, wax_n, ba_n, bx_n, rate_n):
    gates = _dot(xn.astype(BF16), wax_n)
    r = jax.nn.sigmoid(gates[:, :RG_BLOCK] + ba_n)
    ig = jax.nn.sigmoid(gates[:, RG_BLOCK:] + bx_n)
    neg_log_a = r * rate_n
    a = jnp.exp(-neg_log_a)
    u = jnp.sqrt(jnp.tanh(neg_log_a) * (a * a + 1.0)) * (ig * xn)
    return a, u


def _scan_slabs(a_ref, u_ref, hc_ref, nb, tt):
    pitch = tt + SUBLANES
    n_slab = D_RNN // LANES
    for grp in range(nb // SUBLANES):
        base = grp * SUBLANES * pitch
        rows = slice(grp * SUBLANES, (grp + 1) * SUBLANES)
        h0 = tuple(hc_ref[rows, n * LANES:(n + 1) * LANES] for n in range(n_slab))

        def step(t, hs, base=base):
            out = []
            for n in range(n_slab):
                idx = pl.ds(base + t, SUBLANES, stride=pitch)
                hn = a_ref[n, idx, :] * hs[n] + u_ref[n, idx, :]
                u_ref[n, idx, :] = hn
                out.append(hn)
            return tuple(out)

        hs = lax.fori_loop(0, tt, step, h0, unroll=min(tt, 8))
        for n in range(n_slab):
            hc_ref[rows, n * LANES:(n + 1) * LANES] = hs[n]


def _gated_branch_output(y_ref, h_ref, grg_ref, nb, tt):
    pitch = tt + SUBLANES
    for b in range(nb):
        for n in range(D_RNN // LANES):
            cs = slice(n * LANES, (n + 1) * LANES)
            y_ref[b, :, cs] = (h_ref[n, b * pitch:b * pitch + tt, :] * grg_ref[b, :, cs]).astype(BF16)


def _in_proj_kernel(x_ref, cos_ref, slo_ref, shi_ref, lng_ref, lnb_ref, w_in_ref, w_kpe_ref, w_gate_ref, bg_ref,
                    qg_ref, wuq_ref, kvg_ref, wuk_ref, wuv_ref, *rest, token_minor):
    if token_minor:
        cost_ref, sint_ref, *rest = rest
    rx_ref, grg_ref, g_ref, q_ref, k_ref, v_ref, ckv_ref, kpe_ref = rest
    h = _layer_norm(x_ref[...], lng_ref[...], lnb_ref[...])
    hb = h.astype(BF16)

    def proj(lo, hi):
        return _dot(hb, w_in_ref[:, lo:hi])

    cos, slo, shi = cos_ref[...], slo_ref[...], shi_ref[...]
    cqn = _rms_norm(proj(_C_CQ, _C_CKV), qg_ref[...])
    ckv = _rms_norm(proj(_C_CKV, _C_KPE), kvg_ref[...])
    kpe = _rope_block(_dot(hb, w_kpe_ref[...]), cos, slo, shi)
    ckv_ref[...] = ckv
    kpe_ref[...] = kpe[:, :QK_ROPE]
    ckvb = ckv.astype(BF16)
    kpeb = kpe.astype(BF16)

    rx_ref[...] = proj(_C_RX, _C_RG)
    grg_ref[...] = _gelu_tanh(proj(_C_RG, _C_CQ))
    g_ref[...] = jax.nn.sigmoid(_dot(hb, w_gate_ref[...]) + bg_ref[...])

    cqb = cqn.astype(BF16)
    if token_minor:
        qt = _dot_nt(wuq_ref[...], cqb) * Q_PRESCALE
        cos_t, sin_t = cost_ref[...], sint_ref[...]
        half = QK_ROPE // 2
        for hh in range(N_HEADS):
            r0 = hh * HEAD_PAD
            r1, r2, r3 = r0 + QK_NOPE, r0 + QK_NOPE + half, r0 + QK_NOPE + QK_ROPE
            x1, x2 = qt[r1:r2], qt[r2:r3]
            q_ref[r0:r1, :] = qt[r0:r1].astype(BF16)
            q_ref[r1:r2, :] = (x1 * cos_t - x2 * sin_t).astype(BF16)
            q_ref[r2:r3, :] = (x1 * sin_t + x2 * cos_t).astype(BF16)
            q_ref[r3:r0 + HEAD_PAD, :] = qt[r3:r0 + HEAD_PAD].astype(BF16)
    else:
        q = _dot(cqb, wuq_ref[...]) * Q_PRESCALE
        for hh in range(N_HEADS):
            c0 = hh * HEAD_PAD
            q_ref[:, c0:c0 + QK_NOPE] = q[:, c0:c0 + QK_NOPE].astype(BF16)
            q_ref[:, c0 + QK_NOPE:c0 + HEAD_PAD] = _rope_block(q[:, c0 + QK_NOPE:c0 + HEAD_PAD],
                                                               cos, slo, shi).astype(BF16)

    kn = _dot(ckvb, wuk_ref[...])
    if token_minor:
        v_ref[...] = _dot_nt(wuv_ref[...], ckvb).astype(BF16)
    else:
        v_ref[...] = _dot(ckvb, wuv_ref[...]).astype(BF16)
    for hh in range(N_HEADS):
        c0 = hh * HEAD_PAD
        k_ref[:, c0:c0 + QK_NOPE] = kn[:, hh * QK_NOPE:(hh + 1) * QK_NOPE].astype(BF16)
        k_ref[:, c0 + QK_NOPE:c0 + HEAD_PAD] = kpeb


def _in_proj(x, tabs, tab_blocks, p, tm, tabs_t=None):
    token_minor = tabs_t is not None
    rows = x.shape[0]
    assert rows % tm == 0
    n_tiles = rows // tm
    row = lambda w: pl.BlockSpec((tm, w), lambda i: (i, 0))
    tab = pl.BlockSpec((tm, ROPE_PAD), lambda i: (i % tab_blocks, 0))
    outs = [
        (D_RNN, F32),
        (D_RNN, F32),
        (2 * D_MODEL, F32),
        (N_HEADS * HEAD_PAD, BF16),
        (N_HEADS * HEAD_PAD, BF16),
        (N_HEADS * V_HEAD, BF16),
        (KV_LORA, F32),
        (QK_ROPE, F32),
    ]
    out_specs = [row(w) for w, _ in outs]
    out_shape = [jax.ShapeDtypeStruct((rows, w), dt) for w, dt in outs]
    maybe_t = lambda shape: shape[::-1] if token_minor else shape
    in_specs = [row(D_MODEL), tab, tab, tab,
                _const_spec((1, D_MODEL)), _const_spec((1, D_MODEL)),
                _const_spec((D_MODEL, _C_KPE)), _const_spec((D_MODEL, ROPE_PAD)),
                _const_spec((D_MODEL, 2 * D_MODEL)), _const_spec((1, 2 * D_MODEL)),
                _const_spec((1, Q_LORA)), _const_spec(maybe_t((Q_LORA, N_HEADS * HEAD_PAD))),
                _const_spec((1, KV_LORA)), _const_spec((KV_LORA, N_HEADS * QK_NOPE)),
                _const_spec(maybe_t((KV_LORA, N_HEADS * V_HEAD)))]
    args = [x, *tabs, p["ln_in_g"], p["ln_in_b"], p["w_in"], p["w_kpe"], p["w_gate"], p["b_gate"], p["q_norm_g"],
            p["w_uq_t"] if token_minor else p["w_uq"], p["kv_norm_g"], p["w_uk"],
            p["w_uv_t"] if token_minor else p["w_uv"]]
    if token_minor:
        for j in (3, 5):
            out_specs[j] = pl.BlockSpec((None, outs[j][0], tm), lambda i: (i, 0, 0))
            out_shape[j] = jax.ShapeDtypeStruct((n_tiles, outs[j][0], tm), BF16)
        tab_t = pl.BlockSpec((QK_ROPE // 2, tm), lambda i: (0, i % tab_blocks))
        in_specs += [tab_t, tab_t]
        args += list(tabs_t)
    return pl.pallas_call(
        functools.partial(_in_proj_kernel, token_minor=token_minor),
        grid=(n_tiles,),
        in_specs=in_specs,
        out_specs=out_specs,
        out_shape=out_shape,
        compiler_params=pltpu.CompilerParams(dimension_semantics=("arbitrary",), vmem_limit_bytes=VMEM_LIMIT),
        name="in_proj",
    )(*args)


def _rglru_kernel(rx_ref, grg_ref, cprev_ref, hprev_ref, cw_ref, cb_ref, wax_ref, ba_ref, bx_ref, lam_ref,
                  y_ref, cout_ref, hout_ref, xe_ref, a_ref, u_ref, hc_ref, *, nb, tt):
    pitch = tt + SUBLANES

    @pl.when(pl.program_id(0) == 0)
    def _():
        xe_ref[:, CONV_TAIL:SUBLANES, :] = cprev_ref[...]
        hc_ref[...] = hprev_ref[...]

    xc, new_tail = _conv_tile(xe_ref, rx_ref[...], cw_ref, cb_ref)
    cout_ref[...] = new_tail
    rate = _decay_rate(lam_ref)
    xc2 = xc.reshape(nb * tt, D_RNN)
    for n in range(D_RNN // LANES):
        cs = slice(n * LANES, (n + 1) * LANES)
        a, u = _recurrence_inputs(xc2[:, cs], wax_ref[n], ba_ref[:, cs], bx_ref[:, cs], rate[:, cs])
        for b in range(nb):
            a_ref[n, b * pitch:b * pitch + tt, :] = a[b * tt:(b + 1) * tt]
            u_ref[n, b * pitch:b * pitch + tt, :] = u[b * tt:(b + 1) * tt]
    _scan_slabs(a_ref, u_ref, hc_ref, nb, tt)
    hout_ref[...] = hc_ref[...]
    _gated_branch_output(y_ref, u_ref, grg_ref, nb, tt)


def _rglru(rx, grg, conv_prev, h_prev, p, tt):
    nb, t_len, _ = rx.shape
    assert nb % SUBLANES == 0 and t_len % tt == 0 and tt % SUBLANES == 0
    pitch = tt + SUBLANES
    seq = pl.BlockSpec((nb, tt, D_RNN), lambda i: (0, i, 0))
    return pl.pallas_call(
        functools.partial(_rglru_kernel, nb=nb, tt=tt),
        grid=(t_len // tt,),
        in_specs=[seq, seq, _const_spec((nb, CONV_W - 1, D_RNN)), _const_spec((nb, D_RNN)),
                  _const_spec((CONV_W, D_RNN)), _const_spec((1, D_RNN)),
                  _const_spec((RG_BLOCKS, RG_BLOCK, 2 * RG_BLOCK)),
                  _const_spec((1, D_RNN)), _const_spec((1, D_RNN)), _const_spec((1, D_RNN))],
        out_specs=[seq,
                   pl.BlockSpec((nb, CONV_W - 1, D_RNN), lambda i: (0, 0, 0)),
                   pl.BlockSpec((nb, D_RNN), lambda i: (0, 0))],
        out_shape=[jax.ShapeDtypeStruct((nb, t_len, D_RNN), BF16),
                   jax.ShapeDtypeStruct((nb, CONV_W - 1, D_RNN), F32),
                   jax.ShapeDtypeStruct((nb, D_RNN), F32)],
        scratch_shapes=[pltpu.VMEM((nb, tt + SUBLANES, D_RNN), F32),
                        pltpu.VMEM((D_RNN // LANES, nb * pitch, LANES), F32),
                        pltpu.VMEM((D_RNN // LANES, nb * pitch, LANES), F32),
                        pltpu.VMEM((nb, D_RNN), F32)],
        compiler_params=pltpu.CompilerParams(dimension_semantics=("arbitrary",), vmem_limit_bytes=VMEM_LIMIT),
        name="rglru",
    )(rx, grg, conv_prev, h_prev, p["conv_w"], p["conv_b"], p["w_ax"], p["rg_ba"], p["rg_bx"], p["rg_lambda"])


def _softmax_update(s, m, l, acc, v):
    m_new = jnp.maximum(m, jnp.max(s, axis=1, keepdims=True))
    alpha = jnp.exp2(m - m_new)
    pr = jnp.exp2(s - m_new)
    l = alpha * l + jnp.sum(pr, axis=1, keepdims=True)
    acc = alpha * acc + _dot(pr.astype(BF16), v)
    return m_new, l, acc


def _softmax_update_t(s_t, m, l, acc_t, v_t):
    m_new = jnp.maximum(m, jnp.max(s_t, axis=0, keepdims=True))
    alpha = jnp.exp2(m - m_new)
    pr = jnp.exp2(s_t - m_new)
    l = alpha * l + jnp.sum(pr, axis=0, keepdims=True)
    acc_t = alpha * acc_t + _dot(v_t, pr.astype(BF16))
    return m_new, l, acc_t


def _flash_kernel(qt_ref, k_ref, vt_ref, km_ref, vmt_ref, o_ref, *, tq, tv, hps):
    qcols = [slice(hh * HEAD_PAD, (hh + 1) * HEAD_PAD) for hh in range(hps)]
    vrows = [slice(hh * V_HEAD, (hh + 1) * V_HEAD) for hh in range(hps)]

    def queries_t(hh):
        return jnp.concatenate([qt_ref[c, qcols[hh], :] for c in range(tq // tv)], axis=1)

    def scores(j, hh):
        return _dot(k_ref[j * tq:(j + 1) * tq, qcols[hh]], queries_t(hh))

    def values_t(j, hh):
        return jnp.concatenate([vt_ref[j * (tq // tv) + c, vrows[hh], :] for c in range(tq // tv)], axis=1)

    def tile(n_full):
        kpos = lax.broadcasted_iota(jnp.int32, (tq, tq), 0)
        qpos = lax.broadcasted_iota(jnp.int32, (tq, tq), 1)
        state, s_next = [], []
        for hh in range(hps):
            s_t = _dot(km_ref[:, qcols[hh]], queries_t(hh))
            m = jnp.max(s_t, axis=0, keepdims=True)
            pr = jnp.exp2(s_t - m)
            state.append((m, jnp.sum(pr, axis=0, keepdims=True), _dot(vmt_ref[vrows[hh], :], pr.astype(BF16))))
            s_next.append(scores(0, hh))
        for j in range(n_full + 1):
            for hh in range(hps):
                s_t = s_next[hh]
                if j < n_full:
                    s_next[hh] = scores(j + 1, hh)
                else:
                    s_t = jnp.where(kpos <= qpos, s_t, NEG)
                state[hh] = _softmax_update_t(s_t, *state[hh], values_t(j, hh))
        for hh in range(hps):
            _, l, acc_t = state[hh]
            o_ref[:, vrows[hh]] = jnp.transpose(acc_t / l).astype(BF16)

    for c in range(k_ref.shape[0] // tq):
        pl.when(pl.program_id(2) == c)(functools.partial(tile, c))


def _flash(q_t, k, v_t, k_pre, v_pre_t, tq, hps):
    nb, t_len, _ = k.shape
    tv = v_t.shape[3]
    assert t_len % tq == 0 and tq % tv == 0 and N_HEADS % hps == 0
    return pl.pallas_call(
        functools.partial(_flash_kernel, tq=tq, tv=tv, hps=hps),
        grid=(nb, N_HEADS // hps, t_len // tq),
        in_specs=[pl.BlockSpec((None, tq // tv, hps * HEAD_PAD, tv), lambda b, h, i: (b, i, h, 0)),
                  pl.BlockSpec((None, t_len, hps * HEAD_PAD), lambda b, h, i: (b, 0, h)),
                  pl.BlockSpec((None, t_len // tv, hps * V_HEAD, tv), lambda b, h, i: (b, 0, h, 0)),
                  pl.BlockSpec((N_META, hps * HEAD_PAD), lambda b, h, i: (0, h)),
                  pl.BlockSpec((hps * V_HEAD, N_META), lambda b, h, i: (h, 0))],
        out_specs=pl.BlockSpec((None, tq, hps * V_HEAD), lambda b, h, i: (b, i, h)),
        out_shape=jax.ShapeDtypeStruct((nb, t_len, N_HEADS * V_HEAD), BF16),
        compiler_params=pltpu.CompilerParams(dimension_semantics=("arbitrary",) * 3, vmem_limit_bytes=VMEM_LIMIT),
        name="flash_prompt",
    )(q_t, k, v_t, k_pre, v_pre_t)


def _meta_attn_kernel(q_ref, k_ref, v_ref, o_ref):
    qpos = lax.broadcasted_iota(jnp.int32, (N_META, N_META), 0)
    kpos = lax.broadcasted_iota(jnp.int32, (N_META, N_META), 1)
    for hh in range(N_HEADS):
        qs = slice(hh * HEAD_PAD, (hh + 1) * HEAD_PAD)
        vs = slice(hh * V_HEAD, (hh + 1) * V_HEAD)
        s = jnp.where(kpos <= qpos, _dot_nt(q_ref[:, qs], k_ref[:, qs]), NEG)
        pr = jnp.exp2(s - jnp.max(s, axis=1, keepdims=True))
        o = _dot(pr.astype(BF16), v_ref[:, vs]) / jnp.sum(pr, axis=1, keepdims=True)
        o_ref[:, vs] = o.astype(BF16)


def _meta_attn(q, k, v):
    return pl.pallas_call(
        _meta_attn_kernel,
        out_shape=jax.ShapeDtypeStruct((N_META, N_HEADS * V_HEAD), BF16),
        name="meta_attn",
    )(q, k, v)


def _absorb_kernel(q_ref, wukt_ref, qlat_ref, qpe_ref):
    q = q_ref[...]
    nb = qlat_ref.shape[0]
    qlat = _dot(q[:, :QK_NOPE], wukt_ref[...])
    qlat_ref[...] = qlat.reshape(nb, -1, KV_LORA)
    qpe_ref[...] = q[:, QK_NOPE:].astype(F32).reshape(nb, -1, ROPE_PAD)


def _absorb(q, w_ukt, nb):
    rows = q.shape[0]
    ts = rows // nb
    return pl.pallas_call(
        _absorb_kernel,
        grid=(N_HEADS,),
        in_specs=[pl.BlockSpec((rows, HEAD_PAD), lambda h: (0, h)),
                  pl.BlockSpec((None, QK_NOPE, KV_LORA), lambda h: (h, 0, 0))],
        out_specs=[pl.BlockSpec((nb, None, ts, KV_LORA), lambda h: (0, h, 0, 0)),
                   pl.BlockSpec((nb, None, ts, ROPE_PAD), lambda h: (0, h, 0, 0))],
        out_shape=[jax.ShapeDtypeStruct((nb, N_HEADS, ts, KV_LORA), F32),
                   jax.ShapeDtypeStruct((nb, N_HEADS, ts, ROPE_PAD), F32)],
        compiler_params=pltpu.CompilerParams(dimension_semantics=("arbitrary",)),
        name="absorb_q",
    )(q, w_ukt)


PAGED_BUFS = 2


def _paged_scratch(n_q, pps, page):
    return [pltpu.VMEM((n_q, 1), F32), pltpu.VMEM((n_q, 1), F32), pltpu.VMEM((n_q, KV_LORA), F32),
            pltpu.VMEM((PAGED_BUFS, pps * page, KV_LORA), F32), pltpu.VMEM((PAGED_BUFS, pps, QK_ROPE, page), F32),
            pltpu.VMEM((pps * page, KV_LORA), BF16), pltpu.VMEM((QK_ROPE, pps * page), BF16),
            pltpu.SemaphoreType.DMA((PAGED_BUFS,))]


def _paged_copies(pt_ref, ckv_hbm, kpe_hbm, cin_ref, kin_ref, sem, seq, chunk, slot, pps):
    page = kin_ref.shape[3]
    out = []
    for i in range(pps):
        pid = pt_ref[seq, chunk * pps + i]
        out.append(pltpu.make_async_copy(ckv_hbm.at[pid], cin_ref.at[slot, pl.ds(i * page, page)], sem.at[slot]))
        out.append(pltpu.make_async_copy(kpe_hbm.at[pid], kin_ref.at[slot, i], sem.at[slot]))
    return out


def _paged_new_tokens(qlat, qpe, cnew_ref, knew_ref, m_ref, l_ref, acc_ref, ts):
    cn = cnew_ref[...].astype(BF16)
    kn = knew_ref[...].astype(BF16)
    s = _dot_nt(qlat, cn) + _dot_nt(qpe, kn)
    t_q = lax.broadcasted_iota(jnp.int32, s.shape, 0) % ts
    t_k = lax.broadcasted_iota(jnp.int32, s.shape, 1)
    s = jnp.where(t_k <= t_q, s, NEG)
    m = jnp.max(s, axis=1, keepdims=True)
    pr = jnp.exp2(s - m)
    m_ref[...] = m
    l_ref[...] = jnp.sum(pr, axis=1, keepdims=True)
    acc_ref[...] = _dot(pr.astype(BF16), cn)


def _paged_scores(qlat, qpe, cin_ref, kin_ref, cbuf_ref, kbuf_ref, slot, pps):
    page = kin_ref.shape[3]
    for i in range(pps):
        rows = pl.ds(i * page, page)
        cbuf_ref[rows, :] = cin_ref[slot, rows, :].astype(BF16)
        kbuf_ref[:, i * page:(i + 1) * page] = kin_ref[slot, i].astype(BF16)
    return _dot_nt(qlat, cbuf_ref[...]) + _dot(qpe, kbuf_ref[...])


def _paged_probs(s, m_ref, l_ref):
    m = m_ref[...]
    m_new = jnp.maximum(m, jnp.max(s, axis=1, keepdims=True))
    alpha = jnp.exp2(m - m_new)
    pr = jnp.exp2(s - m_new)
    m_ref[...] = m_new
    l_ref[...] = alpha * l_ref[...] + jnp.sum(pr, axis=1, keepdims=True)
    return alpha, pr.astype(BF16)


def _paged_accumulate(alpha, pr, cbuf_ref, acc_ref):
    acc_ref[...] = alpha * acc_ref[...] + _dot(pr, cbuf_ref[...])


def _unabsorb_kernel(olat_ref, wuv_ref, o_ref):
    x = olat_ref[...]
    x = x.reshape(x.shape[0] * x.shape[1], KV_LORA).astype(BF16)
    o_ref[...] = _dot(x, wuv_ref[...]).astype(BF16)


def _unabsorb(o_lat, w_uvh):
    nb, _, ts, _ = o_lat.shape
    return pl.pallas_call(
        _unabsorb_kernel,
        grid=(N_HEADS,),
        in_specs=[pl.BlockSpec((nb, None, ts, KV_LORA), lambda h: (0, h, 0, 0)),
                  pl.BlockSpec((None, KV_LORA, V_HEAD), lambda h: (h, 0, 0))],
        out_specs=pl.BlockSpec((nb * ts, V_HEAD), lambda h: (0, h)),
        out_shape=jax.ShapeDtypeStruct((nb * ts, N_HEADS * V_HEAD), BF16),
        compiler_params=pltpu.CompilerParams(dimension_semantics=("arbitrary",)),
        name="unabsorb_o",
    )(o_lat, w_uvh)


FF_CHUNKS = 4
N_POST_WEIGHTS = 13


def _post_compute(x_ref, y_ref, o_ref, g_ref, weights, out_ref, side_work=None):
    before_up, before_down, after_down = side_work if side_work is not None else (lambda c: None,) * 3
    lng_ref, lnb_ref, wr_ref, wm_ref, wo_ref, g1_ref, b1_ref, wup_ref, bup_ref, wdn_ref, bdn_ref, g2_ref, b2_ref = weights
    h = _layer_norm(x_ref[...], lng_ref[...], lnb_ref[...])
    z_r = _dot(y_ref[...], wr_ref[...])
    z_m = _dot(o_ref[...], wm_ref[...])
    mix_in = g_ref[:, :D_MODEL] * z_r + g_ref[:, D_MODEL:] * z_m
    mix = _dot(mix_in.astype(BF16), wo_ref[...])
    before_up(0)
    x1 = _layer_norm(DN_ALPHA * h + mix, g1_ref[...], b1_ref[...])
    x1b = x1.astype(BF16)
    f = bdn_ref[...]
    ff_chunk = D_FF // FF_CHUNKS
    for c in range(FF_CHUNKS):
        cs = slice(c * ff_chunk, (c + 1) * ff_chunk)
        if c > 0:
            before_up(c)
        up = jnp.maximum(_dot(x1b, wup_ref[:, cs]) + bup_ref[:, cs], 0.0)
        before_down(c)
        f = f + _dot((up * up).astype(BF16), wdn_ref[cs, :])
        after_down(c)
    out_ref[...] = _layer_norm(DN_ALPHA * x1 + f, g2_ref[...], b2_ref[...])


def _post_kernel(x_ref, y_ref, o_ref, g_ref, *rest):
    _post_compute(x_ref, y_ref, o_ref, g_ref, rest[:N_POST_WEIGHTS], rest[N_POST_WEIGHTS])


def _post_paged_kernel(pt_ref, x_ref, y_ref, o_ref, g_ref, *rest, pps, ts, n_j, n_steps, tiles_per_seq):
    weights = rest[:N_POST_WEIGHTS]
    (qlat_ref, qpe_ref, cnew_ref, knew_ref, lat_ref, latm_ref, ckv_hbm, kpe_hbm, out_ref, olat_ref, platent_hbm,
     m_ref, l_ref, acc_ref, cin_ref, kin_ref, cbuf_ref, kbuf_ref, sem, lsem) = rest[N_POST_WEIGHTS:]
    s = pl.program_id(0)
    spb = n_j // FF_CHUNKS
    seq, j0 = lax.div(s, spb), lax.rem(s, spb) * FF_CHUNKS
    copies = functools.partial(_paged_copies, pt_ref, ckv_hbm, kpe_hbm, cin_ref, kin_ref, sem, pps=pps)

    tm = out_ref.shape[0]
    lat_rows = pltpu.make_async_copy(
        lat_ref,
        platent_hbm.at[lax.div(s, tiles_per_seq), pl.ds(N_META + lax.rem(s, tiles_per_seq) * tm, tm)], lsem.at[0])
    lat_meta = [pltpu.make_async_copy(latm_ref, platent_hbm.at[b, pl.ds(0, N_META)], lsem.at[1])
                for b in range(platent_hbm.shape[0])]
    lat_rows.start()

    @pl.when(s == 0)
    def _():
        for cp in lat_meta:
            cp.start()
        for cp in copies(0, 0, 0):
            cp.start()

    live = {}

    def gather_and_score(c):
        slot, nxt = c % PAGED_BUFS, (c + 1) % PAGED_BUFS
        if c + 1 < FF_CHUNKS:
            for cp in copies(seq, j0 + c + 1, nxt):
                cp.start()
        else:
            @pl.when(s + 1 < n_steps)
            def _():
                for cp in copies(lax.div(s + 1, spb), lax.rem(s + 1, spb) * FF_CHUNKS, nxt):
                    cp.start()
        for cp in copies(seq, j0 + c, slot):
            cp.wait()
        qlat = qlat_ref[...].astype(BF16)
        qpe = qpe_ref[...][:, :QK_ROPE].astype(BF16)
        if c == 0:
            pl.when(j0 == 0)(functools.partial(_paged_new_tokens, qlat, qpe, cnew_ref, knew_ref,
                                               m_ref, l_ref, acc_ref, ts))
        live["s"] = _paged_scores(qlat, qpe, cin_ref, kin_ref, cbuf_ref, kbuf_ref, slot, pps)

    def probs(c):
        live["alpha"], live["pr"] = _paged_probs(live.pop("s"), m_ref, l_ref)

    def accumulate(c):
        _paged_accumulate(live.pop("alpha"), live.pop("pr"), cbuf_ref, acc_ref)
        if c == FF_CHUNKS - 1:
            @pl.when(j0 == n_j - FF_CHUNKS)
            def _():
                olat_ref[...] = acc_ref[...] / l_ref[...]

    _post_compute(x_ref, y_ref, o_ref, g_ref, weights, out_ref, side_work=(gather_and_score, probs, accumulate))

    lat_rows.wait()

    @pl.when(s == 0)
    def _():
        for cp in lat_meta:
            cp.wait()


def _post_weight_specs():
    vec = lambda w: _const_spec((1, w))
    sq = _const_spec((D_MODEL, D_MODEL))
    return [vec(D_MODEL), vec(D_MODEL), sq, sq, sq, vec(D_MODEL), vec(D_MODEL),
            _const_spec((D_MODEL, D_FF)), vec(D_FF), _const_spec((D_FF, D_MODEL)), vec(D_MODEL),
            vec(D_MODEL), vec(D_MODEL)]


def _post_weights(p):
    return (p["ln_in_g"], p["ln_in_b"], p["w_br_r"], p["w_br_m"], p["w_o"], p["ln1_g"], p["ln1_b"],
            p["w_up"], p["b_up"], p["w_down"], p["b_down"], p["ln2_g"], p["ln2_b"])


def _post(x, y, o, g, p, tm):
    rows = x.shape[0]
    assert rows % tm == 0
    row = lambda w: pl.BlockSpec((tm, w), lambda i: (i, 0))
    return pl.pallas_call(
        _post_kernel,
        grid=(rows // tm,),
        in_specs=[row(D_MODEL), row(D_RNN), row(N_HEADS * V_HEAD), row(2 * D_MODEL)] + _post_weight_specs(),
        out_specs=row(D_MODEL),
        out_shape=jax.ShapeDtypeStruct((rows, D_MODEL), F32),
        compiler_params=pltpu.CompilerParams(dimension_semantics=("arbitrary",), vmem_limit_bytes=VMEM_LIMIT),
        name="merge_mlp",
    )(x, y, o, g, *_post_weights(p))


def _post_with_paged_attn(x, y, o, g, p, tm, page_table, q_lat, q_pe, ckv_new, kpe_new, ckv_pool, kpe_pool_t,
                          latent, latent_meta, n_seq):
    rows = x.shape[0]
    n_steps = rows // tm
    assert n_steps % n_seq == 0
    nb, n_pages = page_table.shape
    n_q = q_lat.shape[1]
    page = ckv_pool.shape[1]
    n_new = ckv_new.shape[1]
    chunks = n_steps * FF_CHUNKS
    assert rows % tm == 0 and (nb * n_pages) % chunks == 0 and FF_CHUNKS % PAGED_BUFS == 0
    pps = nb * n_pages // chunks
    n_j = n_pages // pps
    assert n_pages % pps == 0 and n_j % FF_CHUNKS == 0
    spb = n_j // FF_CHUNKS
    row = lambda w: pl.BlockSpec((tm, w), lambda i, pt: (i, 0))
    per_seq = lambda r, w: pl.BlockSpec((None, r, w), lambda i, pt: (i // spb, 0, 0))
    grid_spec = pltpu.PrefetchScalarGridSpec(
        num_scalar_prefetch=1,
        grid=(n_steps,),
        in_specs=[row(D_MODEL), row(D_RNN), row(N_HEADS * V_HEAD), row(2 * D_MODEL)] + _post_weight_specs()
                 + [per_seq(n_q, KV_LORA), per_seq(n_q, ROPE_PAD), per_seq(n_new, KV_LORA), per_seq(n_new, QK_ROPE),
                    row(KV_LORA), _const_spec((N_META, KV_LORA))]
                 + [pl.BlockSpec(memory_space=pl.ANY)] * 2,
        out_specs=[row(D_MODEL), per_seq(n_q, KV_LORA), pl.BlockSpec(memory_space=pl.ANY)],
        scratch_shapes=_paged_scratch(n_q, pps, page) + [pltpu.SemaphoreType.DMA((2,))],
    )
    return pl.pallas_call(
        functools.partial(_post_paged_kernel, pps=pps, ts=n_q // N_HEADS, n_j=n_j, n_steps=n_steps,
                          tiles_per_seq=n_steps // n_seq),
        grid_spec=grid_spec,
        out_shape=[jax.ShapeDtypeStruct((rows, D_MODEL), F32), jax.ShapeDtypeStruct((nb, n_q, KV_LORA), F32),
                   jax.ShapeDtypeStruct((n_seq, N_META + rows // n_seq, KV_LORA), F32)],
        compiler_params=pltpu.CompilerParams(dimension_semantics=("arbitrary",), vmem_limit_bytes=VMEM_LIMIT),
        name="merge_mlp_paged_attn",
    )(page_table, x, y, o, g, *_post_weights(p), q_lat, q_pe, ckv_new, kpe_new, latent, latent_meta,
      ckv_pool, kpe_pool_t)


def _rope_tables(pos):
    half = QK_ROPE // 2
    inv = 1.0 / (ROPE_THETA ** (jnp.arange(half, dtype=F32) / half))
    ang = pos.astype(F32)[:, None] * inv[None, :]
    cos, sin = jnp.cos(ang), jnp.sin(ang)
    z = jnp.zeros_like(cos)
    return ((jnp.concatenate([cos, cos, z, z], 1), jnp.concatenate([-sin, z, z, z], 1),
             jnp.concatenate([z, sin, z, z], 1)), (cos.T, sin.T))


def _prep_params(w_in, b_gate, conv_w, conv_b, rg_wa, rg_ba, rg_wx, rg_bx, rg_lambda, w_br_r, q_norm_g, w_uq,
                 kv_norm_g, w_uk, w_uv, w_br_m, w_o, ln1_g, ln1_b, w_up, b_up, w_down, b_down, ln2_g, ln2_b,
                 ln_in_g, ln_in_b):
    l = 0
    vec = lambda a: a.reshape(1, -1).astype(F32)
    wi = w_in[l]
    w_uq_p = jnp.pad(w_uq[l], ((0, 0), (0, 0), (0, HEAD_PAD - QK_NOPE - QK_ROPE))).reshape(Q_LORA, N_HEADS * HEAD_PAD)
    return {
        "ln_in_g": vec(ln_in_g), "ln_in_b": vec(ln_in_b),
        "w_in": wi[:, :_C_KPE].astype(BF16),
        "w_kpe": jnp.pad(wi[:, _C_KPE:_C_GATE], ((0, 0), (0, ROPE_PAD - QK_ROPE))).astype(BF16),
        "w_gate": wi[:, _C_GATE:].astype(BF16), "b_gate": vec(b_gate[l]),
        "q_norm_g": vec(q_norm_g[l]), "w_uq": w_uq_p.astype(BF16), "w_uq_t": w_uq_p.T.astype(BF16),
        "kv_norm_g": vec(kv_norm_g[l]),
        "w_uk": w_uk[l].reshape(KV_LORA, N_HEADS * QK_NOPE).astype(BF16),
        "w_uv": w_uv[l].reshape(KV_LORA, N_HEADS * V_HEAD).astype(BF16),
        "w_uv_t": w_uv[l].reshape(KV_LORA, N_HEADS * V_HEAD).T.astype(BF16),
        "w_ukt": jnp.transpose(w_uk[l], (1, 2, 0)).astype(BF16),
        "w_uvh": jnp.transpose(w_uv[l], (1, 0, 2)).astype(BF16),
        "conv_w": conv_w[l].astype(F32), "conv_b": vec(conv_b[l]),
        "w_ax": jnp.concatenate([rg_wa[l], rg_wx[l]], axis=-1).astype(BF16),
        "rg_ba": vec(rg_ba[l]), "rg_bx": vec(rg_bx[l]), "rg_lambda": vec(rg_lambda[l]),
        "w_br_r": w_br_r[l].astype(BF16), "w_br_m": w_br_m[l].astype(BF16), "w_o": w_o[l].astype(BF16),
        "ln1_g": vec(ln1_g[l]), "ln1_b": vec(ln1_b[l]),
        "w_up": w_up[l].astype(BF16), "b_up": vec(b_up[l]),
        "w_down": w_down[l].astype(BF16), "b_down": vec(b_down[l]),
        "ln2_g": vec(ln2_g[l]), "ln2_b": vec(ln2_b[l]),
    }


def kernel(x_prompt, x_sample, cache_ckv, cache_kpe, page_table, state_conv, state_rglru, meta_tokens, ln_in_g, ln_in_b, w_in, b_gate, conv_w, conv_b, rg_wa, rg_ba, rg_wx, rg_bx, rg_lambda, w_br_r, q_norm_g, w_uq, kv_norm_g, w_uk, w_uv, w_br_m, w_o, ln1_g, ln1_b, w_up, b_up, w_down, b_down, ln2_g, ln2_b):
    assert w_in.shape[0] == DEPTH == 1
    bn, seq, _ = x_prompt.shape
    bd, ts, _ = x_sample.shape
    past_len = page_table.shape[1] * cache_ckv.shape[2]
    p = _prep_params(w_in, b_gate, conv_w, conv_b, rg_wa, rg_ba, rg_wx, rg_bx, rg_lambda, w_br_r, q_norm_g, w_uq,
                     kv_norm_g, w_uk, w_uv, w_br_m, w_o, ln1_g, ln1_b, w_up, b_up, w_down, b_down, ln2_g, ln2_b,
                     ln_in_g, ln_in_b)
    n_s = bd * ts

    x_small = jnp.concatenate([x_sample.reshape(n_s, D_MODEL), meta_tokens.astype(F32)], axis=0)
    pos_small = jnp.concatenate([jnp.tile(past_len + jnp.arange(ts), bd), jnp.arange(N_META)])
    n_small = n_s + N_META
    rx, grg, g_small, q, k, v, ckv, kpe = _in_proj(x_small, _rope_tables(pos_small)[0], 1, p, n_small)

    rep = lambda a: jnp.broadcast_to(a[None], (SUBLANES,) + a.shape)
    y_m, conv_m, h_m = _rglru(rep(rx[n_s:]), rep(grg[n_s:]), jnp.zeros((SUBLANES, CONV_W - 1, D_RNN), F32),
                              jnp.zeros((SUBLANES, D_RNN), F32), p, N_META)
    k_meta, v_meta = k[n_s:], v[n_s:]
    o_m = _meta_attn(q[n_s:], k_meta, v_meta)

    y_s, conv_s, h_s = _rglru(rx[:n_s].reshape(bd, ts, D_RNN), grg[:n_s].reshape(bd, ts, D_RNN),
                              state_conv[0], state_rglru[0], p, ts)
    q_lat, q_pe = _absorb(q[:n_s], p["w_ukt"], bd)
    ckv_s = ckv[:n_s].reshape(bd, ts, KV_LORA)
    kpe_s = kpe[:n_s].reshape(bd, ts, QK_ROPE)
    pad_new = lambda a: jnp.pad(a, ((0, 0), (0, 2 * SUBLANES - ts), (0, 0)))

    n_p = bn * seq
    pos_p = N_META + jnp.arange(seq)
    tm = TM_IN_PROJ
    bcast = lambda a: jnp.broadcast_to(a[:1], (bn,) + a.shape[1:])
    tabs_p, tabs_pt = _rope_tables(pos_p)
    rx_p, grg_p, g_p, qt_p, k_p, vt_p, ckv_p, kpe_p = _in_proj(x_prompt.reshape(n_p, D_MODEL), tabs_p,
                                                               seq // tm, p, tm, tabs_t=tabs_pt)
    y_p, conv_p, h_p = _rglru(rx_p.reshape(bn, seq, D_RNN), grg_p.reshape(bn, seq, D_RNN), bcast(conv_m), bcast(h_m),
                              p, TT_SCAN)
    o_p = _flash(qt_p.reshape(bn, seq // tm, -1, tm), k_p.reshape(bn, seq, -1), vt_p.reshape(bn, seq // tm, -1, tm),
                 k_meta, v_meta.T, TQ_FLASH, FLASH_HEADS)
    out_p, o_lat, prompt_ckv = _post_with_paged_attn(
        x_prompt.reshape(n_p, D_MODEL), y_p.reshape(n_p, D_RNN), o_p.reshape(n_p, -1), g_p, p, TM_POST,
        page_table, q_lat.reshape(bd, N_HEADS * ts, KV_LORA), q_pe.reshape(bd, N_HEADS * ts, ROPE_PAD),
        pad_new(ckv_s), pad_new(kpe_s), cache_ckv[0], jnp.swapaxes(cache_kpe[0], 1, 2),
        ckv_p, ckv[n_s:], bn)

    o_s = _unabsorb(o_lat.reshape(bd, N_HEADS, ts, KV_LORA), p["w_uvh"])
    y_small = jnp.concatenate([y_s.reshape(n_s, D_RNN), y_m[0]], axis=0)
    o_small = jnp.concatenate([o_s, o_m], axis=0)
    out_small = _post(x_small, y_small, o_small, g_small, p, n_small)

    meta_rows = lambda a: jnp.broadcast_to(a[n_s:][None], (bn, N_META, a.shape[-1]))
    y_prompt = out_p.reshape(bn, seq, D_MODEL)
    y_sample = out_small[:n_s].reshape(bd, ts, D_MODEL)
    prompt_kpe = jnp.concatenate([meta_rows(kpe), kpe_p.reshape(bn, seq, QK_ROPE)], axis=1)[None]
    return (y_prompt, y_sample, prompt_ckv[None], prompt_kpe, conv_p[None], h_p[None],
            ckv_s[None], kpe_s[None], conv_s[None], h_s[None])
```

```python
import functools
import math

import jax
import jax.numpy as jnp
from jax import lax
from jax.experimental import pallas as pl
from jax.experimental.pallas import tpu as pltpu

F32 = jnp.float32
BF16 = jnp.bfloat16

D_MODEL = 1024
N_META = 16
D_RNN = D_MODEL
RG_BLOCKS = 8
RG_BLOCK = D_RNN // RG_BLOCKS
CONV_W = 4
RG_C = 8.0
N_HEADS = 8
QK_NOPE = 128
QK_ROPE = 64
V_HEAD = 128
KV_LORA = 512
Q_LORA = 768
ROPE_THETA = 10000.0
D_FF = 4 * D_MODEL
DEPTH = 1
DN_ALPHA = (2.0 * DEPTH) ** 0.25
EPS = 1e-5
SM_SCALE = (QK_NOPE + QK_ROPE) ** -0.5
Q_PRESCALE = SM_SCALE * math.log2(math.e)

LANES = 128
SUBLANES = 8
HEAD_PAD = 2 * LANES
ROPE_PAD = LANES
_C_RX, _C_RG, _C_CQ, _C_CKV, _C_KPE = 0, D_RNN, 2 * D_RNN, 2 * D_RNN + Q_LORA, 2 * D_RNN + Q_LORA + KV_LORA
_C_GATE = _C_KPE + QK_ROPE
VMEM_LIMIT = 56 * 1024 * 1024
TM_IN_PROJ = 256
TM_POST = 256
TT_SCAN = 128
TQ_FLASH = 512
FLASH_HEADS = 4
NEG = float(jnp.finfo(jnp.float32).min)


def _const_spec(shape):
    return pl.BlockSpec(shape, lambda *_: (0,) * len(shape), pipeline_mode=pl.Buffered(1))


def _layer_norm(x, g, b):
    mu = jnp.mean(x, -1, keepdims=True)
    xc = x - mu
    var = jnp.mean(xc * xc, -1, keepdims=True)
    return xc * lax.rsqrt(var + EPS) * g + b


def _rms_norm(x, g):
    return x * lax.rsqrt(jnp.mean(x * x, -1, keepdims=True) + EPS) * g


def _gelu_tanh(x):
    return 0.5 * x * (1.0 + jnp.tanh(math.sqrt(2.0 / math.pi) * (x + 0.044715 * (x * x * x))))


def _rope_block(y, cos, sin_lo, sin_hi):
    left = pltpu.roll(y, ROPE_PAD - QK_ROPE // 2, 1)
    right = pltpu.roll(y, QK_ROPE // 2, 1)
    return y * cos + left * sin_lo + right * sin_hi


def _dot(a, b):
    return jnp.dot(a, b, preferred_element_type=F32)


def _dot_nt(a, b):
    return lax.dot_general(a, b, (((1,), (1,)), ((), ())), preferred_element_type=F32)


CONV_TAIL = SUBLANES - (CONV_W - 1)


def _conv_tile(xe_ref, x, cw_ref, cb_ref):
    tt = x.shape[1]
    xe_ref[:, SUBLANES:, :] = x
    xc = cb_ref[...] + x * cw_ref[CONV_W - 1:CONV_W, :]
    for k in range(CONV_W - 1):
        xc = xc + xe_ref[:, CONV_TAIL + k:CONV_TAIL + k + tt, :] * cw_ref[k:k + 1, :]
    new_tail = xe_ref[:, tt + CONV_TAIL:tt + SUBLANES, :]
    xe_ref[:, CONV_TAIL:SUBLANES, :] = new_tail
    return xc, new_tail


def _decay_rate(lam_ref):
    z = -lam_ref[...]
    return RG_C * (jnp.maximum(z, 0.0) + jnp.log1p(jnp.exp(-jnp.abs(z))))


def _recurrence_inputs(xn, wax_n, ba_n, bx_n, rate_n):
    gates = _dot(xn.astype(BF16), wax_n)
    r = jax.nn.sigmoid(gates[:, :RG_BLOCK] + ba_n)
    ig = jax.nn.sigmoid(gates[:, RG_BLOCK:] + bx_n)
    neg_log_a = r * rate_n
    a = jnp.exp(-neg_log_a)
    u = jnp.sqrt(jnp.tanh(neg_log_a) * (a * a + 1.0)) * (ig * xn)
    return a, u


def _scan_slabs(a_ref, u_ref, hc_ref, nb, tt):
    pitch = tt + SUBLANES
    n_slab = D_RNN // LANES
    for grp in range(nb // SUBLANES):
        base = grp * SUBLANES * pitch
        rows = slice(grp * SUBLANES, (grp + 1) * SUBLANES)
        h0 = tuple(hc_ref[rows, n * LANES:(n + 1) * LANES] for n in range(n_slab))

        def step(t, hs, base=base):
            out = []
            for n in range(n_slab):
                idx = pl.ds(base + t, SUBLANES, stride=pitch)
                hn = a_ref[n, idx, :] * hs[n] + u_ref[n, idx, :]
                u_ref[n, idx, :] = hn
                out.append(hn)
            return tuple(out)

        hs = lax.fori_loop(0, tt, step, h0, unroll=min(tt, 8))
        for n in range(n_slab):
            hc_ref[rows, n * LANES:(n + 1) * LANES] = hs[n]


def _gated_branch_output(y_ref, h_ref, grg_ref, nb, tt):
    pitch = tt + SUBLANES
    for b in range(nb):
        for n in range(D_RNN // LANES):
            cs = slice(n * LANES, (n + 1) * LANES)
            y_ref[b, :, cs] = (h_ref[n, b * pitch:b * pitch + tt, :] * grg_ref[b, :, cs]).astype(BF16)


def _in_proj_kernel(x_ref, cos_ref, slo_ref, shi_ref, lng_ref, lnb_ref, w_in_ref, w_gate_ref, bg_ref,
                    qg_ref, wuq_ref, kvg_ref, wuk_ref, wuv_ref, *rest, token_minor):
    if token_minor:
        cost_ref, sint_ref, *rest = rest
    rx_ref, grg_ref, g_ref, q_ref, k_ref, v_ref, ckv_ref, kpe_ref = rest
    h = _layer_norm(x_ref[...], lng_ref[...], lnb_ref[...])
    hb = h.astype(BF16)

    def proj(lo, hi):
        return _dot(hb, w_in_ref[:, lo:hi])

    cos, slo, shi = cos_ref[...], slo_ref[...], shi_ref[...]
    cqn = _rms_norm(proj(_C_CQ, _C_CKV), qg_ref[...])
    ckv = _rms_norm(proj(_C_CKV, _C_KPE), kvg_ref[...])
    kpe_raw = proj(_C_KPE, _C_KPE + ROPE_PAD)
    lane = lax.broadcasted_iota(jnp.int32, kpe_raw.shape, 1)
    kpe = _rope_block(jnp.where(lane < QK_ROPE, kpe_raw, 0.0), cos, slo, shi)
    ckv_ref[...] = ckv
    kpe_ref[...] = kpe[:, :QK_ROPE]
    ckvb = ckv.astype(BF16)
    kpeb = kpe.astype(BF16)

    rx_ref[...] = proj(_C_RX, _C_RG)
    grg_ref[...] = _gelu_tanh(proj(_C_RG, _C_CQ))
    g_ref[...] = jax.nn.sigmoid(_dot(hb, w_gate_ref[...]) + bg_ref[...])

    cqb = cqn.astype(BF16)
    if token_minor:
        qt = _dot_nt(wuq_ref[...], cqb) * Q_PRESCALE
        cos_t, sin_t = cost_ref[...], sint_ref[...]
        half = QK_ROPE // 2
        for hh in range(N_HEADS):
            r0 = hh * HEAD_PAD
            r1, r2, r3 = r0 + QK_NOPE, r0 + QK_NOPE + half, r0 + QK_NOPE + QK_ROPE
            x1, x2 = qt[r1:r2], qt[r2:r3]
            q_ref[r0:r1, :] = qt[r0:r1].astype(BF16)
            q_ref[r1:r2, :] = (x1 * cos_t - x2 * sin_t).astype(BF16)
            q_ref[r2:r3, :] = (x1 * sin_t + x2 * cos_t).astype(BF16)
            q_ref[r3:r0 + HEAD_PAD, :] = qt[r3:r0 + HEAD_PAD].astype(BF16)
    else:
        q = _dot(cqb, wuq_ref[...]) * Q_PRESCALE
        for hh in range(N_HEADS):
            c0 = hh * HEAD_PAD
            q_ref[:, c0:c0 + QK_NOPE] = q[:, c0:c0 + QK_NOPE].astype(BF16)
            q_ref[:, c0 + QK_NOPE:c0 + HEAD_PAD] = _rope_block(q[:, c0 + QK_NOPE:c0 + HEAD_PAD],
                                                               cos, slo, shi).astype(BF16)

    kn = _dot(ckvb, wuk_ref[...])
    if token_minor:
        v_ref[...] = _dot_nt(wuv_ref[...], ckvb).astype(BF16)
    else:
        v_ref[...] = _dot(ckvb, wuv_ref[...]).astype(BF16)
    for hh in range(N_HEADS):
        c0 = hh * HEAD_PAD
        k_ref[:, c0:c0 + QK_NOPE] = kn[:, hh * QK_NOPE:(hh + 1) * QK_NOPE].astype(BF16)
        k_ref[:, c0 + QK_NOPE:c0 + HEAD_PAD] = kpeb


def _in_proj(x, tabs, tab_blocks, p, tm, tabs_t=None):
    token_minor = tabs_t is not None
    rows = x.shape[0]
    assert rows % tm == 0
    n_tiles = rows // tm
    row = lambda w: pl.BlockSpec((tm, w), lambda i: (i, 0))
    tab = pl.BlockSpec((tm, ROPE_PAD), lambda i: (i % tab_blocks, 0))
    outs = [
        (D_RNN, F32),
        (D_RNN, F32),
        (2 * D_MODEL, F32),
        (N_HEADS * HEAD_PAD, BF16),
        (N_HEADS * HEAD_PAD, BF16),
        (N_HEADS * V_HEAD, BF16),
        (KV_LORA, F32),
        (QK_ROPE, F32),
    ]
    out_specs = [row(w) for w, _ in outs]
    out_shape = [jax.ShapeDtypeStruct((rows, w), dt) for w, dt in outs]
    maybe_t = lambda shape: shape[::-1] if token_minor else shape
    in_specs = [row(D_MODEL), tab, tab, tab,
                _const_spec((1, D_MODEL)), _const_spec((1, D_MODEL)),
                _const_spec((D_MODEL, _C_GATE + 2 * D_MODEL)),
                _const_spec((D_MODEL, 2 * D_MODEL)), _const_spec((1, 2 * D_MODEL)),
                _const_spec((1, Q_LORA)), _const_spec(maybe_t((Q_LORA, N_HEADS * HEAD_PAD))),
                _const_spec((1, KV_LORA)), _const_spec((KV_LORA, N_HEADS * QK_NOPE)),
                _const_spec(maybe_t((KV_LORA, N_HEADS * V_HEAD)))]
    args = [x, *tabs, p["ln_in_g"], p["ln_in_b"], p["w_in"], p["w_gate"], p["b_gate"], p["q_norm_g"],
            p["w_uq_t"] if token_minor else p["w_uq"], p["kv_norm_g"], p["w_uk"],
            p["w_uv_t"] if token_minor else p["w_uv"]]
    if token_minor:
        for j in (3, 5):
            out_specs[j] = pl.BlockSpec((None, outs[j][0], tm), lambda i: (i, 0, 0))
            out_shape[j] = jax.ShapeDtypeStruct((n_tiles, outs[j][0], tm), BF16)
        tab_t = pl.BlockSpec((QK_ROPE // 2, tm), lambda i: (0, i % tab_blocks))
        in_specs += [tab_t, tab_t]
        args += list(tabs_t)
    return pl.pallas_call(
        functools.partial(_in_proj_kernel, token_minor=token_minor),
        grid=(n_tiles,),
        in_specs=in_specs,
        out_specs=out_specs,
        out_shape=out_shape,
        compiler_params=pltpu.CompilerParams(dimension_semantics=("arbitrary",), vmem_limit_bytes=VMEM_LIMIT),
        name="in_proj",
    )(*args)


def _rglru_kernel(rx_ref, grg_ref, cprev_ref, hprev_ref, cw_ref, cb_ref, wax_ref, ba_ref, bx_ref, lam_ref,
                  y_ref, cout_ref, hout_ref, xe_ref, a_ref, u_ref, hc_ref, *, nb, tt):
    pitch = tt + SUBLANES

    @pl.when(pl.program_id(0) == 0)
    def _():
        xe_ref[:, CONV_TAIL:SUBLANES, :] = cprev_ref[...]
        hc_ref[...] = hprev_ref[...]

    xc, new_tail = _conv_tile(xe_ref, rx_ref[...], cw_ref, cb_ref)
    cout_ref[...] = new_tail
    rate = _decay_rate(lam_ref)
    xc2 = xc.reshape(nb * tt, D_RNN)
    for n in range(D_RNN // LANES):
        cs = slice(n * LANES, (n + 1) * LANES)
        a, u = _recurrence_inputs(xc2[:, cs], wax_ref[n], ba_ref[:, cs], bx_ref[:, cs], rate[:, cs])
        for b in range(nb):
            a_ref[n, b * pitch:b * pitch + tt, :] = a[b * tt:(b + 1) * tt]
            u_ref[n, b * pitch:b * pitch + tt, :] = u[b * tt:(b + 1) * tt]
    _scan_slabs(a_ref, u_ref, hc_ref, nb, tt)
    hout_ref[...] = hc_ref[...]
    _gated_branch_output(y_ref, u_ref, grg_ref, nb, tt)


def _rglru(rx, grg, conv_prev, h_prev, p, tt):
    nb, t_len, _ = rx.shape
    assert nb % SUBLANES == 0 and t_len % tt == 0 and tt % SUBLANES == 0
    pitch = tt + SUBLANES
    seq = pl.BlockSpec((nb, tt, D_RNN), lambda i: (0, i, 0))
    return pl.pallas_call(
        functools.partial(_rglru_kernel, nb=nb, tt=tt),
        grid=(t_len // tt,),
        in_specs=[seq, seq, _const_spec((nb, CONV_W - 1, D_RNN)), _const_spec((nb, D_RNN)),
                  _const_spec((CONV_W, D_RNN)), _const_spec((1, D_RNN)),
                  _const_spec((RG_BLOCKS, RG_BLOCK, 2 * RG_BLOCK)),
                  _const_spec((1, D_RNN)), _const_spec((1, D_RNN)), _const_spec((1, D_RNN))],
        out_specs=[seq,
                   pl.BlockSpec((nb, CONV_W - 1, D_RNN), lambda i: (0, 0, 0)),
                   pl.BlockSpec((nb, D_RNN), lambda i: (0, 0))],
        out_shape=[jax.ShapeDtypeStruct((nb, t_len, D_RNN), BF16),
                   jax.ShapeDtypeStruct((nb, CONV_W - 1, D_RNN), F32),
                   jax.ShapeDtypeStruct((nb, D_RNN), F32)],
        scratch_shapes=[pltpu.VMEM((nb, tt + SUBLANES, D_RNN), F32),
                        pltpu.VMEM((D_RNN // LANES, nb * pitch, LANES), F32),
                        pltpu.VMEM((D_RNN // LANES, nb * pitch, LANES), F32),
                        pltpu.VMEM((nb, D_RNN), F32)],
        compiler_params=pltpu.CompilerParams(dimension_semantics=("arbitrary",), vmem_limit_bytes=VMEM_LIMIT),
        name="rglru",
    )(rx, grg, conv_prev, h_prev, p["conv_w"], p["conv_b"], p["w_ax"], p["rg_ba"], p["rg_bx"], p["rg_lambda"])


def _softmax_update(s, m, l, acc, v):
    m_new = jnp.maximum(m, jnp.max(s, axis=1, keepdims=True))
    alpha = jnp.exp2(m - m_new)
    pr = jnp.exp2(s - m_new)
    l = alpha * l + jnp.sum(pr, axis=1, keepdims=True)
    acc = alpha * acc + _dot(pr.astype(BF16), v)
    return m_new, l, acc


def _softmax_update_t(s_t, m, l, acc_t, v_t):
    m_new = jnp.maximum(m, jnp.max(s_t, axis=0, keepdims=True))
    alpha = jnp.exp2(m - m_new)
    pr = jnp.exp2(s_t - m_new)
    l = alpha * l + jnp.sum(pr, axis=0, keepdims=True)
    acc_t = alpha * acc_t + _dot(v_t, pr.astype(BF16))
    return m_new, l, acc_t


def _flash_kernel(qt_ref, k_ref, vt_ref, km_ref, vmt_ref, o_ref, *, tq, tv, hps):
    qcols = [slice(hh * HEAD_PAD, (hh + 1) * HEAD_PAD) for hh in range(hps)]
    vrows = [slice(hh * V_HEAD, (hh + 1) * V_HEAD) for hh in range(hps)]

    def queries_t(hh):
        return jnp.concatenate([qt_ref[c, qcols[hh], :] for c in range(tq // tv)], axis=1)

    def scores(j, hh):
        return _dot(k_ref[j * tq:(j + 1) * tq, qcols[hh]], queries_t(hh))

    def values_t(j, hh):
        return jnp.concatenate([vt_ref[j * (tq // tv) + c, vrows[hh], :] for c in range(tq // tv)], axis=1)

    def tile(n_full):
        kpos = lax.broadcasted_iota(jnp.int32, (tq, tq), 0)
        qpos = lax.broadcasted_iota(jnp.int32, (tq, tq), 1)
        state, s_next = [], []
        for hh in range(hps):
            s_t = _dot(km_ref[:, qcols[hh]], queries_t(hh))
            m = jnp.max(s_t, axis=0, keepdims=True)
            pr = jnp.exp2(s_t - m)
            state.append((m, jnp.sum(pr, axis=0, keepdims=True), _dot(vmt_ref[vrows[hh], :], pr.astype(BF16))))
            s_next.append(scores(0, hh))
        for j in range(n_full + 1):
            for hh in range(hps):
                s_t = s_next[hh]
                if j < n_full:
                    s_next[hh] = scores(j + 1, hh)
                else:
                    s_t = jnp.where(kpos <= qpos, s_t, NEG)
                state[hh] = _softmax_update_t(s_t, *state[hh], values_t(j, hh))
        for hh in range(hps):
            _, l, acc_t = state[hh]
            o_ref[:, vrows[hh]] = jnp.transpose(acc_t / l).astype(BF16)

    for c in range(k_ref.shape[0] // tq):
        pl.when(pl.program_id(2) == c)(functools.partial(tile, c))


def _flash(q_t, k, v_t, k_pre, v_pre_t, tq, hps):
    nb, t_len, _ = k.shape
    tv = v_t.shape[3]
    assert t_len % tq == 0 and tq % tv == 0 and N_HEADS % hps == 0
    return pl.pallas_call(
        functools.partial(_flash_kernel, tq=tq, tv=tv, hps=hps),
        grid=(nb, N_HEADS // hps, t_len // tq),
        in_specs=[pl.BlockSpec((None, tq // tv, hps * HEAD_PAD, tv), lambda b, h, i: (b, i, h, 0)),
                  pl.BlockSpec((None, t_len, hps * HEAD_PAD), lambda b, h, i: (b, 0, h)),
                  pl.BlockSpec((None, t_len // tv, hps * V_HEAD, tv), lambda b, h, i: (b, 0, h, 0)),
                  pl.BlockSpec((N_META, hps * HEAD_PAD), lambda b, h, i: (0, h)),
                  pl.BlockSpec((hps * V_HEAD, N_META), lambda b, h, i: (h, 0))],
        out_specs=pl.BlockSpec((None, tq, hps * V_HEAD), lambda b, h, i: (b, i, h)),
        out_shape=jax.ShapeDtypeStruct((nb, t_len, N_HEADS * V_HEAD), BF16),
        compiler_params=pltpu.CompilerParams(dimension_semantics=("arbitrary",) * 3, vmem_limit_bytes=VMEM_LIMIT),
        name="flash_prompt",
    )(q_t, k, v_t, k_pre, v_pre_t)


def _meta_attn_kernel(q_ref, k_ref, v_ref, o_ref):
    qpos = lax.broadcasted_iota(jnp.int32, (N_META, N_META), 0)
    kpos = lax.broadcasted_iota(jnp.int32, (N_META, N_META), 1)
    for hh in range(N_HEADS):
        qs = slice(hh * HEAD_PAD, (hh + 1) * HEAD_PAD)
        vs = slice(hh * V_HEAD, (hh + 1) * V_HEAD)
        s = jnp.where(kpos <= qpos, _dot_nt(q_ref[:, qs], k_ref[:, qs]), NEG)
        pr = jnp.exp2(s - jnp.max(s, axis=1, keepdims=True))
        o = _dot(pr.astype(BF16), v_ref[:, vs]) / jnp.sum(pr, axis=1, keepdims=True)
        o_ref[:, vs] = o.astype(BF16)


def _meta_attn(q, k, v):
    return pl.pallas_call(
        _meta_attn_kernel,
        out_shape=jax.ShapeDtypeStruct((N_META, N_HEADS * V_HEAD), BF16),
        name="meta_attn",
    )(q, k, v)


def _absorb_kernel(q_ref, wukt_ref, qlat_ref, qpe_ref):
    q = q_ref[...]
    nb = qlat_ref.shape[0]
    qlat = _dot(q[:, :QK_NOPE], wukt_ref[...])
    qlat_ref[...] = qlat.reshape(nb, -1, KV_LORA)
    qpe_ref[...] = q[:, QK_NOPE:].astype(F32).reshape(nb, -1, ROPE_PAD)


def _absorb(q, w_ukt, nb):
    rows = q.shape[0]
    ts = rows // nb
    return pl.pallas_call(
        _absorb_kernel,
        grid=(N_HEADS,),
        in_specs=[pl.BlockSpec((rows, HEAD_PAD), lambda h: (0, h)),
                  pl.BlockSpec((None, QK_NOPE, KV_LORA), lambda h: (h, 0, 0))],
        out_specs=[pl.BlockSpec((nb, None, ts, KV_LORA), lambda h: (0, h, 0, 0)),
                   pl.BlockSpec((nb, None, ts, ROPE_PAD), lambda h: (0, h, 0, 0))],
        out_shape=[jax.ShapeDtypeStruct((nb, N_HEADS, ts, KV_LORA), F32),
                   jax.ShapeDtypeStruct((nb, N_HEADS, ts, ROPE_PAD), F32)],
        compiler_params=pltpu.CompilerParams(dimension_semantics=("arbitrary",)),
        name="absorb_q",
    )(q, w_ukt)


PAGED_BUFS = 2


def _paged_scratch(n_q, pps, page):
    return [pltpu.VMEM((n_q, 1), F32), pltpu.VMEM((n_q, 1), F32), pltpu.VMEM((n_q, KV_LORA), F32),
            pltpu.VMEM((PAGED_BUFS, pps * page, KV_LORA), F32), pltpu.VMEM((PAGED_BUFS, pps, QK_ROPE, page), F32),
            pltpu.VMEM((pps * page, KV_LORA), BF16), pltpu.VMEM((QK_ROPE, pps * page), BF16),
            pltpu.SemaphoreType.DMA((PAGED_BUFS,))]


def _paged_copies(pt_ref, ckv_hbm, kpe_hbm, cin_ref, kin_ref, sem, seq, chunk, slot, pps):
    page = kin_ref.shape[3]
    out = []
    for i in range(pps):
        pid = pt_ref[seq, chunk * pps + i]
        out.append(pltpu.make_async_copy(ckv_hbm.at[pid], cin_ref.at[slot, pl.ds(i * page, page)], sem.at[slot]))
        out.append(pltpu.make_async_copy(kpe_hbm.at[pid], kin_ref.at[slot, i], sem.at[slot]))
    return out


def _paged_new_tokens(qlat, qpe, cnew_ref, knew_ref, m_ref, l_ref, acc_ref, ts):
    cn = cnew_ref[...].astype(BF16)
    kn = knew_ref[...].astype(BF16)
    s = _dot_nt(qlat, cn) + _dot_nt(qpe, kn)
    t_q = lax.broadcasted_iota(jnp.int32, s.shape, 0) % ts
    t_k = lax.broadcasted_iota(jnp.int32, s.shape, 1)
    s = jnp.where(t_k <= t_q, s, NEG)
    m = jnp.max(s, axis=1, keepdims=True)
    pr = jnp.exp2(s - m)
    m_ref[...] = m
    l_ref[...] = jnp.sum(pr, axis=1, keepdims=True)
    acc_ref[...] = _dot(pr.astype(BF16), cn)


def _paged_scores(qlat, qpe, cin_ref, kin_ref, cbuf_ref, kbuf_ref, slot, pps):
    page = kin_ref.shape[3]
    for i in range(pps):
        rows = pl.ds(i * page, page)
        cbuf_ref[rows, :] = cin_ref[slot, rows, :].astype(BF16)
        kbuf_ref[:, i * page:(i + 1) * page] = kin_ref[slot, i].astype(BF16)
    return _dot_nt(qlat, cbuf_ref[...]) + _dot(qpe, kbuf_ref[...])


def _paged_probs(s, m_ref, l_ref):
    m = m_ref[...]
    m_new = jnp.maximum(m, jnp.max(s, axis=1, keepdims=True))
    alpha = jnp.exp2(m - m_new)
    pr = jnp.exp2(s - m_new)
    m_ref[...] = m_new
    l_ref[...] = alpha * l_ref[...] + jnp.sum(pr, axis=1, keepdims=True)
    return alpha, pr.astype(BF16)


def _paged_accumulate(alpha, pr, cbuf_ref, acc_ref):
    acc_ref[...] = alpha * acc_ref[...] + _dot(pr, cbuf_ref[...])


def _unabsorb_kernel(olat_ref, wuv_ref, o_ref):
    x = olat_ref[...]
    x = x.reshape(x.shape[0] * x.shape[1], KV_LORA).astype(BF16)
    o_ref[...] = _dot(x, wuv_ref[...]).astype(BF16)


def _unabsorb(o_lat, w_uvh):
    nb, _, ts, _ = o_lat.shape
    return pl.pallas_call(
        _unabsorb_kernel,
        grid=(N_HEADS,),
        in_specs=[pl.BlockSpec((nb, None, ts, KV_LORA), lambda h: (0, h, 0, 0)),
                  pl.BlockSpec((None, KV_LORA, V_HEAD), lambda h: (h, 0, 0))],
        out_specs=pl.BlockSpec((nb * ts, V_HEAD), lambda h: (0, h)),
        out_shape=jax.ShapeDtypeStruct((nb * ts, N_HEADS * V_HEAD), BF16),
        compiler_params=pltpu.CompilerParams(dimension_semantics=("arbitrary",)),
        name="unabsorb_o",
    )(o_lat, w_uvh)


FF_CHUNKS = 4
N_POST_WEIGHTS = 13


def _post_compute(x_ref, y_ref, o_ref, g_ref, weights, out_ref, side_work=None):
    before_up, before_down, after_down = side_work if side_work is not None else (lambda c: None,) * 3
    lng_ref, lnb_ref, wr_ref, wm_ref, wo_ref, g1_ref, b1_ref, wup_ref, bup_ref, wdn_ref, bdn_ref, g2_ref, b2_ref = weights
    h = _layer_norm(x_ref[...], lng_ref[...], lnb_ref[...])
    z_r = _dot(y_ref[...], wr_ref[...])
    z_m = _dot(o_ref[...], wm_ref[...])
    mix_in = g_ref[:, :D_MODEL] * z_r + g_ref[:, D_MODEL:] * z_m
    mix = _dot(mix_in.astype(BF16), wo_ref[...])
    before_up(0)
    x1 = _layer_norm(DN_ALPHA * h + mix, g1_ref[...], b1_ref[...])
    x1b = x1.astype(BF16)
    f = bdn_ref[...]
    ff_chunk = D_FF // FF_CHUNKS
    for c in range(FF_CHUNKS):
        cs = slice(c * ff_chunk, (c + 1) * ff_chunk)
        if c > 0:
            before_up(c)
        up = jnp.maximum(_dot(x1b, wup_ref[:, cs]) + bup_ref[:, cs], 0.0)
        before_down(c)
        f = f + _dot((up * up).astype(BF16), wdn_ref[cs, :])
        after_down(c)
    out_ref[...] = _layer_norm(DN_ALPHA * x1 + f, g2_ref[...], b2_ref[...])


def _post_kernel(x_ref, y_ref, o_ref, g_ref, *rest):
    _post_compute(x_ref, y_ref, o_ref, g_ref, rest[:N_POST_WEIGHTS], rest[N_POST_WEIGHTS])


def _post_paged_kernel(pt_ref, x_ref, y_ref, o_ref, g_ref, *rest, pps, ts, n_j, n_steps, tiles_per_seq):
    weights = rest[:N_POST_WEIGHTS]
    (qlat_ref, qpe_ref, cnew_ref, knew_ref, lat_ref, latm_ref, ckv_hbm, kpe_hbm, out_ref, olat_ref, platent_hbm,
     m_ref, l_ref, acc_ref, cin_ref, kin_ref, cbuf_ref, kbuf_ref, sem, lsem) = rest[N_POST_WEIGHTS:]
    s = pl.program_id(0)
    spb = n_j // FF_CHUNKS
    seq, j0 = lax.div(s, spb), lax.rem(s, spb) * FF_CHUNKS
    copies = functools.partial(_paged_copies, pt_ref, ckv_hbm, kpe_hbm, cin_ref, kin_ref, sem, pps=pps)

    tm = out_ref.shape[0]
    lat_rows = pltpu.make_async_copy(
        lat_ref,
        platent_hbm.at[lax.div(s, tiles_per_seq), pl.ds(N_META + lax.rem(s, tiles_per_seq) * tm, tm)], lsem.at[0])
    lat_meta = [pltpu.make_async_copy(latm_ref, platent_hbm.at[b, pl.ds(0, N_META)], lsem.at[1])
                for b in range(platent_hbm.shape[0])]
    lat_rows.start()

    @pl.when(s == 0)
    def _():
        for cp in lat_meta:
            cp.start()
        for cp in copies(0, 0, 0):
            cp.start()

    live = {}

    def gather_and_score(c):
        slot, nxt = c % PAGED_BUFS, (c + 1) % PAGED_BUFS
        if c + 1 < FF_CHUNKS:
            for cp in copies(seq, j0 + c + 1, nxt):
                cp.start()
        else:
            @pl.when(s + 1 < n_steps)
            def _():
                for cp in copies(lax.div(s + 1, spb), lax.rem(s + 1, spb) * FF_CHUNKS, nxt):
                    cp.start()
        for cp in copies(seq, j0 + c, slot):
            cp.wait()
        qlat = qlat_ref[...].astype(BF16)
        qpe = qpe_ref[...][:, :QK_ROPE].astype(BF16)
        if c == 0:
            pl.when(j0 == 0)(functools.partial(_paged_new_tokens, qlat, qpe, cnew_ref, knew_ref,
                                               m_ref, l_ref, acc_ref, ts))
        live["s"] = _paged_scores(qlat, qpe, cin_ref, kin_ref, cbuf_ref, kbuf_ref, slot, pps)

    def probs(c):
        live["alpha"], live["pr"] = _paged_probs(live.pop("s"), m_ref, l_ref)

    def accumulate(c):
        _paged_accumulate(live.pop("alpha"), live.pop("pr"), cbuf_ref, acc_ref)
        if c == FF_CHUNKS - 1:
            @pl.when(j0 == n_j - FF_CHUNKS)
            def _():
                olat_ref[...] = acc_ref[...] / l_ref[...]

    _post_compute(x_ref, y_ref, o_ref, g_ref, weights, out_ref, side_work=(gather_and_score, probs, accumulate))

    lat_rows.wait()

    @pl.when(s == 0)
    def _():
        for cp in lat_meta:
            cp.wait()


def _post_weight_specs():
    vec = lambda w: _const_spec((1, w))
    sq = _const_spec((D_MODEL, D_MODEL))
    return [vec(D_MODEL), vec(D_MODEL), sq, sq, sq, vec(D_MODEL), vec(D_MODEL),
            _const_spec((D_MODEL, D_FF)), vec(D_FF), _const_spec((D_FF, D_MODEL)), vec(D_MODEL),
            vec(D_MODEL), vec(D_MODEL)]


def _post_weights(p):
    return (p["ln_in_g"], p["ln_in_b"], p["w_br_r"], p["w_br_m"], p["w_o"], p["ln1_g"], p["ln1_b"],
            p["w_up"], p["b_up"], p["w_down"], p["b_down"], p["ln2_g"], p["ln2_b"])


def _post(x, y, o, g, p, tm):
    rows = x.shape[0]
    assert rows % tm == 0
    row = lambda w: pl.BlockSpec((tm, w), lambda i: (i, 0))
    return pl.pallas_call(
        _post_kernel,
        grid=(rows // tm,),
        in_specs=[row(D_MODEL), row(D_RNN), row(N_HEADS * V_HEAD), row(2 * D_MODEL)] + _post_weight_specs(),
        out_specs=row(D_MODEL),
        out_shape=jax.ShapeDtypeStruct((rows, D_MODEL), F32),
        compiler_params=pltpu.CompilerParams(dimension_semantics=("arbitrary",), vmem_limit_bytes=VMEM_LIMIT),
        name="merge_mlp",
    )(x, y, o, g, *_post_weights(p))


def _post_with_paged_attn(x, y, o, g, p, tm, page_table, q_lat, q_pe, ckv_new, kpe_new, ckv_pool, kpe_pool_t,
                          latent, latent_meta, n_seq):
    rows = x.shape[0]
    n_steps = rows // tm
    assert n_steps % n_seq == 0
    nb, n_pages = page_table.shape
    n_q = q_lat.shape[1]
    page = ckv_pool.shape[1]
    n_new = ckv_new.shape[1]
    chunks = n_steps * FF_CHUNKS
    assert rows % tm == 0 and (nb * n_pages) % chunks == 0 and FF_CHUNKS % PAGED_BUFS == 0
    pps = nb * n_pages // chunks
    n_j = n_pages // pps
    assert n_pages % pps == 0 and n_j % FF_CHUNKS == 0
    spb = n_j // FF_CHUNKS
    row = lambda w: pl.BlockSpec((tm, w), lambda i, pt: (i, 0))
    per_seq = lambda r, w: pl.BlockSpec((None, r, w), lambda i, pt: (i // spb, 0, 0))
    grid_spec = pltpu.PrefetchScalarGridSpec(
        num_scalar_prefetch=1,
        grid=(n_steps,),
        in_specs=[row(D_MODEL), row(D_RNN), row(N_HEADS * V_HEAD), row(2 * D_MODEL)] + _post_weight_specs()
                 + [per_seq(n_q, KV_LORA), per_seq(n_q, ROPE_PAD), per_seq(n_new, KV_LORA), per_seq(n_new, QK_ROPE),
                    row(KV_LORA), _const_spec((N_META, KV_LORA))]
                 + [pl.BlockSpec(memory_space=pl.ANY)] * 2,
        out_specs=[row(D_MODEL), per_seq(n_q, KV_LORA), pl.BlockSpec(memory_space=pl.ANY)],
        scratch_shapes=_paged_scratch(n_q, pps, page) + [pltpu.SemaphoreType.DMA((2,))],
    )
    return pl.pallas_call(
        functools.partial(_post_paged_kernel, pps=pps, ts=n_q // N_HEADS, n_j=n_j, n_steps=n_steps,
                          tiles_per_seq=n_steps // n_seq),
        grid_spec=grid_spec,
        out_shape=[jax.ShapeDtypeStruct((rows, D_MODEL), F32), jax.ShapeDtypeStruct((nb, n_q, KV_LORA), F32),
                   jax.ShapeDtypeStruct((n_seq, N_META + rows // n_seq, KV_LORA), F32)],
        compiler_params=pltpu.CompilerParams(dimension_semantics=("arbitrary",), vmem_limit_bytes=VMEM_LIMIT),
        name="merge_mlp_paged_attn",
    )(page_table, x, y, o, g, *_post_weights(p), q_lat, q_pe, ckv_new, kpe_new, latent, latent_meta,
      ckv_pool, kpe_pool_t)


def _rope_tables(pos):
    half = QK_ROPE // 2
    inv = 1.0 / (ROPE_THETA ** (jnp.arange(half, dtype=F32) / half))
    ang = pos.astype(F32)[:, None] * inv[None, :]
    cos, sin = jnp.cos(ang), jnp.sin(ang)
    z = jnp.zeros_like(cos)
    return ((jnp.concatenate([cos, cos, z, z], 1), jnp.concatenate([-sin, z, z, z], 1),
             jnp.concatenate([z, sin, z, z], 1)), (cos.T, sin.T))


def _prep_params(w_in, b_gate, conv_w, conv_b, rg_wa, rg_ba, rg_wx, rg_bx, rg_lambda, w_br_r, q_norm_g, w_uq,
                 kv_norm_g, w_uk, w_uv, w_br_m, w_o, ln1_g, ln1_b, w_up, b_up, w_down, b_down, ln2_g, ln2_b,
                 ln_in_g, ln_in_b):
    l = 0
    vec = lambda a: a.reshape(1, -1).astype(F32)
    wi = w_in[l].astype(BF16)
    w_uq_p = jnp.pad(w_uq[l], ((0, 0), (0, 0), (0, HEAD_PAD - QK_NOPE - QK_ROPE))).reshape(Q_LORA, N_HEADS * HEAD_PAD)
    return {
        "ln_in_g": vec(ln_in_g), "ln_in_b": vec(ln_in_b),
        "w_in": wi, "w_gate": wi[:, _C_GATE:], "b_gate": vec(b_gate[l]),
        "q_norm_g": vec(q_norm_g[l]), "w_uq": w_uq_p.astype(BF16), "w_uq_t": w_uq_p.T.astype(BF16),
        "kv_norm_g": vec(kv_norm_g[l]),
        "w_uk": w_uk[l].reshape(KV_LORA, N_HEADS * QK_NOPE).astype(BF16),
        "w_uv": w_uv[l].reshape(KV_LORA, N_HEADS * V_HEAD).astype(BF16),
        "w_uv_t": w_uv[l].reshape(KV_LORA, N_HEADS * V_HEAD).T.astype(BF16),
        "w_ukt": jnp.transpose(w_uk[l], (1, 2, 0)).astype(BF16),
        "w_uvh": jnp.transpose(w_uv[l], (1, 0, 2)).astype(BF16),
        "conv_w": conv_w[l].astype(F32), "conv_b": vec(conv_b[l]),
        "w_ax": jnp.concatenate([rg_wa[l], rg_wx[l]], axis=-1).astype(BF16),
        "rg_ba": vec(rg_ba[l]), "rg_bx": vec(rg_bx[l]), "rg_lambda": vec(rg_lambda[l]),
        "w_br_r": w_br_r[l].astype(BF16), "w_br_m": w_br_m[l].astype(BF16), "w_o": w_o[l].astype(BF16),
        "ln1_g": vec(ln1_g[l]), "ln1_b": vec(ln1_b[l]),
        "w_up": w_up[l].astype(BF16), "b_up": vec(b_up[l]),
        "w_down": w_down[l].astype(BF16), "b_down": vec(b_down[l]),
        "ln2_g": vec(ln2_g[l]), "ln2_b": vec(ln2_b[l]),
    }


def kernel(x_prompt, x_sample, cache_ckv, cache_kpe, page_table, state_conv, state_rglru, meta_tokens, ln_in_g, ln_in_b, w_in, b_gate, conv_w, conv_b, rg_wa, rg_ba, rg_wx, rg_bx, rg_lambda, w_br_r, q_norm_g, w_uq, kv_norm_g, w_uk, w_uv, w_br_m, w_o, ln1_g, ln1_b, w_up, b_up, w_down, b_down, ln2_g, ln2_b):
    assert w_in.shape[0] == DEPTH == 1
    bn, seq, _ = x_prompt.shape
    bd, ts, _ = x_sample.shape
    past_len = page_table.shape[1] * cache_ckv.shape[2]
    p = _prep_params(w_in, b_gate, conv_w, conv_b, rg_wa, rg_ba, rg_wx, rg_bx, rg_lambda, w_br_r, q_norm_g, w_uq,
                     kv_norm_g, w_uk, w_uv, w_br_m, w_o, ln1_g, ln1_b, w_up, b_up, w_down, b_down, ln2_g, ln2_b,
                     ln_in_g, ln_in_b)
    n_s = bd * ts

    x_small = jnp.concatenate([x_sample.reshape(n_s, D_MODEL), meta_tokens.astype(F32)], axis=0)
    pos_small = jnp.concatenate([jnp.tile(past_len + jnp.arange(ts), bd), jnp.arange(N_META)])
    n_small = n_s + N_META
    rx, grg, g_small, q, k, v, ckv, kpe = _in_proj(x_small, _rope_tables(pos_small)[0], 1, p, n_small)

    rep = lambda a: jnp.broadcast_to(a[None], (SUBLANES,) + a.shape)
    y_m, conv_m, h_m = _rglru(rep(rx[n_s:]), rep(grg[n_s:]), jnp.zeros((SUBLANES, CONV_W - 1, D_RNN), F32),
                              jnp.zeros((SUBLANES, D_RNN), F32), p, N_META)
    k_meta, v_meta = k[n_s:], v[n_s:]
    o_m = _meta_attn(q[n_s:], k_meta, v_meta)

    y_s, conv_s, h_s = _rglru(rx[:n_s].reshape(bd, ts, D_RNN), grg[:n_s].reshape(bd, ts, D_RNN),
                              state_conv[0], state_rglru[0], p, ts)
    q_lat, q_pe = _absorb(q[:n_s], p["w_ukt"], bd)
    ckv_s = ckv[:n_s].reshape(bd, ts, KV_LORA)
    kpe_s = kpe[:n_s].reshape(bd, ts, QK_ROPE)
    pad_new = lambda a: jnp.pad(a, ((0, 0), (0, 2 * SUBLANES - ts), (0, 0)))

    n_p = bn * seq
    pos_p = N_META + jnp.arange(seq)
    tm = TM_IN_PROJ
    bcast = lambda a: jnp.broadcast_to(a[:1], (bn,) + a.shape[1:])
    tabs_p, tabs_pt = _rope_tables(pos_p)
    rx_p, grg_p, g_p, qt_p, k_p, vt_p, ckv_p, kpe_p = _in_proj(x_prompt.reshape(n_p, D_MODEL), tabs_p,
                                                               seq // tm, p, tm, tabs_t=tabs_pt)
    y_p, conv_p, h_p = _rglru(rx_p.reshape(bn, seq, D_RNN), grg_p.reshape(bn, seq, D_RNN), bcast(conv_m), bcast(h_m),
                              p, TT_SCAN)
    o_p = _flash(qt_p.reshape(bn, seq // tm, -1, tm), k_p.reshape(bn, seq, -1), vt_p.reshape(bn, seq // tm, -1, tm),
                 k_meta, v_meta.T, TQ_FLASH, FLASH_HEADS)
    out_p, o_lat, prompt_ckv = _post_with_paged_attn(
        x_prompt.reshape(n_p, D_MODEL), y_p.reshape(n_p, D_RNN), o_p.reshape(n_p, -1), g_p, p, TM_POST,
        page_table, q_lat.reshape(bd, N_HEADS * ts, KV_LORA), q_pe.reshape(bd, N_HEADS * ts, ROPE_PAD),
        pad_new(ckv_s), pad_new(kpe_s), cache_ckv[0], jnp.swapaxes(cache_kpe[0], 1, 2),
        ckv_p, ckv[n_s:], bn)

    o_s = _unabsorb(o_lat.reshape(bd, N_HEADS, ts, KV_LORA), p["w_uvh"])
    y_small = jnp.concatenate([y_s.reshape(n_s, D_RNN), y_m[0]], axis=0)
    o_small = jnp.concatenate([o_s, o_m], axis=0)
    out_small = _post(x_small, y_small, o_small, g_small, p, n_small)

    meta_rows = lambda a: jnp.broadcast_to(a[n_s:][None], (bn, N_META, a.shape[-1]))
    y_prompt = out_p.reshape(bn, seq, D_MODEL)
    y_sample = out_small[:n_s].reshape(bd, ts, D_MODEL)
    prompt_kpe = jnp.concatenate([meta_rows(kpe), kpe_p.reshape(bn, seq, QK_ROPE)], axis=1)[None]
    return (y_prompt, y_sample, prompt_ckv[None], prompt_kpe, conv_p[None], h_p[None],
            ckv_s[None], kpe_s[None], conv_s[None], h_s[None])
```

```python
import functools
import math

import jax
import jax.numpy as jnp
from jax import lax
from jax.experimental import pallas as pl
from jax.experimental.pallas import tpu as pltpu

F32 = jnp.float32
BF16 = jnp.bfloat16

D_MODEL = 1024
N_META = 16
D_RNN = D_MODEL
RG_BLOCKS = 8
RG_BLOCK = D_RNN // RG_BLOCKS
CONV_W = 4
RG_C = 8.0
N_HEADS = 8
QK_NOPE = 128
QK_ROPE = 64
V_HEAD = 128
KV_LORA = 512
Q_LORA = 768
ROPE_THETA = 10000.0
D_FF = 4 * D_MODEL
DEPTH = 1
DN_ALPHA = (2.0 * DEPTH) ** 0.25
EPS = 1e-5
SM_SCALE = (QK_NOPE + QK_ROPE) ** -0.5
Q_PRESCALE = SM_SCALE * math.log2(math.e)

LANES = 128
SUBLANES = 8
HEAD_PAD = 2 * LANES
ROPE_PAD = LANES
_C_RX, _C_RG, _C_CQ, _C_CKV, _C_KPE = 0, D_RNN, 2 * D_RNN, 2 * D_RNN + Q_LORA, 2 * D_RNN + Q_LORA + KV_LORA
_C_GATE = _C_KPE + QK_ROPE
VMEM_LIMIT = 56 * 1024 * 1024
TM_IN_PROJ = 256
TM_POST = 256
TT_SCAN = 128
TQ_FLASH = 512
FLASH_HEADS = 4
NEG = float(jnp.finfo(jnp.float32).min)


def _const_spec(shape):
    return pl.BlockSpec(shape, lambda *_: (0,) * len(shape), pipeline_mode=pl.Buffered(1))


def _layer_norm(x, g, b):
    mu = jnp.mean(x, -1, keepdims=True)
    xc = x - mu
    var = jnp.mean(xc * xc, -1, keepdims=True)
    return xc * lax.rsqrt(var + EPS) * g + b


def _rms_norm(x, g):
    return x * lax.rsqrt(jnp.mean(x * x, -1, keepdims=True) + EPS) * g


def _gelu_tanh(x):
    return 0.5 * x * (1.0 + jnp.tanh(math.sqrt(2.0 / math.pi) * (x + 0.044715 * (x * x * x))))


def _rope_block(y, cos, sin_lo, sin_hi):
    left = pltpu.roll(y, ROPE_PAD - QK_ROPE // 2, 1)
    right = pltpu.roll(y, QK_ROPE // 2, 1)
    return y * cos + left * sin_lo + right * sin_hi


def _dot(a, b):
    return jnp.dot(a, b, preferred_element_type=F32)


def _dot_nt(a, b):
    return lax.dot_general(a, b, (((1,), (1,)), ((), ())), preferred_element_type=F32)


CONV_TAIL = SUBLANES - (CONV_W - 1)


def _conv_tile(xe_ref, x, cw_ref, cb_ref):
    tt = x.shape[1]
    xe_ref[:, SUBLANES:, :] = x
    xc = cb_ref[...] + x * cw_ref[CONV_W - 1:CONV_W, :]
    for k in range(CONV_W - 1):
        xc = xc + xe_ref[:, CONV_TAIL + k:CONV_TAIL + k + tt, :] * cw_ref[k:k + 1, :]
    new_tail = xe_ref[:, tt + CONV_TAIL:tt + SUBLANES, :]
    xe_ref[:, CONV_TAIL:SUBLANES, :] = new_tail
    return xc, new_tail


def _decay_rate(lam_ref):
    z = -lam_ref[...]
    return RG_C * (jnp.maximum(z, 0.0) + jnp.log1p(jnp.exp(-jnp.abs(z))))


def _recurrence_inputs(xn, wax_n, ba_n, bx_n, rate_n):
    gates = _dot(xn.astype(BF16), wax_n)
    r = jax.nn.sigmoid(gates[:, :RG_BLOCK] + ba_n)
    ig = jax.nn.sigmoid(gates[:, RG_BLOCK:] + bx_n)
    neg_log_a = r * rate_n
    a = jnp.exp(-neg_log_a)
    u = jnp.sqrt(jnp.tanh(neg_log_a) * (a * a + 1.0)) * (ig * xn)
    return a, u


def _scan_slabs(a_ref, u_ref, hc_ref, nb, tt):
    pitch = tt + SUBLANES
    n_slab = D_RNN // LANES
    for grp in range(nb // SUBLANES):
        base = grp * SUBLANES * pitch
        rows = slice(grp * SUBLANES, (grp + 1) * SUBLANES)
        h0 = tuple(hc_ref[rows, n * LANES:(n + 1) * LANES] for n in range(n_slab))

        def step(t, hs, base=base):
            out = []
            for n in range(n_slab):
                idx = pl.ds(base + t, SUBLANES, stride=pitch)
                hn = a_ref[n, idx, :] * hs[n] + u_ref[n, idx, :]
                u_ref[n, idx, :] = hn
                out.append(hn)
            return tuple(out)

        hs = lax.fori_loop(0, tt, step, h0, unroll=min(tt, 8))
        for n in range(n_slab):
            hc_ref[rows, n * LANES:(n + 1) * LANES] = hs[n]


def _gated_branch_output(y_ref, h_ref, grg_ref, nb, tt):
    pitch = tt + SUBLANES
    for b in range(nb):
        for n in range(D_RNN // LANES):
            cs = slice(n * LANES, (n + 1) * LANES)
            y_ref[b, :, cs] = (h_ref[n, b * pitch:b * pitch + tt, :] * grg_ref[b, :, cs]).astype(BF16)


def _in_proj_kernel(x_ref, cos_ref, slo_ref, shi_ref, lng_ref, lnb_ref, w_in_ref, w_gate_ref, bg_ref,
                    qg_ref, wuq_ref, kvg_ref, wuk_ref, wuv_ref, *rest, token_minor):
    if token_minor:
        cost_ref, sint_ref, *rest = rest
    rx_ref, grg_ref, g_ref, q_ref, k_ref, v_ref, ckv_ref, kpe_ref = rest
    h = _layer_norm(x_ref[...], lng_ref[...], lnb_ref[...])
    hb = h.astype(BF16)

    def proj(lo, hi):
        return _dot(hb, w_in_ref[:, lo:hi])

    cos, slo, shi = cos_ref[...], slo_ref[...], shi_ref[...]
    cqn = _rms_norm(proj(_C_CQ, _C_CKV), qg_ref[...])
    ckv = _rms_norm(proj(_C_CKV, _C_KPE), kvg_ref[...])
    kpe_raw = proj(_C_KPE, _C_KPE + ROPE_PAD)
    lane = lax.broadcasted_iota(jnp.int32, kpe_raw.shape, 1)
    kpe = _rope_block(jnp.where(lane < QK_ROPE, kpe_raw, 0.0), cos, slo, shi)
    ckv_ref[...] = ckv
    kpe_ref[...] = kpe[:, :QK_ROPE]
    ckvb = ckv.astype(BF16)
    kpeb = kpe.astype(BF16)

    rx_ref[...] = proj(_C_RX, _C_RG)
    grg_ref[...] = _gelu_tanh(proj(_C_RG, _C_CQ))
    g_ref[...] = jax.nn.sigmoid(_dot(hb, w_gate_ref[...]) + bg_ref[...])

    cqb = cqn.astype(BF16)
    if token_minor:
        qt = _dot_nt(wuq_ref[...], cqb) * Q_PRESCALE
        cos_t, sin_t = cost_ref[...], sint_ref[...]
        half = QK_ROPE // 2
        for hh in range(N_HEADS):
            r0 = hh * HEAD_PAD
            r1, r2, r3 = r0 + QK_NOPE, r0 + QK_NOPE + half, r0 + QK_NOPE + QK_ROPE
            x1, x2 = qt[r1:r2], qt[r2:r3]
            q_ref[r0:r1, :] = qt[r0:r1].astype(BF16)
            q_ref[r1:r2, :] = (x1 * cos_t - x2 * sin_t).astype(BF16)
            q_ref[r2:r3, :] = (x1 * sin_t + x2 * cos_t).astype(BF16)
            q_ref[r3:r0 + HEAD_PAD, :] = qt[r3:r0 + HEAD_PAD].astype(BF16)
    else:
        q = _dot(cqb, wuq_ref[...]) * Q_PRESCALE
        for hh in range(N_HEADS):
            c0 = hh * HEAD_PAD
            q_ref[:, c0:c0 + QK_NOPE] = q[:, c0:c0 + QK_NOPE].astype(BF16)
            q_ref[:, c0 + QK_NOPE:c0 + HEAD_PAD] = _rope_block(q[:, c0 + QK_NOPE:c0 + HEAD_PAD],
                                                               cos, slo, shi).astype(BF16)

    kn = _dot(ckvb, wuk_ref[...])
    if token_minor:
        v_ref[...] = _dot_nt(wuv_ref[...], ckvb).astype(BF16)
    else:
        v_ref[...] = _dot(ckvb, wuv_ref[...]).astype(BF16)
    for hh in range(N_HEADS):
        c0 = hh * HEAD_PAD
        k_ref[:, c0:c0 + QK_NOPE] = kn[:, hh * QK_NOPE:(hh + 1) * QK_NOPE].astype(BF16)
        k_ref[:, c0 + QK_NOPE:c0 + HEAD_PAD] = kpeb


def _in_proj(x, tabs, tab_blocks, p, tm, tabs_t=None):
    token_minor = tabs_t is not None
    rows = x.shape[0]
    assert rows % tm == 0
    n_tiles = rows // tm
    row = lambda w: pl.BlockSpec((tm, w), lambda i: (i, 0))
    tab = pl.BlockSpec((tm, ROPE_PAD), lambda i: (i % tab_blocks, 0))
    outs = [
        (D_RNN, F32),
        (D_RNN, F32),
        (2 * D_MODEL, F32),
        (N_HEADS * HEAD_PAD, BF16),
        (N_HEADS * HEAD_PAD, BF16),
        (N_HEADS * V_HEAD, BF16),
        (KV_LORA, F32),
        (QK_ROPE, F32),
    ]
    out_specs = [row(w) for w, _ in outs]
    out_shape = [jax.ShapeDtypeStruct((rows, w), dt) for w, dt in outs]
    maybe_t = lambda shape: shape[::-1] if token_minor else shape
    in_specs = [row(D_MODEL), tab, tab, tab,
                _const_spec((1, D_MODEL)), _const_spec((1, D_MODEL)),
                _const_spec((D_MODEL, _C_GATE + 2 * D_MODEL)),
                _const_spec((D_MODEL, 2 * D_MODEL)), _const_spec((1, 2 * D_MODEL)),
                _const_spec((1, Q_LORA)), _const_spec(maybe_t((Q_LORA, N_HEADS * HEAD_PAD))),
                _const_spec((1, KV_LORA)), _const_spec((KV_LORA, N_HEADS * QK_NOPE)),
                _const_spec(maybe_t((KV_LORA, N_HEADS * V_HEAD)))]
    args = [x, *tabs, p["ln_in_g"], p["ln_in_b"], p["w_in"], p["w_gate"], p["b_gate"], p["q_norm_g"],
            p["w_uq_t"] if token_minor else p["w_uq"], p["kv_norm_g"], p["w_uk"],
            p["w_uv_t"] if token_minor else p["w_uv"]]
    if token_minor:
        for j in (3, 5):
            out_specs[j] = pl.BlockSpec((None, outs[j][0], tm), lambda i: (i, 0, 0))
            out_shape[j] = jax.ShapeDtypeStruct((n_tiles, outs[j][0], tm), BF16)
        tab_t = pl.BlockSpec((QK_ROPE // 2, tm), lambda i: (0, i % tab_blocks))
        in_specs += [tab_t, tab_t]
        args += list(tabs_t)
    return pl.pallas_call(
        functools.partial(_in_proj_kernel, token_minor=token_minor),
        grid=(n_tiles,),
        in_specs=in_specs,
        out_specs=out_specs,
        out_shape=out_shape,
        compiler_params=pltpu.CompilerParams(dimension_semantics=("arbitrary",), vmem_limit_bytes=VMEM_LIMIT),
        name="in_proj",
    )(*args)


def _rglru_kernel(rx_ref, grg_ref, cprev_ref, hprev_ref, cw_ref, cb_ref, wax_ref, ba_ref, bx_ref, lam_ref,
                  y_ref, cout_ref, hout_ref, xe_ref, a_ref, u_ref, hc_ref, *, nb, tt):
    pitch = tt + SUBLANES

    @pl.when(pl.program_id(0) == 0)
    def _():
        xe_ref[:, CONV_TAIL:SUBLANES, :] = cprev_ref[...]
        hc_ref[...] = hprev_ref[...]

    xc, new_tail = _conv_tile(xe_ref, rx_ref[...], cw_ref, cb_ref)
    cout_ref[...] = new_tail
    rate = _decay_rate(lam_ref)
    xc2 = xc.reshape(nb * tt, D_RNN)
    for n in range(D_RNN // LANES):
        cs = slice(n * LANES, (n + 1) * LANES)
        a, u = _recurrence_inputs(xc2[:, cs], wax_ref[n], ba_ref[:, cs], bx_ref[:, cs], rate[:, cs])
        for b in range(nb):
            a_ref[n, b * pitch:b * pitch + tt, :] = a[b * tt:(b + 1) * tt]
            u_ref[n, b * pitch:b * pitch + tt, :] = u[b * tt:(b + 1) * tt]
    _scan_slabs(a_ref, u_ref, hc_ref, nb, tt)
    hout_ref[...] = hc_ref[...]
    _gated_branch_output(y_ref, u_ref, grg_ref, nb, tt)


def _rglru(rx, grg, conv_prev, h_prev, p, tt):
    nb, t_len, _ = rx.shape
    assert nb % SUBLANES == 0 and t_len % tt == 0 and tt % SUBLANES == 0
    pitch = tt + SUBLANES
    seq = pl.BlockSpec((nb, tt, D_RNN), lambda i: (0, i, 0))
    return pl.pallas_call(
        functools.partial(_rglru_kernel, nb=nb, tt=tt),
        grid=(t_len // tt,),
        in_specs=[seq, seq, _const_spec((nb, CONV_W - 1, D_RNN)), _const_spec((nb, D_RNN)),
                  _const_spec((CONV_W, D_RNN)), _const_spec((1, D_RNN)),
                  _const_spec((RG_BLOCKS, RG_BLOCK, 2 * RG_BLOCK)),
                  _const_spec((1, D_RNN)), _const_spec((1, D_RNN)), _const_spec((1, D_RNN))],
        out_specs=[seq,
                   pl.BlockSpec((nb, CONV_W - 1, D_RNN), lambda i: (0, 0, 0)),
                   pl.BlockSpec((nb, D_RNN), lambda i: (0, 0))],
        out_shape=[jax.ShapeDtypeStruct((nb, t_len, D_RNN), BF16),
                   jax.ShapeDtypeStruct((nb, CONV_W - 1, D_RNN), F32),
                   jax.ShapeDtypeStruct((nb, D_RNN), F32)],
        scratch_shapes=[pltpu.VMEM((nb, tt + SUBLANES, D_RNN), F32),
                        pltpu.VMEM((D_RNN // LANES, nb * pitch, LANES), F32),
                        pltpu.VMEM((D_RNN // LANES, nb * pitch, LANES), F32),
                        pltpu.VMEM((nb, D_RNN), F32)],
        compiler_params=pltpu.CompilerParams(dimension_semantics=("arbitrary",), vmem_limit_bytes=VMEM_LIMIT),
        name="rglru",
    )(rx, grg, conv_prev, h_prev, p["conv_w"], p["conv_b"], p["w_ax"], p["rg_ba"], p["rg_bx"], p["rg_lambda"])


def _softmax_update_t(s_t, m, l, acc_t, v_t):
    m_new = jnp.maximum(m, jnp.max(s_t, axis=0, keepdims=True))
    alpha = jnp.exp2(m - m_new)
    pr = jnp.exp2(s_t - m_new)
    l = alpha * l + jnp.sum(pr, axis=0, keepdims=True)
    acc_t = alpha * acc_t + _dot(v_t, pr.astype(BF16))
    return m_new, l, acc_t


def _flash_kernel(qt_ref, k_ref, vt_ref, km_ref, vmt_ref, o_ref, *, tq, tv, hps):
    qcols = [slice(hh * HEAD_PAD, (hh + 1) * HEAD_PAD) for hh in range(hps)]
    vrows = [slice(hh * V_HEAD, (hh + 1) * V_HEAD) for hh in range(hps)]

    def queries_t(hh):
        return jnp.concatenate([qt_ref[c, qcols[hh], :] for c in range(tq // tv)], axis=1)

    def scores(j, hh):
        return _dot(k_ref[j * tq:(j + 1) * tq, qcols[hh]], queries_t(hh))

    def values_t(j, hh):
        return jnp.concatenate([vt_ref[j * (tq // tv) + c, vrows[hh], :] for c in range(tq // tv)], axis=1)

    def tile(n_full):
        kpos = lax.broadcasted_iota(jnp.int32, (tq, tq), 0)
        qpos = lax.broadcasted_iota(jnp.int32, (tq, tq), 1)
        state, s_next = [], []
        for hh in range(hps):
            s_t = _dot(km_ref[:, qcols[hh]], queries_t(hh))
            m = jnp.max(s_t, axis=0, keepdims=True)
            pr = jnp.exp2(s_t - m)
            state.append((m, jnp.sum(pr, axis=0, keepdims=True), _dot(vmt_ref[vrows[hh], :], pr.astype(BF16))))
            s_next.append(scores(0, hh))
        for j in range(n_full + 1):
            for hh in range(hps):
                s_t = s_next[hh]
                if j < n_full:
                    s_next[hh] = scores(j + 1, hh)
                else:
                    s_t = jnp.where(kpos <= qpos, s_t, NEG)
                state[hh] = _softmax_update_t(s_t, *state[hh], values_t(j, hh))
        for hh in range(hps):
            _, l, acc_t = state[hh]
            o_ref[:, vrows[hh]] = jnp.transpose(acc_t / l).astype(BF16)

    for c in range(k_ref.shape[0] // tq):
        pl.when(pl.program_id(2) == c)(functools.partial(tile, c))


def _flash(q_t, k, v_t, k_pre, v_pre_t, tq, hps):
    nb, t_len, _ = k.shape
    tv = v_t.shape[3]
    assert t_len % tq == 0 and tq % tv == 0 and N_HEADS % hps == 0
    return pl.pallas_call(
        functools.partial(_flash_kernel, tq=tq, tv=tv, hps=hps),
        grid=(nb, N_HEADS // hps, t_len // tq),
        in_specs=[pl.BlockSpec((None, tq // tv, hps * HEAD_PAD, tv), lambda b, h, i: (b, i, h, 0)),
                  pl.BlockSpec((None, t_len, hps * HEAD_PAD), lambda b, h, i: (b, 0, h)),
                  pl.BlockSpec((None, t_len // tv, hps * V_HEAD, tv), lambda b, h, i: (b, 0, h, 0)),
                  pl.BlockSpec((N_META, hps * HEAD_PAD), lambda b, h, i: (0, h)),
                  pl.BlockSpec((hps * V_HEAD, N_META), lambda b, h, i: (h, 0))],
        out_specs=pl.BlockSpec((None, tq, hps * V_HEAD), lambda b, h, i: (b, i, h)),
        out_shape=jax.ShapeDtypeStruct((nb, t_len, N_HEADS * V_HEAD), BF16),
        compiler_params=pltpu.CompilerParams(dimension_semantics=("arbitrary",) * 3, vmem_limit_bytes=VMEM_LIMIT),
        name="flash_prompt",
    )(q_t, k, v_t, k_pre, v_pre_t)


def _meta_attn_kernel(q_ref, k_ref, v_ref, o_ref):
    qpos = lax.broadcasted_iota(jnp.int32, (N_META, N_META), 0)
    kpos = lax.broadcasted_iota(jnp.int32, (N_META, N_META), 1)
    for hh in range(N_HEADS):
        qs = slice(hh * HEAD_PAD, (hh + 1) * HEAD_PAD)
        vs = slice(hh * V_HEAD, (hh + 1) * V_HEAD)
        s = jnp.where(kpos <= qpos, _dot_nt(q_ref[:, qs], k_ref[:, qs]), NEG)
        pr = jnp.exp2(s - jnp.max(s, axis=1, keepdims=True))
        o = _dot(pr.astype(BF16), v_ref[:, vs]) / jnp.sum(pr, axis=1, keepdims=True)
        o_ref[:, vs] = o.astype(BF16)


def _meta_attn(q, k, v):
    return pl.pallas_call(
        _meta_attn_kernel,
        out_shape=jax.ShapeDtypeStruct((N_META, N_HEADS * V_HEAD), BF16),
        name="meta_attn",
    )(q, k, v)


def _absorb_kernel(q_ref, wukt_ref, qlat_ref, qpe_ref):
    q = q_ref[...]
    nb = qlat_ref.shape[0]
    qlat = _dot(q[:, :QK_NOPE], wukt_ref[...])
    qlat_ref[...] = qlat.reshape(nb, -1, KV_LORA)
    qpe_ref[...] = q[:, QK_NOPE:].astype(F32).reshape(nb, -1, ROPE_PAD)


def _absorb(q, w_ukt, nb):
    rows = q.shape[0]
    ts = rows // nb
    return pl.pallas_call(
        _absorb_kernel,
        grid=(N_HEADS,),
        in_specs=[pl.BlockSpec((rows, HEAD_PAD), lambda h: (0, h)),
                  pl.BlockSpec((None, QK_NOPE, KV_LORA), lambda h: (h, 0, 0))],
        out_specs=[pl.BlockSpec((nb, None, ts, KV_LORA), lambda h: (0, h, 0, 0)),
                   pl.BlockSpec((nb, None, ts, ROPE_PAD), lambda h: (0, h, 0, 0))],
        out_shape=[jax.ShapeDtypeStruct((nb, N_HEADS, ts, KV_LORA), F32),
                   jax.ShapeDtypeStruct((nb, N_HEADS, ts, ROPE_PAD), F32)],
        compiler_params=pltpu.CompilerParams(dimension_semantics=("arbitrary",)),
        name="absorb_q",
    )(q, w_ukt)


PAGED_BUFS = 2


def _paged_scratch(n_q, pps, page):
    return [pltpu.VMEM((n_q, 1), F32), pltpu.VMEM((n_q, 1), F32), pltpu.VMEM((n_q, KV_LORA), F32),
            pltpu.VMEM((PAGED_BUFS, pps * page, KV_LORA), F32), pltpu.VMEM((PAGED_BUFS, pps, QK_ROPE, page), F32),
            pltpu.VMEM((pps * page, KV_LORA), BF16), pltpu.VMEM((QK_ROPE, pps * page), BF16),
            pltpu.SemaphoreType.DMA((PAGED_BUFS,))]


def _paged_copies(pt_ref, ckv_hbm, kpe_hbm, cin_ref, kin_ref, sem, seq, chunk, slot, pps):
    page = kin_ref.shape[3]
    out = []
    for i in range(pps):
        pid = pt_ref[seq, chunk * pps + i]
        out.append(pltpu.make_async_copy(ckv_hbm.at[pid], cin_ref.at[slot, pl.ds(i * page, page)], sem.at[slot]))
        out.append(pltpu.make_async_copy(kpe_hbm.at[pid], kin_ref.at[slot, i], sem.at[slot]))
    return out


def _paged_new_tokens(qlat, qpe, cnew_ref, knew_ref, m_ref, l_ref, acc_ref, ts):
    cn = cnew_ref[...].astype(BF16)
    kn = knew_ref[...].astype(BF16)
    s = _dot_nt(qlat, cn) + _dot_nt(qpe, kn)
    t_q = lax.broadcasted_iota(jnp.int32, s.shape, 0) % ts
    t_k = lax.broadcasted_iota(jnp.int32, s.shape, 1)
    s = jnp.where(t_k <= t_q, s, NEG)
    m = jnp.max(s, axis=1, keepdims=True)
    pr = jnp.exp2(s - m)
    m_ref[...] = m
    l_ref[...] = jnp.sum(pr, axis=1, keepdims=True)
    acc_ref[...] = _dot(pr.astype(BF16), cn)


def _paged_scores(qlat, qpe, cin_ref, kin_ref, cbuf_ref, kbuf_ref, slot, pps):
    page = kin_ref.shape[3]
    for i in range(pps):
        rows = pl.ds(i * page, page)
        cbuf_ref[rows, :] = cin_ref[slot, rows, :].astype(BF16)
        kbuf_ref[:, i * page:(i + 1) * page] = kin_ref[slot, i].astype(BF16)
    return _dot_nt(qlat, cbuf_ref[...]) + _dot(qpe, kbuf_ref[...])


def _paged_probs(s, m_ref, l_ref):
    m = m_ref[...]
    m_new = jnp.maximum(m, jnp.max(s, axis=1, keepdims=True))
    alpha = jnp.exp2(m - m_new)
    pr = jnp.exp2(s - m_new)
    m_ref[...] = m_new
    l_ref[...] = alpha * l_ref[...] + jnp.sum(pr, axis=1, keepdims=True)
    return alpha, pr.astype(BF16)


def _paged_accumulate(alpha, pr, cbuf_ref, acc_ref):
    acc_ref[...] = alpha * acc_ref[...] + _dot(pr, cbuf_ref[...])


def _unabsorb_kernel(olat_ref, wuv_ref, o_ref):
    x = olat_ref[...]
    x = x.reshape(x.shape[0] * x.shape[1], KV_LORA).astype(BF16)
    o_ref[...] = _dot(x, wuv_ref[...]).astype(BF16)


def _unabsorb(o_lat, w_uvh):
    nb, _, ts, _ = o_lat.shape
    return pl.pallas_call(
        _unabsorb_kernel,
        grid=(N_HEADS,),
        in_specs=[pl.BlockSpec((nb, None, ts, KV_LORA), lambda h: (0, h, 0, 0)),
                  pl.BlockSpec((None, KV_LORA, V_HEAD), lambda h: (h, 0, 0))],
        out_specs=pl.BlockSpec((nb * ts, V_HEAD), lambda h: (0, h)),
        out_shape=jax.ShapeDtypeStruct((nb * ts, N_HEADS * V_HEAD), BF16),
        compiler_params=pltpu.CompilerParams(dimension_semantics=("arbitrary",)),
        name="unabsorb_o",
    )(o_lat, w_uvh)


FF_CHUNKS = 4
N_POST_WEIGHTS = 13


def _post_compute(x_ref, y_ref, o_ref, g_ref, weights, out_ref, side_work=None):
    before_up, before_down, after_down = side_work if side_work is not None else (lambda c: None,) * 3
    lng_ref, lnb_ref, wr_ref, wm_ref, wo_ref, g1_ref, b1_ref, wup_ref, bup_ref, wdn_ref, bdn_ref, g2_ref, b2_ref = weights
    h = _layer_norm(x_ref[...], lng_ref[...], lnb_ref[...])
    z_r = _dot(y_ref[...], wr_ref[...])
    z_m = _dot(o_ref[...], wm_ref[...])
    mix_in = g_ref[:, :D_MODEL] * z_r + g_ref[:, D_MODEL:] * z_m
    mix = _dot(mix_in.astype(BF16), wo_ref[...])
    before_up(0)
    x1 = _layer_norm(DN_ALPHA * h + mix, g1_ref[...], b1_ref[...])
    x1b = x1.astype(BF16)
    f = bdn_ref[...]
    ff_chunk = D_FF // FF_CHUNKS
    for c in range(FF_CHUNKS):
        cs = slice(c * ff_chunk, (c + 1) * ff_chunk)
        if c > 0:
            before_up(c)
        up = jnp.maximum(_dot(x1b, wup_ref[:, cs]) + bup_ref[:, cs], 0.0)
        before_down(c)
        f = f + _dot((up * up).astype(BF16), wdn_ref[cs, :])
        after_down(c)
    out_ref[...] = _layer_norm(DN_ALPHA * x1 + f, g2_ref[...], b2_ref[...])


def _post_kernel(x_ref, y_ref, o_ref, g_ref, *rest):
    _post_compute(x_ref, y_ref, o_ref, g_ref, rest[:N_POST_WEIGHTS], rest[N_POST_WEIGHTS])


def _post_paged_kernel(pt_ref, x_ref, y_ref, o_ref, g_ref, *rest, pps, ts, n_j, n_steps, tiles_per_seq):
    weights = rest[:N_POST_WEIGHTS]
    (qlat_ref, qpe_ref, cnew_ref, knew_ref, lat_ref, latm_ref, ckv_hbm, kpe_hbm, out_ref, olat_ref, platent_hbm,
     m_ref, l_ref, acc_ref, cin_ref, kin_ref, cbuf_ref, kbuf_ref, sem, lsem) = rest[N_POST_WEIGHTS:]
    s = pl.program_id(0)
    spb = n_j // FF_CHUNKS
    seq, j0 = lax.div(s, spb), lax.rem(s, spb) * FF_CHUNKS
    copies = functools.partial(_paged_copies, pt_ref, ckv_hbm, kpe_hbm, cin_ref, kin_ref, sem, pps=pps)

    tm = out_ref.shape[0]
    lat_rows = pltpu.make_async_copy(
        lat_ref,
        platent_hbm.at[lax.div(s, tiles_per_seq), pl.ds(N_META + lax.rem(s, tiles_per_seq) * tm, tm)], lsem.at[0])
    lat_meta = [pltpu.make_async_copy(latm_ref, platent_hbm.at[b, pl.ds(0, N_META)], lsem.at[1])
                for b in range(platent_hbm.shape[0])]
    lat_rows.start()

    @pl.when(s == 0)
    def _():
        for cp in lat_meta:
            cp.start()
        for cp in copies(0, 0, 0):
            cp.start()

    live = {}

    def gather_and_score(c):
        slot, nxt = c % PAGED_BUFS, (c + 1) % PAGED_BUFS
        if c + 1 < FF_CHUNKS:
            for cp in copies(seq, j0 + c + 1, nxt):
                cp.start()
        else:
            @pl.when(s + 1 < n_steps)
            def _():
                for cp in copies(lax.div(s + 1, spb), lax.rem(s + 1, spb) * FF_CHUNKS, nxt):
                    cp.start()
        for cp in copies(seq, j0 + c, slot):
            cp.wait()
        qlat = qlat_ref[...].astype(BF16)
        qpe = qpe_ref[...][:, :QK_ROPE].astype(BF16)
        if c == 0:
            pl.when(j0 == 0)(functools.partial(_paged_new_tokens, qlat, qpe, cnew_ref, knew_ref,
                                               m_ref, l_ref, acc_ref, ts))
        live["s"] = _paged_scores(qlat, qpe, cin_ref, kin_ref, cbuf_ref, kbuf_ref, slot, pps)

    def probs(c):
        live["alpha"], live["pr"] = _paged_probs(live.pop("s"), m_ref, l_ref)

    def accumulate(c):
        _paged_accumulate(live.pop("alpha"), live.pop("pr"), cbuf_ref, acc_ref)
        if c == FF_CHUNKS - 1:
            @pl.when(j0 == n_j - FF_CHUNKS)
            def _():
                olat_ref[...] = acc_ref[...] / l_ref[...]

    _post_compute(x_ref, y_ref, o_ref, g_ref, weights, out_ref, side_work=(gather_and_score, probs, accumulate))

    lat_rows.wait()

    @pl.when(s == 0)
    def _():
        for cp in lat_meta:
            cp.wait()


def _post_weight_specs():
    vec = lambda w: _const_spec((1, w))
    sq = _const_spec((D_MODEL, D_MODEL))
    return [vec(D_MODEL), vec(D_MODEL), sq, sq, sq, vec(D_MODEL), vec(D_MODEL),
            _const_spec((D_MODEL, D_FF)), vec(D_FF), _const_spec((D_FF, D_MODEL)), vec(D_MODEL),
            vec(D_MODEL), vec(D_MODEL)]


def _post_weights(p):
    return (p["ln_in_g"], p["ln_in_b"], p["w_br_r"], p["w_br_m"], p["w_o"], p["ln1_g"], p["ln1_b"],
            p["w_up"], p["b_up"], p["w_down"], p["b_down"], p["ln2_g"], p["ln2_b"])


def _post(x, y, o, g, p, tm):
    rows = x.shape[0]
    assert rows % tm == 0
    row = lambda w: pl.BlockSpec((tm, w), lambda i: (i, 0))
    return pl.pallas_call(
        _post_kernel,
        grid=(rows // tm,),
        in_specs=[row(D_MODEL), row(D_RNN), row(N_HEADS * V_HEAD), row(2 * D_MODEL)] + _post_weight_specs(),
        out_specs=row(D_MODEL),
        out_shape=jax.ShapeDtypeStruct((rows, D_MODEL), F32),
        compiler_params=pltpu.CompilerParams(dimension_semantics=("arbitrary",), vmem_limit_bytes=VMEM_LIMIT),
        name="merge_mlp",
    )(x, y, o, g, *_post_weights(p))


def _post_with_paged_attn(x, y, o, g, p, tm, page_table, q_lat, q_pe, ckv_new, kpe_new, ckv_pool, kpe_pool_t,
                          latent, latent_meta, n_seq):
    rows = x.shape[0]
    n_steps = rows // tm
    assert n_steps % n_seq == 0
    nb, n_pages = page_table.shape
    n_q = q_lat.shape[1]
    page = ckv_pool.shape[1]
    n_new = ckv_new.shape[1]
    chunks = n_steps * FF_CHUNKS
    assert rows % tm == 0 and (nb * n_pages) % chunks == 0 and FF_CHUNKS % PAGED_BUFS == 0
    pps = nb * n_pages // chunks
    n_j = n_pages // pps
    assert n_pages % pps == 0 and n_j % FF_CHUNKS == 0
    spb = n_j // FF_CHUNKS
    row = lambda w: pl.BlockSpec((tm, w), lambda i, pt: (i, 0))
    per_seq = lambda r, w: pl.BlockSpec((None, r, w), lambda i, pt: (i // spb, 0, 0))
    grid_spec = pltpu.PrefetchScalarGridSpec(
        num_scalar_prefetch=1,
        grid=(n_steps,),
        in_specs=[row(D_MODEL), row(D_RNN), row(N_HEADS * V_HEAD), row(2 * D_MODEL)] + _post_weight_specs()
                 + [per_seq(n_q, KV_LORA), per_seq(n_q, ROPE_PAD), per_seq(n_new, KV_LORA), per_seq(n_new, QK_ROPE),
                    row(KV_LORA), _const_spec((N_META, KV_LORA))]
                 + [pl.BlockSpec(memory_space=pl.ANY)] * 2,
        out_specs=[row(D_MODEL), per_seq(n_q, KV_LORA), pl.BlockSpec(memory_space=pl.ANY)],
        scratch_shapes=_paged_scratch(n_q, pps, page) + [pltpu.SemaphoreType.DMA((2,))],
    )
    return pl.pallas_call(
        functools.partial(_post_paged_kernel, pps=pps, ts=n_q // N_HEADS, n_j=n_j, n_steps=n_steps,
                          tiles_per_seq=n_steps // n_seq),
        grid_spec=grid_spec,
        out_shape=[jax.ShapeDtypeStruct((rows, D_MODEL), F32), jax.ShapeDtypeStruct((nb, n_q, KV_LORA), F32),
                   jax.ShapeDtypeStruct((n_seq, N_META + rows // n_seq, KV_LORA), F32)],
        compiler_params=pltpu.CompilerParams(dimension_semantics=("arbitrary",), vmem_limit_bytes=VMEM_LIMIT),
        name="merge_mlp_paged_attn",
    )(page_table, x, y, o, g, *_post_weights(p), q_lat, q_pe, ckv_new, kpe_new, latent, latent_meta,
      ckv_pool, kpe_pool_t)


def _rope_tables(pos):
    half = QK_ROPE // 2
    inv = 1.0 / (ROPE_THETA ** (jnp.arange(half, dtype=F32) / half))
    ang = pos.astype(F32)[:, None] * inv[None, :]
    cos, sin = jnp.cos(ang), jnp.sin(ang)
    z = jnp.zeros_like(cos)
    return ((jnp.concatenate([cos, cos, z, z], 1), jnp.concatenate([-sin, z, z, z], 1),
             jnp.concatenate([z, sin, z, z], 1)), (cos.T, sin.T))


def _prep_params(w_in, b_gate, conv_w, conv_b, rg_wa, rg_ba, rg_wx, rg_bx, rg_lambda, w_br_r, q_norm_g, w_uq,
                 kv_norm_g, w_uk, w_uv, w_br_m, w_o, ln1_g, ln1_b, w_up, b_up, w_down, b_down, ln2_g, ln2_b,
                 ln_in_g, ln_in_b):
    l = 0
    vec = lambda a: a.reshape(1, -1).astype(F32)
    wi = w_in[l].astype(BF16)
    w_uq_p = jnp.pad(w_uq[l], ((0, 0), (0, 0), (0, HEAD_PAD - QK_NOPE - QK_ROPE))).reshape(Q_LORA, N_HEADS * HEAD_PAD)
    return {
        "ln_in_g": vec(ln_in_g), "ln_in_b": vec(ln_in_b),
        "w_in": wi, "w_gate": wi[:, _C_GATE:], "b_gate": vec(b_gate[l]),
        "q_norm_g": vec(q_norm_g[l]), "w_uq": w_uq_p.astype(BF16), "w_uq_t": w_uq_p.T.astype(BF16),
        "kv_norm_g": vec(kv_norm_g[l]),
        "w_uk": w_uk[l].reshape(KV_LORA, N_HEADS * QK_NOPE).astype(BF16),
        "w_uv": w_uv[l].reshape(KV_LORA, N_HEADS * V_HEAD).astype(BF16),
        "w_uv_t": w_uv[l].reshape(KV_LORA, N_HEADS * V_HEAD).T.astype(BF16),
        "w_ukt": jnp.transpose(w_uk[l], (1, 2, 0)).astype(BF16),
        "w_uvh": jnp.transpose(w_uv[l], (1, 0, 2)).astype(BF16),
        "conv_w": conv_w[l].astype(F32), "conv_b": vec(conv_b[l]),
        "w_ax": jnp.concatenate([rg_wa[l], rg_wx[l]], axis=-1).astype(BF16),
        "rg_ba": vec(rg_ba[l]), "rg_bx": vec(rg_bx[l]), "rg_lambda": vec(rg_lambda[l]),
        "w_br_r": w_br_r[l].astype(BF16), "w_br_m": w_br_m[l].astype(BF16), "w_o": w_o[l].astype(BF16),
        "ln1_g": vec(ln1_g[l]), "ln1_b": vec(ln1_b[l]),
        "w_up": w_up[l].astype(BF16), "b_up": vec(b_up[l]),
        "w_down": w_down[l].astype(BF16), "b_down": vec(b_down[l]),
        "ln2_g": vec(ln2_g[l]), "ln2_b": vec(ln2_b[l]),
    }


def kernel(x_prompt, x_sample, cache_ckv, cache_kpe, page_table, state_conv, state_rglru, meta_tokens, ln_in_g, ln_in_b, w_in, b_gate, conv_w, conv_b, rg_wa, rg_ba, rg_wx, rg_bx, rg_lambda, w_br_r, q_norm_g, w_uq, kv_norm_g, w_uk, w_uv, w_br_m, w_o, ln1_g, ln1_b, w_up, b_up, w_down, b_down, ln2_g, ln2_b):
    assert w_in.shape[0] == DEPTH == 1
    bn, seq, _ = x_prompt.shape
    bd, ts, _ = x_sample.shape
    past_len = page_table.shape[1] * cache_ckv.shape[2]
    p = _prep_params(w_in, b_gate, conv_w, conv_b, rg_wa, rg_ba, rg_wx, rg_bx, rg_lambda, w_br_r, q_norm_g, w_uq,
                     kv_norm_g, w_uk, w_uv, w_br_m, w_o, ln1_g, ln1_b, w_up, b_up, w_down, b_down, ln2_g, ln2_b,
                     ln_in_g, ln_in_b)
    n_s = bd * ts

    x_small = jnp.concatenate([x_sample.reshape(n_s, D_MODEL), meta_tokens.astype(F32)], axis=0)
    pos_small = jnp.concatenate([jnp.tile(past_len + jnp.arange(ts), bd), jnp.arange(N_META)])
    n_small = n_s + N_META
    rx, grg, g_small, q, k, v, ckv, kpe = _in_proj(x_small, _rope_tables(pos_small)[0], 1, p, n_small)

    rep = lambda a: jnp.broadcast_to(a[None], (SUBLANES,) + a.shape)
    y_m, conv_m, h_m = _rglru(rep(rx[n_s:]), rep(grg[n_s:]), jnp.zeros((SUBLANES, CONV_W - 1, D_RNN), F32),
                              jnp.zeros((SUBLANES, D_RNN), F32), p, N_META)
    k_meta, v_meta = k[n_s:], v[n_s:]
    o_m = _meta_attn(q[n_s:], k_meta, v_meta)

    y_s, conv_s, h_s = _rglru(rx[:n_s].reshape(bd, ts, D_RNN), grg[:n_s].reshape(bd, ts, D_RNN),
                              state_conv[0], state_rglru[0], p, ts)
    q_lat, q_pe = _absorb(q[:n_s], p["w_ukt"], bd)
    ckv_s = ckv[:n_s].reshape(bd, ts, KV_LORA)
    kpe_s = kpe[:n_s].reshape(bd, ts, QK_ROPE)
    pad_new = lambda a: jnp.pad(a, ((0, 0), (0, 2 * SUBLANES - ts), (0, 0)))

    n_p = bn * seq
    pos_p = N_META + jnp.arange(seq)
    tm = TM_IN_PROJ
    bcast = lambda a: jnp.broadcast_to(a[:1], (bn,) + a.shape[1:])
    tabs_p, tabs_pt = _rope_tables(pos_p)
    rx_p, grg_p, g_p, qt_p, k_p, vt_p, ckv_p, kpe_p = _in_proj(x_prompt.reshape(n_p, D_MODEL), tabs_p,
                                                               seq // tm, p, tm, tabs_t=tabs_pt)
    y_p, conv_p, h_p = _rglru(rx_p.reshape(bn, seq, D_RNN), grg_p.reshape(bn, seq, D_RNN), bcast(conv_m), bcast(h_m),
                              p, TT_SCAN)
    o_p = _flash(qt_p.reshape(bn, seq // tm, -1, tm), k_p.reshape(bn, seq, -1), vt_p.reshape(bn, seq // tm, -1, tm),
                 k_meta, v_meta.T, TQ_FLASH, FLASH_HEADS)
    out_p, o_lat, prompt_ckv = _post_with_paged_attn(
        x_prompt.reshape(n_p, D_MODEL), y_p.reshape(n_p, D_RNN), o_p.reshape(n_p, -1), g_p, p, TM_POST,
        page_table, q_lat.reshape(bd, N_HEADS * ts, KV_LORA), q_pe.reshape(bd, N_HEADS * ts, ROPE_PAD),
        pad_new(ckv_s), pad_new(kpe_s), cache_ckv[0], jnp.swapaxes(cache_kpe[0], 1, 2),
        ckv_p, ckv[n_s:], bn)

    o_s = _unabsorb(o_lat.reshape(bd, N_HEADS, ts, KV_LORA), p["w_uvh"])
    y_small = jnp.concatenate([y_s.reshape(n_s, D_RNN), y_m[0]], axis=0)
    o_small = jnp.concatenate([o_s, o_m], axis=0)
    out_small = _post(x_small, y_small, o_small, g_small, p, n_small)

    meta_rows = lambda a: jnp.broadcast_to(a[n_s:][None], (bn, N_META, a.shape[-1]))
    y_prompt = out_p.reshape(bn, seq, D_MODEL)
    y_sample = out_small[:n_s].reshape(bd, ts, D_MODEL)
    prompt_kpe = jnp.concatenate([meta_rows(kpe), kpe_p.reshape(bn, seq, QK_ROPE)], axis=1)[None]
    return (y_prompt, y_sample, prompt_ckv[None], prompt_kpe, conv_p[None], h_p[None],
            ckv_s[None], kpe_s[None], conv_s[None], h_s[None])
```

```python
import functools
import math

import jax
import jax.numpy as jnp
from jax import lax
from jax.experimental import pallas as pl
from jax.experimental.pallas import tpu as pltpu

F32 = jnp.float32
BF16 = jnp.bfloat16

D_MODEL = 1024
N_META = 16
D_RNN = D_MODEL
RG_BLOCKS = 8
RG_BLOCK = D_RNN // RG_BLOCKS
CONV_W = 4
RG_C = 8.0
N_HEADS = 8
QK_NOPE = 128
QK_ROPE = 64
V_HEAD = 128
KV_LORA = 512
Q_LORA = 768
ROPE_THETA = 10000.0
D_FF = 4 * D_MODEL
DEPTH = 1
DN_ALPHA = (2.0 * DEPTH) ** 0.25
EPS = 1e-5
SM_SCALE = (QK_NOPE + QK_ROPE) ** -0.5
Q_PRESCALE = SM_SCALE * math.log2(math.e)

LANES = 128
SUBLANES = 8
HEAD_PAD = 2 * LANES
ROPE_PAD = LANES
_C_RX, _C_RG, _C_CQ, _C_CKV, _C_KPE = 0, D_RNN, 2 * D_RNN, 2 * D_RNN + Q_LORA, 2 * D_RNN + Q_LORA + KV_LORA
_C_GATE = _C_KPE + QK_ROPE
VMEM_LIMIT = 56 * 1024 * 1024
TM_IN_PROJ = 256
TM_POST = 256
TT_SCAN = 128
TQ_FLASH = 512
FLASH_HEADS = 4
NEG = float(jnp.finfo(jnp.float32).min)


def _const_spec(shape):
    return pl.BlockSpec(shape, lambda *_: (0,) * len(shape), pipeline_mode=pl.Buffered(1))


def _layer_norm(x, g, b):
    mu = jnp.mean(x, -1, keepdims=True)
    xc = x - mu
    var = jnp.mean(xc * xc, -1, keepdims=True)
    return xc * lax.rsqrt(var + EPS) * g + b


def _rms_norm(x, g):
    return x * lax.rsqrt(jnp.mean(x * x, -1, keepdims=True) + EPS) * g


def _gelu_tanh(x):
    return 0.5 * x * (1.0 + jnp.tanh(math.sqrt(2.0 / math.pi) * (x + 0.044715 * (x * x * x))))


def _rope_block(y, cos, sin_lo, sin_hi):
    left = pltpu.roll(y, ROPE_PAD - QK_ROPE // 2, 1)
    right = pltpu.roll(y, QK_ROPE // 2, 1)
    return y * cos + left * sin_lo + right * sin_hi


def _dot(a, b):
    return jnp.dot(a, b, preferred_element_type=F32)


def _dot_nt(a, b):
    return lax.dot_general(a, b, (((1,), (1,)), ((), ())), preferred_element_type=F32)


CONV_TAIL = SUBLANES - (CONV_W - 1)


def _conv_tile(xe_ref, x, cw_ref, cb_ref):
    tt = x.shape[1]
    xe_ref[:, SUBLANES:, :] = x
    xc = cb_ref[...] + x * cw_ref[CONV_W - 1:CONV_W, :]
    for k in range(CONV_W - 1):
        xc = xc + xe_ref[:, CONV_TAIL + k:CONV_TAIL + k + tt, :] * cw_ref[k:k + 1, :]
    new_tail = xe_ref[:, tt + CONV_TAIL:tt + SUBLANES, :]
    xe_ref[:, CONV_TAIL:SUBLANES, :] = new_tail
    return xc, new_tail


def _decay_rate(lam_ref):
    z = -lam_ref[...]
    return RG_C * (jnp.maximum(z, 0.0) + jnp.log1p(jnp.exp(-jnp.abs(z))))


def _recurrence_inputs(xn, wax_n, ba_n, bx_n, rate_n):
    gates = _dot(xn.astype(BF16), wax_n)
    r = jax.nn.sigmoid(gates[:, :RG_BLOCK] + ba_n)
    ig = jax.nn.sigmoid(gates[:, RG_BLOCK:] + bx_n)
    neg_log_a = r * rate_n
    a = jnp.exp(-neg_log_a)
    u = jnp.sqrt(jnp.tanh(neg_log_a) * (a * a + 1.0)) * (ig * xn)
    return a, u


def _scan_slabs(a_ref, u_ref, hc_ref, nb, tt):
    pitch = tt + SUBLANES
    n_slab = D_RNN // LANES
    for grp in range(nb // SUBLANES):
        base = grp * SUBLANES * pitch
        rows = slice(grp * SUBLANES, (grp + 1) * SUBLANES)
        h0 = tuple(hc_ref[rows, n * LANES:(n + 1) * LANES] for n in range(n_slab))

        def step(t, hs, base=base):
            out = []
            for n in range(n_slab):
                idx = pl.ds(base + t, SUBLANES, stride=pitch)
                hn = a_ref[n, idx, :] * hs[n] + u_ref[n, idx, :]
                u_ref[n, idx, :] = hn
                out.append(hn)
            return tuple(out)

        hs = lax.fori_loop(0, tt, step, h0, unroll=min(tt, 8))
        for n in range(n_slab):
            hc_ref[rows, n * LANES:(n + 1) * LANES] = hs[n]


def _gated_branch_output(y_ref, h_ref, grg_ref, nb, tt):
    pitch = tt + SUBLANES
    for b in range(nb):
        for n in range(D_RNN // LANES):
            cs = slice(n * LANES, (n + 1) * LANES)
            y_ref[b, :, cs] = (h_ref[n, b * pitch:b * pitch + tt, :] * grg_ref[b, :, cs]).astype(BF16)


def _in_proj_kernel(x_ref, cos_ref, slo_ref, shi_ref, lng_ref, lnb_ref, w_in_ref, w_gate_ref, bg_ref,
                    qg_ref, wuq_ref, kvg_ref, wuk_ref, wuv_ref, *rest, token_minor):
    if token_minor:
        cost_ref, sint_ref, *rest = rest
    rx_ref, grg_ref, g_ref, q_ref, k_ref, v_ref, ckv_ref, kpe_ref = rest
    h = _layer_norm(x_ref[...], lng_ref[...], lnb_ref[...])
    hb = h.astype(BF16)

    def proj(lo, hi):
        return _dot(hb, w_in_ref[:, lo:hi])

    cos, slo, shi = cos_ref[...], slo_ref[...], shi_ref[...]
    cqn = _rms_norm(proj(_C_CQ, _C_CKV), qg_ref[...])
    ckv = _rms_norm(proj(_C_CKV, _C_KPE), kvg_ref[...])
    kpe_raw = proj(_C_KPE, _C_KPE + ROPE_PAD)
    lane = lax.broadcasted_iota(jnp.int32, kpe_raw.shape, 1)
    kpe = _rope_block(jnp.where(lane < QK_ROPE, kpe_raw, 0.0), cos, slo, shi)
    ckv_ref[...] = ckv
    kpe_ref[...] = kpe[:, :QK_ROPE]
    ckvb = ckv.astype(BF16)
    kpeb = kpe.astype(BF16)

    rx_ref[...] = proj(_C_RX, _C_RG)
    grg_ref[...] = _gelu_tanh(proj(_C_RG, _C_CQ))
    g_ref[...] = jax.nn.sigmoid(_dot(hb, w_gate_ref[...]) + bg_ref[...])

    cqb = cqn.astype(BF16)
    if token_minor:
        qt = _dot_nt(wuq_ref[...], cqb) * Q_PRESCALE
        cos_t, sin_t = cost_ref[...], sint_ref[...]
        half = QK_ROPE // 2
        for hh in range(N_HEADS):
            r0 = hh * HEAD_PAD
            r1, r2, r3 = r0 + QK_NOPE, r0 + QK_NOPE + half, r0 + QK_NOPE + QK_ROPE
            x1, x2 = qt[r1:r2], qt[r2:r3]
            q_ref[r0:r1, :] = qt[r0:r1].astype(BF16)
            q_ref[r1:r2, :] = (x1 * cos_t - x2 * sin_t).astype(BF16)
            q_ref[r2:r3, :] = (x1 * sin_t + x2 * cos_t).astype(BF16)
            q_ref[r3:r0 + HEAD_PAD, :] = qt[r3:r0 + HEAD_PAD].astype(BF16)
    else:
        q = _dot(cqb, wuq_ref[...]) * Q_PRESCALE
        for hh in range(N_HEADS):
            c0 = hh * HEAD_PAD
            q_ref[:, c0:c0 + QK_NOPE] = q[:, c0:c0 + QK_NOPE].astype(BF16)
            q_ref[:, c0 + QK_NOPE:c0 + HEAD_PAD] = _rope_block(q[:, c0 + QK_NOPE:c0 + HEAD_PAD],
                                                               cos, slo, shi).astype(BF16)

    kn = _dot(ckvb, wuk_ref[...])
    if token_minor:
        v_ref[...] = _dot_nt(wuv_ref[...], ckvb).astype(BF16)
    else:
        v_ref[...] = _dot(ckvb, wuv_ref[...]).astype(BF16)
    for hh in range(N_HEADS):
        c0 = hh * HEAD_PAD
        k_ref[:, c0:c0 + QK_NOPE] = kn[:, hh * QK_NOPE:(hh + 1) * QK_NOPE].astype(BF16)
        k_ref[:, c0 + QK_NOPE:c0 + HEAD_PAD] = kpeb


def _in_proj(x, tabs, tab_blocks, p, tm, tabs_t=None):
    token_minor = tabs_t is not None
    rows = x.shape[0]
    assert rows % tm == 0
    n_tiles = rows // tm
    row = lambda w: pl.BlockSpec((tm, w), lambda i: (i, 0))
    tab = pl.BlockSpec((tm, ROPE_PAD), lambda i: (i % tab_blocks, 0))
    outs = [
        (D_RNN, F32),
        (D_RNN, F32),
        (2 * D_MODEL, F32),
        (N_HEADS * HEAD_PAD, BF16),
        (N_HEADS * HEAD_PAD, BF16),
        (N_HEADS * V_HEAD, BF16),
        (KV_LORA, F32),
        (QK_ROPE, F32),
    ]
    out_specs = [row(w) for w, _ in outs]
    out_shape = [jax.ShapeDtypeStruct((rows, w), dt) for w, dt in outs]
    maybe_t = lambda shape: shape[::-1] if token_minor else shape
    in_specs = [row(D_MODEL), tab, tab, tab,
                _const_spec((1, D_MODEL)), _const_spec((1, D_MODEL)),
                _const_spec((D_MODEL, _C_GATE + 2 * D_MODEL)),
                _const_spec((D_MODEL, 2 * D_MODEL)), _const_spec((1, 2 * D_MODEL)),
                _const_spec((1, Q_LORA)), _const_spec(maybe_t((Q_LORA, N_HEADS * HEAD_PAD))),
                _const_spec((1, KV_LORA)), _const_spec((KV_LORA, N_HEADS * QK_NOPE)),
                _const_spec(maybe_t((KV_LORA, N_HEADS * V_HEAD)))]
    args = [x, *tabs, p["ln_in_g"], p["ln_in_b"], p["w_in"], p["w_gate"], p["b_gate"], p["q_norm_g"],
            p["w_uq_t"] if token_minor else p["w_uq"], p["kv_norm_g"], p["w_uk"],
            p["w_uv_t"] if token_minor else p["w_uv"]]
    if token_minor:
        for j in (3, 5):
            out_specs[j] = pl.BlockSpec((None, outs[j][0], tm), lambda i: (i, 0, 0))
            out_shape[j] = jax.ShapeDtypeStruct((n_tiles, outs[j][0], tm), BF16)
        tab_t = pl.BlockSpec((QK_ROPE // 2, tm), lambda i: (0, i % tab_blocks))
        in_specs += [tab_t, tab_t]
        args += list(tabs_t)
    return pl.pallas_call(
        functools.partial(_in_proj_kernel, token_minor=token_minor),
        grid=(n_tiles,),
        in_specs=in_specs,
        out_specs=out_specs,
        out_shape=out_shape,
        compiler_params=pltpu.CompilerParams(dimension_semantics=("arbitrary",), vmem_limit_bytes=VMEM_LIMIT),
        name="in_proj",
    )(*args)


def _rglru_kernel(rx_ref, grg_ref, cprev_ref, hprev_ref, cw_ref, cb_ref, wax_ref, ba_ref, bx_ref, lam_ref,
                  y_ref, cout_ref, hout_ref, xe_ref, a_ref, u_ref, hc_ref, *, nb, tt):
    pitch = tt + SUBLANES

    @pl.when(pl.program_id(0) == 0)
    def _():
        xe_ref[:, CONV_TAIL:SUBLANES, :] = cprev_ref[...]
        hc_ref[...] = hprev_ref[...]

    xc, new_tail = _conv_tile(xe_ref, rx_ref[...], cw_ref, cb_ref)
    cout_ref[...] = new_tail
    rate = _decay_rate(lam_ref)
    xc2 = xc.reshape(nb * tt, D_RNN)
    for n in range(D_RNN // LANES):
        cs = slice(n * LANES, (n + 1) * LANES)
        a, u = _recurrence_inputs(xc2[:, cs], wax_ref[n], ba_ref[:, cs], bx_ref[:, cs], rate[:, cs])
        for b in range(nb):
            a_ref[n, b * pitch:b * pitch + tt, :] = a[b * tt:(b + 1) * tt]
            u_ref[n, b * pitch:b * pitch + tt, :] = u[b * tt:(b + 1) * tt]
    _scan_slabs(a_ref, u_ref, hc_ref, nb, tt)
    hout_ref[...] = hc_ref[...]
    _gated_branch_output(y_ref, u_ref, grg_ref, nb, tt)


def _rglru(rx, grg, conv_prev, h_prev, p, tt):
    nb, t_len, _ = rx.shape
    assert nb % SUBLANES == 0 and t_len % tt == 0 and tt % SUBLANES == 0
    pitch = tt + SUBLANES
    seq = pl.BlockSpec((nb, tt, D_RNN), lambda i: (0, i, 0))
    return pl.pallas_call(
        functools.partial(_rglru_kernel, nb=nb, tt=tt),
        grid=(t_len // tt,),
        in_specs=[seq, seq, _const_spec((nb, CONV_W - 1, D_RNN)), _const_spec((nb, D_RNN)),
                  _const_spec((CONV_W, D_RNN)), _const_spec((1, D_RNN)),
                  _const_spec((RG_BLOCKS, RG_BLOCK, 2 * RG_BLOCK)),
                  _const_spec((1, D_RNN)), _const_spec((1, D_RNN)), _const_spec((1, D_RNN))],
        out_specs=[seq,
                   pl.BlockSpec((nb, CONV_W - 1, D_RNN), lambda i: (0, 0, 0)),
                   pl.BlockSpec((nb, D_RNN), lambda i: (0, 0))],
        out_shape=[jax.ShapeDtypeStruct((nb, t_len, D_RNN), BF16),
                   jax.ShapeDtypeStruct((nb, CONV_W - 1, D_RNN), F32),
                   jax.ShapeDtypeStruct((nb, D_RNN), F32)],
        scratch_shapes=[pltpu.VMEM((nb, tt + SUBLANES, D_RNN), F32),
                        pltpu.VMEM((D_RNN // LANES, nb * pitch, LANES), F32),
                        pltpu.VMEM((D_RNN // LANES, nb * pitch, LANES), F32),
                        pltpu.VMEM((nb, D_RNN), F32)],
        compiler_params=pltpu.CompilerParams(dimension_semantics=("arbitrary",), vmem_limit_bytes=VMEM_LIMIT),
        name="rglru",
    )(rx, grg, conv_prev, h_prev, p["conv_w"], p["conv_b"], p["w_ax"], p["rg_ba"], p["rg_bx"], p["rg_lambda"])


def _softmax_update_t(s_t, m, l, acc_t, v_t):
    m_new = jnp.maximum(m, jnp.max(s_t, axis=0, keepdims=True))
    alpha = jnp.exp2(m - m_new)
    pr = jnp.exp2(s_t - m_new)
    l = alpha * l + jnp.sum(pr, axis=0, keepdims=True)
    acc_t = alpha * acc_t + _dot(v_t, pr.astype(BF16))
    return m_new, l, acc_t


def _flash_kernel(qt_ref, k_ref, vt_ref, km_ref, vmt_ref, o_ref, *, tq, tv, hps):
    qcols = [slice(hh * HEAD_PAD, (hh + 1) * HEAD_PAD) for hh in range(hps)]
    vrows = [slice(hh * V_HEAD, (hh + 1) * V_HEAD) for hh in range(hps)]

    def queries_t(hh):
        return jnp.concatenate([qt_ref[c, qcols[hh], :] for c in range(tq // tv)], axis=1)

    def scores(j, hh):
        return _dot(k_ref[j * tq:(j + 1) * tq, qcols[hh]], queries_t(hh))

    def values_t(j, hh):
        return jnp.concatenate([vt_ref[j * (tq // tv) + c, vrows[hh], :] for c in range(tq // tv)], axis=1)

    def tile(n_full):
        kpos = lax.broadcasted_iota(jnp.int32, (tq, tq), 0)
        qpos = lax.broadcasted_iota(jnp.int32, (tq, tq), 1)

        def chunk_scores(j, hh):
            s_t = scores(j, hh)
            return s_t if j < n_full else jnp.where(kpos <= qpos, s_t, NEG)

        state, s_next = [], []
        for hh in range(hps):
            s_pre = _dot(km_ref[:, qcols[hh]], queries_t(hh))
            s_0 = chunk_scores(0, hh)
            s_next.append(chunk_scores(1, hh) if n_full else None)
            m = jnp.maximum(jnp.max(s_pre, axis=0, keepdims=True), jnp.max(s_0, axis=0, keepdims=True))
            p_pre, p_0 = jnp.exp2(s_pre - m), jnp.exp2(s_0 - m)
            l = jnp.sum(p_pre, axis=0, keepdims=True) + jnp.sum(p_0, axis=0, keepdims=True)
            acc_t = _dot(vmt_ref[vrows[hh], :], p_pre.astype(BF16)) + _dot(values_t(0, hh), p_0.astype(BF16))
            state.append((m, l, acc_t))
        for j in range(1, n_full + 1):
            for hh in range(hps):
                s_t = s_next[hh]
                if j < n_full:
                    s_next[hh] = chunk_scores(j + 1, hh)
                state[hh] = _softmax_update_t(s_t, *state[hh], values_t(j, hh))
        for hh in range(hps):
            _, l, acc_t = state[hh]
            o_ref[:, vrows[hh]] = jnp.transpose(acc_t / l).astype(BF16)

    for c in range(k_ref.shape[0] // tq):
        pl.when(pl.program_id(2) == c)(functools.partial(tile, c))


def _flash(q_t, k, v_t, k_pre, v_pre_t, tq, hps):
    nb, t_len, _ = k.shape
    tv = v_t.shape[3]
    assert t_len % tq == 0 and tq % tv == 0 and N_HEADS % hps == 0
    return pl.pallas_call(
        functools.partial(_flash_kernel, tq=tq, tv=tv, hps=hps),
        grid=(nb, N_HEADS // hps, t_len // tq),
        in_specs=[pl.BlockSpec((None, tq // tv, hps * HEAD_PAD, tv), lambda b, h, i: (b, i, h, 0)),
                  pl.BlockSpec((None, t_len, hps * HEAD_PAD), lambda b, h, i: (b, 0, h)),
                  pl.BlockSpec((None, t_len // tv, hps * V_HEAD, tv), lambda b, h, i: (b, 0, h, 0)),
                  pl.BlockSpec((N_META, hps * HEAD_PAD), lambda b, h, i: (0, h)),
                  pl.BlockSpec((hps * V_HEAD, N_META), lambda b, h, i: (h, 0))],
        out_specs=pl.BlockSpec((None, tq, hps * V_HEAD), lambda b, h, i: (b, i, h)),
        out_shape=jax.ShapeDtypeStruct((nb, t_len, N_HEADS * V_HEAD), BF16),
        compiler_params=pltpu.CompilerParams(dimension_semantics=("arbitrary",) * 3, vmem_limit_bytes=VMEM_LIMIT),
        name="flash_prompt",
    )(q_t, k, v_t, k_pre, v_pre_t)


def _meta_attn_kernel(q_ref, k_ref, v_ref, o_ref):
    qpos = lax.broadcasted_iota(jnp.int32, (N_META, N_META), 0)
    kpos = lax.broadcasted_iota(jnp.int32, (N_META, N_META), 1)
    for hh in range(N_HEADS):
        qs = slice(hh * HEAD_PAD, (hh + 1) * HEAD_PAD)
        vs = slice(hh * V_HEAD, (hh + 1) * V_HEAD)
        s = jnp.where(kpos <= qpos, _dot_nt(q_ref[:, qs], k_ref[:, qs]), NEG)
        pr = jnp.exp2(s - jnp.max(s, axis=1, keepdims=True))
        o = _dot(pr.astype(BF16), v_ref[:, vs]) / jnp.sum(pr, axis=1, keepdims=True)
        o_ref[:, vs] = o.astype(BF16)


def _meta_attn(q, k, v):
    return pl.pallas_call(
        _meta_attn_kernel,
        out_shape=jax.ShapeDtypeStruct((N_META, N_HEADS * V_HEAD), BF16),
        name="meta_attn",
    )(q, k, v)


def _absorb_kernel(q_ref, wukt_ref, qlat_ref, qpe_ref):
    q = q_ref[...]
    nb = qlat_ref.shape[0]
    qlat = _dot(q[:, :QK_NOPE], wukt_ref[...])
    qlat_ref[...] = qlat.reshape(nb, -1, KV_LORA)
    qpe_ref[...] = q[:, QK_NOPE:].astype(F32).reshape(nb, -1, ROPE_PAD)


def _absorb(q, w_ukt, nb):
    rows = q.shape[0]
    ts = rows // nb
    return pl.pallas_call(
        _absorb_kernel,
        grid=(N_HEADS,),
        in_specs=[pl.BlockSpec((rows, HEAD_PAD), lambda h: (0, h)),
                  pl.BlockSpec((None, QK_NOPE, KV_LORA), lambda h: (h, 0, 0))],
        out_specs=[pl.BlockSpec((nb, None, ts, KV_LORA), lambda h: (0, h, 0, 0)),
                   pl.BlockSpec((nb, None, ts, ROPE_PAD), lambda h: (0, h, 0, 0))],
        out_shape=[jax.ShapeDtypeStruct((nb, N_HEADS, ts, KV_LORA), F32),
                   jax.ShapeDtypeStruct((nb, N_HEADS, ts, ROPE_PAD), F32)],
        compiler_params=pltpu.CompilerParams(dimension_semantics=("arbitrary",)),
        name="absorb_q",
    )(q, w_ukt)


PAGED_BUFS = 2


def _paged_scratch(n_q, pps, page):
    return [pltpu.VMEM((n_q, 1), F32), pltpu.VMEM((n_q, 1), F32), pltpu.VMEM((n_q, KV_LORA), F32),
            pltpu.VMEM((PAGED_BUFS, pps * page, KV_LORA), F32), pltpu.VMEM((PAGED_BUFS, pps, QK_ROPE, page), F32),
            pltpu.VMEM((pps * page, KV_LORA), BF16), pltpu.VMEM((QK_ROPE, pps * page), BF16),
            pltpu.SemaphoreType.DMA((PAGED_BUFS,))]


def _paged_copies(pt_ref, ckv_hbm, kpe_hbm, cin_ref, kin_ref, sem, seq, chunk, slot, pps):
    page = kin_ref.shape[3]
    out = []
    for i in range(pps):
        pid = pt_ref[seq, chunk * pps + i]
        out.append(pltpu.make_async_copy(ckv_hbm.at[pid], cin_ref.at[slot, pl.ds(i * page, page)], sem.at[slot]))
        out.append(pltpu.make_async_copy(kpe_hbm.at[pid], kin_ref.at[slot, i], sem.at[slot]))
    return out


def _paged_new_tokens(qlat, qpe, cnew_ref, knew_ref, m_ref, l_ref, acc_ref, ts):
    cn = cnew_ref[...].astype(BF16)
    kn = knew_ref[...].astype(BF16)
    s = _dot_nt(qlat, cn) + _dot_nt(qpe, kn)
    t_q = lax.broadcasted_iota(jnp.int32, s.shape, 0) % ts
    t_k = lax.broadcasted_iota(jnp.int32, s.shape, 1)
    s = jnp.where(t_k <= t_q, s, NEG)
    m = jnp.max(s, axis=1, keepdims=True)
    pr = jnp.exp2(s - m)
    m_ref[...] = m
    l_ref[...] = jnp.sum(pr, axis=1, keepdims=True)
    acc_ref[...] = _dot(pr.astype(BF16), cn)


def _paged_scores(qlat, qpe, cin_ref, kin_ref, cbuf_ref, kbuf_ref, slot, pps):
    page = kin_ref.shape[3]
    for i in range(pps):
        rows = pl.ds(i * page, page)
        cbuf_ref[rows, :] = cin_ref[slot, rows, :].astype(BF16)
        kbuf_ref[:, i * page:(i + 1) * page] = kin_ref[slot, i].astype(BF16)
    return _dot_nt(qlat, cbuf_ref[...]) + _dot(qpe, kbuf_ref[...])


def _paged_probs(s, m_ref, l_ref):
    m = m_ref[...]
    m_new = jnp.maximum(m, jnp.max(s, axis=1, keepdims=True))
    alpha = jnp.exp2(m - m_new)
    pr = jnp.exp2(s - m_new)
    m_ref[...] = m_new
    l_ref[...] = alpha * l_ref[...] + jnp.sum(pr, axis=1, keepdims=True)
    return alpha, pr.astype(BF16)


def _paged_accumulate(alpha, pr, cbuf_ref, acc_ref):
    acc_ref[...] = alpha * acc_ref[...] + _dot(pr, cbuf_ref[...])


def _unabsorb_kernel(olat_ref, wuv_ref, o_ref):
    x = olat_ref[...]
    x = x.reshape(x.shape[0] * x.shape[1], KV_LORA).astype(BF16)
    o_ref[...] = _dot(x, wuv_ref[...]).astype(BF16)


def _unabsorb(o_lat, w_uvh):
    nb, _, ts, _ = o_lat.shape
    return pl.pallas_call(
        _unabsorb_kernel,
        grid=(N_HEADS,),
        in_specs=[pl.BlockSpec((nb, None, ts, KV_LORA), lambda h: (0, h, 0, 0)),
                  pl.BlockSpec((None, KV_LORA, V_HEAD), lambda h: (h, 0, 0))],
        out_specs=pl.BlockSpec((nb * ts, V_HEAD), lambda h: (0, h)),
        out_shape=jax.ShapeDtypeStruct((nb * ts, N_HEADS * V_HEAD), BF16),
        compiler_params=pltpu.CompilerParams(dimension_semantics=("arbitrary",)),
        name="unabsorb_o",
    )(o_lat, w_uvh)


FF_CHUNKS = 4
N_POST_WEIGHTS = 13


def _post_compute(x_ref, y_ref, o_ref, g_ref, weights, out_ref, side_work=None):
    before_up, before_down, after_down = side_work if side_work is not None else (lambda c: None,) * 3
    lng_ref, lnb_ref, wr_ref, wm_ref, wo_ref, g1_ref, b1_ref, wup_ref, bup_ref, wdn_ref, bdn_ref, g2_ref, b2_ref = weights
    h = _layer_norm(x_ref[...], lng_ref[...], lnb_ref[...])
    z_r = _dot(y_ref[...], wr_ref[...])
    z_m = _dot(o_ref[...], wm_ref[...])
    mix_in = g_ref[:, :D_MODEL] * z_r + g_ref[:, D_MODEL:] * z_m
    mix = _dot(mix_in.astype(BF16), wo_ref[...])
    before_up(0)
    x1 = _layer_norm(DN_ALPHA * h + mix, g1_ref[...], b1_ref[...])
    x1b = x1.astype(BF16)
    f = bdn_ref[...]
    ff_chunk = D_FF // FF_CHUNKS
    for c in range(FF_CHUNKS):
        cs = slice(c * ff_chunk, (c + 1) * ff_chunk)
        if c > 0:
            before_up(c)
        up = jnp.maximum(_dot(x1b, wup_ref[:, cs]) + bup_ref[:, cs], 0.0)
        before_down(c)
        f = f + _dot((up * up).astype(BF16), wdn_ref[cs, :])
        after_down(c)
    out_ref[...] = _layer_norm(DN_ALPHA * x1 + f, g2_ref[...], b2_ref[...])


def _post_kernel(x_ref, y_ref, o_ref, g_ref, *rest):
    _post_compute(x_ref, y_ref, o_ref, g_ref, rest[:N_POST_WEIGHTS], rest[N_POST_WEIGHTS])


def _post_paged_kernel(pt_ref, x_ref, y_ref, o_ref, g_ref, *rest, pps, ts, n_j, n_steps, tiles_per_seq):
    weights = rest[:N_POST_WEIGHTS]
    (qlat_ref, qpe_ref, cnew_ref, knew_ref, lat_ref, latm_ref, ckv_hbm, kpe_hbm, out_ref, olat_ref, platent_hbm,
     m_ref, l_ref, acc_ref, cin_ref, kin_ref, cbuf_ref, kbuf_ref, sem, lsem) = rest[N_POST_WEIGHTS:]
    s = pl.program_id(0)
    spb = n_j // FF_CHUNKS
    seq, j0 = lax.div(s, spb), lax.rem(s, spb) * FF_CHUNKS
    copies = functools.partial(_paged_copies, pt_ref, ckv_hbm, kpe_hbm, cin_ref, kin_ref, sem, pps=pps)

    tm = out_ref.shape[0]
    lat_rows = pltpu.make_async_copy(
        lat_ref,
        platent_hbm.at[lax.div(s, tiles_per_seq), pl.ds(N_META + lax.rem(s, tiles_per_seq) * tm, tm)], lsem.at[0])
    lat_meta = [pltpu.make_async_copy(latm_ref, platent_hbm.at[b, pl.ds(0, N_META)], lsem.at[1])
                for b in range(platent_hbm.shape[0])]
    lat_rows.start()

    @pl.when(s == 0)
    def _():
        for cp in lat_meta:
            cp.start()
        for cp in copies(0, 0, 0):
            cp.start()

    live = {}

    def gather_and_score(c):
        slot, nxt = c % PAGED_BUFS, (c + 1) % PAGED_BUFS
        if c + 1 < FF_CHUNKS:
            for cp in copies(seq, j0 + c + 1, nxt):
                cp.start()
        else:
            @pl.when(s + 1 < n_steps)
            def _():
                for cp in copies(lax.div(s + 1, spb), lax.rem(s + 1, spb) * FF_CHUNKS, nxt):
                    cp.start()
        for cp in copies(seq, j0 + c, slot):
            cp.wait()
        qlat = qlat_ref[...].astype(BF16)
        qpe = qpe_ref[...][:, :QK_ROPE].astype(BF16)
        if c == 0:
            pl.when(j0 == 0)(functools.partial(_paged_new_tokens, qlat, qpe, cnew_ref, knew_ref,
                                               m_ref, l_ref, acc_ref, ts))
        live["s"] = _paged_scores(qlat, qpe, cin_ref, kin_ref, cbuf_ref, kbuf_ref, slot, pps)

    def probs(c):
        live["alpha"], live["pr"] = _paged_probs(live.pop("s"), m_ref, l_ref)

    def accumulate(c):
        _paged_accumulate(live.pop("alpha"), live.pop("pr"), cbuf_ref, acc_ref)
        if c == FF_CHUNKS - 1:
            @pl.when(j0 == n_j - FF_CHUNKS)
            def _():
                olat_ref[...] = acc_ref[...] / l_ref[...]

    _post_compute(x_ref, y_ref, o_ref, g_ref, weights, out_ref, side_work=(gather_and_score, probs, accumulate))

    lat_rows.wait()

    @pl.when(s == 0)
    def _():
        for cp in lat_meta:
            cp.wait()


def _post_weight_specs():
    vec = lambda w: _const_spec((1, w))
    sq = _const_spec((D_MODEL, D_MODEL))
    return [vec(D_MODEL), vec(D_MODEL), sq, sq, sq, vec(D_MODEL), vec(D_MODEL),
            _const_spec((D_MODEL, D_FF)), vec(D_FF), _const_spec((D_FF, D_MODEL)), vec(D_MODEL),
            vec(D_MODEL), vec(D_MODEL)]


def _post_weights(p):
    return (p["ln_in_g"], p["ln_in_b"], p["w_br_r"], p["w_br_m"], p["w_o"], p["ln1_g"], p["ln1_b"],
            p["w_up"], p["b_up"], p["w_down"], p["b_down"], p["ln2_g"], p["ln2_b"])


def _post(x, y, o, g, p, tm):
    rows = x.shape[0]
    assert rows % tm == 0
    row = lambda w: pl.BlockSpec((tm, w), lambda i: (i, 0))
    return pl.pallas_call(
        _post_kernel,
        grid=(rows // tm,),
        in_specs=[row(D_MODEL), row(D_RNN), row(N_HEADS * V_HEAD), row(2 * D_MODEL)] + _post_weight_specs(),
        out_specs=row(D_MODEL),
        out_shape=jax.ShapeDtypeStruct((rows, D_MODEL), F32),
        compiler_params=pltpu.CompilerParams(dimension_semantics=("arbitrary",), vmem_limit_bytes=VMEM_LIMIT),
        name="merge_mlp",
    )(x, y, o, g, *_post_weights(p))


def _post_with_paged_attn(x, y, o, g, p, tm, page_table, q_lat, q_pe, ckv_new, kpe_new, ckv_pool, kpe_pool_t,
                          latent, latent_meta, n_seq):
    rows = x.shape[0]
    n_steps = rows // tm
    assert n_steps % n_seq == 0
    nb, n_pages = page_table.shape
    n_q = q_lat.shape[1]
    page = ckv_pool.shape[1]
    n_new = ckv_new.shape[1]
    chunks = n_steps * FF_CHUNKS
    assert rows % tm == 0 and (nb * n_pages) % chunks == 0 and FF_CHUNKS % PAGED_BUFS == 0
    pps = nb * n_pages // chunks
    n_j = n_pages // pps
    assert n_pages % pps == 0 and n_j % FF_CHUNKS == 0
    spb = n_j // FF_CHUNKS
    row = lambda w: pl.BlockSpec((tm, w), lambda i, pt: (i, 0))
    per_seq = lambda r, w: pl.BlockSpec((None, r, w), lambda i, pt: (i // spb, 0, 0))
    grid_spec = pltpu.PrefetchScalarGridSpec(
        num_scalar_prefetch=1,
        grid=(n_steps,),
        in_specs=[row(D_MODEL), row(D_RNN), row(N_HEADS * V_HEAD), row(2 * D_MODEL)] + _post_weight_specs()
                 + [per_seq(n_q, KV_LORA), per_seq(n_q, ROPE_PAD), per_seq(n_new, KV_LORA), per_seq(n_new, QK_ROPE),
                    row(KV_LORA), _const_spec((N_META, KV_LORA))]
                 + [pl.BlockSpec(memory_space=pl.ANY)] * 2,
        out_specs=[row(D_MODEL), per_seq(n_q, KV_LORA), pl.BlockSpec(memory_space=pl.ANY)],
        scratch_shapes=_paged_scratch(n_q, pps, page) + [pltpu.SemaphoreType.DMA((2,))],
    )
    return pl.pallas_call(
        functools.partial(_post_paged_kernel, pps=pps, ts=n_q // N_HEADS, n_j=n_j, n_steps=n_steps,
                          tiles_per_seq=n_steps // n_seq),
        grid_spec=grid_spec,
        out_shape=[jax.ShapeDtypeStruct((rows, D_MODEL), F32), jax.ShapeDtypeStruct((nb, n_q, KV_LORA), F32),
                   jax.ShapeDtypeStruct((n_seq, N_META + rows // n_seq, KV_LORA), F32)],
        compiler_params=pltpu.CompilerParams(dimension_semantics=("arbitrary",), vmem_limit_bytes=VMEM_LIMIT),
        name="merge_mlp_paged_attn",
    )(page_table, x, y, o, g, *_post_weights(p), q_lat, q_pe, ckv_new, kpe_new, latent, latent_meta,
      ckv_pool, kpe_pool_t)


def _rope_tables(pos):
    half = QK_ROPE // 2
    inv = 1.0 / (ROPE_THETA ** (jnp.arange(half, dtype=F32) / half))
    ang = pos.astype(F32)[:, None] * inv[None, :]
    cos, sin = jnp.cos(ang), jnp.sin(ang)
    z = jnp.zeros_like(cos)
    return ((jnp.concatenate([cos, cos, z, z], 1), jnp.concatenate([-sin, z, z, z], 1),
             jnp.concatenate([z, sin, z, z], 1)), (cos.T, sin.T))


def _prep_params(w_in, b_gate, conv_w, conv_b, rg_wa, rg_ba, rg_wx, rg_bx, rg_lambda, w_br_r, q_norm_g, w_uq,
                 kv_norm_g, w_uk, w_uv, w_br_m, w_o, ln1_g, ln1_b, w_up, b_up, w_down, b_down, ln2_g, ln2_b,
                 ln_in_g, ln_in_b):
    l = 0
    vec = lambda a: a.reshape(1, -1).astype(F32)
    wi = w_in[l].astype(BF16)
    w_uq_p = jnp.pad(w_uq[l], ((0, 0), (0, 0), (0, HEAD_PAD - QK_NOPE - QK_ROPE))).reshape(Q_LORA, N_HEADS * HEAD_PAD)
    return {
        "ln_in_g": vec(ln_in_g), "ln_in_b": vec(ln_in_b),
        "w_in": wi, "w_gate": wi[:, _C_GATE:], "b_gate": vec(b_gate[l]),
        "q_norm_g": vec(q_norm_g[l]), "w_uq": w_uq_p.astype(BF16), "w_uq_t": w_uq_p.T.astype(BF16),
        "kv_norm_g": vec(kv_norm_g[l]),
        "w_uk": w_uk[l].reshape(KV_LORA, N_HEADS * QK_NOPE).astype(BF16),
        "w_uv": w_uv[l].reshape(KV_LORA, N_HEADS * V_HEAD).astype(BF16),
        "w_uv_t": w_uv[l].reshape(KV_LORA, N_HEADS * V_HEAD).T.astype(BF16),
        "w_ukt": jnp.transpose(w_uk[l], (1, 2, 0)).astype(BF16),
        "w_uvh": jnp.transpose(w_uv[l], (1, 0, 2)).astype(BF16),
        "conv_w": conv_w[l].astype(F32), "conv_b": vec(conv_b[l]),
        "w_ax": jnp.concatenate([rg_wa[l], rg_wx[l]], axis=-1).astype(BF16),
        "rg_ba": vec(rg_ba[l]), "rg_bx": vec(rg_bx[l]), "rg_lambda": vec(rg_lambda[l]),
        "w_br_r": w_br_r[l].astype(BF16), "w_br_m": w_br_m[l].astype(BF16), "w_o": w_o[l].astype(BF16),
        "ln1_g": vec(ln1_g[l]), "ln1_b": vec(ln1_b[l]),
        "w_up": w_up[l].astype(BF16), "b_up": vec(b_up[l]),
        "w_down": w_down[l].astype(BF16), "b_down": vec(b_down[l]),
        "ln2_g": vec(ln2_g[l]), "ln2_b": vec(ln2_b[l]),
    }


def kernel(x_prompt, x_sample, cache_ckv, cache_kpe, page_table, state_conv, state_rglru, meta_tokens, ln_in_g, ln_in_b, w_in, b_gate, conv_w, conv_b, rg_wa, rg_ba, rg_wx, rg_bx, rg_lambda, w_br_r, q_norm_g, w_uq, kv_norm_g, w_uk, w_uv, w_br_m, w_o, ln1_g, ln1_b, w_up, b_up, w_down, b_down, ln2_g, ln2_b):
    assert w_in.shape[0] == DEPTH == 1
    bn, seq, _ = x_prompt.shape
    bd, ts, _ = x_sample.shape
    past_len = page_table.shape[1] * cache_ckv.shape[2]
    p = _prep_params(w_in, b_gate, conv_w, conv_b, rg_wa, rg_ba, rg_wx, rg_bx, rg_lambda, w_br_r, q_norm_g, w_uq,
                     kv_norm_g, w_uk, w_uv, w_br_m, w_o, ln1_g, ln1_b, w_up, b_up, w_down, b_down, ln2_g, ln2_b,
                     ln_in_g, ln_in_b)
    n_s = bd * ts

    x_small = jnp.concatenate([x_sample.reshape(n_s, D_MODEL), meta_tokens.astype(F32)], axis=0)
    pos_small = jnp.concatenate([jnp.tile(past_len + jnp.arange(ts), bd), jnp.arange(N_META)])
    n_small = n_s + N_META
    rx, grg, g_small, q, k, v, ckv, kpe = _in_proj(x_small, _rope_tables(pos_small)[0], 1, p, n_small)

    rep = lambda a: jnp.broadcast_to(a[None], (SUBLANES,) + a.shape)
    y_m, conv_m, h_m = _rglru(rep(rx[n_s:]), rep(grg[n_s:]), jnp.zeros((SUBLANES, CONV_W - 1, D_RNN), F32),
                              jnp.zeros((SUBLANES, D_RNN), F32), p, N_META)
    k_meta, v_meta = k[n_s:], v[n_s:]
    o_m = _meta_attn(q[n_s:], k_meta, v_meta)

    y_s, conv_s, h_s = _rglru(rx[:n_s].reshape(bd, ts, D_RNN), grg[:n_s].reshape(bd, ts, D_RNN),
                              state_conv[0], state_rglru[0], p, ts)
    q_lat, q_pe = _absorb(q[:n_s], p["w_ukt"], bd)
    ckv_s = ckv[:n_s].reshape(bd, ts, KV_LORA)
    kpe_s = kpe[:n_s].reshape(bd, ts, QK_ROPE)
    pad_new = lambda a: jnp.pad(a, ((0, 0), (0, 2 * SUBLANES - ts), (0, 0)))

    n_p = bn * seq
    pos_p = N_META + jnp.arange(seq)
    tm = TM_IN_PROJ
    bcast = lambda a: jnp.broadcast_to(a[:1], (bn,) + a.shape[1:])
    tabs_p, tabs_pt = _rope_tables(pos_p)
    rx_p, grg_p, g_p, qt_p, k_p, vt_p, ckv_p, kpe_p = _in_proj(x_prompt.reshape(n_p, D_MODEL), tabs_p,
                                                               seq // tm, p, tm, tabs_t=tabs_pt)
    y_p, conv_p, h_p = _rglru(rx_p.reshape(bn, seq, D_RNN), grg_p.reshape(bn, seq, D_RNN), bcast(conv_m), bcast(h_m),
                              p, TT_SCAN)
    o_p = _flash(qt_p.reshape(bn, seq // tm, -1, tm), k_p.reshape(bn, seq, -1), vt_p.reshape(bn, seq // tm, -1, tm),
                 k_meta, v_meta.T, TQ_FLASH, FLASH_HEADS)
    out_p, o_lat, prompt_ckv = _post_with_paged_attn(
        x_prompt.reshape(n_p, D_MODEL), y_p.reshape(n_p, D_RNN), o_p.reshape(n_p, -1), g_p, p, TM_POST,
        page_table, q_lat.reshape(bd, N_HEADS * ts, KV_LORA), q_pe.reshape(bd, N_HEADS * ts, ROPE_PAD),
        pad_new(ckv_s), pad_new(kpe_s), cache_ckv[0], jnp.swapaxes(cache_kpe[0], 1, 2),
        ckv_p, ckv[n_s:], bn)

    o_s = _unabsorb(o_lat.reshape(bd, N_HEADS, ts, KV_LORA), p["w_uvh"])
    y_small = jnp.concatenate([y_s.reshape(n_s, D_RNN), y_m[0]], axis=0)
    o_small = jnp.concatenate([o_s, o_m], axis=0)
    out_small = _post(x_small, y_small, o_small, g_small, p, n_small)

    meta_rows = lambda a: jnp.broadcast_to(a[n_s:][None], (bn, N_META, a.shape[-1]))
    y_prompt = out_p.reshape(bn, seq, D_MODEL)
    y_sample = out_small[:n_s].reshape(bd, ts, D_MODEL)
    prompt_kpe = jnp.concatenate([meta_rows(kpe), kpe_p.reshape(bn, seq, QK_ROPE)], axis=1)[None]
    return (y_prompt, y_sample, prompt_ckv[None], prompt_kpe, conv_p[None], h_p[None],
            ckv_s[None], kpe_s[None], conv_s[None], h_s[None])
```

```python
import functools
import math

import jax
import jax.numpy as jnp
from jax import lax
from jax.experimental import pallas as pl
from jax.experimental.pallas import tpu as pltpu

F32 = jnp.float32
BF16 = jnp.bfloat16

D_MODEL = 1024
N_META = 16
D_RNN = D_MODEL
RG_BLOCKS = 8
RG_BLOCK = D_RNN // RG_BLOCKS
CONV_W = 4
RG_C = 8.0
N_HEADS = 8
QK_NOPE = 128
QK_ROPE = 64
V_HEAD = 128
KV_LORA = 512
Q_LORA = 768
ROPE_THETA = 10000.0
D_FF = 4 * D_MODEL
DEPTH = 1
DN_ALPHA = (2.0 * DEPTH) ** 0.25
EPS = 1e-5
SM_SCALE = (QK_NOPE + QK_ROPE) ** -0.5
Q_PRESCALE = SM_SCALE * math.log2(math.e)

LANES = 128
SUBLANES = 8
HEAD_PAD = 2 * LANES
ROPE_PAD = LANES
_C_RX, _C_RG, _C_CQ, _C_CKV, _C_KPE = 0, D_RNN, 2 * D_RNN, 2 * D_RNN + Q_LORA, 2 * D_RNN + Q_LORA + KV_LORA
_C_GATE = _C_KPE + QK_ROPE
VMEM_LIMIT = 56 * 1024 * 1024
TM_IN_PROJ = 256
TM_POST = 256
TT_SCAN = 128
TQ_FLASH = 512
FLASH_HEADS = 4
FLASH_GROUP = 2
NEG = float(jnp.finfo(jnp.float32).min)


def _const_spec(shape):
    return pl.BlockSpec(shape, lambda *_: (0,) * len(shape), pipeline_mode=pl.Buffered(1))


def _layer_norm(x, g, b):
    mu = jnp.mean(x, -1, keepdims=True)
    xc = x - mu
    var = jnp.mean(xc * xc, -1, keepdims=True)
    return xc * lax.rsqrt(var + EPS) * g + b


def _rms_norm(x, g):
    return x * lax.rsqrt(jnp.mean(x * x, -1, keepdims=True) + EPS) * g


def _gelu_tanh(x):
    return 0.5 * x * (1.0 + jnp.tanh(math.sqrt(2.0 / math.pi) * (x + 0.044715 * (x * x * x))))


def _rope_block(y, cos, sin_lo, sin_hi):
    left = pltpu.roll(y, ROPE_PAD - QK_ROPE // 2, 1)
    right = pltpu.roll(y, QK_ROPE // 2, 1)
    return y * cos + left * sin_lo + right * sin_hi


def _dot(a, b):
    return jnp.dot(a, b, preferred_element_type=F32)


def _dot_nt(a, b):
    return lax.dot_general(a, b, (((1,), (1,)), ((), ())), preferred_element_type=F32)


CONV_TAIL = SUBLANES - (CONV_W - 1)


def _conv_tile(xe_ref, x, cw_ref, cb_ref):
    tt = x.shape[1]
    xe_ref[:, SUBLANES:, :] = x
    xc = cb_ref[...] + x * cw_ref[CONV_W - 1:CONV_W, :]
    for k in range(CONV_W - 1):
        xc = xc + xe_ref[:, CONV_TAIL + k:CONV_TAIL + k + tt, :] * cw_ref[k:k + 1, :]
    new_tail = xe_ref[:, tt + CONV_TAIL:tt + SUBLANES, :]
    xe_ref[:, CONV_TAIL:SUBLANES, :] = new_tail
    return xc, new_tail


def _decay_rate(lam_ref):
    z = -lam_ref[...]
    return RG_C * (jnp.maximum(z, 0.0) + jnp.log1p(jnp.exp(-jnp.abs(z))))


def _recurrence_inputs(xn, wax_n, ba_n, bx_n, rate_n):
    gates = _dot(xn.astype(BF16), wax_n)
    r = jax.nn.sigmoid(gates[:, :RG_BLOCK] + ba_n)
    ig = jax.nn.sigmoid(gates[:, RG_BLOCK:] + bx_n)
    neg_log_a = r * rate_n
    a = jnp.exp(-neg_log_a)
    u = jnp.sqrt(jnp.tanh(neg_log_a) * (a * a + 1.0)) * (ig * xn)
    return a, u


def _scan_slabs(a_ref, u_ref, hc_ref, nb, tt):
    pitch = tt + SUBLANES
    n_slab = D_RNN // LANES
    for grp in range(nb // SUBLANES):
        base = grp * SUBLANES * pitch
        rows = slice(grp * SUBLANES, (grp + 1) * SUBLANES)
        h0 = tuple(hc_ref[rows, n * LANES:(n + 1) * LANES] for n in range(n_slab))

        def step(t, hs, base=base):
            out = []
            for n in range(n_slab):
                idx = pl.ds(base + t, SUBLANES, stride=pitch)
                hn = a_ref[n, idx, :] * hs[n] + u_ref[n, idx, :]
                u_ref[n, idx, :] = hn
                out.append(hn)
            return tuple(out)

        hs = lax.fori_loop(0, tt, step, h0, unroll=min(tt, 8))
        for n in range(n_slab):
            hc_ref[rows, n * LANES:(n + 1) * LANES] = hs[n]


def _gated_branch_output(y_ref, h_ref, grg_ref, nb, tt):
    pitch = tt + SUBLANES
    for b in range(nb):
        for n in range(D_RNN // LANES):
            cs = slice(n * LANES, (n + 1) * LANES)
            y_ref[b, :, cs] = (h_ref[n, b * pitch:b * pitch + tt, :] * grg_ref[b, :, cs]).astype(BF16)


def _in_proj_kernel(x_ref, cos_ref, slo_ref, shi_ref, lng_ref, lnb_ref, w_in_ref, w_gate_ref, bg_ref,
                    qg_ref, wuq_ref, kvg_ref, wuk_ref, wuv_ref, *rest, token_minor):
    if token_minor:
        cost_ref, sint_ref, *rest = rest
    rx_ref, grg_ref, g_ref, q_ref, k_ref, v_ref, ckv_ref, kpe_ref = rest
    h = _layer_norm(x_ref[...], lng_ref[...], lnb_ref[...])
    hb = h.astype(BF16)

    def proj(lo, hi):
        return _dot(hb, w_in_ref[:, lo:hi])

    cos, slo, shi = cos_ref[...], slo_ref[...], shi_ref[...]
    cqn = _rms_norm(proj(_C_CQ, _C_CKV), qg_ref[...])
    ckv = _rms_norm(proj(_C_CKV, _C_KPE), kvg_ref[...])
    kpe_raw = proj(_C_KPE, _C_KPE + ROPE_PAD)
    lane = lax.broadcasted_iota(jnp.int32, kpe_raw.shape, 1)
    kpe = _rope_block(jnp.where(lane < QK_ROPE, kpe_raw, 0.0), cos, slo, shi)
    ckv_ref[...] = ckv
    kpe_ref[...] = kpe[:, :QK_ROPE]
    ckvb = ckv.astype(BF16)
    kpeb = kpe.astype(BF16)

    rx_ref[...] = proj(_C_RX, _C_RG)
    grg_ref[...] = _gelu_tanh(proj(_C_RG, _C_CQ))
    g_ref[...] = jax.nn.sigmoid(_dot(hb, w_gate_ref[...]) + bg_ref[...])

    cqb = cqn.astype(BF16)
    if token_minor:
        qt = _dot_nt(wuq_ref[...], cqb) * Q_PRESCALE
        cos_t, sin_t = cost_ref[...], sint_ref[...]
        half = QK_ROPE // 2
        for hh in range(N_HEADS):
            r0 = hh * HEAD_PAD
            r1, r2, r3 = r0 + QK_NOPE, r0 + QK_NOPE + half, r0 + QK_NOPE + QK_ROPE
            x1, x2 = qt[r1:r2], qt[r2:r3]
            q_ref[r0:r1, :] = qt[r0:r1].astype(BF16)
            q_ref[r1:r2, :] = (x1 * cos_t - x2 * sin_t).astype(BF16)
            q_ref[r2:r3, :] = (x1 * sin_t + x2 * cos_t).astype(BF16)
            q_ref[r3:r0 + HEAD_PAD, :] = qt[r3:r0 + HEAD_PAD].astype(BF16)
    else:
        q = _dot(cqb, wuq_ref[...]) * Q_PRESCALE
        for hh in range(N_HEADS):
            c0 = hh * HEAD_PAD
            q_ref[:, c0:c0 + QK_NOPE] = q[:, c0:c0 + QK_NOPE].astype(BF16)
            q_ref[:, c0 + QK_NOPE:c0 + HEAD_PAD] = _rope_block(q[:, c0 + QK_NOPE:c0 + HEAD_PAD],
                                                               cos, slo, shi).astype(BF16)

    kn = _dot(ckvb, wuk_ref[...])
    if token_minor:
        v_ref[...] = _dot_nt(wuv_ref[...], ckvb).astype(BF16)
    else:
        v_ref[...] = _dot(ckvb, wuv_ref[...]).astype(BF16)
    for hh in range(N_HEADS):
        c0 = hh * HEAD_PAD
        k_ref[:, c0:c0 + QK_NOPE] = kn[:, hh * QK_NOPE:(hh + 1) * QK_NOPE].astype(BF16)
        k_ref[:, c0 + QK_NOPE:c0 + HEAD_PAD] = kpeb


def _in_proj(x, tabs, tab_blocks, p, tm, tabs_t=None):
    token_minor = tabs_t is not None
    rows = x.shape[0]
    assert rows % tm == 0
    n_tiles = rows // tm
    row = lambda w: pl.BlockSpec((tm, w), lambda i: (i, 0))
    tab = pl.BlockSpec((tm, ROPE_PAD), lambda i: (i % tab_blocks, 0))
    outs = [
        (D_RNN, F32),
        (D_RNN, F32),
        (2 * D_MODEL, F32),
        (N_HEADS * HEAD_PAD, BF16),
        (N_HEADS * HEAD_PAD, BF16),
        (N_HEADS * V_HEAD, BF16),
        (KV_LORA, F32),
        (QK_ROPE, F32),
    ]
    out_specs = [row(w) for w, _ in outs]
    out_shape = [jax.ShapeDtypeStruct((rows, w), dt) for w, dt in outs]
    maybe_t = lambda shape: shape[::-1] if token_minor else shape
    in_specs = [row(D_MODEL), tab, tab, tab,
                _const_spec((1, D_MODEL)), _const_spec((1, D_MODEL)),
                _const_spec((D_MODEL, _C_GATE + 2 * D_MODEL)),
                _const_spec((D_MODEL, 2 * D_MODEL)), _const_spec((1, 2 * D_MODEL)),
                _const_spec((1, Q_LORA)), _const_spec(maybe_t((Q_LORA, N_HEADS * HEAD_PAD))),
                _const_spec((1, KV_LORA)), _const_spec((KV_LORA, N_HEADS * QK_NOPE)),
                _const_spec(maybe_t((KV_LORA, N_HEADS * V_HEAD)))]
    args = [x, *tabs, p["ln_in_g"], p["ln_in_b"], p["w_in"], p["w_gate"], p["b_gate"], p["q_norm_g"],
            p["w_uq_t"] if token_minor else p["w_uq"], p["kv_norm_g"], p["w_uk"],
            p["w_uv_t"] if token_minor else p["w_uv"]]
    if token_minor:
        for j in (3, 5):
            out_specs[j] = pl.BlockSpec((None, outs[j][0], tm), lambda i: (i, 0, 0))
            out_shape[j] = jax.ShapeDtypeStruct((n_tiles, outs[j][0], tm), BF16)
        tab_t = pl.BlockSpec((QK_ROPE // 2, tm), lambda i: (0, i % tab_blocks))
        in_specs += [tab_t, tab_t]
        args += list(tabs_t)
    return pl.pallas_call(
        functools.partial(_in_proj_kernel, token_minor=token_minor),
        grid=(n_tiles,),
        in_specs=in_specs,
        out_specs=out_specs,
        out_shape=out_shape,
        compiler_params=pltpu.CompilerParams(dimension_semantics=("arbitrary",), vmem_limit_bytes=VMEM_LIMIT),
        name="in_proj",
    )(*args)


def _rglru_kernel(rx_ref, grg_ref, cprev_ref, hprev_ref, cw_ref, cb_ref, wax_ref, ba_ref, bx_ref, lam_ref,
                  y_ref, cout_ref, hout_ref, xe_ref, a_ref, u_ref, hc_ref, *, nb, tt):
    pitch = tt + SUBLANES

    @pl.when(pl.program_id(0) == 0)
    def _():
        xe_ref[:, CONV_TAIL:SUBLANES, :] = cprev_ref[...]
        hc_ref[...] = hprev_ref[...]

    xc, new_tail = _conv_tile(xe_ref, rx_ref[...], cw_ref, cb_ref)
    cout_ref[...] = new_tail
    rate = _decay_rate(lam_ref)
    xc2 = xc.reshape(nb * tt, D_RNN)
    for n in range(D_RNN // LANES):
        cs = slice(n * LANES, (n + 1) * LANES)
        a, u = _recurrence_inputs(xc2[:, cs], wax_ref[n], ba_ref[:, cs], bx_ref[:, cs], rate[:, cs])
        for b in range(nb):
            a_ref[n, b * pitch:b * pitch + tt, :] = a[b * tt:(b + 1) * tt]
            u_ref[n, b * pitch:b * pitch + tt, :] = u[b * tt:(b + 1) * tt]
    _scan_slabs(a_ref, u_ref, hc_ref, nb, tt)
    hout_ref[...] = hc_ref[...]
    _gated_branch_output(y_ref, u_ref, grg_ref, nb, tt)


def _rglru(rx, grg, conv_prev, h_prev, p, tt):
    nb, t_len, _ = rx.shape
    assert nb % SUBLANES == 0 and t_len % tt == 0 and tt % SUBLANES == 0
    pitch = tt + SUBLANES
    seq = pl.BlockSpec((nb, tt, D_RNN), lambda i: (0, i, 0))
    return pl.pallas_call(
        functools.partial(_rglru_kernel, nb=nb, tt=tt),
        grid=(t_len // tt,),
        in_specs=[seq, seq, _const_spec((nb, CONV_W - 1, D_RNN)), _const_spec((nb, D_RNN)),
                  _const_spec((CONV_W, D_RNN)), _const_spec((1, D_RNN)),
                  _const_spec((RG_BLOCKS, RG_BLOCK, 2 * RG_BLOCK)),
                  _const_spec((1, D_RNN)), _const_spec((1, D_RNN)), _const_spec((1, D_RNN))],
        out_specs=[seq,
                   pl.BlockSpec((nb, CONV_W - 1, D_RNN), lambda i: (0, 0, 0)),
                   pl.BlockSpec((nb, D_RNN), lambda i: (0, 0))],
        out_shape=[jax.ShapeDtypeStruct((nb, t_len, D_RNN), BF16),
                   jax.ShapeDtypeStruct((nb, CONV_W - 1, D_RNN), F32),
                   jax.ShapeDtypeStruct((nb, D_RNN), F32)],
        scratch_shapes=[pltpu.VMEM((nb, tt + SUBLANES, D_RNN), F32),
                        pltpu.VMEM((D_RNN // LANES, nb * pitch, LANES), F32),
                        pltpu.VMEM((D_RNN // LANES, nb * pitch, LANES), F32),
                        pltpu.VMEM((nb, D_RNN), F32)],
        compiler_params=pltpu.CompilerParams(dimension_semantics=("arbitrary",), vmem_limit_bytes=VMEM_LIMIT),
        name="rglru",
    )(rx, grg, conv_prev, h_prev, p["conv_w"], p["conv_b"], p["w_ax"], p["rg_ba"], p["rg_bx"], p["rg_lambda"])


def _flash_kernel(qt_ref, k_ref, vt_ref, km_ref, vmt_ref, o_ref, *, tq, tv, hps):
    qcols = [slice(hh * HEAD_PAD, (hh + 1) * HEAD_PAD) for hh in range(hps)]
    vrows = [slice(hh * V_HEAD, (hh + 1) * V_HEAD) for hh in range(hps)]

    def queries_t(hh):
        return jnp.concatenate([qt_ref[c, qcols[hh], :] for c in range(tq // tv)], axis=1)

    def scores(j, hh):
        return _dot(k_ref[j * tq:(j + 1) * tq, qcols[hh]], queries_t(hh))

    def values_t(j, hh):
        return jnp.concatenate([vt_ref[j * (tq // tv) + c, vrows[hh], :] for c in range(tq // tv)], axis=1)

    def tile(n_full):
        kpos = lax.broadcasted_iota(jnp.int32, (tq, tq), 0)
        qpos = lax.broadcasted_iota(jnp.int32, (tq, tq), 1)

        def chunk_scores(j, hh):
            s_t = scores(j, hh)
            return s_t if j < n_full else jnp.where(kpos <= qpos, s_t, NEG)

        size = FLASH_GROUP if n_full + 1 > FLASH_GROUP else 1
        groups = [list(range(g0, min(g0 + size, n_full + 1))) for g0 in range(0, n_full + 1, size)]

        def group_scores(gi, hh):
            return [chunk_scores(j, hh) for j in groups[gi]]

        state = [None] * hps
        s_next = [group_scores(0, hh) for hh in range(hps)]
        for gi, group in enumerate(groups):
            for hh in range(hps):
                ss = s_next[hh]
                vs = [values_t(j, hh) for j in group]
                if gi == 0:
                    ss = [_dot(km_ref[:, qcols[hh]], queries_t(hh))] + ss
                    vs = [vmt_ref[vrows[hh], :]] + vs
                if gi + 1 < len(groups):
                    s_next[hh] = group_scores(gi + 1, hh)
                m_new = functools.reduce(jnp.maximum, [jnp.max(s_t, axis=0, keepdims=True) for s_t in ss])
                if state[hh] is not None:
                    m_new = jnp.maximum(state[hh][0], m_new)
                ps = [jnp.exp2(s_t - m_new) for s_t in ss]
                l = functools.reduce(jnp.add, [jnp.sum(p, axis=0, keepdims=True) for p in ps])
                acc_t = functools.reduce(jnp.add, [_dot(v_t, p.astype(BF16)) for v_t, p in zip(vs, ps)])
                if state[hh] is not None:
                    m, l_old, acc_old = state[hh]
                    alpha = jnp.exp2(m - m_new)
                    l, acc_t = alpha * l_old + l, alpha * acc_old + acc_t
                state[hh] = (m_new, l, acc_t)
        for hh in range(hps):
            _, l, acc_t = state[hh]
            o_ref[:, vrows[hh]] = jnp.transpose(acc_t / l).astype(BF16)

    for c in range(k_ref.shape[0] // tq):
        pl.when(pl.program_id(2) == c)(functools.partial(tile, c))


def _flash(q_t, k, v_t, k_pre, v_pre_t, tq, hps):
    nb, t_len, _ = k.shape
    tv = v_t.shape[3]
    assert t_len % tq == 0 and tq % tv == 0 and N_HEADS % hps == 0
    return pl.pallas_call(
        functools.partial(_flash_kernel, tq=tq, tv=tv, hps=hps),
        grid=(nb, N_HEADS // hps, t_len // tq),
        in_specs=[pl.BlockSpec((None, tq // tv, hps * HEAD_PAD, tv), lambda b, h, i: (b, i, h, 0)),
                  pl.BlockSpec((None, t_len, hps * HEAD_PAD), lambda b, h, i: (b, 0, h)),
                  pl.BlockSpec((None, t_len // tv, hps * V_HEAD, tv), lambda b, h, i: (b, 0, h, 0)),
                  pl.BlockSpec((N_META, hps * HEAD_PAD), lambda b, h, i: (0, h)),
                  pl.BlockSpec((hps * V_HEAD, N_META), lambda b, h, i: (h, 0))],
        out_specs=pl.BlockSpec((None, tq, hps * V_HEAD), lambda b, h, i: (b, i, h)),
        out_shape=jax.ShapeDtypeStruct((nb, t_len, N_HEADS * V_HEAD), BF16),
        compiler_params=pltpu.CompilerParams(dimension_semantics=("arbitrary",) * 3, vmem_limit_bytes=VMEM_LIMIT),
        name="flash_prompt",
    )(q_t, k, v_t, k_pre, v_pre_t)


def _meta_attn_kernel(q_ref, k_ref, v_ref, o_ref):
    qpos = lax.broadcasted_iota(jnp.int32, (N_META, N_META), 0)
    kpos = lax.broadcasted_iota(jnp.int32, (N_META, N_META), 1)
    for hh in range(N_HEADS):
        qs = slice(hh * HEAD_PAD, (hh + 1) * HEAD_PAD)
        vs = slice(hh * V_HEAD, (hh + 1) * V_HEAD)
        s = jnp.where(kpos <= qpos, _dot_nt(q_ref[:, qs], k_ref[:, qs]), NEG)
        pr = jnp.exp2(s - jnp.max(s, axis=1, keepdims=True))
        o = _dot(pr.astype(BF16), v_ref[:, vs]) / jnp.sum(pr, axis=1, keepdims=True)
        o_ref[:, vs] = o.astype(BF16)


def _meta_attn(q, k, v):
    return pl.pallas_call(
        _meta_attn_kernel,
        out_shape=jax.ShapeDtypeStruct((N_META, N_HEADS * V_HEAD), BF16),
        name="meta_attn",
    )(q, k, v)


def _absorb_kernel(q_ref, wukt_ref, qlat_ref, qpe_ref):
    q = q_ref[...]
    nb = qlat_ref.shape[0]
    qlat = _dot(q[:, :QK_NOPE], wukt_ref[...])
    qlat_ref[...] = qlat.reshape(nb, -1, KV_LORA)
    qpe_ref[...] = q[:, QK_NOPE:].astype(F32).reshape(nb, -1, ROPE_PAD)


def _absorb(q, w_ukt, nb):
    rows = q.shape[0]
    ts = rows // nb
    return pl.pallas_call(
        _absorb_kernel,
        grid=(N_HEADS,),
        in_specs=[pl.BlockSpec((rows, HEAD_PAD), lambda h: (0, h)),
                  pl.BlockSpec((None, QK_NOPE, KV_LORA), lambda h: (h, 0, 0))],
        out_specs=[pl.BlockSpec((nb, None, ts, KV_LORA), lambda h: (0, h, 0, 0)),
                   pl.BlockSpec((nb, None, ts, ROPE_PAD), lambda h: (0, h, 0, 0))],
        out_shape=[jax.ShapeDtypeStruct((nb, N_HEADS, ts, KV_LORA), F32),
                   jax.ShapeDtypeStruct((nb, N_HEADS, ts, ROPE_PAD), F32)],
        compiler_params=pltpu.CompilerParams(dimension_semantics=("arbitrary",)),
        name="absorb_q",
    )(q, w_ukt)


PAGED_BUFS = 2


def _paged_scratch(n_q, pps, page):
    return [pltpu.VMEM((n_q, 1), F32), pltpu.VMEM((n_q, 1), F32), pltpu.VMEM((n_q, KV_LORA), F32),
            pltpu.VMEM((PAGED_BUFS, pps * page, KV_LORA), F32), pltpu.VMEM((PAGED_BUFS, pps, QK_ROPE, page), F32),
            pltpu.VMEM((pps * page, KV_LORA), BF16), pltpu.VMEM((QK_ROPE, pps * page), BF16),
            pltpu.SemaphoreType.DMA((PAGED_BUFS,))]


def _paged_copies(pt_ref, ckv_hbm, kpe_hbm, cin_ref, kin_ref, sem, seq, chunk, slot, pps):
    page = kin_ref.shape[3]
    out = []
    for i in range(pps):
        pid = pt_ref[seq, chunk * pps + i]
        out.append(pltpu.make_async_copy(ckv_hbm.at[pid], cin_ref.at[slot, pl.ds(i * page, page)], sem.at[slot]))
        out.append(pltpu.make_async_copy(kpe_hbm.at[pid], kin_ref.at[slot, i], sem.at[slot]))
    return out


def _paged_new_tokens(qlat, qpe, cnew_ref, knew_ref, m_ref, l_ref, acc_ref, ts):
    cn = cnew_ref[...].astype(BF16)
    kn = knew_ref[...].astype(BF16)
    s = _dot_nt(qlat, cn) + _dot_nt(qpe, kn)
    t_q = lax.broadcasted_iota(jnp.int32, s.shape, 0) % ts
    t_k = lax.broadcasted_iota(jnp.int32, s.shape, 1)
    s = jnp.where(t_k <= t_q, s, NEG)
    m = jnp.max(s, axis=1, keepdims=True)
    pr = jnp.exp2(s - m)
    m_ref[...] = m
    l_ref[...] = jnp.sum(pr, axis=1, keepdims=True)
    acc_ref[...] = _dot(pr.astype(BF16), cn)


def _paged_scores(qlat, qpe, cin_ref, kin_ref, cbuf_ref, kbuf_ref, slot, pps):
    page = kin_ref.shape[3]
    for i in range(pps):
        rows = pl.ds(i * page, page)
        cbuf_ref[rows, :] = cin_ref[slot, rows, :].astype(BF16)
        kbuf_ref[:, i * page:(i + 1) * page] = kin_ref[slot, i].astype(BF16)
    return _dot_nt(qlat, cbuf_ref[...]) + _dot(qpe, kbuf_ref[...])


def _paged_probs(s, m_ref, l_ref):
    m = m_ref[...]
    m_new = jnp.maximum(m, jnp.max(s, axis=1, keepdims=True))
    alpha = jnp.exp2(m - m_new)
    pr = jnp.exp2(s - m_new)
    m_ref[...] = m_new
    l_ref[...] = alpha * l_ref[...] + jnp.sum(pr, axis=1, keepdims=True)
    return alpha, pr.astype(BF16)


def _paged_accumulate(alpha, pr, cbuf_ref, acc_ref):
    acc_ref[...] = alpha * acc_ref[...] + _dot(pr, cbuf_ref[...])


def _unabsorb_kernel(olat_ref, wuv_ref, o_ref):
    x = olat_ref[...]
    x = x.reshape(x.shape[0] * x.shape[1], KV_LORA).astype(BF16)
    o_ref[...] = _dot(x, wuv_ref[...]).astype(BF16)


def _unabsorb(o_lat, w_uvh):
    nb, _, ts, _ = o_lat.shape
    return pl.pallas_call(
        _unabsorb_kernel,
        grid=(N_HEADS,),
        in_specs=[pl.BlockSpec((nb, None, ts, KV_LORA), lambda h: (0, h, 0, 0)),
                  pl.BlockSpec((None, KV_LORA, V_HEAD), lambda h: (h, 0, 0))],
        out_specs=pl.BlockSpec((nb * ts, V_HEAD), lambda h: (0, h)),
        out_shape=jax.ShapeDtypeStruct((nb * ts, N_HEADS * V_HEAD), BF16),
        compiler_params=pltpu.CompilerParams(dimension_semantics=("arbitrary",)),
        name="unabsorb_o",
    )(o_lat, w_uvh)


FF_CHUNKS = 4
N_POST_WEIGHTS = 13


def _post_compute(x_ref, y_ref, o_ref, g_ref, weights, out_ref, side_work=None):
    before_up, before_down, after_down = side_work if side_work is not None else (lambda c: None,) * 3
    lng_ref, lnb_ref, wr_ref, wm_ref, wo_ref, g1_ref, b1_ref, wup_ref, bup_ref, wdn_ref, bdn_ref, g2_ref, b2_ref = weights
    h = _layer_norm(x_ref[...], lng_ref[...], lnb_ref[...])
    z_r = _dot(y_ref[...], wr_ref[...])
    z_m = _dot(o_ref[...], wm_ref[...])
    mix_in = g_ref[:, :D_MODEL] * z_r + g_ref[:, D_MODEL:] * z_m
    mix = _dot(mix_in.astype(BF16), wo_ref[...])
    before_up(0)
    x1 = _layer_norm(DN_ALPHA * h + mix, g1_ref[...], b1_ref[...])
    x1b = x1.astype(BF16)
    f = bdn_ref[...]
    ff_chunk = D_FF // FF_CHUNKS
    for c in range(FF_CHUNKS):
        cs = slice(c * ff_chunk, (c + 1) * ff_chunk)
        if c > 0:
            before_up(c)
        up = jnp.maximum(_dot(x1b, wup_ref[:, cs]) + bup_ref[:, cs], 0.0)
        before_down(c)
        f = f + _dot((up * up).astype(BF16), wdn_ref[cs, :])
        after_down(c)
    out_ref[...] = _layer_norm(DN_ALPHA * x1 + f, g2_ref[...], b2_ref[...])


def _post_kernel(x_ref, y_ref, o_ref, g_ref, *rest):
    _post_compute(x_ref, y_ref, o_ref, g_ref, rest[:N_POST_WEIGHTS], rest[N_POST_WEIGHTS])


def _post_paged_kernel(pt_ref, x_ref, y_ref, o_ref, g_ref, *rest, pps, ts, n_j, n_steps, tiles_per_seq):
    weights = rest[:N_POST_WEIGHTS]
    (qlat_ref, qpe_ref, cnew_ref, knew_ref, lat_ref, latm_ref, ckv_hbm, kpe_hbm, out_ref, olat_ref, platent_hbm,
     m_ref, l_ref, acc_ref, cin_ref, kin_ref, cbuf_ref, kbuf_ref, sem, lsem) = rest[N_POST_WEIGHTS:]
    s = pl.program_id(0)
    spb = n_j // FF_CHUNKS
    seq, j0 = lax.div(s, spb), lax.rem(s, spb) * FF_CHUNKS
    copies = functools.partial(_paged_copies, pt_ref, ckv_hbm, kpe_hbm, cin_ref, kin_ref, sem, pps=pps)

    tm = out_ref.shape[0]
    lat_rows = pltpu.make_async_copy(
        lat_ref,
        platent_hbm.at[lax.div(s, tiles_per_seq), pl.ds(N_META + lax.rem(s, tiles_per_seq) * tm, tm)], lsem.at[0])
    lat_meta = [pltpu.make_async_copy(latm_ref, platent_hbm.at[b, pl.ds(0, N_META)], lsem.at[1])
                for b in range(platent_hbm.shape[0])]
    lat_rows.start()

    @pl.when(s == 0)
    def _():
        for cp in lat_meta:
            cp.start()
        for cp in copies(0, 0, 0):
            cp.start()

    live = {}

    def gather_and_score(c):
        slot, nxt = c % PAGED_BUFS, (c + 1) % PAGED_BUFS
        if c + 1 < FF_CHUNKS:
            for cp in copies(seq, j0 + c + 1, nxt):
                cp.start()
        else:
            @pl.when(s + 1 < n_steps)
            def _():
                for cp in copies(lax.div(s + 1, spb), lax.rem(s + 1, spb) * FF_CHUNKS, nxt):
                    cp.start()
        for cp in copies(seq, j0 + c, slot):
            cp.wait()
        qlat = qlat_ref[...].astype(BF16)
        qpe = qpe_ref[...][:, :QK_ROPE].astype(BF16)
        if c == 0:
            pl.when(j0 == 0)(functools.partial(_paged_new_tokens, qlat, qpe, cnew_ref, knew_ref,
                                               m_ref, l_ref, acc_ref, ts))
        live["s"] = _paged_scores(qlat, qpe, cin_ref, kin_ref, cbuf_ref, kbuf_ref, slot, pps)

    def probs(c):
        live["alpha"], live["pr"] = _paged_probs(live.pop("s"), m_ref, l_ref)

    def accumulate(c):
        _paged_accumulate(live.pop("alpha"), live.pop("pr"), cbuf_ref, acc_ref)
        if c == FF_CHUNKS - 1:
            @pl.when(j0 == n_j - FF_CHUNKS)
            def _():
                olat_ref[...] = acc_ref[...] / l_ref[...]

    _post_compute(x_ref, y_ref, o_ref, g_ref, weights, out_ref, side_work=(gather_and_score, probs, accumulate))

    lat_rows.wait()

    @pl.when(s == 0)
    def _():
        for cp in lat_meta:
            cp.wait()


def _post_weight_specs():
    vec = lambda w: _const_spec((1, w))
    sq = _const_spec((D_MODEL, D_MODEL))
    return [vec(D_MODEL), vec(D_MODEL), sq, sq, sq, vec(D_MODEL), vec(D_MODEL),
            _const_spec((D_MODEL, D_FF)), vec(D_FF), _const_spec((D_FF, D_MODEL)), vec(D_MODEL),
            vec(D_MODEL), vec(D_MODEL)]


def _post_weights(p):
    return (p["ln_in_g"], p["ln_in_b"], p["w_br_r"], p["w_br_m"], p["w_o"], p["ln1_g"], p["ln1_b"],
            p["w_up"], p["b_up"], p["w_down"], p["b_down"], p["ln2_g"], p["ln2_b"])


def _post(x, y, o, g, p, tm):
    rows = x.shape[0]
    assert rows % tm == 0
    row = lambda w: pl.BlockSpec((tm, w), lambda i: (i, 0))
    return pl.pallas_call(
        _post_kernel,
        grid=(rows // tm,),
        in_specs=[row(D_MODEL), row(D_RNN), row(N_HEADS * V_HEAD), row(2 * D_MODEL)] + _post_weight_specs(),
        out_specs=row(D_MODEL),
        out_shape=jax.ShapeDtypeStruct((rows, D_MODEL), F32),
        compiler_params=pltpu.CompilerParams(dimension_semantics=("arbitrary",), vmem_limit_bytes=VMEM_LIMIT),
        name="merge_mlp",
    )(x, y, o, g, *_post_weights(p))


def _post_with_paged_attn(x, y, o, g, p, tm, page_table, q_lat, q_pe, ckv_new, kpe_new, ckv_pool, kpe_pool_t,
                          latent, latent_meta, n_seq):
    rows = x.shape[0]
    n_steps = rows // tm
    assert n_steps % n_seq == 0
    nb, n_pages = page_table.shape
    n_q = q_lat.shape[1]
    page = ckv_pool.shape[1]
    n_new = ckv_new.shape[1]
    chunks = n_steps * FF_CHUNKS
    assert rows % tm == 0 and (nb * n_pages) % chunks == 0 and FF_CHUNKS % PAGED_BUFS == 0
    pps = nb * n_pages // chunks
    n_j = n_pages // pps
    assert n_pages % pps == 0 and n_j % FF_CHUNKS == 0
    spb = n_j // FF_CHUNKS
    row = lambda w: pl.BlockSpec((tm, w), lambda i, pt: (i, 0))
    per_seq = lambda r, w: pl.BlockSpec((None, r, w), lambda i, pt: (i // spb, 0, 0))
    grid_spec = pltpu.PrefetchScalarGridSpec(
        num_scalar_prefetch=1,
        grid=(n_steps,),
        in_specs=[row(D_MODEL), row(D_RNN), row(N_HEADS * V_HEAD), row(2 * D_MODEL)] + _post_weight_specs()
                 + [per_seq(n_q, KV_LORA), per_seq(n_q, ROPE_PAD), per_seq(n_new, KV_LORA), per_seq(n_new, QK_ROPE),
                    row(KV_LORA), _const_spec((N_META, KV_LORA))]
                 + [pl.BlockSpec(memory_space=pl.ANY)] * 2,
        out_specs=[row(D_MODEL), per_seq(n_q, KV_LORA), pl.BlockSpec(memory_space=pl.ANY)],
        scratch_shapes=_paged_scratch(n_q, pps, page) + [pltpu.SemaphoreType.DMA((2,))],
    )
    return pl.pallas_call(
        functools.partial(_post_paged_kernel, pps=pps, ts=n_q // N_HEADS, n_j=n_j, n_steps=n_steps,
                          tiles_per_seq=n_steps // n_seq),
        grid_spec=grid_spec,
        out_shape=[jax.ShapeDtypeStruct((rows, D_MODEL), F32), jax.ShapeDtypeStruct((nb, n_q, KV_LORA), F32),
                   jax.ShapeDtypeStruct((n_seq, N_META + rows // n_seq, KV_LORA), F32)],
        compiler_params=pltpu.CompilerParams(dimension_semantics=("arbitrary",), vmem_limit_bytes=VMEM_LIMIT),
        name="merge_mlp_paged_attn",
    )(page_table, x, y, o, g, *_post_weights(p), q_lat, q_pe, ckv_new, kpe_new, latent, latent_meta,
      ckv_pool, kpe_pool_t)


def _rope_tables(pos):
    half = QK_ROPE // 2
    inv = 1.0 / (ROPE_THETA ** (jnp.arange(half, dtype=F32) / half))
    ang = pos.astype(F32)[:, None] * inv[None, :]
    cos, sin = jnp.cos(ang), jnp.sin(ang)
    z = jnp.zeros_like(cos)
    return ((jnp.concatenate([cos, cos, z, z], 1), jnp.concatenate([-sin, z, z, z], 1),
             jnp.concatenate([z, sin, z, z], 1)), (cos.T, sin.T))


def _prep_params(w_in, b_gate, conv_w, conv_b, rg_wa, rg_ba, rg_wx, rg_bx, rg_lambda, w_br_r, q_norm_g, w_uq,
                 kv_norm_g, w_uk, w_uv, w_br_m, w_o, ln1_g, ln1_b, w_up, b_up, w_down, b_down, ln2_g, ln2_b,
                 ln_in_g, ln_in_b):
    l = 0
    vec = lambda a: a.reshape(1, -1).astype(F32)
    wi = w_in[l].astype(BF16)
    w_uq_p = jnp.pad(w_uq[l], ((0, 0), (0, 0), (0, HEAD_PAD - QK_NOPE - QK_ROPE))).reshape(Q_LORA, N_HEADS * HEAD_PAD)
    return {
        "ln_in_g": vec(ln_in_g), "ln_in_b": vec(ln_in_b),
        "w_in": wi, "w_gate": wi[:, _C_GATE:], "b_gate": vec(b_gate[l]),
        "q_norm_g": vec(q_norm_g[l]), "w_uq": w_uq_p.astype(BF16), "w_uq_t": w_uq_p.T.astype(BF16),
        "kv_norm_g": vec(kv_norm_g[l]),
        "w_uk": w_uk[l].reshape(KV_LORA, N_HEADS * QK_NOPE).astype(BF16),
        "w_uv": w_uv[l].reshape(KV_LORA, N_HEADS * V_HEAD).astype(BF16),
        "w_uv_t": w_uv[l].reshape(KV_LORA, N_HEADS * V_HEAD).T.astype(BF16),
        "w_ukt": jnp.transpose(w_uk[l], (1, 2, 0)).astype(BF16),
        "w_uvh": jnp.transpose(w_uv[l], (1, 0, 2)).astype(BF16),
        "conv_w": conv_w[l].astype(F32), "conv_b": vec(conv_b[l]),
        "w_ax": jnp.concatenate([rg_wa[l], rg_wx[l]], axis=-1).astype(BF16),
        "rg_ba": vec(rg_ba[l]), "rg_bx": vec(rg_bx[l]), "rg_lambda": vec(rg_lambda[l]),
        "w_br_r": w_br_r[l].astype(BF16), "w_br_m": w_br_m[l].astype(BF16), "w_o": w_o[l].astype(BF16),
        "ln1_g": vec(ln1_g[l]), "ln1_b": vec(ln1_b[l]),
        "w_up": w_up[l].astype(BF16), "b_up": vec(b_up[l]),
        "w_down": w_down[l].astype(BF16), "b_down": vec(b_down[l]),
        "ln2_g": vec(ln2_g[l]), "ln2_b": vec(ln2_b[l]),
    }


def kernel(x_prompt, x_sample, cache_ckv, cache_kpe, page_table, state_conv, state_rglru, meta_tokens, ln_in_g, ln_in_b, w_in, b_gate, conv_w, conv_b, rg_wa, rg_ba, rg_wx, rg_bx, rg_lambda, w_br_r, q_norm_g, w_uq, kv_norm_g, w_uk, w_uv, w_br_m, w_o, ln1_g, ln1_b, w_up, b_up, w_down, b_down, ln2_g, ln2_b):
    assert w_in.shape[0] == DEPTH == 1
    bn, seq, _ = x_prompt.shape
    bd, ts, _ = x_sample.shape
    past_len = page_table.shape[1] * cache_ckv.shape[2]
    p = _prep_params(w_in, b_gate, conv_w, conv_b, rg_wa, rg_ba, rg_wx, rg_bx, rg_lambda, w_br_r, q_norm_g, w_uq,
                     kv_norm_g, w_uk, w_uv, w_br_m, w_o, ln1_g, ln1_b, w_up, b_up, w_down, b_down, ln2_g, ln2_b,
                     ln_in_g, ln_in_b)
    n_s = bd * ts

    x_small = jnp.concatenate([x_sample.reshape(n_s, D_MODEL), meta_tokens.astype(F32)], axis=0)
    pos_small = jnp.concatenate([jnp.tile(past_len + jnp.arange(ts), bd), jnp.arange(N_META)])
    n_small = n_s + N_META
    rx, grg, g_small, q, k, v, ckv, kpe = _in_proj(x_small, _rope_tables(pos_small)[0], 1, p, n_small)

    rep = lambda a: jnp.broadcast_to(a[None], (SUBLANES,) + a.shape)
    y_m, conv_m, h_m = _rglru(rep(rx[n_s:]), rep(grg[n_s:]), jnp.zeros((SUBLANES, CONV_W - 1, D_RNN), F32),
                              jnp.zeros((SUBLANES, D_RNN), F32), p, N_META)
    k_meta, v_meta = k[n_s:], v[n_s:]
    o_m = _meta_attn(q[n_s:], k_meta, v_meta)

    y_s, conv_s, h_s = _rglru(rx[:n_s].reshape(bd, ts, D_RNN), grg[:n_s].reshape(bd, ts, D_RNN),
                              state_conv[0], state_rglru[0], p, ts)
    q_lat, q_pe = _absorb(q[:n_s], p["w_ukt"], bd)
    ckv_s = ckv[:n_s].reshape(bd, ts, KV_LORA)
    kpe_s = kpe[:n_s].reshape(bd, ts, QK_ROPE)
    pad_new = lambda a: jnp.pad(a, ((0, 0), (0, 2 * SUBLANES - ts), (0, 0)))

    n_p = bn * seq
    pos_p = N_META + jnp.arange(seq)
    tm = TM_IN_PROJ
    bcast = lambda a: jnp.broadcast_to(a[:1], (bn,) + a.shape[1:])
    tabs_p, tabs_pt = _rope_tables(pos_p)
    rx_p, grg_p, g_p, qt_p, k_p, vt_p, ckv_p, kpe_p = _in_proj(x_prompt.reshape(n_p, D_MODEL), tabs_p,
                                                               seq // tm, p, tm, tabs_t=tabs_pt)
    y_p, conv_p, h_p = _rglru(rx_p.reshape(bn, seq, D_RNN), grg_p.reshape(bn, seq, D_RNN), bcast(conv_m), bcast(h_m),
                              p, TT_SCAN)
    o_p = _flash(qt_p.reshape(bn, seq // tm, -1, tm), k_p.reshape(bn, seq, -1), vt_p.reshape(bn, seq // tm, -1, tm),
                 k_meta, v_meta.T, TQ_FLASH, FLASH_HEADS)
    out_p, o_lat, prompt_ckv = _post_with_paged_attn(
        x_prompt.reshape(n_p, D_MODEL), y_p.reshape(n_p, D_RNN), o_p.reshape(n_p, -1), g_p, p, TM_POST,
        page_table, q_lat.reshape(bd, N_HEADS * ts, KV_LORA), q_pe.reshape(bd, N_HEADS * ts, ROPE_PAD),
        pad_new(ckv_s), pad_new(kpe_s), cache_ckv[0], jnp.swapaxes(cache_kpe[0], 1, 2),
        ckv_p, ckv[n_s:], bn)

    o_s = _unabsorb(o_lat.reshape(bd, N_HEADS, ts, KV_LORA), p["w_uvh"])
    y_small = jnp.concatenate([y_s.reshape(n_s, D_RNN), y_m[0]], axis=0)
    o_small = jnp.concatenate([o_s, o_m], axis=0)
    out_small = _post(x_small, y_small, o_small, g_small, p, n_small)

    meta_rows = lambda a: jnp.broadcast_to(a[n_s:][None], (bn, N_META, a.shape[-1]))
    y_prompt = out_p.reshape(bn, seq, D_MODEL)
    y_sample = out_small[:n_s].reshape(bd, ts, D_MODEL)
    prompt_kpe = jnp.concatenate([meta_rows(kpe), kpe_p.reshape(bn, seq, QK_ROPE)], axis=1)[None]
    return (y_prompt, y_sample, prompt_ckv[None], prompt_kpe, conv_p[None], h_p[None],
            ckv_s[None], kpe_s[None], conv_s[None], h_s[None])
```

```python
import functools
import math

import jax
import jax.numpy as jnp
from jax import lax
from jax.experimental import pallas as pl
from jax.experimental.pallas import tpu as pltpu

F32 = jnp.float32
BF16 = jnp.bfloat16

D_MODEL = 1024
N_META = 16
D_RNN = D_MODEL
RG_BLOCKS = 8
RG_BLOCK = D_RNN // RG_BLOCKS
CONV_W = 4
RG_C = 8.0
N_HEADS = 8
QK_NOPE = 128
QK_ROPE = 64
V_HEAD = 128
KV_LORA = 512
Q_LORA = 768
ROPE_THETA = 10000.0
D_FF = 4 * D_MODEL
DEPTH = 1
DN_ALPHA = (2.0 * DEPTH) ** 0.25
EPS = 1e-5
SM_SCALE = (QK_NOPE + QK_ROPE) ** -0.5
Q_PRESCALE = SM_SCALE * math.log2(math.e)

LANES = 128
SUBLANES = 8
HEAD_PAD = 2 * LANES
ROPE_PAD = LANES
_C_RX, _C_RG, _C_CQ, _C_CKV, _C_KPE = 0, D_RNN, 2 * D_RNN, 2 * D_RNN + Q_LORA, 2 * D_RNN + Q_LORA + KV_LORA
_C_GATE = _C_KPE + QK_ROPE
VMEM_LIMIT = 56 * 1024 * 1024
TM_IN_PROJ = 256
TM_POST = 256
TT_SCAN = 128
TQ_FLASH = 512
FLASH_HEADS = 4
FLASH_GROUP = 2
NEG = float(jnp.finfo(jnp.float32).min)


def _const_spec(shape):
    return pl.BlockSpec(shape, lambda *_: (0,) * len(shape), pipeline_mode=pl.Buffered(1))


def _layer_norm(x, g, b):
    mu = jnp.mean(x, -1, keepdims=True)
    xc = x - mu
    var = jnp.mean(xc * xc, -1, keepdims=True)
    return xc * lax.rsqrt(var + EPS) * g + b


def _rms_norm(x, g):
    return x * lax.rsqrt(jnp.mean(x * x, -1, keepdims=True) + EPS) * g


def _gelu_tanh(x):
    return 0.5 * x * (1.0 + jnp.tanh(math.sqrt(2.0 / math.pi) * (x + 0.044715 * (x * x * x))))


def _rope_block(y, cos, sin_lo, sin_hi):
    left = pltpu.roll(y, ROPE_PAD - QK_ROPE // 2, 1)
    right = pltpu.roll(y, QK_ROPE // 2, 1)
    return y * cos + left * sin_lo + right * sin_hi


def _dot(a, b):
    return jnp.dot(a, b, preferred_element_type=F32)


def _dot_nt(a, b):
    return lax.dot_general(a, b, (((1,), (1,)), ((), ())), preferred_element_type=F32)


CONV_TAIL = SUBLANES - (CONV_W - 1)


def _conv_tile(xe_ref, x, cw_ref, cb_ref):
    tt = x.shape[1]
    xe_ref[:, SUBLANES:, :] = x
    xc = cb_ref[...] + x * cw_ref[CONV_W - 1:CONV_W, :]
    for k in range(CONV_W - 1):
        xc = xc + xe_ref[:, CONV_TAIL + k:CONV_TAIL + k + tt, :] * cw_ref[k:k + 1, :]
    new_tail = xe_ref[:, tt + CONV_TAIL:tt + SUBLANES, :]
    xe_ref[:, CONV_TAIL:SUBLANES, :] = new_tail
    return xc, new_tail


def _decay_rate(lam_ref):
    z = -lam_ref[...]
    return RG_C * (jnp.maximum(z, 0.0) + jnp.log1p(jnp.exp(-jnp.abs(z))))


def _recurrence_inputs(xn, wax_n, ba_n, bx_n, rate_n):
    gates = _dot(xn.astype(BF16), wax_n)
    r = jax.nn.sigmoid(gates[:, :RG_BLOCK] + ba_n)
    ig = jax.nn.sigmoid(gates[:, RG_BLOCK:] + bx_n)
    neg_log_a = r * rate_n
    a = jnp.exp(-neg_log_a)
    u = jnp.sqrt(jnp.tanh(neg_log_a) * (a * a + 1.0)) * (ig * xn)
    return a, u


def _scan_slabs(a_ref, u_ref, hc_ref, nb, tt):
    pitch = tt + SUBLANES
    n_slab = D_RNN // LANES
    for grp in range(nb // SUBLANES):
        base = grp * SUBLANES * pitch
        rows = slice(grp * SUBLANES, (grp + 1) * SUBLANES)
        h0 = tuple(hc_ref[rows, n * LANES:(n + 1) * LANES] for n in range(n_slab))

        def step(t, hs, base=base):
            out = []
            for n in range(n_slab):
                idx = pl.ds(base + t, SUBLANES, stride=pitch)
                hn = a_ref[n, idx, :] * hs[n] + u_ref[n, idx, :]
                u_ref[n, idx, :] = hn
                out.append(hn)
            return tuple(out)

        hs = lax.fori_loop(0, tt, step, h0, unroll=min(tt, 8))
        for n in range(n_slab):
            hc_ref[rows, n * LANES:(n + 1) * LANES] = hs[n]


def _gated_branch_output(y_ref, h_ref, grg_ref, nb, tt):
    pitch = tt + SUBLANES
    for b in range(nb):
        for n in range(D_RNN // LANES):
            cs = slice(n * LANES, (n + 1) * LANES)
            y_ref[b, :, cs] = (h_ref[n, b * pitch:b * pitch + tt, :] * grg_ref[b, :, cs]).astype(BF16)


def _in_proj_kernel(x_ref, cos_ref, slo_ref, shi_ref, lng_ref, lnb_ref, w_in_ref, w_gate_ref, bg_ref,
                    qg_ref, wuq_ref, kvg_ref, wuk_ref, wuv_ref, *rest, token_minor):
    if token_minor:
        cost_ref, sint_ref, *rest = rest
    rx_ref, grg_ref, g_ref, q_ref, k_ref, v_ref, ckv_ref, kpe_ref = rest
    h = _layer_norm(x_ref[...], lng_ref[...], lnb_ref[...])
    hb = h.astype(BF16)

    def proj(lo, hi):
        return _dot(hb, w_in_ref[:, lo:hi])

    cos, slo, shi = cos_ref[...], slo_ref[...], shi_ref[...]
    cqn = _rms_norm(proj(_C_CQ, _C_CKV), qg_ref[...])
    ckv = _rms_norm(proj(_C_CKV, _C_KPE), kvg_ref[...])
    kpe_raw = proj(_C_KPE, _C_KPE + ROPE_PAD)
    lane = lax.broadcasted_iota(jnp.int32, kpe_raw.shape, 1)
    kpe = _rope_block(jnp.where(lane < QK_ROPE, kpe_raw, 0.0), cos, slo, shi)
    ckv_ref[...] = ckv
    kpe_ref[...] = kpe[:, :QK_ROPE]
    ckvb = ckv.astype(BF16)
    kpeb = kpe.astype(BF16)

    rx_ref[...] = proj(_C_RX, _C_RG)
    grg_ref[...] = _gelu_tanh(proj(_C_RG, _C_CQ))
    g_ref[...] = jax.nn.sigmoid(_dot(hb, w_gate_ref[...]) + bg_ref[...])

    cqb = cqn.astype(BF16)
    if token_minor:
        qt = _dot_nt(wuq_ref[...], cqb) * Q_PRESCALE
        cos_t, sin_t = cost_ref[...], sint_ref[...]
        half = QK_ROPE // 2
        for hh in range(N_HEADS):
            r0 = hh * HEAD_PAD
            r1, r2, r3 = r0 + QK_NOPE, r0 + QK_NOPE + half, r0 + QK_NOPE + QK_ROPE
            x1, x2 = qt[r1:r2], qt[r2:r3]
            q_ref[r0:r1, :] = qt[r0:r1].astype(BF16)
            q_ref[r1:r2, :] = (x1 * cos_t - x2 * sin_t).astype(BF16)
            q_ref[r2:r3, :] = (x1 * sin_t + x2 * cos_t).astype(BF16)
            q_ref[r3:r0 + HEAD_PAD, :] = qt[r3:r0 + HEAD_PAD].astype(BF16)
    else:
        q = _dot(cqb, wuq_ref[...]) * Q_PRESCALE
        for hh in range(N_HEADS):
            c0 = hh * HEAD_PAD
            q_ref[:, c0:c0 + QK_NOPE] = q[:, c0:c0 + QK_NOPE].astype(BF16)
            q_ref[:, c0 + QK_NOPE:c0 + HEAD_PAD] = _rope_block(q[:, c0 + QK_NOPE:c0 + HEAD_PAD],
                                                               cos, slo, shi).astype(BF16)

    kn = _dot(ckvb, wuk_ref[...])
    if token_minor:
        v_ref[...] = _dot_nt(wuv_ref[...], ckvb).astype(BF16)
    else:
        v_ref[...] = _dot(ckvb, wuv_ref[...]).astype(BF16)
    for hh in range(N_HEADS):
        c0 = hh * HEAD_PAD
        k_ref[:, c0:c0 + QK_NOPE] = kn[:, hh * QK_NOPE:(hh + 1) * QK_NOPE].astype(BF16)
        k_ref[:, c0 + QK_NOPE:c0 + HEAD_PAD] = kpeb


def _in_proj(x, tabs, tab_blocks, p, tm, tabs_t=None):
    token_minor = tabs_t is not None
    rows = x.shape[0]
    assert rows % tm == 0
    n_tiles = rows // tm
    row = lambda w: pl.BlockSpec((tm, w), lambda i: (i, 0))
    tab = pl.BlockSpec((tm, ROPE_PAD), lambda i: (i % tab_blocks, 0))
    outs = [
        (D_RNN, F32),
        (D_RNN, F32),
        (2 * D_MODEL, F32),
        (N_HEADS * HEAD_PAD, BF16),
        (N_HEADS * HEAD_PAD, BF16),
        (N_HEADS * V_HEAD, BF16),
        (KV_LORA, F32),
        (QK_ROPE, F32),
    ]
    out_specs = [row(w) for w, _ in outs]
    out_shape = [jax.ShapeDtypeStruct((rows, w), dt) for w, dt in outs]
    maybe_t = lambda shape: shape[::-1] if token_minor else shape
    in_specs = [row(D_MODEL), tab, tab, tab,
                _const_spec((1, D_MODEL)), _const_spec((1, D_MODEL)),
                _const_spec((D_MODEL, _C_GATE + 2 * D_MODEL)),
                _const_spec((D_MODEL, 2 * D_MODEL)), _const_spec((1, 2 * D_MODEL)),
                _const_spec((1, Q_LORA)), _const_spec(maybe_t((Q_LORA, N_HEADS * HEAD_PAD))),
                _const_spec((1, KV_LORA)), _const_spec((KV_LORA, N_HEADS * QK_NOPE)),
                _const_spec(maybe_t((KV_LORA, N_HEADS * V_HEAD)))]
    args = [x, *tabs, p["ln_in_g"], p["ln_in_b"], p["w_in"], p["w_gate"], p["b_gate"], p["q_norm_g"],
            p["w_uq_t"] if token_minor else p["w_uq"], p["kv_norm_g"], p["w_uk"],
            p["w_uv_t"] if token_minor else p["w_uv"]]
    if token_minor:
        for j in (3, 5):
            out_specs[j] = pl.BlockSpec((None, outs[j][0], tm), lambda i: (i, 0, 0))
            out_shape[j] = jax.ShapeDtypeStruct((n_tiles, outs[j][0], tm), BF16)
        tab_t = pl.BlockSpec((QK_ROPE // 2, tm), lambda i: (0, i % tab_blocks))
        in_specs += [tab_t, tab_t]
        args += list(tabs_t)
    return pl.pallas_call(
        functools.partial(_in_proj_kernel, token_minor=token_minor),
        grid=(n_tiles,),
        in_specs=in_specs,
        out_specs=out_specs,
        out_shape=out_shape,
        compiler_params=pltpu.CompilerParams(dimension_semantics=("arbitrary",), vmem_limit_bytes=VMEM_LIMIT),
        name="in_proj",
    )(*args)


def _rglru_kernel(rx_ref, grg_ref, cprev_ref, hprev_ref, cw_ref, cb_ref, wax_ref, ba_ref, bx_ref, lam_ref,
                  y_ref, cout_ref, hout_ref, xe_ref, a_ref, u_ref, hc_ref, *, nb, tt):
    pitch = tt + SUBLANES

    @pl.when(pl.program_id(0) == 0)
    def _():
        xe_ref[:, CONV_TAIL:SUBLANES, :] = cprev_ref[...]
        hc_ref[...] = hprev_ref[...]

    xc, new_tail = _conv_tile(xe_ref, rx_ref[...], cw_ref, cb_ref)
    cout_ref[...] = new_tail
    rate = _decay_rate(lam_ref)
    xc2 = xc.reshape(nb * tt, D_RNN)
    for n in range(D_RNN // LANES):
        cs = slice(n * LANES, (n + 1) * LANES)
        a, u = _recurrence_inputs(xc2[:, cs], wax_ref[n], ba_ref[:, cs], bx_ref[:, cs], rate[:, cs])
        for b in range(nb):
            a_ref[n, b * pitch:b * pitch + tt, :] = a[b * tt:(b + 1) * tt]
            u_ref[n, b * pitch:b * pitch + tt, :] = u[b * tt:(b + 1) * tt]
    _scan_slabs(a_ref, u_ref, hc_ref, nb, tt)
    hout_ref[...] = hc_ref[...]
    _gated_branch_output(y_ref, u_ref, grg_ref, nb, tt)


def _rglru(rx, grg, conv_prev, h_prev, p, tt):
    nb, t_len, _ = rx.shape
    assert nb % SUBLANES == 0 and t_len % tt == 0 and tt % SUBLANES == 0
    pitch = tt + SUBLANES
    seq = pl.BlockSpec((nb, tt, D_RNN), lambda i: (0, i, 0))
    return pl.pallas_call(
        functools.partial(_rglru_kernel, nb=nb, tt=tt),
        grid=(t_len // tt,),
        in_specs=[seq, seq, _const_spec((nb, CONV_W - 1, D_RNN)), _const_spec((nb, D_RNN)),
                  _const_spec((CONV_W, D_RNN)), _const_spec((1, D_RNN)),
                  _const_spec((RG_BLOCKS, RG_BLOCK, 2 * RG_BLOCK)),
                  _const_spec((1, D_RNN)), _const_spec((1, D_RNN)), _const_spec((1, D_RNN))],
        out_specs=[seq,
                   pl.BlockSpec((nb, CONV_W - 1, D_RNN), lambda i: (0, 0, 0)),
                   pl.BlockSpec((nb, D_RNN), lambda i: (0, 0))],
        out_shape=[jax.ShapeDtypeStruct((nb, t_len, D_RNN), BF16),
                   jax.ShapeDtypeStruct((nb, CONV_W - 1, D_RNN), F32),
                   jax.ShapeDtypeStruct((nb, D_RNN), F32)],
        scratch_shapes=[pltpu.VMEM((nb, tt + SUBLANES, D_RNN), F32),
                        pltpu.VMEM((D_RNN // LANES, nb * pitch, LANES), F32),
                        pltpu.VMEM((D_RNN // LANES, nb * pitch, LANES), F32),
                        pltpu.VMEM((nb, D_RNN), F32)],
        compiler_params=pltpu.CompilerParams(dimension_semantics=("arbitrary",), vmem_limit_bytes=VMEM_LIMIT),
        name="rglru",
    )(rx, grg, conv_prev, h_prev, p["conv_w"], p["conv_b"], p["w_ax"], p["rg_ba"], p["rg_bx"], p["rg_lambda"])


def _flash_kernel(qt_ref, k_ref, vt_ref, km_ref, vmt_ref, o_ref, *, tq, tv, hps):
    qcols = [slice(hh * HEAD_PAD, (hh + 1) * HEAD_PAD) for hh in range(hps)]
    vrows = [slice(hh * V_HEAD, (hh + 1) * V_HEAD) for hh in range(hps)]

    def queries_t(hh):
        return jnp.concatenate([qt_ref[c, qcols[hh], :] for c in range(tq // tv)], axis=1)

    def scores(j, hh):
        return _dot(k_ref[j * tq:(j + 1) * tq, qcols[hh]], queries_t(hh))

    def values_t(j, hh):
        return jnp.concatenate([vt_ref[j * (tq // tv) + c, vrows[hh], :] for c in range(tq // tv)], axis=1)

    def tile(n_full):
        kpos = lax.broadcasted_iota(jnp.int32, (tq, tq), 0)
        qpos = lax.broadcasted_iota(jnp.int32, (tq, tq), 1)

        def chunk_scores(j, hh):
            s_t = scores(j, hh)
            return s_t if j < n_full else jnp.where(kpos <= qpos, s_t, NEG)

        size = FLASH_GROUP if n_full + 1 > FLASH_GROUP else 1
        groups = [list(range(g0, min(g0 + size, n_full + 1))) for g0 in range(0, n_full + 1, size)]

        def group_scores(gi, hh):
            return [chunk_scores(j, hh) for j in groups[gi]]

        state = [None] * hps
        s_next = [group_scores(0, hh) for hh in range(hps)]
        for gi, group in enumerate(groups):
            for hh in range(hps):
                ss = s_next[hh]
                vs = [values_t(j, hh) for j in group]
                if gi == 0:
                    ss = [_dot(km_ref[:, qcols[hh]], queries_t(hh))] + ss
                    vs = [vmt_ref[vrows[hh], :]] + vs
                if gi + 1 < len(groups):
                    s_next[hh] = group_scores(gi + 1, hh)
                m_new = functools.reduce(jnp.maximum, [jnp.max(s_t, axis=0, keepdims=True) for s_t in ss])
                if state[hh] is not None:
                    m_new = jnp.maximum(state[hh][0], m_new)
                ps = [jnp.exp2(s_t - m_new) for s_t in ss]
                l = functools.reduce(jnp.add, [jnp.sum(p, axis=0, keepdims=True) for p in ps])
                acc_t = functools.reduce(jnp.add, [_dot(v_t, p.astype(BF16)) for v_t, p in zip(vs, ps)])
                if state[hh] is not None:
                    m, l_old, acc_old = state[hh]
                    alpha = jnp.exp2(m - m_new)
                    l, acc_t = alpha * l_old + l, alpha * acc_old + acc_t
                state[hh] = (m_new, l, acc_t)
        for hh in range(hps):
            _, l, acc_t = state[hh]
            o_ref[:, vrows[hh]] = jnp.transpose(acc_t / l).astype(BF16)

    for c in range(k_ref.shape[0] // tq):
        pl.when(pl.program_id(2) == c)(functools.partial(tile, c))


def _flash(q_t, k, v_t, k_pre, v_pre_t, tq, hps):
    nb, t_len, _ = k.shape
    tv = v_t.shape[3]
    assert t_len % tq == 0 and tq % tv == 0 and N_HEADS % hps == 0
    return pl.pallas_call(
        functools.partial(_flash_kernel, tq=tq, tv=tv, hps=hps),
        grid=(nb, N_HEADS // hps, t_len // tq),
        in_specs=[pl.BlockSpec((None, tq // tv, hps * HEAD_PAD, tv), lambda b, h, i: (b, i, h, 0)),
                  pl.BlockSpec((None, t_len, hps * HEAD_PAD), lambda b, h, i: (b, 0, h)),
                  pl.BlockSpec((None, t_len // tv, hps * V_HEAD, tv), lambda b, h, i: (b, 0, h, 0)),
                  pl.BlockSpec((N_META, hps * HEAD_PAD), lambda b, h, i: (0, h)),
                  pl.BlockSpec((hps * V_HEAD, N_META), lambda b, h, i: (h, 0))],
        out_specs=pl.BlockSpec((None, tq, hps * V_HEAD), lambda b, h, i: (b, i, h)),
        out_shape=jax.ShapeDtypeStruct((nb, t_len, N_HEADS * V_HEAD), BF16),
        compiler_params=pltpu.CompilerParams(dimension_semantics=("arbitrary",) * 3, vmem_limit_bytes=VMEM_LIMIT),
        name="flash_prompt",
    )(q_t, k, v_t, k_pre, v_pre_t)


def _meta_attn_kernel(q_ref, k_ref, v_ref, o_ref):
    qpos = lax.broadcasted_iota(jnp.int32, (N_META, N_META), 0)
    kpos = lax.broadcasted_iota(jnp.int32, (N_META, N_META), 1)
    for hh in range(N_HEADS):
        qs = slice(hh * HEAD_PAD, (hh + 1) * HEAD_PAD)
        vs = slice(hh * V_HEAD, (hh + 1) * V_HEAD)
        s = jnp.where(kpos <= qpos, _dot_nt(q_ref[:, qs], k_ref[:, qs]), NEG)
        pr = jnp.exp2(s - jnp.max(s, axis=1, keepdims=True))
        o = _dot(pr.astype(BF16), v_ref[:, vs]) / jnp.sum(pr, axis=1, keepdims=True)
        o_ref[:, vs] = o.astype(BF16)


def _meta_attn(q, k, v):
    return pl.pallas_call(
        _meta_attn_kernel,
        out_shape=jax.ShapeDtypeStruct((N_META, N_HEADS * V_HEAD), BF16),
        name="meta_attn",
    )(q, k, v)


def _absorb_kernel(q_ref, wukt_ref, qlat_ref, qpe_ref):
    for hh in range(N_HEADS):
        q = q_ref[:, hh * HEAD_PAD:(hh + 1) * HEAD_PAD]
        qlat_ref[hh] = _dot(q[:, :QK_NOPE], wukt_ref[hh])
        qpe_ref[hh] = q[:, QK_NOPE:].astype(F32)


def _absorb(q, w_ukt):
    rows = q.shape[0]
    return pl.pallas_call(
        _absorb_kernel,
        out_shape=[jax.ShapeDtypeStruct((N_HEADS, rows, KV_LORA), F32),
                   jax.ShapeDtypeStruct((N_HEADS, rows, ROPE_PAD), F32)],
        name="absorb_q",
    )(q, w_ukt)


PAGED_BUFS = 2


def _paged_scratch(n_q, pps, page):
    return [pltpu.VMEM((n_q, 1), F32), pltpu.VMEM((n_q, 1), F32), pltpu.VMEM((n_q, KV_LORA), F32),
            pltpu.VMEM((PAGED_BUFS, pps * page, KV_LORA), F32), pltpu.VMEM((PAGED_BUFS, pps, QK_ROPE, page), F32),
            pltpu.VMEM((pps * page, KV_LORA), BF16), pltpu.VMEM((QK_ROPE, pps * page), BF16),
            pltpu.SemaphoreType.DMA((PAGED_BUFS,))]


def _paged_copies(pt_ref, ckv_hbm, kpe_hbm, cin_ref, kin_ref, sem, seq, chunk, slot, pps):
    page = kin_ref.shape[3]
    out = []
    for i in range(pps):
        pid = pt_ref[seq, chunk * pps + i]
        out.append(pltpu.make_async_copy(ckv_hbm.at[pid], cin_ref.at[slot, pl.ds(i * page, page)], sem.at[slot]))
        out.append(pltpu.make_async_copy(kpe_hbm.at[pid], kin_ref.at[slot, i], sem.at[slot]))
    return out


def _paged_new_tokens(qlat, qpe, cnew_ref, knew_ref, m_ref, l_ref, acc_ref, ts):
    cn = cnew_ref[...].astype(BF16)
    kn = knew_ref[...].astype(BF16)
    s = _dot_nt(qlat, cn) + _dot_nt(qpe, kn)
    t_q = lax.broadcasted_iota(jnp.int32, s.shape, 0) % ts
    t_k = lax.broadcasted_iota(jnp.int32, s.shape, 1)
    s = jnp.where(t_k <= t_q, s, NEG)
    m = jnp.max(s, axis=1, keepdims=True)
    pr = jnp.exp2(s - m)
    m_ref[...] = m
    l_ref[...] = jnp.sum(pr, axis=1, keepdims=True)
    acc_ref[...] = _dot(pr.astype(BF16), cn)


def _paged_scores(qlat, qpe, cin_ref, kin_ref, cbuf_ref, kbuf_ref, slot, pps):
    page = kin_ref.shape[3]
    for i in range(pps):
        rows = pl.ds(i * page, page)
        cbuf_ref[rows, :] = cin_ref[slot, rows, :].astype(BF16)
        kbuf_ref[:, i * page:(i + 1) * page] = kin_ref[slot, i].astype(BF16)
    return _dot_nt(qlat, cbuf_ref[...]) + _dot(qpe, kbuf_ref[...])


def _paged_probs(s, m_ref, l_ref):
    m = m_ref[...]
    m_new = jnp.maximum(m, jnp.max(s, axis=1, keepdims=True))
    alpha = jnp.exp2(m - m_new)
    pr = jnp.exp2(s - m_new)
    m_ref[...] = m_new
    l_ref[...] = alpha * l_ref[...] + jnp.sum(pr, axis=1, keepdims=True)
    return alpha, pr.astype(BF16)


def _paged_accumulate(alpha, pr, cbuf_ref, acc_ref):
    acc_ref[...] = alpha * acc_ref[...] + _dot(pr, cbuf_ref[...])


def _unabsorb_kernel(olat_ref, wuv_ref, o_ref):
    for hh in range(N_HEADS):
        o_ref[:, hh * V_HEAD:(hh + 1) * V_HEAD] = _dot(olat_ref[hh].astype(BF16), wuv_ref[hh]).astype(BF16)


def _unabsorb(o_lat, w_uvh):
    return pl.pallas_call(
        _unabsorb_kernel,
        out_shape=jax.ShapeDtypeStruct((o_lat.shape[1], N_HEADS * V_HEAD), BF16),
        name="unabsorb_o",
    )(o_lat, w_uvh)


FF_CHUNKS = 4
N_POST_WEIGHTS = 13


def _post_compute(x_ref, y_ref, o_ref, g_ref, weights, out_ref, side_work=None):
    before_up, before_down, after_down = side_work if side_work is not None else (lambda c: None,) * 3
    lng_ref, lnb_ref, wr_ref, wm_ref, wo_ref, g1_ref, b1_ref, wup_ref, bup_ref, wdn_ref, bdn_ref, g2_ref, b2_ref = weights
    h = _layer_norm(x_ref[...], lng_ref[...], lnb_ref[...])
    z_r = _dot(y_ref[...], wr_ref[...])
    z_m = _dot(o_ref[...], wm_ref[...])
    mix_in = g_ref[:, :D_MODEL] * z_r + g_ref[:, D_MODEL:] * z_m
    mix = _dot(mix_in.astype(BF16), wo_ref[...])
    before_up(0)
    x1 = _layer_norm(DN_ALPHA * h + mix, g1_ref[...], b1_ref[...])
    x1b = x1.astype(BF16)
    f = bdn_ref[...]
    ff_chunk = D_FF // FF_CHUNKS
    for c in range(FF_CHUNKS):
        cs = slice(c * ff_chunk, (c + 1) * ff_chunk)
        if c > 0:
            before_up(c)
        up = jnp.maximum(_dot(x1b, wup_ref[:, cs]) + bup_ref[:, cs], 0.0)
        before_down(c)
        f = f + _dot((up * up).astype(BF16), wdn_ref[cs, :])
        after_down(c)
    out_ref[...] = _layer_norm(DN_ALPHA * x1 + f, g2_ref[...], b2_ref[...])


def _post_kernel(x_ref, y_ref, o_ref, g_ref, *rest):
    _post_compute(x_ref, y_ref, o_ref, g_ref, rest[:N_POST_WEIGHTS], rest[N_POST_WEIGHTS])


def _post_paged_kernel(pt_ref, x_ref, y_ref, o_ref, g_ref, *rest, pps, ts, n_j, n_steps, tiles_per_seq):
    weights = rest[:N_POST_WEIGHTS]
    (qlat_ref, qpe_ref, cnew_ref, knew_ref, lat_ref, latm_ref, ckv_hbm, kpe_hbm, out_ref, olat_ref, platent_hbm,
     m_ref, l_ref, acc_ref, cin_ref, kin_ref, cbuf_ref, kbuf_ref, sem, lsem) = rest[N_POST_WEIGHTS:]
    s = pl.program_id(0)
    spb = n_j // FF_CHUNKS
    seq, j0 = lax.div(s, spb), lax.rem(s, spb) * FF_CHUNKS
    copies = functools.partial(_paged_copies, pt_ref, ckv_hbm, kpe_hbm, cin_ref, kin_ref, sem, pps=pps)

    tm = out_ref.shape[0]
    lat_rows = pltpu.make_async_copy(
        lat_ref,
        platent_hbm.at[lax.div(s, tiles_per_seq), pl.ds(N_META + lax.rem(s, tiles_per_seq) * tm, tm)], lsem.at[0])
    lat_meta = [pltpu.make_async_copy(latm_ref, platent_hbm.at[b, pl.ds(0, N_META)], lsem.at[1])
                for b in range(platent_hbm.shape[0])]
    lat_rows.start()

    @pl.when(s == 0)
    def _():
        for cp in lat_meta:
            cp.start()
        for cp in copies(0, 0, 0):
            cp.start()

    live = {}

    def gather_and_score(c):
        slot, nxt = c % PAGED_BUFS, (c + 1) % PAGED_BUFS
        if c + 1 < FF_CHUNKS:
            for cp in copies(seq, j0 + c + 1, nxt):
                cp.start()
        else:
            @pl.when(s + 1 < n_steps)
            def _():
                for cp in copies(lax.div(s + 1, spb), lax.rem(s + 1, spb) * FF_CHUNKS, nxt):
                    cp.start()
        for cp in copies(seq, j0 + c, slot):
            cp.wait()
        n_q = acc_ref.shape[0]
        qlat = qlat_ref[...].reshape(n_q, KV_LORA).astype(BF16)
        qpe = qpe_ref[...].reshape(n_q, ROPE_PAD)[:, :QK_ROPE].astype(BF16)
        if c == 0:
            pl.when(j0 == 0)(functools.partial(_paged_new_tokens, qlat, qpe, cnew_ref, knew_ref,
                                               m_ref, l_ref, acc_ref, ts))
        live["s"] = _paged_scores(qlat, qpe, cin_ref, kin_ref, cbuf_ref, kbuf_ref, slot, pps)

    def probs(c):
        live["alpha"], live["pr"] = _paged_probs(live.pop("s"), m_ref, l_ref)

    def accumulate(c):
        _paged_accumulate(live.pop("alpha"), live.pop("pr"), cbuf_ref, acc_ref)
        if c == FF_CHUNKS - 1:
            @pl.when(j0 == n_j - FF_CHUNKS)
            def _():
                olat_ref[...] = (acc_ref[...] / l_ref[...]).reshape(olat_ref.shape)

    _post_compute(x_ref, y_ref, o_ref, g_ref, weights, out_ref, side_work=(gather_and_score, probs, accumulate))

    lat_rows.wait()

    @pl.when(s == 0)
    def _():
        for cp in lat_meta:
            cp.wait()


def _post_weight_specs():
    vec = lambda w: _const_spec((1, w))
    sq = _const_spec((D_MODEL, D_MODEL))
    return [vec(D_MODEL), vec(D_MODEL), sq, sq, sq, vec(D_MODEL), vec(D_MODEL),
            _const_spec((D_MODEL, D_FF)), vec(D_FF), _const_spec((D_FF, D_MODEL)), vec(D_MODEL),
            vec(D_MODEL), vec(D_MODEL)]


def _post_weights(p):
    return (p["ln_in_g"], p["ln_in_b"], p["w_br_r"], p["w_br_m"], p["w_o"], p["ln1_g"], p["ln1_b"],
            p["w_up"], p["b_up"], p["w_down"], p["b_down"], p["ln2_g"], p["ln2_b"])


def _post(x, y, o, g, p, tm):
    rows = x.shape[0]
    assert rows % tm == 0
    row = lambda w: pl.BlockSpec((tm, w), lambda i: (i, 0))
    return pl.pallas_call(
        _post_kernel,
        grid=(rows // tm,),
        in_specs=[row(D_MODEL), row(D_RNN), row(N_HEADS * V_HEAD), row(2 * D_MODEL)] + _post_weight_specs(),
        out_specs=row(D_MODEL),
        out_shape=jax.ShapeDtypeStruct((rows, D_MODEL), F32),
        compiler_params=pltpu.CompilerParams(dimension_semantics=("arbitrary",), vmem_limit_bytes=VMEM_LIMIT),
        name="merge_mlp",
    )(x, y, o, g, *_post_weights(p))


def _post_with_paged_attn(x, y, o, g, p, tm, page_table, q_lat, q_pe, ckv_new, kpe_new, ckv_pool, kpe_pool_t,
                          latent, latent_meta, n_seq):
    rows = x.shape[0]
    n_steps = rows // tm
    assert n_steps % n_seq == 0
    nb, n_pages = page_table.shape
    ts = q_lat.shape[2]
    n_q = N_HEADS * ts
    page = ckv_pool.shape[1]
    n_new = ckv_new.shape[1]
    chunks = n_steps * FF_CHUNKS
    assert rows % tm == 0 and (nb * n_pages) % chunks == 0 and FF_CHUNKS % PAGED_BUFS == 0
    pps = nb * n_pages // chunks
    n_j = n_pages // pps
    assert n_pages % pps == 0 and n_j % FF_CHUNKS == 0
    spb = n_j // FF_CHUNKS
    row = lambda w: pl.BlockSpec((tm, w), lambda i, pt: (i, 0))
    per_seq = lambda r, w: pl.BlockSpec((None, r, w), lambda i, pt: (i // spb, 0, 0))
    heads_of_seq = lambda w: pl.BlockSpec((N_HEADS, None, ts, w), lambda i, pt: (0, i // spb, 0, 0))
    grid_spec = pltpu.PrefetchScalarGridSpec(
        num_scalar_prefetch=1,
        grid=(n_steps,),
        in_specs=[row(D_MODEL), row(D_RNN), row(N_HEADS * V_HEAD), row(2 * D_MODEL)] + _post_weight_specs()
                 + [heads_of_seq(KV_LORA), heads_of_seq(ROPE_PAD), per_seq(n_new, KV_LORA), per_seq(n_new, QK_ROPE),
                    row(KV_LORA), _const_spec((N_META, KV_LORA))]
                 + [pl.BlockSpec(memory_space=pl.ANY)] * 2,
        out_specs=[row(D_MODEL), heads_of_seq(KV_LORA), pl.BlockSpec(memory_space=pl.ANY)],
        scratch_shapes=_paged_scratch(n_q, pps, page) + [pltpu.SemaphoreType.DMA((2,))],
    )
    return pl.pallas_call(
        functools.partial(_post_paged_kernel, pps=pps, ts=ts, n_j=n_j, n_steps=n_steps,
                          tiles_per_seq=n_steps // n_seq),
        grid_spec=grid_spec,
        out_shape=[jax.ShapeDtypeStruct((rows, D_MODEL), F32),
                   jax.ShapeDtypeStruct((N_HEADS, nb, ts, KV_LORA), F32),
                   jax.ShapeDtypeStruct((n_seq, N_META + rows // n_seq, KV_LORA), F32)],
        compiler_params=pltpu.CompilerParams(dimension_semantics=("arbitrary",), vmem_limit_bytes=VMEM_LIMIT),
        name="merge_mlp_paged_attn",
    )(page_table, x, y, o, g, *_post_weights(p), q_lat, q_pe, ckv_new, kpe_new, latent, latent_meta,
      ckv_pool, kpe_pool_t)


def _rope_tables(pos):
    half = QK_ROPE // 2
    inv = 1.0 / (ROPE_THETA ** (jnp.arange(half, dtype=F32) / half))
    ang = pos.astype(F32)[:, None] * inv[None, :]
    cos, sin = jnp.cos(ang), jnp.sin(ang)
    z = jnp.zeros_like(cos)
    return ((jnp.concatenate([cos, cos, z, z], 1), jnp.concatenate([-sin, z, z, z], 1),
             jnp.concatenate([z, sin, z, z], 1)), (cos.T, sin.T))


def _prep_params(w_in, b_gate, conv_w, conv_b, rg_wa, rg_ba, rg_wx, rg_bx, rg_lambda, w_br_r, q_norm_g, w_uq,
                 kv_norm_g, w_uk, w_uv, w_br_m, w_o, ln1_g, ln1_b, w_up, b_up, w_down, b_down, ln2_g, ln2_b,
                 ln_in_g, ln_in_b):
    l = 0
    vec = lambda a: a.reshape(1, -1).astype(F32)
    wi = w_in[l].astype(BF16)
    w_uq_p = jnp.pad(w_uq[l], ((0, 0), (0, 0), (0, HEAD_PAD - QK_NOPE - QK_ROPE))).reshape(Q_LORA, N_HEADS * HEAD_PAD)
    return {
        "ln_in_g": vec(ln_in_g), "ln_in_b": vec(ln_in_b),
        "w_in": wi, "w_gate": wi[:, _C_GATE:], "b_gate": vec(b_gate[l]),
        "q_norm_g": vec(q_norm_g[l]), "w_uq": w_uq_p.astype(BF16), "w_uq_t": w_uq_p.T.astype(BF16),
        "kv_norm_g": vec(kv_norm_g[l]),
        "w_uk": w_uk[l].reshape(KV_LORA, N_HEADS * QK_NOPE).astype(BF16),
        "w_uv": w_uv[l].reshape(KV_LORA, N_HEADS * V_HEAD).astype(BF16),
        "w_uv_t": w_uv[l].reshape(KV_LORA, N_HEADS * V_HEAD).T.astype(BF16),
        "w_ukt": jnp.transpose(w_uk[l], (1, 2, 0)).astype(BF16),
        "w_uvh": jnp.transpose(w_uv[l], (1, 0, 2)).astype(BF16),
        "conv_w": conv_w[l].astype(F32), "conv_b": vec(conv_b[l]),
        "w_ax": jnp.concatenate([rg_wa[l], rg_wx[l]], axis=-1).astype(BF16),
        "rg_ba": vec(rg_ba[l]), "rg_bx": vec(rg_bx[l]), "rg_lambda": vec(rg_lambda[l]),
        "w_br_r": w_br_r[l].astype(BF16), "w_br_m": w_br_m[l].astype(BF16), "w_o": w_o[l].astype(BF16),
        "ln1_g": vec(ln1_g[l]), "ln1_b": vec(ln1_b[l]),
        "w_up": w_up[l].astype(BF16), "b_up": vec(b_up[l]),
        "w_down": w_down[l].astype(BF16), "b_down": vec(b_down[l]),
        "ln2_g": vec(ln2_g[l]), "ln2_b": vec(ln2_b[l]),
    }


def kernel(x_prompt, x_sample, cache_ckv, cache_kpe, page_table, state_conv, state_rglru, meta_tokens, ln_in_g, ln_in_b, w_in, b_gate, conv_w, conv_b, rg_wa, rg_ba, rg_wx, rg_bx, rg_lambda, w_br_r, q_norm_g, w_uq, kv_norm_g, w_uk, w_uv, w_br_m, w_o, ln1_g, ln1_b, w_up, b_up, w_down, b_down, ln2_g, ln2_b):
    assert w_in.shape[0] == DEPTH == 1
    bn, seq, _ = x_prompt.shape
    bd, ts, _ = x_sample.shape
    past_len = page_table.shape[1] * cache_ckv.shape[2]
    p = _prep_params(w_in, b_gate, conv_w, conv_b, rg_wa, rg_ba, rg_wx, rg_bx, rg_lambda, w_br_r, q_norm_g, w_uq,
                     kv_norm_g, w_uk, w_uv, w_br_m, w_o, ln1_g, ln1_b, w_up, b_up, w_down, b_down, ln2_g, ln2_b,
                     ln_in_g, ln_in_b)
    n_s = bd * ts

    x_small = jnp.concatenate([x_sample.reshape(n_s, D_MODEL), meta_tokens.astype(F32)], axis=0)
    pos_small = jnp.concatenate([jnp.tile(past_len + jnp.arange(ts), bd), jnp.arange(N_META)])
    n_small = n_s + N_META
    rx, grg, g_small, q, k, v, ckv, kpe = _in_proj(x_small, _rope_tables(pos_small)[0], 1, p, n_small)

    rep = lambda a: jnp.broadcast_to(a[None], (SUBLANES,) + a.shape)
    y_m, conv_m, h_m = _rglru(rep(rx[n_s:]), rep(grg[n_s:]), jnp.zeros((SUBLANES, CONV_W - 1, D_RNN), F32),
                              jnp.zeros((SUBLANES, D_RNN), F32), p, N_META)
    k_meta, v_meta = k[n_s:], v[n_s:]
    o_m = _meta_attn(q[n_s:], k_meta, v_meta)

    y_s, conv_s, h_s = _rglru(rx[:n_s].reshape(bd, ts, D_RNN), grg[:n_s].reshape(bd, ts, D_RNN),
                              state_conv[0], state_rglru[0], p, ts)
    q_lat, q_pe = _absorb(q[:n_s], p["w_ukt"])
    ckv_s = ckv[:n_s].reshape(bd, ts, KV_LORA)
    kpe_s = kpe[:n_s].reshape(bd, ts, QK_ROPE)
    pad_new = lambda a: jnp.pad(a, ((0, 0), (0, 2 * SUBLANES - ts), (0, 0)))

    n_p = bn * seq
    pos_p = N_META + jnp.arange(seq)
    tm = TM_IN_PROJ
    bcast = lambda a: jnp.broadcast_to(a[:1], (bn,) + a.shape[1:])
    tabs_p, tabs_pt = _rope_tables(pos_p)
    rx_p, grg_p, g_p, qt_p, k_p, vt_p, ckv_p, kpe_p = _in_proj(x_prompt.reshape(n_p, D_MODEL), tabs_p,
                                                               seq // tm, p, tm, tabs_t=tabs_pt)
    y_p, conv_p, h_p = _rglru(rx_p.reshape(bn, seq, D_RNN), grg_p.reshape(bn, seq, D_RNN), bcast(conv_m), bcast(h_m),
                              p, TT_SCAN)
    o_p = _flash(qt_p.reshape(bn, seq // tm, -1, tm), k_p.reshape(bn, seq, -1), vt_p.reshape(bn, seq // tm, -1, tm),
                 k_meta, v_meta.T, TQ_FLASH, FLASH_HEADS)
    out_p, o_lat, prompt_ckv = _post_with_paged_attn(
        x_prompt.reshape(n_p, D_MODEL), y_p.reshape(n_p, D_RNN), o_p.reshape(n_p, -1), g_p, p, TM_POST,
        page_table, q_lat.reshape(N_HEADS, bd, ts, KV_LORA), q_pe.reshape(N_HEADS, bd, ts, ROPE_PAD),
        pad_new(ckv_s), pad_new(kpe_s), cache_ckv[0], jnp.swapaxes(cache_kpe[0], 1, 2),
        ckv_p, ckv[n_s:], bn)

    o_s = _unabsorb(o_lat.reshape(N_HEADS, n_s, KV_LORA), p["w_uvh"])
    y_small = jnp.concatenate([y_s.reshape(n_s, D_RNN), y_m[0]], axis=0)
    o_small = jnp.concatenate([o_s, o_m], axis=0)
    out_small = _post(x_small, y_small, o_small, g_small, p, n_small)

    meta_rows = lambda a: jnp.broadcast_to(a[n_s:][None], (bn, N_META, a.shape[-1]))
    y_prompt = out_p.reshape(bn, seq, D_MODEL)
    y_sample = out_small[:n_s].reshape(bd, ts, D_MODEL)
    prompt_kpe = jnp.concatenate([meta_rows(kpe), kpe_p.reshape(bn, seq, QK_ROPE)], axis=1)[None]
    return (y_prompt, y_sample, prompt_ckv[None], prompt_kpe, conv_p[None], h_p[None],
            ckv_s[None], kpe_s[None], conv_s[None], h_s[None])
```

```python
import functools
import math

import jax
import jax.numpy as jnp
from jax import lax
from jax.experimental import pallas as pl
from jax.experimental.pallas import tpu as pltpu

F32 = jnp.float32
BF16 = jnp.bfloat16

D_MODEL = 1024
N_META = 16
D_RNN = D_MODEL
RG_BLOCKS = 8
RG_BLOCK = D_RNN // RG_BLOCKS
CONV_W = 4
RG_C = 8.0
N_HEADS = 8
QK_NOPE = 128
QK_ROPE = 64
V_HEAD = 128
KV_LORA = 512
Q_LORA = 768
ROPE_THETA = 10000.0
D_FF = 4 * D_MODEL
DEPTH = 1
DN_ALPHA = (2.0 * DEPTH) ** 0.25
EPS = 1e-5
SM_SCALE = (QK_NOPE + QK_ROPE) ** -0.5
Q_PRESCALE = SM_SCALE * math.log2(math.e)

LANES = 128
SUBLANES = 8
HEAD_PAD = 2 * LANES
ROPE_PAD = LANES
_C_RX, _C_RG, _C_CQ, _C_CKV, _C_KPE = 0, D_RNN, 2 * D_RNN, 2 * D_RNN + Q_LORA, 2 * D_RNN + Q_LORA + KV_LORA
_C_GATE = _C_KPE + QK_ROPE
VMEM_LIMIT = 56 * 1024 * 1024
TM_IN_PROJ = 256
TM_POST = 256
TT_SCAN = 128
TQ_FLASH = 512
FLASH_HEADS = 4
FLASH_GROUP = 2
NEG = float(jnp.finfo(jnp.float32).min)


def _const_spec(shape):
    return pl.BlockSpec(shape, lambda *_: (0,) * len(shape), pipeline_mode=pl.Buffered(1))


def _layer_norm(x, g, b):
    mu = jnp.mean(x, -1, keepdims=True)
    xc = x - mu
    var = jnp.mean(xc * xc, -1, keepdims=True)
    return xc * lax.rsqrt(var + EPS) * g + b


def _rms_norm(x, g):
    return x * lax.rsqrt(jnp.mean(x * x, -1, keepdims=True) + EPS) * g


def _gelu_tanh(x):
    return 0.5 * x * (1.0 + jnp.tanh(math.sqrt(2.0 / math.pi) * (x + 0.044715 * (x * x * x))))


def _rope_block(y, cos, sin_lo, sin_hi):
    left = pltpu.roll(y, ROPE_PAD - QK_ROPE // 2, 1)
    right = pltpu.roll(y, QK_ROPE // 2, 1)
    return y * cos + left * sin_lo + right * sin_hi


def _dot(a, b):
    return jnp.dot(a, b, preferred_element_type=F32)


def _dot_nt(a, b):
    return lax.dot_general(a, b, (((1,), (1,)), ((), ())), preferred_element_type=F32)


CONV_TAIL = SUBLANES - (CONV_W - 1)


def _conv_tile(xe_ref, x, cw_ref, cb_ref):
    tt = x.shape[1]
    xe_ref[:, SUBLANES:, :] = x
    xc = cb_ref[...] + x * cw_ref[CONV_W - 1:CONV_W, :]
    for k in range(CONV_W - 1):
        xc = xc + xe_ref[:, CONV_TAIL + k:CONV_TAIL + k + tt, :] * cw_ref[k:k + 1, :]
    new_tail = xe_ref[:, tt + CONV_TAIL:tt + SUBLANES, :]
    xe_ref[:, CONV_TAIL:SUBLANES, :] = new_tail
    return xc, new_tail


def _decay_rate(lam_ref):
    z = -lam_ref[...]
    return RG_C * (jnp.maximum(z, 0.0) + jnp.log1p(jnp.exp(-jnp.abs(z))))


def _recurrence_inputs(xn, wax_n, ba_n, bx_n, rate_n):
    gates = _dot(xn.astype(BF16), wax_n)
    r = jax.nn.sigmoid(gates[:, :RG_BLOCK] + ba_n)
    ig = jax.nn.sigmoid(gates[:, RG_BLOCK:] + bx_n)
    neg_log_a = r * rate_n
    a = jnp.exp(-neg_log_a)
    u = jnp.sqrt(jnp.tanh(neg_log_a) * (a * a + 1.0)) * (ig * xn)
    return a, u


def _scan_slabs(a_ref, u_ref, hc_ref, nb, tt):
    pitch = tt + SUBLANES
    n_slab = D_RNN // LANES
    for grp in range(nb // SUBLANES):
        base = grp * SUBLANES * pitch
        rows = slice(grp * SUBLANES, (grp + 1) * SUBLANES)
        h0 = tuple(hc_ref[rows, n * LANES:(n + 1) * LANES] for n in range(n_slab))

        def step(t, hs, base=base):
            out = []
            for n in range(n_slab):
                idx = pl.ds(base + t, SUBLANES, stride=pitch)
                hn = a_ref[n, idx, :] * hs[n] + u_ref[n, idx, :]
                u_ref[n, idx, :] = hn
                out.append(hn)
            return tuple(out)

        hs = lax.fori_loop(0, tt, step, h0, unroll=min(tt, 8))
        for n in range(n_slab):
            hc_ref[rows, n * LANES:(n + 1) * LANES] = hs[n]


def _gated_branch_output(y_ref, h_ref, grg_ref, nb, tt):
    pitch = tt + SUBLANES
    for b in range(nb):
        for n in range(D_RNN // LANES):
            cs = slice(n * LANES, (n + 1) * LANES)
            y_ref[b, :, cs] = (h_ref[n, b * pitch:b * pitch + tt, :] * grg_ref[b, :, cs]).astype(BF16)


def _in_proj_kernel(x_ref, cos_ref, slo_ref, shi_ref, lng_ref, lnb_ref, w_in_ref, w_gate_ref, bg_ref,
                    qg_ref, wuq_ref, kvg_ref, wuk_ref, wuv_ref, *rest, token_minor):
    if token_minor:
        cost_ref, sint_ref, *rest = rest
    rx_ref, grg_ref, g_ref, q_ref, k_ref, v_ref, ckv_ref, kpe_ref = rest
    tm = x_ref.shape[0]
    half = tm // 2 if tm % (4 * SUBLANES) == 0 else tm
    halves = [_layer_norm(x_ref[r0:r0 + half, :], lng_ref[...], lnb_ref[...]).astype(BF16)
              for r0 in range(0, tm, half)]

    def proj_early(lo, hi):
        return jnp.concatenate([_dot(hb_half, w_in_ref[:, lo:hi]) for hb_half in halves], axis=0)

    hb = jnp.concatenate(halves, axis=0)

    def proj(lo, hi):
        return _dot(hb, w_in_ref[:, lo:hi])

    cos, slo, shi = cos_ref[...], slo_ref[...], shi_ref[...]
    cqn = _rms_norm(proj_early(_C_CQ, _C_CKV), qg_ref[...])
    ckv = _rms_norm(proj_early(_C_CKV, _C_KPE), kvg_ref[...])
    kpe_raw = proj_early(_C_KPE, _C_KPE + ROPE_PAD)
    lane = lax.broadcasted_iota(jnp.int32, kpe_raw.shape, 1)
    kpe = _rope_block(jnp.where(lane < QK_ROPE, kpe_raw, 0.0), cos, slo, shi)
    ckv_ref[...] = ckv
    kpe_ref[...] = kpe[:, :QK_ROPE]
    ckvb = ckv.astype(BF16)
    kpeb = kpe.astype(BF16)

    rx_ref[...] = proj(_C_RX, _C_RG)
    grg_ref[...] = _gelu_tanh(proj(_C_RG, _C_CQ))
    g_ref[...] = jax.nn.sigmoid(_dot(hb, w_gate_ref[...]) + bg_ref[...])

    cqb = cqn.astype(BF16)
    if token_minor:
        qt = _dot_nt(wuq_ref[...], cqb) * Q_PRESCALE
        cos_t, sin_t = cost_ref[...], sint_ref[...]
        half = QK_ROPE // 2
        for hh in range(N_HEADS):
            r0 = hh * HEAD_PAD
            r1, r2, r3 = r0 + QK_NOPE, r0 + QK_NOPE + half, r0 + QK_NOPE + QK_ROPE
            x1, x2 = qt[r1:r2], qt[r2:r3]
            q_ref[r0:r1, :] = qt[r0:r1].astype(BF16)
            q_ref[r1:r2, :] = (x1 * cos_t - x2 * sin_t).astype(BF16)
            q_ref[r2:r3, :] = (x1 * sin_t + x2 * cos_t).astype(BF16)
            q_ref[r3:r0 + HEAD_PAD, :] = qt[r3:r0 + HEAD_PAD].astype(BF16)
    else:
        q = _dot(cqb, wuq_ref[...]) * Q_PRESCALE
        for hh in range(N_HEADS):
            c0 = hh * HEAD_PAD
            q_ref[:, c0:c0 + QK_NOPE] = q[:, c0:c0 + QK_NOPE].astype(BF16)
            q_ref[:, c0 + QK_NOPE:c0 + HEAD_PAD] = _rope_block(q[:, c0 + QK_NOPE:c0 + HEAD_PAD],
                                                               cos, slo, shi).astype(BF16)

    kn = _dot(ckvb, wuk_ref[...])
    if token_minor:
        v_ref[...] = _dot_nt(wuv_ref[...], ckvb).astype(BF16)
    else:
        v_ref[...] = _dot(ckvb, wuv_ref[...]).astype(BF16)
    for hh in range(N_HEADS):
        c0 = hh * HEAD_PAD
        k_ref[:, c0:c0 + QK_NOPE] = kn[:, hh * QK_NOPE:(hh + 1) * QK_NOPE].astype(BF16)
        k_ref[:, c0 + QK_NOPE:c0 + HEAD_PAD] = kpeb


def _in_proj(x, tabs, tab_blocks, p, tm, tabs_t=None):
    token_minor = tabs_t is not None
    rows = x.shape[0]
    assert rows % tm == 0
    n_tiles = rows // tm
    row = lambda w: pl.BlockSpec((tm, w), lambda i: (i, 0))
    tab = pl.BlockSpec((tm, ROPE_PAD), lambda i: (i % tab_blocks, 0))
    outs = [
        (D_RNN, F32),
        (D_RNN, F32),
        (2 * D_MODEL, F32),
        (N_HEADS * HEAD_PAD, BF16),
        (N_HEADS * HEAD_PAD, BF16),
        (N_HEADS * V_HEAD, BF16),
        (KV_LORA, F32),
        (QK_ROPE, F32),
    ]
    out_specs = [row(w) for w, _ in outs]
    out_shape = [jax.ShapeDtypeStruct((rows, w), dt) for w, dt in outs]
    maybe_t = lambda shape: shape[::-1] if token_minor else shape
    in_specs = [row(D_MODEL), tab, tab, tab,
                _const_spec((1, D_MODEL)), _const_spec((1, D_MODEL)),
                _const_spec((D_MODEL, _C_GATE + 2 * D_MODEL)),
                _const_spec((D_MODEL, 2 * D_MODEL)), _const_spec((1, 2 * D_MODEL)),
                _const_spec((1, Q_LORA)), _const_spec(maybe_t((Q_LORA, N_HEADS * HEAD_PAD))),
                _const_spec((1, KV_LORA)), _const_spec((KV_LORA, N_HEADS * QK_NOPE)),
                _const_spec(maybe_t((KV_LORA, N_HEADS * V_HEAD)))]
    args = [x, *tabs, p["ln_in_g"], p["ln_in_b"], p["w_in"], p["w_gate"], p["b_gate"], p["q_norm_g"],
            p["w_uq_t"] if token_minor else p["w_uq"], p["kv_norm_g"], p["w_uk"],
            p["w_uv_t"] if token_minor else p["w_uv"]]
    if token_minor:
        for j in (3, 5):
            out_specs[j] = pl.BlockSpec((None, outs[j][0], tm), lambda i: (i, 0, 0))
            out_shape[j] = jax.ShapeDtypeStruct((n_tiles, outs[j][0], tm), BF16)
        tab_t = pl.BlockSpec((QK_ROPE // 2, tm), lambda i: (0, i % tab_blocks))
        in_specs += [tab_t, tab_t]
        args += list(tabs_t)
    return pl.pallas_call(
        functools.partial(_in_proj_kernel, token_minor=token_minor),
        grid=(n_tiles,),
        in_specs=in_specs,
        out_specs=out_specs,
        out_shape=out_shape,
        compiler_params=pltpu.CompilerParams(dimension_semantics=("arbitrary",), vmem_limit_bytes=VMEM_LIMIT),
        name="in_proj",
    )(*args)


def _rglru_kernel(rx_ref, grg_ref, cprev_ref, hprev_ref, cw_ref, cb_ref, wax_ref, ba_ref, bx_ref, lam_ref,
                  y_ref, cout_ref, hout_ref, xe_ref, a_ref, u_ref, hc_ref, *, nb, tt):
    pitch = tt + SUBLANES

    @pl.when(pl.program_id(0) == 0)
    def _():
        xe_ref[:, CONV_TAIL:SUBLANES, :] = cprev_ref[...]
        hc_ref[...] = hprev_ref[...]

    xc, new_tail = _conv_tile(xe_ref, rx_ref[...], cw_ref, cb_ref)
    cout_ref[...] = new_tail
    rate = _decay_rate(lam_ref)
    xc2 = xc.reshape(nb * tt, D_RNN)
    for n in range(D_RNN // LANES):
        cs = slice(n * LANES, (n + 1) * LANES)
        a, u = _recurrence_inputs(xc2[:, cs], wax_ref[n], ba_ref[:, cs], bx_ref[:, cs], rate[:, cs])
        for b in range(nb):
            a_ref[n, b * pitch:b * pitch + tt, :] = a[b * tt:(b + 1) * tt]
            u_ref[n, b * pitch:b * pitch + tt, :] = u[b * tt:(b + 1) * tt]
    _scan_slabs(a_ref, u_ref, hc_ref, nb, tt)
    hout_ref[...] = hc_ref[...]
    _gated_branch_output(y_ref, u_ref, grg_ref, nb, tt)


def _rglru(rx, grg, conv_prev, h_prev, p, tt):
    nb, t_len, _ = rx.shape
    assert nb % SUBLANES == 0 and t_len % tt == 0 and tt % SUBLANES == 0
    pitch = tt + SUBLANES
    seq = pl.BlockSpec((nb, tt, D_RNN), lambda i: (0, i, 0))
    return pl.pallas_call(
        functools.partial(_rglru_kernel, nb=nb, tt=tt),
        grid=(t_len // tt,),
        in_specs=[seq, seq, _const_spec((nb, CONV_W - 1, D_RNN)), _const_spec((nb, D_RNN)),
                  _const_spec((CONV_W, D_RNN)), _const_spec((1, D_RNN)),
                  _const_spec((RG_BLOCKS, RG_BLOCK, 2 * RG_BLOCK)),
                  _const_spec((1, D_RNN)), _const_spec((1, D_RNN)), _const_spec((1, D_RNN))],
        out_specs=[seq,
                   pl.BlockSpec((nb, CONV_W - 1, D_RNN), lambda i: (0, 0, 0)),
                   pl.BlockSpec((nb, D_RNN), lambda i: (0, 0))],
        out_shape=[jax.ShapeDtypeStruct((nb, t_len, D_RNN), BF16),
                   jax.ShapeDtypeStruct((nb, CONV_W - 1, D_RNN), F32),
                   jax.ShapeDtypeStruct((nb, D_RNN), F32)],
        scratch_shapes=[pltpu.VMEM((nb, tt + SUBLANES, D_RNN), F32),
                        pltpu.VMEM((D_RNN // LANES, nb * pitch, LANES), F32),
                        pltpu.VMEM((D_RNN // LANES, nb * pitch, LANES), F32),
                        pltpu.VMEM((nb, D_RNN), F32)],
        compiler_params=pltpu.CompilerParams(dimension_semantics=("arbitrary",), vmem_limit_bytes=VMEM_LIMIT),
        name="rglru",
    )(rx, grg, conv_prev, h_prev, p["conv_w"], p["conv_b"], p["w_ax"], p["rg_ba"], p["rg_bx"], p["rg_lambda"])


def _flash_kernel(qt_ref, k_ref, vt_ref, km_ref, vmt_ref, o_ref, *, tq, tv, hps):
    qcols = [slice(hh * HEAD_PAD, (hh + 1) * HEAD_PAD) for hh in range(hps)]
    vrows = [slice(hh * V_HEAD, (hh + 1) * V_HEAD) for hh in range(hps)]

    def queries_t(hh):
        return jnp.concatenate([qt_ref[c, qcols[hh], :] for c in range(tq // tv)], axis=1)

    def scores(j, hh):
        return _dot(k_ref[j * tq:(j + 1) * tq, qcols[hh]], queries_t(hh))

    def values_t(j, hh):
        return jnp.concatenate([vt_ref[j * (tq // tv) + c, vrows[hh], :] for c in range(tq // tv)], axis=1)

    def tile(n_full):
        kpos = lax.broadcasted_iota(jnp.int32, (tq, tq), 0)
        qpos = lax.broadcasted_iota(jnp.int32, (tq, tq), 1)

        def chunk_scores(j, hh):
            s_t = scores(j, hh)
            return s_t if j < n_full else jnp.where(kpos <= qpos, s_t, NEG)

        size = FLASH_GROUP if n_full + 1 > FLASH_GROUP else 1
        groups = [list(range(g0, min(g0 + size, n_full + 1))) for g0 in range(0, n_full + 1, size)]

        def group_scores(gi, hh):
            return [chunk_scores(j, hh) for j in groups[gi]]

        state = [None] * hps
        s_next = [group_scores(0, hh) for hh in range(hps)]
        for gi, group in enumerate(groups):
            for hh in range(hps):
                ss = s_next[hh]
                vs = [values_t(j, hh) for j in group]
                if gi == 0:
                    ss = [_dot(km_ref[:, qcols[hh]], queries_t(hh))] + ss
                    vs = [vmt_ref[vrows[hh], :]] + vs
                if gi + 1 < len(groups):
                    s_next[hh] = group_scores(gi + 1, hh)
                m_new = functools.reduce(jnp.maximum, [jnp.max(s_t, axis=0, keepdims=True) for s_t in ss])
                if state[hh] is not None:
                    m_new = jnp.maximum(state[hh][0], m_new)
                ps = [jnp.exp2(s_t - m_new) for s_t in ss]
                l = functools.reduce(jnp.add, [jnp.sum(p, axis=0, keepdims=True) for p in ps])
                acc_t = functools.reduce(jnp.add, [_dot(v_t, p.astype(BF16)) for v_t, p in zip(vs, ps)])
                if state[hh] is not None:
                    m, l_old, acc_old = state[hh]
                    alpha = jnp.exp2(m - m_new)
                    l, acc_t = alpha * l_old + l, alpha * acc_old + acc_t
                state[hh] = (m_new, l, acc_t)
        for hh in range(hps):
            _, l, acc_t = state[hh]
            o_ref[:, vrows[hh]] = jnp.transpose(acc_t / l).astype(BF16)

    for c in range(k_ref.shape[0] // tq):
        pl.when(pl.program_id(2) == c)(functools.partial(tile, c))


def _flash(q_t, k, v_t, k_pre, v_pre_t, tq, hps):
    nb, t_len, _ = k.shape
    tv = v_t.shape[3]
    assert t_len % tq == 0 and tq % tv == 0 and N_HEADS % hps == 0
    return pl.pallas_call(
        functools.partial(_flash_kernel, tq=tq, tv=tv, hps=hps),
        grid=(nb, N_HEADS // hps, t_len // tq),
        in_specs=[pl.BlockSpec((None, tq // tv, hps * HEAD_PAD, tv), lambda b, h, i: (b, i, h, 0)),
                  pl.BlockSpec((None, t_len, hps * HEAD_PAD), lambda b, h, i: (b, 0, h)),
                  pl.BlockSpec((None, t_len // tv, hps * V_HEAD, tv), lambda b, h, i: (b, 0, h, 0)),
                  pl.BlockSpec((N_META, hps * HEAD_PAD), lambda b, h, i: (0, h)),
                  pl.BlockSpec((hps * V_HEAD, N_META), lambda b, h, i: (h, 0))],
        out_specs=pl.BlockSpec((None, tq, hps * V_HEAD), lambda b, h, i: (b, i, h)),
        out_shape=jax.ShapeDtypeStruct((nb, t_len, N_HEADS * V_HEAD), BF16),
        compiler_params=pltpu.CompilerParams(dimension_semantics=("arbitrary",) * 3, vmem_limit_bytes=VMEM_LIMIT),
        name="flash_prompt",
    )(q_t, k, v_t, k_pre, v_pre_t)


def _meta_attn_kernel(q_ref, k_ref, v_ref, o_ref):
    qpos = lax.broadcasted_iota(jnp.int32, (N_META, N_META), 0)
    kpos = lax.broadcasted_iota(jnp.int32, (N_META, N_META), 1)
    for hh in range(N_HEADS):
        qs = slice(hh * HEAD_PAD, (hh + 1) * HEAD_PAD)
        vs = slice(hh * V_HEAD, (hh + 1) * V_HEAD)
        s = jnp.where(kpos <= qpos, _dot_nt(q_ref[:, qs], k_ref[:, qs]), NEG)
        pr = jnp.exp2(s - jnp.max(s, axis=1, keepdims=True))
        o = _dot(pr.astype(BF16), v_ref[:, vs]) / jnp.sum(pr, axis=1, keepdims=True)
        o_ref[:, vs] = o.astype(BF16)


def _meta_attn(q, k, v):
    return pl.pallas_call(
        _meta_attn_kernel,
        out_shape=jax.ShapeDtypeStruct((N_META, N_HEADS * V_HEAD), BF16),
        name="meta_attn",
    )(q, k, v)


def _absorb_kernel(q_ref, wukt_ref, qlat_ref, qpe_ref):
    for hh in range(N_HEADS):
        q = q_ref[:, hh * HEAD_PAD:(hh + 1) * HEAD_PAD]
        qlat_ref[hh] = _dot(q[:, :QK_NOPE], wukt_ref[hh])
        qpe_ref[hh] = q[:, QK_NOPE:].astype(F32)


def _absorb(q, w_ukt):
    rows = q.shape[0]
    return pl.pallas_call(
        _absorb_kernel,
        out_shape=[jax.ShapeDtypeStruct((N_HEADS, rows, KV_LORA), F32),
                   jax.ShapeDtypeStruct((N_HEADS, rows, ROPE_PAD), F32)],
        name="absorb_q",
    )(q, w_ukt)


PAGED_BUFS = 2


def _paged_scratch(n_q, pps, page):
    return [pltpu.VMEM((n_q, 1), F32), pltpu.VMEM((n_q, 1), F32), pltpu.VMEM((n_q, KV_LORA), F32),
            pltpu.VMEM((PAGED_BUFS, pps * page, KV_LORA), F32), pltpu.VMEM((PAGED_BUFS, pps, QK_ROPE, page), F32),
            pltpu.VMEM((pps * page, KV_LORA), BF16), pltpu.VMEM((QK_ROPE, pps * page), BF16),
            pltpu.SemaphoreType.DMA((PAGED_BUFS,))]


def _paged_copies(pt_ref, ckv_hbm, kpe_hbm, cin_ref, kin_ref, sem, seq, chunk, slot, pps):
    page = kin_ref.shape[3]
    out = []
    for i in range(pps):
        pid = pt_ref[seq, chunk * pps + i]
        out.append(pltpu.make_async_copy(ckv_hbm.at[pid], cin_ref.at[slot, pl.ds(i * page, page)], sem.at[slot]))
        out.append(pltpu.make_async_copy(kpe_hbm.at[pid], kin_ref.at[slot, i], sem.at[slot]))
    return out


def _paged_new_tokens(qlat, qpe, cnew_ref, knew_ref, m_ref, l_ref, acc_ref, ts):
    cn = cnew_ref[...].astype(BF16)
    kn = knew_ref[...].astype(BF16)
    s = _dot_nt(qlat, cn) + _dot_nt(qpe, kn)
    t_q = lax.broadcasted_iota(jnp.int32, s.shape, 0) % ts
    t_k = lax.broadcasted_iota(jnp.int32, s.shape, 1)
    s = jnp.where(t_k <= t_q, s, NEG)
    m = jnp.max(s, axis=1, keepdims=True)
    pr = jnp.exp2(s - m)
    m_ref[...] = m
    l_ref[...] = jnp.sum(pr, axis=1, keepdims=True)
    acc_ref[...] = _dot(pr.astype(BF16), cn)


def _paged_scores(qlat, qpe, cin_ref, kin_ref, cbuf_ref, kbuf_ref, slot, pps):
    page = kin_ref.shape[3]
    for i in range(pps):
        rows = pl.ds(i * page, page)
        cbuf_ref[rows, :] = cin_ref[slot, rows, :].astype(BF16)
        kbuf_ref[:, i * page:(i + 1) * page] = kin_ref[slot, i].astype(BF16)
    return _dot_nt(qlat, cbuf_ref[...]) + _dot(qpe, kbuf_ref[...])


def _paged_probs(s, m_ref, l_ref):
    m = m_ref[...]
    m_new = jnp.maximum(m, jnp.max(s, axis=1, keepdims=True))
    alpha = jnp.exp2(m - m_new)
    pr = jnp.exp2(s - m_new)
    m_ref[...] = m_new
    l_ref[...] = alpha * l_ref[...] + jnp.sum(pr, axis=1, keepdims=True)
    return alpha, pr.astype(BF16)


def _paged_accumulate(alpha, pr, cbuf_ref, acc_ref):
    acc_ref[...] = alpha * acc_ref[...] + _dot(pr, cbuf_ref[...])


def _unabsorb_kernel(olat_ref, wuv_ref, o_ref):
    for hh in range(N_HEADS):
        o_ref[:, hh * V_HEAD:(hh + 1) * V_HEAD] = _dot(olat_ref[hh].astype(BF16), wuv_ref[hh]).astype(BF16)


def _unabsorb(o_lat, w_uvh):
    return pl.pallas_call(
        _unabsorb_kernel,
        out_shape=jax.ShapeDtypeStruct((o_lat.shape[1], N_HEADS * V_HEAD), BF16),
        name="unabsorb_o",
    )(o_lat, w_uvh)


FF_CHUNKS = 4
N_POST_WEIGHTS = 13


def _post_compute(x_ref, y_ref, o_ref, g_ref, weights, out_ref, side_work=None):
    before_up, before_down, after_down = side_work if side_work is not None else (lambda c: None,) * 3
    lng_ref, lnb_ref, wr_ref, wm_ref, wo_ref, g1_ref, b1_ref, wup_ref, bup_ref, wdn_ref, bdn_ref, g2_ref, b2_ref = weights
    h = _layer_norm(x_ref[...], lng_ref[...], lnb_ref[...])
    z_r = _dot(y_ref[...], wr_ref[...])
    z_m = _dot(o_ref[...], wm_ref[...])
    mix_in = g_ref[:, :D_MODEL] * z_r + g_ref[:, D_MODEL:] * z_m
    mix = _dot(mix_in.astype(BF16), wo_ref[...])
    before_up(0)
    x1 = _layer_norm(DN_ALPHA * h + mix, g1_ref[...], b1_ref[...])
    x1b = x1.astype(BF16)
    f = bdn_ref[...]
    ff_chunk = D_FF // FF_CHUNKS
    for c in range(FF_CHUNKS):
        cs = slice(c * ff_chunk, (c + 1) * ff_chunk)
        if c > 0:
            before_up(c)
        up = jnp.maximum(_dot(x1b, wup_ref[:, cs]) + bup_ref[:, cs], 0.0)
        before_down(c)
        f = f + _dot((up * up).astype(BF16), wdn_ref[cs, :])
        after_down(c)
    out_ref[...] = _layer_norm(DN_ALPHA * x1 + f, g2_ref[...], b2_ref[...])


def _post_kernel(x_ref, y_ref, o_ref, g_ref, *rest):
    _post_compute(x_ref, y_ref, o_ref, g_ref, rest[:N_POST_WEIGHTS], rest[N_POST_WEIGHTS])


def _post_paged_kernel(pt_ref, x_ref, y_ref, o_ref, g_ref, *rest, pps, ts, n_j, n_steps, tiles_per_seq):
    weights = rest[:N_POST_WEIGHTS]
    (qlat_ref, qpe_ref, cnew_ref, knew_ref, lat_ref, latm_ref, ckv_hbm, kpe_hbm, out_ref, olat_ref, platent_hbm,
     m_ref, l_ref, acc_ref, cin_ref, kin_ref, cbuf_ref, kbuf_ref, sem, lsem) = rest[N_POST_WEIGHTS:]
    s = pl.program_id(0)
    spb = n_j // FF_CHUNKS
    seq, j0 = lax.div(s, spb), lax.rem(s, spb) * FF_CHUNKS
    copies = functools.partial(_paged_copies, pt_ref, ckv_hbm, kpe_hbm, cin_ref, kin_ref, sem, pps=pps)

    tm = out_ref.shape[0]
    lat_rows = pltpu.make_async_copy(
        lat_ref,
        platent_hbm.at[lax.div(s, tiles_per_seq), pl.ds(N_META + lax.rem(s, tiles_per_seq) * tm, tm)], lsem.at[0])
    lat_meta = [pltpu.make_async_copy(latm_ref, platent_hbm.at[b, pl.ds(0, N_META)], lsem.at[1])
                for b in range(platent_hbm.shape[0])]
    lat_rows.start()

    @pl.when(s == 0)
    def _():
        for cp in lat_meta:
            cp.start()
        for cp in copies(0, 0, 0):
            cp.start()

    live = {}

    def gather_and_score(c):
        slot, nxt = c % PAGED_BUFS, (c + 1) % PAGED_BUFS
        if c + 1 < FF_CHUNKS:
            for cp in copies(seq, j0 + c + 1, nxt):
                cp.start()
        else:
            @pl.when(s + 1 < n_steps)
            def _():
                for cp in copies(lax.div(s + 1, spb), lax.rem(s + 1, spb) * FF_CHUNKS, nxt):
                    cp.start()
        for cp in copies(seq, j0 + c, slot):
            cp.wait()
        n_q = acc_ref.shape[0]
        qlat = qlat_ref[...].reshape(n_q, KV_LORA).astype(BF16)
        qpe = qpe_ref[...].reshape(n_q, ROPE_PAD)[:, :QK_ROPE].astype(BF16)
        if c == 0:
            pl.when(j0 == 0)(functools.partial(_paged_new_tokens, qlat, qpe, cnew_ref, knew_ref,
                                               m_ref, l_ref, acc_ref, ts))
        live["s"] = _paged_scores(qlat, qpe, cin_ref, kin_ref, cbuf_ref, kbuf_ref, slot, pps)

    def probs(c):
        live["alpha"], live["pr"] = _paged_probs(live.pop("s"), m_ref, l_ref)

    def accumulate(c):
        _paged_accumulate(live.pop("alpha"), live.pop("pr"), cbuf_ref, acc_ref)
        if c == FF_CHUNKS - 1:
            @pl.when(j0 == n_j - FF_CHUNKS)
            def _():
                olat_ref[...] = (acc_ref[...] / l_ref[...]).reshape(olat_ref.shape)

    _post_compute(x_ref, y_ref, o_ref, g_ref, weights, out_ref, side_work=(gather_and_score, probs, accumulate))

    lat_rows.wait()

    @pl.when(s == 0)
    def _():
        for cp in lat_meta:
            cp.wait()


def _post_weight_specs():
    vec = lambda w: _const_spec((1, w))
    sq = _const_spec((D_MODEL, D_MODEL))
    return [vec(D_MODEL), vec(D_MODEL), sq, sq, sq, vec(D_MODEL), vec(D_MODEL),
            _const_spec((D_MODEL, D_FF)), vec(D_FF), _const_spec((D_FF, D_MODEL)), vec(D_MODEL),
            vec(D_MODEL), vec(D_MODEL)]


def _post_weights(p):
    return (p["ln_in_g"], p["ln_in_b"], p["w_br_r"], p["w_br_m"], p["w_o"], p["ln1_g"], p["ln1_b"],
            p["w_up"], p["b_up"], p["w_down"], p["b_down"], p["ln2_g"], p["ln2_b"])


def _post(x, y, o, g, p, tm):
    rows = x.shape[0]
    assert rows % tm == 0
    row = lambda w: pl.BlockSpec((tm, w), lambda i: (i, 0))
    return pl.pallas_call(
        _post_kernel,
        grid=(rows // tm,),
        in_specs=[row(D_MODEL), row(D_RNN), row(N_HEADS * V_HEAD), row(2 * D_MODEL)] + _post_weight_specs(),
        out_specs=row(D_MODEL),
        out_shape=jax.ShapeDtypeStruct((rows, D_MODEL), F32),
        compiler_params=pltpu.CompilerParams(dimension_semantics=("arbitrary",), vmem_limit_bytes=VMEM_LIMIT),
        name="merge_mlp",
    )(x, y, o, g, *_post_weights(p))


def _post_with_paged_attn(x, y, o, g, p, tm, page_table, q_lat, q_pe, ckv_new, kpe_new, ckv_pool, kpe_pool_t,
                          latent, latent_meta, n_seq):
    rows = x.shape[0]
    n_steps = rows // tm
    assert n_steps % n_seq == 0
    nb, n_pages = page_table.shape
    ts = q_lat.shape[2]
    n_q = N_HEADS * ts
    page = ckv_pool.shape[1]
    n_new = ckv_new.shape[1]
    chunks = n_steps * FF_CHUNKS
    assert rows % tm == 0 and (nb * n_pages) % chunks == 0 and FF_CHUNKS % PAGED_BUFS == 0
    pps = nb * n_pages // chunks
    n_j = n_pages // pps
    assert n_pages % pps == 0 and n_j % FF_CHUNKS == 0
    spb = n_j // FF_CHUNKS
    row = lambda w: pl.BlockSpec((tm, w), lambda i, pt: (i, 0))
    per_seq = lambda r, w: pl.BlockSpec((None, r, w), lambda i, pt: (i // spb, 0, 0))
    heads_of_seq = lambda w: pl.BlockSpec((N_HEADS, None, ts, w), lambda i, pt: (0, i // spb, 0, 0))
    grid_spec = pltpu.PrefetchScalarGridSpec(
        num_scalar_prefetch=1,
        grid=(n_steps,),
        in_specs=[row(D_MODEL), row(D_RNN), row(N_HEADS * V_HEAD), row(2 * D_MODEL)] + _post_weight_specs()
                 + [heads_of_seq(KV_LORA), heads_of_seq(ROPE_PAD), per_seq(n_new, KV_LORA), per_seq(n_new, QK_ROPE),
                    row(KV_LORA), _const_spec((N_META, KV_LORA))]
                 + [pl.BlockSpec(memory_space=pl.ANY)] * 2,
        out_specs=[row(D_MODEL), heads_of_seq(KV_LORA), pl.BlockSpec(memory_space=pl.ANY)],
        scratch_shapes=_paged_scratch(n_q, pps, page) + [pltpu.SemaphoreType.DMA((2,))],
    )
    return pl.pallas_call(
        functools.partial(_post_paged_kernel, pps=pps, ts=ts, n_j=n_j, n_steps=n_steps,
                          tiles_per_seq=n_steps // n_seq),
        grid_spec=grid_spec,
        out_shape=[jax.ShapeDtypeStruct((rows, D_MODEL), F32),
                   jax.ShapeDtypeStruct((N_HEADS, nb, ts, KV_LORA), F32),
                   jax.ShapeDtypeStruct((n_seq, N_META + rows // n_seq, KV_LORA), F32)],
        compiler_params=pltpu.CompilerParams(dimension_semantics=("arbitrary",), vmem_limit_bytes=VMEM_LIMIT),
        name="merge_mlp_paged_attn",
    )(page_table, x, y, o, g, *_post_weights(p), q_lat, q_pe, ckv_new, kpe_new, latent, latent_meta,
      ckv_pool, kpe_pool_t)


def _rope_tables(pos):
    half = QK_ROPE // 2
    inv = 1.0 / (ROPE_THETA ** (jnp.arange(half, dtype=F32) / half))
    ang = pos.astype(F32)[:, None] * inv[None, :]
    cos, sin = jnp.cos(ang), jnp.sin(ang)
    z = jnp.zeros_like(cos)
    return ((jnp.concatenate([cos, cos, z, z], 1), jnp.concatenate([-sin, z, z, z], 1),
             jnp.concatenate([z, sin, z, z], 1)), (cos.T, sin.T))


def _prep_params(w_in, b_gate, conv_w, conv_b, rg_wa, rg_ba, rg_wx, rg_bx, rg_lambda, w_br_r, q_norm_g, w_uq,
                 kv_norm_g, w_uk, w_uv, w_br_m, w_o, ln1_g, ln1_b, w_up, b_up, w_down, b_down, ln2_g, ln2_b,
                 ln_in_g, ln_in_b):
    l = 0
    vec = lambda a: a.reshape(1, -1).astype(F32)
    wi = w_in[l].astype(BF16)
    w_uq_p = jnp.pad(w_uq[l], ((0, 0), (0, 0), (0, HEAD_PAD - QK_NOPE - QK_ROPE))).reshape(Q_LORA, N_HEADS * HEAD_PAD)
    return {
        "ln_in_g": vec(ln_in_g), "ln_in_b": vec(ln_in_b),
        "w_in": wi, "w_gate": wi[:, _C_GATE:], "b_gate": vec(b_gate[l]),
        "q_norm_g": vec(q_norm_g[l]), "w_uq": w_uq_p.astype(BF16), "w_uq_t": w_uq_p.T.astype(BF16),
        "kv_norm_g": vec(kv_norm_g[l]),
        "w_uk": w_uk[l].reshape(KV_LORA, N_HEADS * QK_NOPE).astype(BF16),
        "w_uv": w_uv[l].reshape(KV_LORA, N_HEADS * V_HEAD).astype(BF16),
        "w_uv_t": w_uv[l].reshape(KV_LORA, N_HEADS * V_HEAD).T.astype(BF16),
        "w_ukt": jnp.transpose(w_uk[l], (1, 2, 0)).astype(BF16),
        "w_uvh": jnp.transpose(w_uv[l], (1, 0, 2)).astype(BF16),
        "conv_w": conv_w[l].astype(F32), "conv_b": vec(conv_b[l]),
        "w_ax": jnp.concatenate([rg_wa[l], rg_wx[l]], axis=-1).astype(BF16),
        "rg_ba": vec(rg_ba[l]), "rg_bx": vec(rg_bx[l]), "rg_lambda": vec(rg_lambda[l]),
        "w_br_r": w_br_r[l].astype(BF16), "w_br_m": w_br_m[l].astype(BF16), "w_o": w_o[l].astype(BF16),
        "ln1_g": vec(ln1_g[l]), "ln1_b": vec(ln1_b[l]),
        "w_up": w_up[l].astype(BF16), "b_up": vec(b_up[l]),
        "w_down": w_down[l].astype(BF16), "b_down": vec(b_down[l]),
        "ln2_g": vec(ln2_g[l]), "ln2_b": vec(ln2_b[l]),
    }


def kernel(x_prompt, x_sample, cache_ckv, cache_kpe, page_table, state_conv, state_rglru, meta_tokens, ln_in_g, ln_in_b, w_in, b_gate, conv_w, conv_b, rg_wa, rg_ba, rg_wx, rg_bx, rg_lambda, w_br_r, q_norm_g, w_uq, kv_norm_g, w_uk, w_uv, w_br_m, w_o, ln1_g, ln1_b, w_up, b_up, w_down, b_down, ln2_g, ln2_b):
    assert w_in.shape[0] == DEPTH == 1
    bn, seq, _ = x_prompt.shape
    bd, ts, _ = x_sample.shape
    past_len = page_table.shape[1] * cache_ckv.shape[2]
    p = _prep_params(w_in, b_gate, conv_w, conv_b, rg_wa, rg_ba, rg_wx, rg_bx, rg_lambda, w_br_r, q_norm_g, w_uq,
                     kv_norm_g, w_uk, w_uv, w_br_m, w_o, ln1_g, ln1_b, w_up, b_up, w_down, b_down, ln2_g, ln2_b,
                     ln_in_g, ln_in_b)
    n_s = bd * ts

    x_small = jnp.concatenate([x_sample.reshape(n_s, D_MODEL), meta_tokens.astype(F32)], axis=0)
    pos_small = jnp.concatenate([jnp.tile(past_len + jnp.arange(ts), bd), jnp.arange(N_META)])
    n_small = n_s + N_META
    rx, grg, g_small, q, k, v, ckv, kpe = _in_proj(x_small, _rope_tables(pos_small)[0], 1, p, n_small)

    rep = lambda a: jnp.broadcast_to(a[None], (SUBLANES,) + a.shape)
    y_m, conv_m, h_m = _rglru(rep(rx[n_s:]), rep(grg[n_s:]), jnp.zeros((SUBLANES, CONV_W - 1, D_RNN), F32),
                              jnp.zeros((SUBLANES, D_RNN), F32), p, N_META)
    k_meta, v_meta = k[n_s:], v[n_s:]
    o_m = _meta_attn(q[n_s:], k_meta, v_meta)

    y_s, conv_s, h_s = _rglru(rx[:n_s].reshape(bd, ts, D_RNN), grg[:n_s].reshape(bd, ts, D_RNN),
                              state_conv[0], state_rglru[0], p, ts)
    q_lat, q_pe = _absorb(q[:n_s], p["w_ukt"])
    ckv_s = ckv[:n_s].reshape(bd, ts, KV_LORA)
    kpe_s = kpe[:n_s].reshape(bd, ts, QK_ROPE)
    pad_new = lambda a: jnp.pad(a, ((0, 0), (0, 2 * SUBLANES - ts), (0, 0)))

    n_p = bn * seq
    pos_p = N_META + jnp.arange(seq)
    tm = TM_IN_PROJ
    bcast = lambda a: jnp.broadcast_to(a[:1], (bn,) + a.shape[1:])
    tabs_p, tabs_pt = _rope_tables(pos_p)
    rx_p, grg_p, g_p, qt_p, k_p, vt_p, ckv_p, kpe_p = _in_proj(x_prompt.reshape(n_p, D_MODEL), tabs_p,
                                                               seq // tm, p, tm, tabs_t=tabs_pt)
    y_p, conv_p, h_p = _rglru(rx_p.reshape(bn, seq, D_RNN), grg_p.reshape(bn, seq, D_RNN), bcast(conv_m), bcast(h_m),
                              p, TT_SCAN)
    o_p = _flash(qt_p.reshape(bn, seq // tm, -1, tm), k_p.reshape(bn, seq, -1), vt_p.reshape(bn, seq // tm, -1, tm),
                 k_meta, v_meta.T, TQ_FLASH, FLASH_HEADS)
    out_p, o_lat, prompt_ckv = _post_with_paged_attn(
        x_prompt.reshape(n_p, D_MODEL), y_p.reshape(n_p, D_RNN), o_p.reshape(n_p, -1), g_p, p, TM_POST,
        page_table, q_lat.reshape(N_HEADS, bd, ts, KV_LORA), q_pe.reshape(N_HEADS, bd, ts, ROPE_PAD),
        pad_new(ckv_s), pad_new(kpe_s), cache_ckv[0], jnp.swapaxes(cache_kpe[0], 1, 2),
        ckv_p, ckv[n_s:], bn)

    o_s = _unabsorb(o_lat.reshape(N_HEADS, n_s, KV_LORA), p["w_uvh"])
    y_small = jnp.concatenate([y_s.reshape(n_s, D_RNN), y_m[0]], axis=0)
    o_small = jnp.concatenate([o_s, o_m], axis=0)
    out_small = _post(x_small, y_small, o_small, g_small, p, n_small)

    meta_rows = lambda a: jnp.broadcast_to(a[n_s:][None], (bn, N_META, a.shape[-1]))
    y_prompt = out_p.reshape(bn, seq, D_MODEL)
    y_sample = out_small[:n_s].reshape(bd, ts, D_MODEL)
    prompt_kpe = jnp.concatenate([meta_rows(kpe), kpe_p.reshape(bn, seq, QK_ROPE)], axis=1)[None]
    return (y_prompt, y_sample, prompt_ckv[None], prompt_kpe, conv_p[None], h_p[None],
            ckv_s[None], kpe_s[None], conv_s[None], h_s[None])
```

```python
import functools
import math

import jax
import jax.numpy as jnp
from jax import lax
from jax.experimental import pallas as pl
from jax.experimental.pallas import tpu as pltpu

F32 = jnp.float32
BF16 = jnp.bfloat16

D_MODEL = 1024
N_META = 16
D_RNN = D_MODEL
RG_BLOCKS = 8
RG_BLOCK = D_RNN // RG_BLOCKS
CONV_W = 4
RG_C = 8.0
N_HEADS = 8
QK_NOPE = 128
QK_ROPE = 64
V_HEAD = 128
KV_LORA = 512
Q_LORA = 768
ROPE_THETA = 10000.0
D_FF = 4 * D_MODEL
DEPTH = 1
DN_ALPHA = (2.0 * DEPTH) ** 0.25
EPS = 1e-5
SM_SCALE = (QK_NOPE + QK_ROPE) ** -0.5
Q_PRESCALE = SM_SCALE * math.log2(math.e)

LANES = 128
SUBLANES = 8
HEAD_PAD = 2 * LANES
ROPE_PAD = LANES
_C_RX, _C_RG, _C_CQ, _C_CKV, _C_KPE = 0, D_RNN, 2 * D_RNN, 2 * D_RNN + Q_LORA, 2 * D_RNN + Q_LORA + KV_LORA
_C_GATE = _C_KPE + QK_ROPE
VMEM_LIMIT = 56 * 1024 * 1024
TM_IN_PROJ = 256
TM_POST = 256
TT_SCAN = 128
TQ_FLASH = 512
FLASH_HEADS = 4
FLASH_GROUP = 2
NEG = float(jnp.finfo(jnp.float32).min)


def _const_spec(shape):
    return pl.BlockSpec(shape, lambda *_: (0,) * len(shape), pipeline_mode=pl.Buffered(1))


def _layer_norm(x, g, b):
    mu = jnp.mean(x, -1, keepdims=True)
    xc = x - mu
    var = jnp.mean(xc * xc, -1, keepdims=True)
    return xc * lax.rsqrt(var + EPS) * g + b


def _rms_norm(x, g):
    return x * lax.rsqrt(jnp.mean(x * x, -1, keepdims=True) + EPS) * g


def _gelu_tanh(x):
    return 0.5 * x * (1.0 + jnp.tanh(math.sqrt(2.0 / math.pi) * (x + 0.044715 * (x * x * x))))


def _rope_block(y, cos, sin_lo, sin_hi):
    left = pltpu.roll(y, ROPE_PAD - QK_ROPE // 2, 1)
    right = pltpu.roll(y, QK_ROPE // 2, 1)
    return y * cos + left * sin_lo + right * sin_hi


def _dot(a, b):
    return jnp.dot(a, b, preferred_element_type=F32)


def _dot_nt(a, b):
    return lax.dot_general(a, b, (((1,), (1,)), ((), ())), preferred_element_type=F32)


CONV_TAIL = SUBLANES - (CONV_W - 1)


def _conv_tile(xe_ref, x, cw_ref, cb_ref):
    tt = x.shape[1]
    xe_ref[:, SUBLANES:, :] = x
    xc = cb_ref[...] + x * cw_ref[CONV_W - 1:CONV_W, :]
    for k in range(CONV_W - 1):
        xc = xc + xe_ref[:, CONV_TAIL + k:CONV_TAIL + k + tt, :] * cw_ref[k:k + 1, :]
    new_tail = xe_ref[:, tt + CONV_TAIL:tt + SUBLANES, :]
    xe_ref[:, CONV_TAIL:SUBLANES, :] = new_tail
    return xc, new_tail


def _decay_rate(lam_ref):
    z = -lam_ref[...]
    return RG_C * (jnp.maximum(z, 0.0) + jnp.log1p(jnp.exp(-jnp.abs(z))))


def _recurrence_inputs(xn, wax_n, ba_n, bx_n, rate_n):
    gates = _dot(xn.astype(BF16), wax_n)
    r = jax.nn.sigmoid(gates[:, :RG_BLOCK] + ba_n)
    ig = jax.nn.sigmoid(gates[:, RG_BLOCK:] + bx_n)
    neg_log_a = r * rate_n
    a = jnp.exp(-neg_log_a)
    u = jnp.sqrt(jnp.tanh(neg_log_a) * (a * a + 1.0)) * (ig * xn)
    return a, u


def _scan_slabs(a_ref, u_ref, hc_ref, nb, tt):
    pitch = tt + SUBLANES
    n_slab = D_RNN // LANES
    for grp in range(nb // SUBLANES):
        base = grp * SUBLANES * pitch
        rows = slice(grp * SUBLANES, (grp + 1) * SUBLANES)
        h0 = tuple(hc_ref[rows, n * LANES:(n + 1) * LANES] for n in range(n_slab))

        def step(t, hs, base=base):
            out = []
            for n in range(n_slab):
                idx = pl.ds(base + t, SUBLANES, stride=pitch)
                hn = a_ref[n, idx, :] * hs[n] + u_ref[n, idx, :]
                u_ref[n, idx, :] = hn
                out.append(hn)
            return tuple(out)

        hs = lax.fori_loop(0, tt, step, h0, unroll=min(tt, 8))
        for n in range(n_slab):
            hc_ref[rows, n * LANES:(n + 1) * LANES] = hs[n]


def _gated_branch_output(y_ref, h_ref, grg_ref, nb, tt):
    pitch = tt + SUBLANES
    for b in range(nb):
        for n in range(D_RNN // LANES):
            cs = slice(n * LANES, (n + 1) * LANES)
            y_ref[b, :, cs] = (h_ref[n, b * pitch:b * pitch + tt, :] * grg_ref[b, :, cs]).astype(BF16)


def _in_proj_kernel(x_ref, cos_ref, slo_ref, shi_ref, lng_ref, lnb_ref, w_in_ref, w_gate_ref, bg_ref,
                    qg_ref, wuq_ref, kvg_ref, wuk_ref, wuv_ref, *rest, token_minor):
    if token_minor:
        cost_ref, sint_ref, *rest = rest
    rx_ref, grg_ref, g_ref, q_ref, k_ref, v_ref, ckv_ref, kpe_ref = rest
    tm = x_ref.shape[0]
    half = tm // 2 if tm % (4 * SUBLANES) == 0 else tm
    halves = [_layer_norm(x_ref[r0:r0 + half, :], lng_ref[...], lnb_ref[...]).astype(BF16)
              for r0 in range(0, tm, half)]

    def proj_early(lo, hi):
        return jnp.concatenate([_dot(hb_half, w_in_ref[:, lo:hi]) for hb_half in halves], axis=0)

    hb = jnp.concatenate(halves, axis=0)

    def proj(lo, hi):
        return _dot(hb, w_in_ref[:, lo:hi])

    cos, slo, shi = cos_ref[...], slo_ref[...], shi_ref[...]
    cqn = _rms_norm(proj_early(_C_CQ, _C_CKV), qg_ref[...])
    ckv = _rms_norm(proj_early(_C_CKV, _C_KPE), kvg_ref[...])
    kpe_raw = proj_early(_C_KPE, _C_KPE + ROPE_PAD)
    lane = lax.broadcasted_iota(jnp.int32, kpe_raw.shape, 1)
    kpe = _rope_block(jnp.where(lane < QK_ROPE, kpe_raw, 0.0), cos, slo, shi)
    ckv_ref[...] = ckv
    kpe_ref[...] = kpe[:, :QK_ROPE]
    ckvb = ckv.astype(BF16)
    kpeb = kpe.astype(BF16)

    rx_ref[...] = proj(_C_RX, _C_RG)
    grg_ref[...] = _gelu_tanh(proj(_C_RG, _C_CQ))
    g_ref[...] = jax.nn.sigmoid(_dot(hb, w_gate_ref[...]) + bg_ref[...])

    cqb = cqn.astype(BF16)
    if token_minor:
        qt = _dot_nt(wuq_ref[...], cqb) * Q_PRESCALE
        cos_t, sin_t = cost_ref[...], sint_ref[...]
        half = QK_ROPE // 2
        for hh in range(N_HEADS):
            r0 = hh * HEAD_PAD
            r1, r2, r3 = r0 + QK_NOPE, r0 + QK_NOPE + half, r0 + QK_NOPE + QK_ROPE
            x1, x2 = qt[r1:r2], qt[r2:r3]
            q_ref[r0:r1, :] = qt[r0:r1].astype(BF16)
            q_ref[r1:r2, :] = (x1 * cos_t - x2 * sin_t).astype(BF16)
            q_ref[r2:r3, :] = (x1 * sin_t + x2 * cos_t).astype(BF16)
            q_ref[r3:r0 + HEAD_PAD, :] = qt[r3:r0 + HEAD_PAD].astype(BF16)
    else:
        q = _dot(cqb, wuq_ref[...]) * Q_PRESCALE
        for hh in range(N_HEADS):
            c0 = hh * HEAD_PAD
            q_ref[:, c0:c0 + QK_NOPE] = q[:, c0:c0 + QK_NOPE].astype(BF16)
            q_ref[:, c0 + QK_NOPE:c0 + HEAD_PAD] = _rope_block(q[:, c0 + QK_NOPE:c0 + HEAD_PAD],
                                                               cos, slo, shi).astype(BF16)

    kn = _dot(ckvb, wuk_ref[...])
    if token_minor:
        v_ref[...] = _dot_nt(wuv_ref[...], ckvb).astype(BF16)
    else:
        v_ref[...] = _dot(ckvb, wuv_ref[...]).astype(BF16)
    for hh in range(N_HEADS):
        c0 = hh * HEAD_PAD
        k_ref[:, c0:c0 + QK_NOPE] = kn[:, hh * QK_NOPE:(hh + 1) * QK_NOPE].astype(BF16)
        k_ref[:, c0 + QK_NOPE:c0 + HEAD_PAD] = kpeb


def _in_proj(x, tabs, tab_blocks, p, tm, tabs_t=None):
    token_minor = tabs_t is not None
    rows = x.shape[0]
    assert rows % tm == 0
    n_tiles = rows // tm
    row = lambda w: pl.BlockSpec((tm, w), lambda i: (i, 0))
    tab = pl.BlockSpec((tm, ROPE_PAD), lambda i: (i % tab_blocks, 0))
    outs = [
        (D_RNN, F32),
        (D_RNN, F32),
        (2 * D_MODEL, F32),
        (N_HEADS * HEAD_PAD, BF16),
        (N_HEADS * HEAD_PAD, BF16),
        (N_HEADS * V_HEAD, BF16),
        (KV_LORA, F32),
        (QK_ROPE, F32),
    ]
    out_specs = [row(w) for w, _ in outs]
    out_shape = [jax.ShapeDtypeStruct((rows, w), dt) for w, dt in outs]
    maybe_t = lambda shape: shape[::-1] if token_minor else shape
    in_specs = [row(D_MODEL), tab, tab, tab,
                _const_spec((1, D_MODEL)), _const_spec((1, D_MODEL)),
                _const_spec((D_MODEL, _C_GATE + 2 * D_MODEL)),
                _const_spec((D_MODEL, 2 * D_MODEL)), _const_spec((1, 2 * D_MODEL)),
                _const_spec((1, Q_LORA)), _const_spec(maybe_t((Q_LORA, N_HEADS * HEAD_PAD))),
                _const_spec((1, KV_LORA)), _const_spec((KV_LORA, N_HEADS * QK_NOPE)),
                _const_spec(maybe_t((KV_LORA, N_HEADS * V_HEAD)))]
    args = [x, *tabs, p["ln_in_g"], p["ln_in_b"], p["w_in"], p["w_gate"], p["b_gate"], p["q_norm_g"],
            p["w_uq_t"] if token_minor else p["w_uq"], p["kv_norm_g"], p["w_uk"],
            p["w_uv_t"] if token_minor else p["w_uv"]]
    if token_minor:
        for j in (3, 5):
            out_specs[j] = pl.BlockSpec((None, outs[j][0], tm), lambda i: (i, 0, 0))
            out_shape[j] = jax.ShapeDtypeStruct((n_tiles, outs[j][0], tm), BF16)
        tab_t = pl.BlockSpec((QK_ROPE // 2, tm), lambda i: (0, i % tab_blocks))
        in_specs += [tab_t, tab_t]
        args += list(tabs_t)
    return pl.pallas_call(
        functools.partial(_in_proj_kernel, token_minor=token_minor),
        grid=(n_tiles,),
        in_specs=in_specs,
        out_specs=out_specs,
        out_shape=out_shape,
        compiler_params=pltpu.CompilerParams(dimension_semantics=("arbitrary",), vmem_limit_bytes=VMEM_LIMIT),
        name="in_proj",
    )(*args)


def _rglru_kernel(rx_ref, grg_ref, cprev_ref, hprev_ref, cw_ref, cb_ref, wax_ref, ba_ref, bx_ref, lam_ref,
                  y_ref, cout_ref, hout_ref, xe_ref, a_ref, u_ref, hc_ref, *, nb, tt):
    pitch = tt + SUBLANES

    @pl.when(pl.program_id(0) == 0)
    def _():
        xe_ref[:, CONV_TAIL:SUBLANES, :] = cprev_ref[...]
        hc_ref[...] = hprev_ref[...]

    xc, new_tail = _conv_tile(xe_ref, rx_ref[...], cw_ref, cb_ref)
    cout_ref[...] = new_tail
    rate = _decay_rate(lam_ref)
    xc2 = xc.reshape(nb * tt, D_RNN)
    for n in range(D_RNN // LANES):
        cs = slice(n * LANES, (n + 1) * LANES)
        a, u = _recurrence_inputs(xc2[:, cs], wax_ref[n], ba_ref[:, cs], bx_ref[:, cs], rate[:, cs])
        for b in range(nb):
            a_ref[n, b * pitch:b * pitch + tt, :] = a[b * tt:(b + 1) * tt]
            u_ref[n, b * pitch:b * pitch + tt, :] = u[b * tt:(b + 1) * tt]
    _scan_slabs(a_ref, u_ref, hc_ref, nb, tt)
    hout_ref[...] = hc_ref[...]
    _gated_branch_output(y_ref, u_ref, grg_ref, nb, tt)


def _rglru(rx, grg, conv_prev, h_prev, p, tt):
    nb, t_len, _ = rx.shape
    assert nb % SUBLANES == 0 and t_len % tt == 0 and tt % SUBLANES == 0
    pitch = tt + SUBLANES
    seq = pl.BlockSpec((nb, tt, D_RNN), lambda i: (0, i, 0))
    return pl.pallas_call(
        functools.partial(_rglru_kernel, nb=nb, tt=tt),
        grid=(t_len // tt,),
        in_specs=[seq, seq, _const_spec((nb, CONV_W - 1, D_RNN)), _const_spec((nb, D_RNN)),
                  _const_spec((CONV_W, D_RNN)), _const_spec((1, D_RNN)),
                  _const_spec((RG_BLOCKS, RG_BLOCK, 2 * RG_BLOCK)),
                  _const_spec((1, D_RNN)), _const_spec((1, D_RNN)), _const_spec((1, D_RNN))],
        out_specs=[seq,
                   pl.BlockSpec((nb, CONV_W - 1, D_RNN), lambda i: (0, 0, 0)),
                   pl.BlockSpec((nb, D_RNN), lambda i: (0, 0))],
        out_shape=[jax.ShapeDtypeStruct((nb, t_len, D_RNN), BF16),
                   jax.ShapeDtypeStruct((nb, CONV_W - 1, D_RNN), F32),
                   jax.ShapeDtypeStruct((nb, D_RNN), F32)],
        scratch_shapes=[pltpu.VMEM((nb, tt + SUBLANES, D_RNN), F32),
                        pltpu.VMEM((D_RNN // LANES, nb * pitch, LANES), F32),
                        pltpu.VMEM((D_RNN // LANES, nb * pitch, LANES), F32),
                        pltpu.VMEM((nb, D_RNN), F32)],
        compiler_params=pltpu.CompilerParams(dimension_semantics=("arbitrary",), vmem_limit_bytes=VMEM_LIMIT),
        name="rglru",
    )(rx, grg, conv_prev, h_prev, p["conv_w"], p["conv_b"], p["w_ax"], p["rg_ba"], p["rg_bx"], p["rg_lambda"])


def _flash_kernel(qt_ref, k_ref, vt_ref, km_ref, vmt_ref, o_ref, *, tq, tv, hps):
    qcols = [slice(hh * HEAD_PAD, (hh + 1) * HEAD_PAD) for hh in range(hps)]
    vrows = [slice(hh * V_HEAD, (hh + 1) * V_HEAD) for hh in range(hps)]

    def queries_t(hh):
        return jnp.concatenate([qt_ref[c, qcols[hh], :] for c in range(tq // tv)], axis=1)

    def scores(j, hh):
        return _dot(k_ref[j * tq:(j + 1) * tq, qcols[hh]], queries_t(hh))

    def values_t(j, hh):
        return jnp.concatenate([vt_ref[j * (tq // tv) + c, vrows[hh], :] for c in range(tq // tv)], axis=1)

    def tile(n_full):
        kpos = lax.broadcasted_iota(jnp.int32, (tq, tq), 0)
        qpos = lax.broadcasted_iota(jnp.int32, (tq, tq), 1)

        def chunk_scores(j, hh):
            s_t = scores(j, hh)
            return s_t if j < n_full else jnp.where(kpos <= qpos, s_t, NEG)

        size = FLASH_GROUP if n_full + 1 > FLASH_GROUP else 1
        groups = [list(range(g0, min(g0 + size, n_full + 1))) for g0 in range(0, n_full + 1, size)]

        def group_scores(gi, hh):
            return [chunk_scores(j, hh) for j in groups[gi]]

        state = [None] * hps
        s_next = [group_scores(0, hh) for hh in range(hps)]
        for gi, group in enumerate(groups):
            for hh in range(hps):
                ss = s_next[hh]
                vs = [values_t(j, hh) for j in group]
                if gi == 0:
                    ss = [_dot(km_ref[:, qcols[hh]], queries_t(hh))] + ss
                    vs = [vmt_ref[vrows[hh], :]] + vs
                if gi + 1 < len(groups):
                    s_next[hh] = group_scores(gi + 1, hh)
                m_new = functools.reduce(jnp.maximum, [jnp.max(s_t, axis=0, keepdims=True) for s_t in ss])
                if state[hh] is not None:
                    m_new = jnp.maximum(state[hh][0], m_new)
                ps = [jnp.exp2(s_t - m_new) for s_t in ss]
                l = functools.reduce(jnp.add, [jnp.sum(p, axis=0, keepdims=True) for p in ps])
                acc_t = functools.reduce(jnp.add, [_dot(v_t, p.astype(BF16)) for v_t, p in zip(vs, ps)])
                if state[hh] is not None:
                    m, l_old, acc_old = state[hh]
                    alpha = jnp.exp2(m - m_new)
                    l, acc_t = alpha * l_old + l, alpha * acc_old + acc_t
                state[hh] = (m_new, l, acc_t)
        for hh in range(hps):
            _, l, acc_t = state[hh]
            o_ref[:, vrows[hh]] = jnp.transpose(acc_t / l).astype(BF16)

    for c in range(k_ref.shape[0] // tq):
        pl.when(pl.program_id(2) == c)(functools.partial(tile, c))


def _flash(q_t, k, v_t, k_pre, v_pre_t, tq, hps):
    nb, t_len, _ = k.shape
    tv = v_t.shape[3]
    assert t_len % tq == 0 and tq % tv == 0 and N_HEADS % hps == 0
    return pl.pallas_call(
        functools.partial(_flash_kernel, tq=tq, tv=tv, hps=hps),
        grid=(nb, N_HEADS // hps, t_len // tq),
        in_specs=[pl.BlockSpec((None, tq // tv, hps * HEAD_PAD, tv), lambda b, h, i: (b, i, h, 0)),
                  pl.BlockSpec((None, t_len, hps * HEAD_PAD), lambda b, h, i: (b, 0, h)),
                  pl.BlockSpec((None, t_len // tv, hps * V_HEAD, tv), lambda b, h, i: (b, 0, h, 0)),
                  pl.BlockSpec((N_META, hps * HEAD_PAD), lambda b, h, i: (0, h)),
                  pl.BlockSpec((hps * V_HEAD, N_META), lambda b, h, i: (h, 0))],
        out_specs=pl.BlockSpec((None, tq, hps * V_HEAD), lambda b, h, i: (b, i, h)),
        out_shape=jax.ShapeDtypeStruct((nb, t_len, N_HEADS * V_HEAD), BF16),
        compiler_params=pltpu.CompilerParams(dimension_semantics=("arbitrary",) * 3, vmem_limit_bytes=VMEM_LIMIT),
        name="flash_prompt",
    )(q_t, k, v_t, k_pre, v_pre_t)


def _meta_attn_kernel(q_ref, k_ref, v_ref, o_ref):
    qpos = lax.broadcasted_iota(jnp.int32, (N_META, N_META), 0)
    kpos = lax.broadcasted_iota(jnp.int32, (N_META, N_META), 1)
    for hh in range(N_HEADS):
        qs = slice(hh * HEAD_PAD, (hh + 1) * HEAD_PAD)
        vs = slice(hh * V_HEAD, (hh + 1) * V_HEAD)
        s = jnp.where(kpos <= qpos, _dot_nt(q_ref[:, qs], k_ref[:, qs]), NEG)
        pr = jnp.exp2(s - jnp.max(s, axis=1, keepdims=True))
        o = _dot(pr.astype(BF16), v_ref[:, vs]) / jnp.sum(pr, axis=1, keepdims=True)
        o_ref[:, vs] = o.astype(BF16)


def _meta_attn(q, k, v):
    return pl.pallas_call(
        _meta_attn_kernel,
        out_shape=jax.ShapeDtypeStruct((N_META, N_HEADS * V_HEAD), BF16),
        name="meta_attn",
    )(q, k, v)


def _absorb_kernel(q_ref, wukt_ref, qlat_ref, qpe_ref):
    for hh in range(N_HEADS):
        q = q_ref[:, hh * HEAD_PAD:(hh + 1) * HEAD_PAD]
        qlat_ref[hh] = _dot(q[:, :QK_NOPE], wukt_ref[hh])
        qpe_ref[hh] = q[:, QK_NOPE:].astype(F32)


def _absorb(q, w_ukt):
    rows = q.shape[0]
    return pl.pallas_call(
        _absorb_kernel,
        out_shape=[jax.ShapeDtypeStruct((N_HEADS, rows, KV_LORA), F32),
                   jax.ShapeDtypeStruct((N_HEADS, rows, ROPE_PAD), F32)],
        name="absorb_q",
    )(q, w_ukt)


PAGED_BUFS = 2


def _paged_scratch(n_q, pps, page):
    return [pltpu.VMEM((n_q, 1), F32), pltpu.VMEM((n_q, 1), F32), pltpu.VMEM((n_q, KV_LORA), F32),
            pltpu.VMEM((PAGED_BUFS, pps * page, KV_LORA), F32), pltpu.VMEM((PAGED_BUFS, pps, QK_ROPE, page), F32),
            pltpu.VMEM((pps * page, KV_LORA), BF16), pltpu.VMEM((QK_ROPE, pps * page), BF16),
            pltpu.SemaphoreType.DMA((PAGED_BUFS,))]


def _paged_copies(pt_ref, ckv_hbm, kpe_hbm, cin_ref, kin_ref, sem, seq, chunk, slot, pps):
    page = kin_ref.shape[3]
    out = []
    for i in range(pps):
        pid = pt_ref[seq, chunk * pps + i]
        out.append(pltpu.make_async_copy(ckv_hbm.at[pid], cin_ref.at[slot, pl.ds(i * page, page)], sem.at[slot]))
        out.append(pltpu.make_async_copy(kpe_hbm.at[pid], kin_ref.at[slot, i], sem.at[slot]))
    return out


def _start_on_both_dma_threads(page_copies):
    for k, cp in enumerate(page_copies):
        cp.start(priority=(k // 2) % 2)


def _paged_new_tokens(qlat, qpe, cnew_ref, knew_ref, m_ref, l_ref, acc_ref, ts):
    cn = cnew_ref[...].astype(BF16)
    kn = knew_ref[...].astype(BF16)
    s = _dot_nt(qlat, cn) + _dot_nt(qpe, kn)
    t_q = lax.broadcasted_iota(jnp.int32, s.shape, 0) % ts
    t_k = lax.broadcasted_iota(jnp.int32, s.shape, 1)
    s = jnp.where(t_k <= t_q, s, NEG)
    m = jnp.max(s, axis=1, keepdims=True)
    pr = jnp.exp2(s - m)
    m_ref[...] = m
    l_ref[...] = jnp.sum(pr, axis=1, keepdims=True)
    acc_ref[...] = _dot(pr.astype(BF16), cn)


def _paged_scores(qlat, qpe, cin_ref, kin_ref, cbuf_ref, kbuf_ref, slot, pps):
    page = kin_ref.shape[3]
    for i in range(pps):
        rows = pl.ds(i * page, page)
        cbuf_ref[rows, :] = cin_ref[slot, rows, :].astype(BF16)
        kbuf_ref[:, i * page:(i + 1) * page] = kin_ref[slot, i].astype(BF16)
    return _dot_nt(qlat, cbuf_ref[...]) + _dot(qpe, kbuf_ref[...])


def _paged_probs(s, m_ref, l_ref):
    m = m_ref[...]
    m_new = jnp.maximum(m, jnp.max(s, axis=1, keepdims=True))
    alpha = jnp.exp2(m - m_new)
    pr = jnp.exp2(s - m_new)
    m_ref[...] = m_new
    l_ref[...] = alpha * l_ref[...] + jnp.sum(pr, axis=1, keepdims=True)
    return alpha, pr.astype(BF16)


def _paged_accumulate(alpha, pr, cbuf_ref, acc_ref):
    acc_ref[...] = alpha * acc_ref[...] + _dot(pr, cbuf_ref[...])


def _unabsorb_kernel(olat_ref, wuv_ref, o_ref):
    for hh in range(N_HEADS):
        o_ref[:, hh * V_HEAD:(hh + 1) * V_HEAD] = _dot(olat_ref[hh].astype(BF16), wuv_ref[hh]).astype(BF16)


def _unabsorb(o_lat, w_uvh):
    return pl.pallas_call(
        _unabsorb_kernel,
        out_shape=jax.ShapeDtypeStruct((o_lat.shape[1], N_HEADS * V_HEAD), BF16),
        name="unabsorb_o",
    )(o_lat, w_uvh)


FF_CHUNKS = 4
N_POST_WEIGHTS = 13


def _post_compute(x_ref, y_ref, o_ref, g_ref, weights, out_ref, side_work=None):
    before_up, before_down, after_down = side_work if side_work is not None else (lambda c: None,) * 3
    lng_ref, lnb_ref, wr_ref, wm_ref, wo_ref, g1_ref, b1_ref, wup_ref, bup_ref, wdn_ref, bdn_ref, g2_ref, b2_ref = weights
    h = _layer_norm(x_ref[...], lng_ref[...], lnb_ref[...])
    z_r = _dot(y_ref[...], wr_ref[...])
    z_m = _dot(o_ref[...], wm_ref[...])
    mix_in = g_ref[:, :D_MODEL] * z_r + g_ref[:, D_MODEL:] * z_m
    mix = _dot(mix_in.astype(BF16), wo_ref[...])
    before_up(0)
    x1 = _layer_norm(DN_ALPHA * h + mix, g1_ref[...], b1_ref[...])
    x1b = x1.astype(BF16)
    f = bdn_ref[...]
    ff_chunk = D_FF // FF_CHUNKS
    for c in range(FF_CHUNKS):
        cs = slice(c * ff_chunk, (c + 1) * ff_chunk)
        if c > 0:
            before_up(c)
        up = jnp.maximum(_dot(x1b, wup_ref[:, cs]) + bup_ref[:, cs], 0.0)
        before_down(c)
        f = f + _dot((up * up).astype(BF16), wdn_ref[cs, :])
        after_down(c)
    out_ref[...] = _layer_norm(DN_ALPHA * x1 + f, g2_ref[...], b2_ref[...])


def _post_kernel(x_ref, y_ref, o_ref, g_ref, *rest):
    _post_compute(x_ref, y_ref, o_ref, g_ref, rest[:N_POST_WEIGHTS], rest[N_POST_WEIGHTS])


def _post_paged_kernel(pt_ref, x_ref, y_ref, o_ref, g_ref, *rest, pps, ts, n_j, n_steps, tiles_per_seq):
    weights = rest[:N_POST_WEIGHTS]
    (qlat_ref, qpe_ref, cnew_ref, knew_ref, lat_ref, latm_ref, ckv_hbm, kpe_hbm, out_ref, olat_ref, platent_hbm,
     m_ref, l_ref, acc_ref, cin_ref, kin_ref, cbuf_ref, kbuf_ref, sem, lsem) = rest[N_POST_WEIGHTS:]
    s = pl.program_id(0)
    spb = n_j // FF_CHUNKS
    seq, j0 = lax.div(s, spb), lax.rem(s, spb) * FF_CHUNKS
    copies = functools.partial(_paged_copies, pt_ref, ckv_hbm, kpe_hbm, cin_ref, kin_ref, sem, pps=pps)

    tm = out_ref.shape[0]
    lat_rows = pltpu.make_async_copy(
        lat_ref,
        platent_hbm.at[lax.div(s, tiles_per_seq), pl.ds(N_META + lax.rem(s, tiles_per_seq) * tm, tm)], lsem.at[0])
    lat_meta = [pltpu.make_async_copy(latm_ref, platent_hbm.at[b, pl.ds(0, N_META)], lsem.at[1])
                for b in range(platent_hbm.shape[0])]
    lat_rows.start()

    @pl.when(s == 0)
    def _():
        for cp in lat_meta:
            cp.start()
        _start_on_both_dma_threads(copies(0, 0, 0))

    live = {}

    def gather_and_score(c):
        slot, nxt = c % PAGED_BUFS, (c + 1) % PAGED_BUFS
        if c + 1 < FF_CHUNKS:
            _start_on_both_dma_threads(copies(seq, j0 + c + 1, nxt))
        else:
            @pl.when(s + 1 < n_steps)
            def _():
                _start_on_both_dma_threads(copies(lax.div(s + 1, spb), lax.rem(s + 1, spb) * FF_CHUNKS, nxt))
        for cp in copies(seq, j0 + c, slot):
            cp.wait()
        n_q = acc_ref.shape[0]
        qlat = qlat_ref[...].reshape(n_q, KV_LORA).astype(BF16)
        qpe = qpe_ref[...].reshape(n_q, ROPE_PAD)[:, :QK_ROPE].astype(BF16)
        if c == 0:
            pl.when(j0 == 0)(functools.partial(_paged_new_tokens, qlat, qpe, cnew_ref, knew_ref,
                                               m_ref, l_ref, acc_ref, ts))
        live["s"] = _paged_scores(qlat, qpe, cin_ref, kin_ref, cbuf_ref, kbuf_ref, slot, pps)

    def probs(c):
        live["alpha"], live["pr"] = _paged_probs(live.pop("s"), m_ref, l_ref)

    def accumulate(c):
        _paged_accumulate(live.pop("alpha"), live.pop("pr"), cbuf_ref, acc_ref)
        if c == FF_CHUNKS - 1:
            @pl.when(j0 == n_j - FF_CHUNKS)
            def _():
                olat_ref[...] = (acc_ref[...] / l_ref[...]).reshape(olat_ref.shape)

    _post_compute(x_ref, y_ref, o_ref, g_ref, weights, out_ref, side_work=(gather_and_score, probs, accumulate))

    lat_rows.wait()

    @pl.when(s == 0)
    def _():
        for cp in lat_meta:
            cp.wait()


def _post_weight_specs():
    vec = lambda w: _const_spec((1, w))
    sq = _const_spec((D_MODEL, D_MODEL))
    return [vec(D_MODEL), vec(D_MODEL), sq, sq, sq, vec(D_MODEL), vec(D_MODEL),
            _const_spec((D_MODEL, D_FF)), vec(D_FF), _const_spec((D_FF, D_MODEL)), vec(D_MODEL),
            vec(D_MODEL), vec(D_MODEL)]


def _post_weights(p):
    return (p["ln_in_g"], p["ln_in_b"], p["w_br_r"], p["w_br_m"], p["w_o"], p["ln1_g"], p["ln1_b"],
            p["w_up"], p["b_up"], p["w_down"], p["b_down"], p["ln2_g"], p["ln2_b"])


def _post(x, y, o, g, p, tm):
    rows = x.shape[0]
    assert rows % tm == 0
    row = lambda w: pl.BlockSpec((tm, w), lambda i: (i, 0))
    return pl.pallas_call(
        _post_kernel,
        grid=(rows // tm,),
        in_specs=[row(D_MODEL), row(D_RNN), row(N_HEADS * V_HEAD), row(2 * D_MODEL)] + _post_weight_specs(),
        out_specs=row(D_MODEL),
        out_shape=jax.ShapeDtypeStruct((rows, D_MODEL), F32),
        compiler_params=pltpu.CompilerParams(dimension_semantics=("arbitrary",), vmem_limit_bytes=VMEM_LIMIT),
        name="merge_mlp",
    )(x, y, o, g, *_post_weights(p))


def _post_with_paged_attn(x, y, o, g, p, tm, page_table, q_lat, q_pe, ckv_new, kpe_new, ckv_pool, kpe_pool_t,
                          latent, latent_meta, n_seq):
    rows = x.shape[0]
    n_steps = rows // tm
    assert n_steps % n_seq == 0
    nb, n_pages = page_table.shape
    ts = q_lat.shape[2]
    n_q = N_HEADS * ts
    page = ckv_pool.shape[1]
    n_new = ckv_new.shape[1]
    chunks = n_steps * FF_CHUNKS
    assert rows % tm == 0 and (nb * n_pages) % chunks == 0 and FF_CHUNKS % PAGED_BUFS == 0
    pps = nb * n_pages // chunks
    n_j = n_pages // pps
    assert n_pages % pps == 0 and n_j % FF_CHUNKS == 0
    spb = n_j // FF_CHUNKS
    row = lambda w: pl.BlockSpec((tm, w), lambda i, pt: (i, 0))
    per_seq = lambda r, w: pl.BlockSpec((None, r, w), lambda i, pt: (i // spb, 0, 0))
    heads_of_seq = lambda w: pl.BlockSpec((N_HEADS, None, ts, w), lambda i, pt: (0, i // spb, 0, 0))
    grid_spec = pltpu.PrefetchScalarGridSpec(
        num_scalar_prefetch=1,
        grid=(n_steps,),
        in_specs=[row(D_MODEL), row(D_RNN), row(N_HEADS * V_HEAD), row(2 * D_MODEL)] + _post_weight_specs()
                 + [heads_of_seq(KV_LORA), heads_of_seq(ROPE_PAD), per_seq(n_new, KV_LORA), per_seq(n_new, QK_ROPE),
                    row(KV_LORA), _const_spec((N_META, KV_LORA))]
                 + [pl.BlockSpec(memory_space=pl.ANY)] * 2,
        out_specs=[row(D_MODEL), heads_of_seq(KV_LORA), pl.BlockSpec(memory_space=pl.ANY)],
        scratch_shapes=_paged_scratch(n_q, pps, page) + [pltpu.SemaphoreType.DMA((2,))],
    )
    return pl.pallas_call(
        functools.partial(_post_paged_kernel, pps=pps, ts=ts, n_j=n_j, n_steps=n_steps,
                          tiles_per_seq=n_steps // n_seq),
        grid_spec=grid_spec,
        out_shape=[jax.ShapeDtypeStruct((rows, D_MODEL), F32),
                   jax.ShapeDtypeStruct((N_HEADS, nb, ts, KV_LORA), F32),
                   jax.ShapeDtypeStruct((n_seq, N_META + rows // n_seq, KV_LORA), F32)],
        compiler_params=pltpu.CompilerParams(dimension_semantics=("arbitrary",), vmem_limit_bytes=VMEM_LIMIT),
        name="merge_mlp_paged_attn",
    )(page_table, x, y, o, g, *_post_weights(p), q_lat, q_pe, ckv_new, kpe_new, latent, latent_meta,
      ckv_pool, kpe_pool_t)


def _rope_tables(pos):
    half = QK_ROPE // 2
    inv = 1.0 / (ROPE_THETA ** (jnp.arange(half, dtype=F32) / half))
    ang = pos.astype(F32)[:, None] * inv[None, :]
    cos, sin = jnp.cos(ang), jnp.sin(ang)
    z = jnp.zeros_like(cos)
    return ((jnp.concatenate([cos, cos, z, z], 1), jnp.concatenate([-sin, z, z, z], 1),
             jnp.concatenate([z, sin, z, z], 1)), (cos.T, sin.T))


def _prep_params(w_in, b_gate, conv_w, conv_b, rg_wa, rg_ba, rg_wx, rg_bx, rg_lambda, w_br_r, q_norm_g, w_uq,
                 kv_norm_g, w_uk, w_uv, w_br_m, w_o, ln1_g, ln1_b, w_up, b_up, w_down, b_down, ln2_g, ln2_b,
                 ln_in_g, ln_in_b):
    l = 0
    vec = lambda a: a.reshape(1, -1).astype(F32)
    wi = w_in[l].astype(BF16)
    w_uq_p = jnp.pad(w_uq[l], ((0, 0), (0, 0), (0, HEAD_PAD - QK_NOPE - QK_ROPE))).reshape(Q_LORA, N_HEADS * HEAD_PAD)
    return {
        "ln_in_g": vec(ln_in_g), "ln_in_b": vec(ln_in_b),
        "w_in": wi, "w_gate": wi[:, _C_GATE:], "b_gate": vec(b_gate[l]),
        "q_norm_g": vec(q_norm_g[l]), "w_uq": w_uq_p.astype(BF16), "w_uq_t": w_uq_p.T.astype(BF16),
        "kv_norm_g": vec(kv_norm_g[l]),
        "w_uk": w_uk[l].reshape(KV_LORA, N_HEADS * QK_NOPE).astype(BF16),
        "w_uv": w_uv[l].reshape(KV_LORA, N_HEADS * V_HEAD).astype(BF16),
        "w_uv_t": w_uv[l].reshape(KV_LORA, N_HEADS * V_HEAD).T.astype(BF16),
        "w_ukt": jnp.transpose(w_uk[l], (1, 2, 0)).astype(BF16),
        "w_uvh": jnp.transpose(w_uv[l], (1, 0, 2)).astype(BF16),
        "conv_w": conv_w[l].astype(F32), "conv_b": vec(conv_b[l]),
        "w_ax": jnp.concatenate([rg_wa[l], rg_wx[l]], axis=-1).astype(BF16),
        "rg_ba": vec(rg_ba[l]), "rg_bx": vec(rg_bx[l]), "rg_lambda": vec(rg_lambda[l]),
        "w_br_r": w_br_r[l].astype(BF16), "w_br_m": w_br_m[l].astype(BF16), "w_o": w_o[l].astype(BF16),
        "ln1_g": vec(ln1_g[l]), "ln1_b": vec(ln1_b[l]),
        "w_up": w_up[l].astype(BF16), "b_up": vec(b_up[l]),
        "w_down": w_down[l].astype(BF16), "b_down": vec(b_down[l]),
        "ln2_g": vec(ln2_g[l]), "ln2_b": vec(ln2_b[l]),
    }


def kernel(x_prompt, x_sample, cache_ckv, cache_kpe, page_table, state_conv, state_rglru, meta_tokens, ln_in_g, ln_in_b, w_in, b_gate, conv_w, conv_b, rg_wa, rg_ba, rg_wx, rg_bx, rg_lambda, w_br_r, q_norm_g, w_uq, kv_norm_g, w_uk, w_uv, w_br_m, w_o, ln1_g, ln1_b, w_up, b_up, w_down, b_down, ln2_g, ln2_b):
    assert w_in.shape[0] == DEPTH == 1
    bn, seq, _ = x_prompt.shape
    bd, ts, _ = x_sample.shape
    past_len = page_table.shape[1] * cache_ckv.shape[2]
    p = _prep_params(w_in, b_gate, conv_w, conv_b, rg_wa, rg_ba, rg_wx, rg_bx, rg_lambda, w_br_r, q_norm_g, w_uq,
                     kv_norm_g, w_uk, w_uv, w_br_m, w_o, ln1_g, ln1_b, w_up, b_up, w_down, b_down, ln2_g, ln2_b,
                     ln_in_g, ln_in_b)
    n_s = bd * ts

    x_small = jnp.concatenate([x_sample.reshape(n_s, D_MODEL), meta_tokens.astype(F32)], axis=0)
    pos_small = jnp.concatenate([jnp.tile(past_len + jnp.arange(ts), bd), jnp.arange(N_META)])
    n_small = n_s + N_META
    rx, grg, g_small, q, k, v, ckv, kpe = _in_proj(x_small, _rope_tables(pos_small)[0], 1, p, n_small)

    rep = lambda a: jnp.broadcast_to(a[None], (SUBLANES,) + a.shape)
    y_m, conv_m, h_m = _rglru(rep(rx[n_s:]), rep(grg[n_s:]), jnp.zeros((SUBLANES, CONV_W - 1, D_RNN), F32),
                              jnp.zeros((SUBLANES, D_RNN), F32), p, N_META)
    k_meta, v_meta = k[n_s:], v[n_s:]
    o_m = _meta_attn(q[n_s:], k_meta, v_meta)

    y_s, conv_s, h_s = _rglru(rx[:n_s].reshape(bd, ts, D_RNN), grg[:n_s].reshape(bd, ts, D_RNN),
                              state_conv[0], state_rglru[0], p, ts)
    q_lat, q_pe = _absorb(q[:n_s], p["w_ukt"])
    ckv_s = ckv[:n_s].reshape(bd, ts, KV_LORA)
    kpe_s = kpe[:n_s].reshape(bd, ts, QK_ROPE)
    pad_new = lambda a: jnp.pad(a, ((0, 0), (0, 2 * SUBLANES - ts), (0, 0)))

    n_p = bn * seq
    pos_p = N_META + jnp.arange(seq)
    tm = TM_IN_PROJ
    bcast = lambda a: jnp.broadcast_to(a[:1], (bn,) + a.shape[1:])
    tabs_p, tabs_pt = _rope_tables(pos_p)
    rx_p, grg_p, g_p, qt_p, k_p, vt_p, ckv_p, kpe_p = _in_proj(x_prompt.reshape(n_p, D_MODEL), tabs_p,
                                                               seq // tm, p, tm, tabs_t=tabs_pt)
    y_p, conv_p, h_p = _rglru(rx_p.reshape(bn, seq, D_RNN), grg_p.reshape(bn, seq, D_RNN), bcast(conv_m), bcast(h_m),
                              p, TT_SCAN)
    o_p = _flash(qt_p.reshape(bn, seq // tm, -1, tm), k_p.reshape(bn, seq, -1), vt_p.reshape(bn, seq // tm, -1, tm),
                 k_meta, v_meta.T, TQ_FLASH, FLASH_HEADS)
    out_p, o_lat, prompt_ckv = _post_with_paged_attn(
        x_prompt.reshape(n_p, D_MODEL), y_p.reshape(n_p, D_RNN), o_p.reshape(n_p, -1), g_p, p, TM_POST,
        page_table, q_lat.reshape(N_HEADS, bd, ts, KV_LORA), q_pe.reshape(N_HEADS, bd, ts, ROPE_PAD),
        pad_new(ckv_s), pad_new(kpe_s), cache_ckv[0], jnp.swapaxes(cache_kpe[0], 1, 2),
        ckv_p, ckv[n_s:], bn)

    o_s = _unabsorb(o_lat.reshape(N_HEADS, n_s, KV_LORA), p["w_uvh"])
    y_small = jnp.concatenate([y_s.reshape(n_s, D_RNN), y_m[0]], axis=0)
    o_small = jnp.concatenate([o_s, o_m], axis=0)
    out_small = _post(x_small, y_small, o_small, g_small, p, n_small)

    meta_rows = lambda a: jnp.broadcast_to(a[n_s:][None], (bn, N_META, a.shape[-1]))
    y_prompt = out_p.reshape(bn, seq, D_MODEL)
    y_sample = out_small[:n_s].reshape(bd, ts, D_MODEL)
    prompt_kpe = jnp.concatenate([meta_rows(kpe), kpe_p.reshape(bn, seq, QK_ROPE)], axis=1)[None]
    return (y_prompt, y_sample, prompt_ckv[None], prompt_kpe, conv_p[None], h_p[None],
            ckv_s[None], kpe_s[None], conv_s[None], h_s[None])
```
